```python
import math
import jax, jax.numpy as jnp
from jax import lax
import numpy as np

D_MODEL = 1024
BATCH = 16
SEQ = 4096
DEPTH = 1
DEC_BATCH = 16
DEC_SEQ = 16
PAST_LEN = 2048

CHUNK = 64
MIX_WIDTH = D_MODEL
POOL_WIDTH = MIX_WIDTH // 2
POOL_WINDOWS = (2, 4, 8, 16)
N_POOL_GROUPS = len(POOL_WINDOWS)
POOL_GROUP = POOL_WIDTH // N_POOL_GROUPS
POOL_HIST = max(POOL_WINDOWS) - 1
ATTN_WIDTH = MIX_WIDTH - POOL_WIDTH
HEAD_DIM = 64
N_HEADS = ATTN_WIDTH // HEAD_DIM
N_KV_HEADS = 2
GROUP = N_HEADS // N_KV_HEADS
KV_WIDTH = N_KV_HEADS * HEAD_DIM
WINDOW = 128
WINDOW_CHUNKS = WINDOW // CHUNK
BAND = (WINDOW_CHUNKS + 1) * CHUNK
ATTN_SCALE = HEAD_DIM ** -0.5
NUM_BUCKETS = 32
MAX_DISTANCE = 128
N_EXPERTS = 32
TOP_K = 4
D_FF = D_MODEL
SWIGLU_LIMIT = 7.0
SWIGLU_ALPHA = 1.702
ROUTE_BLOCK = 128
EPS = 1e-5
NEG_INF = -1e30
Q_OFF = POOL_WIDTH
K_OFF = Q_OFF + ATTN_WIDTH
V_OFF = K_OFF + KV_WIDTH
IN_WIDTH = V_OFF + KV_WIDTH

kernel_name = "hybrid_pool_swa_moe_stream_step"


def rms_norm(x, g):
    xf = x.astype(jnp.float32)
    y = xf * lax.rsqrt(jnp.mean(xf * xf, axis=-1, keepdims=True) + EPS)
    return (y * g.astype(jnp.float32)).astype(x.dtype)


def t5_bucket(rel):
    half = NUM_BUCKETS // 2
    max_exact = half // 2
    n = jnp.abs(rel)
    nf = jnp.maximum(n, 1).astype(jnp.float32)
    large = max_exact + (jnp.log(nf / max_exact) / math.log(MAX_DISTANCE / max_exact)
                         * (half - max_exact)).astype(jnp.int32)
    large = jnp.minimum(large, half - 1)
    return jnp.where(rel > 0, half, 0) + jnp.where(n < max_exact, n, large)


def rel_pos_bias(rel, rel_bias):
    b = rel_bias[t5_bucket(rel)].astype(jnp.float32)
    return jnp.moveaxis(b, -1, 0).reshape(N_KV_HEADS, GROUP, *rel.shape)


def mixer_inputs(x, norm1_g, w_in, q_norm_g, k_norm_g):
    B, T, _ = x.shape
    z = rms_norm(x, norm1_g) @ w_in
    u = z[..., :POOL_WIDTH]
    q = z[..., Q_OFF:K_OFF].reshape(B, T, N_KV_HEADS, GROUP, HEAD_DIM)
    k = z[..., K_OFF:V_OFF].reshape(B, T, N_KV_HEADS, HEAD_DIM)
    v = z[..., V_OFF:].reshape(B, T, N_KV_HEADS, HEAD_DIM)
    return u, rms_norm(q, q_norm_g), rms_norm(k, k_norm_g), v


def pool_mix(u, hist, pos0, w_pool, pool_scale):
    B, T, _ = u.shape
    up = jnp.concatenate([hist, u], axis=1).astype(jnp.float32)
    cs = jnp.concatenate([jnp.zeros((B, 1, POOL_WIDTH), jnp.float32), jnp.cumsum(up, axis=1)], axis=1)
    pos = pos0 + jnp.arange(T)
    pooled = []
    for g, w in enumerate(POOL_WINDOWS):
        sl = slice(g * POOL_GROUP, (g + 1) * POOL_GROUP)
        hi = cs[:, POOL_HIST + 1:POOL_HIST + 1 + T, sl]
        lo = cs[:, POOL_HIST + 1 - w:POOL_HIST + 1 - w + T, sl]
        cnt = jnp.minimum(pos + 1, w).astype(jnp.float32)[None, :, None]
        pooled.append((hi - lo) / cnt)
    d = (jnp.concatenate(pooled, axis=-1) - up[:, POOL_HIST:]).astype(u.dtype)
    d = d.reshape(B, T, N_POOL_GROUPS, POOL_GROUP)
    y = jnp.einsum('btgc,gce->btge', d, w_pool).reshape(B, T, POOL_WIDTH)
    return y * pool_scale


def sink_softmax(s, sink):
    m = jnp.maximum(jnp.max(s, axis=-1, keepdims=True), sink)
    e = jnp.exp(s - m)
    return e / (jnp.sum(e, axis=-1, keepdims=True) + jnp.exp(sink - m))


def attn_prompt(q, k, v, rel_bias, sinks):
    B, S = q.shape[:2]
    n_c = S // CHUNK
    back = WINDOW_CHUNKS * CHUNK
    qb = q.reshape(B, n_c, CHUNK, N_KV_HEADS, GROUP, HEAD_DIM)

    def band(t):
        tp = jnp.pad(t, ((0, 0), (back, 0), (0, 0), (0, 0)))
        tp = tp.reshape(B, n_c + WINDOW_CHUNKS, CHUNK, N_KV_HEADS, HEAD_DIM)
        return jnp.concatenate([tp[:, i:i + n_c] for i in range(WINDOW_CHUNKS + 1)], axis=2)

    kb, vb = band(k), band(v)
    s = jnp.einsum('bcqhgd,bckhd->bchgqk', qb, kb, preferred_element_type=jnp.float32) * ATTN_SCALE
    rel = (jnp.arange(BAND) - back)[None, :] - jnp.arange(CHUNK)[:, None]
    key_pos = jnp.arange(n_c)[:, None] * CHUNK - back + jnp.arange(BAND)[None, :]
    valid = (key_pos >= 0)[None, :, None, None, None, :]
    s = jnp.where(valid, s + rel_pos_bias(rel, rel_bias), NEG_INF)
    p = sink_softmax(s, sinks.astype(jnp.float32).reshape(N_KV_HEADS, GROUP, 1, 1))
    o = jnp.einsum('bchgqk,bckhd->bcqhgd', p.astype(v.dtype), vb)
    return o.reshape(B, S, ATTN_WIDTH)


def attn_sample(q, k, v, cache_k, cache_v, rel_bias, sinks):
    B, T = q.shape[:2]
    L = cache_k.shape[1]
    kk = jnp.concatenate([cache_k, k], axis=1)
    vv = jnp.concatenate([cache_v, v], axis=1)
    s = jnp.einsum('bqhgd,bkhd->bhgqk', q, kk, preferred_element_type=jnp.float32) * ATTN_SCALE
    rel = jnp.concatenate([jnp.arange(L) - L, jnp.arange(T)])[None, :] - jnp.arange(T)[:, None]
    p = sink_softmax(s + rel_pos_bias(rel, rel_bias), sinks.astype(jnp.float32).reshape(N_KV_HEADS, GROUP, 1, 1))
    o = jnp.einsum('bhgqk,bkhd->bqhgd', p.astype(vv.dtype), vv)
    return o.reshape(B, T, ATTN_WIDTH)


def clamped_swiglu(a, b):
    a = jnp.minimum(a, SWIGLU_LIMIT)
    b = jnp.clip(b, -SWIGLU_LIMIT, SWIGLU_LIMIT)
    return a * jax.nn.sigmoid(SWIGLU_ALPHA * a) * (b + 1)


def moe(h, w_router, b_router, w_gate, b_gate, w_up, b_up, w_down, b_down):
    lead = h.shape[:-1]
    hf = h.reshape(-1, D_MODEL)
    N = hf.shape[0]
    logits = jnp.dot(hf, w_router, preferred_element_type=jnp.float32) + b_router.astype(jnp.float32)
    top_val, top_idx = lax.top_k(logits, TOP_K)
    gates = jax.nn.softmax(top_val, axis=-1)
    A = N * TOP_K
    flat_e = top_idx.reshape(-1)
    flat_tok = jnp.arange(A, dtype=jnp.int32) // TOP_K
    order = jnp.argsort(flat_e)
    se, stok, sg = flat_e[order], flat_tok[order], gates.reshape(-1)[order]
    counts = jnp.bincount(flat_e, length=N_EXPERTS)
    start = jnp.cumsum(counts) - counts
    pcounts = (counts + ROUTE_BLOCK - 1) // ROUTE_BLOCK * ROUTE_BLOCK
    pend = jnp.cumsum(pcounts)
    pstart = pend - pcounts
    dest = pstart[se] + jnp.arange(A) - start[se]
    n_blocks = -(-(A + N_EXPERTS * (ROUTE_BLOCK - 1)) // ROUTE_BLOCK)
    buf_tok = jnp.full((n_blocks * ROUTE_BLOCK,), N, jnp.int32).at[dest].set(stok)
    buf_gate = jnp.zeros((n_blocks * ROUTE_BLOCK,), jnp.float32).at[dest].set(sg)
    block_e = jnp.minimum(jnp.searchsorted(pend, jnp.arange(n_blocks) * ROUTE_BLOCK, side='right'), N_EXPERTS - 1)
    h_pad = jnp.concatenate([hf, jnp.zeros((1, D_MODEL), hf.dtype)], axis=0)

    def step(acc, blk):
        tok, g, e = blk
        xb = h_pad[tok]
        a = xb @ w_gate[e] + b_gate[e]
        b = xb @ w_up[e] + b_up[e]
        yb = clamped_swiglu(a, b) @ w_down[e] + b_down[e]
        return acc.at[tok].add((yb * g[:, None]).astype(acc.dtype)), None

    acc0 = jnp.zeros((N + 1, D_MODEL), h.dtype)
    acc, _ = lax.scan(step, acc0, (buf_tok.reshape(n_blocks, ROUTE_BLOCK),
                                   buf_gate.reshape(n_blocks, ROUTE_BLOCK), block_e))
    return acc[:N].reshape(*lead, D_MODEL)


def setup_inputs(seed: int = 0) -> dict:
    key = jax.random.key(seed)
    ks = jax.random.split(key, 24)
    f32 = jnp.float32

    def nrm(k, shape, scale):
        return scale * jax.random.normal(k, shape, f32)

    cache_len = min(WINDOW, PAST_LEN)
    L = DEPTH
    return {
        "x_prompt": nrm(ks[0], (BATCH, SEQ, D_MODEL), 1.0),
        "x_sample": nrm(ks[1], (DEC_BATCH, DEC_SEQ, D_MODEL), 1.0),
        "cache_k": nrm(ks[2], (L, DEC_BATCH, cache_len, N_KV_HEADS, HEAD_DIM), 1.0),
        "cache_v": nrm(ks[3], (L, DEC_BATCH, cache_len, N_KV_HEADS, HEAD_DIM), 1.0),
        "state_pool": nrm(ks[4], (L, DEC_BATCH, POOL_HIST, POOL_WIDTH), 1.0),
        "norm1_g": 1.0 + nrm(ks[5], (L, D_MODEL), 0.02),
        "w_in": nrm(ks[6], (L, D_MODEL, IN_WIDTH), D_MODEL ** -0.5),
        "q_norm_g": 1.0 + nrm(ks[7], (L, HEAD_DIM), 0.02),
        "k_norm_g": 1.0 + nrm(ks[8], (L, HEAD_DIM), 0.02),
        "rel_bias": nrm(ks[9], (NUM_BUCKETS, N_HEADS), 0.5),
        "sinks": nrm(ks[10], (L, N_HEADS), 0.5),
        "w_pool": nrm(ks[11], (L, N_POOL_GROUPS, POOL_GROUP, POOL_GROUP), POOL_GROUP ** -0.5),
        "pool_scale": 1.0 + nrm(ks[12], (L, POOL_WIDTH), 0.1),
        "w_out": nrm(ks[13], (L, MIX_WIDTH, D_MODEL), MIX_WIDTH ** -0.5),
        "norm2_g": 1.0 + nrm(ks[14], (L, D_MODEL), 0.02),
        "w_router": nrm(ks[15], (L, D_MODEL, N_EXPERTS), D_MODEL ** -0.5),
        "b_router": nrm(ks[16], (L, N_EXPERTS), 0.01),
        "w_gate": nrm(ks[17], (L, N_EXPERTS, D_MODEL, D_FF), D_MODEL ** -0.5),
        "b_gate": nrm(ks[18], (L, N_EXPERTS, D_FF), 0.02),
        "w_up": nrm(ks[19], (L, N_EXPERTS, D_MODEL, D_FF), D_MODEL ** -0.5),
        "b_up": nrm(ks[20], (L, N_EXPERTS, D_FF), 0.02),
        "w_down": nrm(ks[21], (L, N_EXPERTS, D_FF, D_MODEL), D_FF ** -0.5),
        "b_down": nrm(ks[22], (L, N_EXPERTS, D_MODEL), 0.02),
    }


def reference(x_prompt, x_sample, cache_k, cache_v, state_pool, norm1_g, w_in, q_norm_g, k_norm_g,
              rel_bias, sinks, w_pool, pool_scale, w_out, norm2_g, w_router, b_router,
              w_gate, b_gate, w_up, b_up, w_down, b_down):
    xp, xs = x_prompt, x_sample
    kp_l, vp_l, pp_l, ks_l, vs_l, ps_l = [], [], [], [], [], []
    for l in range(DEPTH):
        moe_w = (w_router[l], b_router[l], w_gate[l], b_gate[l], w_up[l], b_up[l], w_down[l], b_down[l])
        u, q, k, v = mixer_inputs(xp, norm1_g[l], w_in[l], q_norm_g[l], k_norm_g[l])
        hist0 = jnp.zeros((xp.shape[0], POOL_HIST, POOL_WIDTH), u.dtype)
        mixed = jnp.concatenate([pool_mix(u, hist0, 0, w_pool[l], pool_scale[l]),
                                 attn_prompt(q, k, v, rel_bias, sinks[l])], axis=-1)
        xp = xp + mixed @ w_out[l]
        xp = xp + moe(rms_norm(xp, norm2_g[l]), *moe_w)
        kp_l.append(k[:, -WINDOW:])
        vp_l.append(v[:, -WINDOW:])
        pp_l.append(u[:, -POOL_HIST:])
        u, q, k, v = mixer_inputs(xs, norm1_g[l], w_in[l], q_norm_g[l], k_norm_g[l])
        mixed = jnp.concatenate([pool_mix(u, state_pool[l], PAST_LEN, w_pool[l], pool_scale[l]),
                                 attn_sample(q, k, v, cache_k[l], cache_v[l], rel_bias, sinks[l])], axis=-1)
        xs = xs + mixed @ w_out[l]
        xs = xs + moe(rms_norm(xs, norm2_g[l]), *moe_w)
        ks_l.append(k)
        vs_l.append(v)
        ps_l.append(jnp.concatenate([state_pool[l], u], axis=1)[:, -POOL_HIST:])
    return (xp, xs, jnp.stack(kp_l), jnp.stack(vp_l), jnp.stack(pp_l),
            jnp.stack(ks_l), jnp.stack(vs_l), jnp.stack(ps_l))
```

```python
import functools
import math

import numpy as np
import jax
import jax.numpy as jnp
from jax import lax
from jax.experimental import pallas as pl
from jax.experimental.pallas import tpu as pltpu
from jax.experimental.pallas import tpu_sc as plsc

D_MODEL = 1024
CHUNK = 64
POOL_WIDTH = 512
POOL_WINDOWS = (2, 4, 8, 16)
POOL_GROUP = 128
POOL_HIST = 15
ATTN_WIDTH = 512
HEAD_DIM = 64
N_HEADS = 8
N_KV_HEADS = 2
GROUP = 4
KV_WIDTH = 128
WINDOW = 128
NUM_BUCKETS = 32
MAX_DISTANCE = 128
PAST_LEN = 2048
N_EXPERTS = 32
TOP_K = 4
SWIGLU_LIMIT = 7.0
SWIGLU_ALPHA = 1.702
EPS = 1e-5
NEG_INF = -1e30
ATTN_SCALE = HEAD_DIM ** -0.5

PAIR = 2 * CHUNK
BAND = PAIR + WINDOW
HIST_ROWS = 16
MIX_TILE = 512
EXPERT_BLOCK = 512
COMBINE_TILE = 256
SC_WORKERS = 32
SC_CHUNK = 64
VMEM_LIMIT = 56 * 1024 * 1024


def _pack_bf16_pair(a, b):
    ab = lax.bitcast_convert_type(a.astype(jnp.bfloat16).astype(jnp.float32), jnp.uint32)
    bb = lax.bitcast_convert_type(b.astype(jnp.bfloat16).astype(jnp.float32), jnp.uint32)
    return (ab >> 16) | (bb & jnp.uint32(0xFFFF0000))


def _unpack_bf16_pair(w):
    a = lax.bitcast_convert_type(w << 16, jnp.float32).astype(jnp.bfloat16)
    b = lax.bitcast_convert_type(w & jnp.uint32(0xFFFF0000), jnp.float32).astype(jnp.bfloat16)
    return a, b


def _mixer_kernel(x_ref, kh_ref, vh_ref, uh_ref, cnt_in_ref,
                  g1_ref, wukv_ref, wqvt_ref, qg_ref, kg_ref, bd_ref, bias_ref, sink_ref,
                  wpool_ref, pscale_ref, wout_ref, g2_ref, wrh_ref, wrl_ref, br_ref, tri_ref,
                  x1_ref, hp_ref, idx_ref, rank_ref, gate_ref, cnt_ref, ko_ref, vo_ref, uo_ref,
                  qt_s, kb_s, vt_s, ub_s, mix_s, cnt_s,
                  *, tile, n_valid, pos0, mask_first):
    b = pl.program_id(0)
    s = pl.program_id(1)
    bf16 = jnp.bfloat16
    f32 = jnp.float32

    @pl.when((b == 0) & (s == 0))
    def _():
        cnt_s[...] = cnt_in_ref[...]

    @pl.when(s == 0)
    def _():
        kb_s[0:WINDOW, :] = kh_ref[...].astype(bf16)
        vt_s[:, 0:WINDOW] = jnp.transpose(vh_ref[...]).astype(bf16)
        ub_s[0:HIST_ROWS, :] = uh_ref[...]

    @pl.when(s > 0)
    def _():
        kb_s[0:WINDOW, :] = kb_s[tile:tile + WINDOW, :]
        vt_s[:, 0:WINDOW] = vt_s[:, tile:tile + WINDOW]
        ub_s[0:HIST_ROWS, :] = ub_s[tile:tile + HIST_ROWS, :]

    x = x_ref[...]
    xn = (x * lax.rsqrt(jnp.mean(x * x, axis=-1, keepdims=True) + EPS) * g1_ref[...]).astype(bf16)
    z = jnp.dot(xn, wukv_ref[...], preferred_element_type=f32)
    zt = lax.dot_general(wqvt_ref[...], xn, (((1,), (1,)), ((), ())),
                         preferred_element_type=f32)
    u = z[:, 0:POOL_WIDTH]
    kz = z[:, POOL_WIDTH:POOL_WIDTH + KV_WIDTH]
    v = z[:, POOL_WIDTH + KV_WIDTH:]

    ksq = kz * kz
    kss = jnp.dot(ksq.astype(bf16), bd_ref[...], preferred_element_type=f32)
    kn = kz * lax.rsqrt(kss * (1.0 / HEAD_DIM) + EPS) * kg_ref[...]
    kb_s[WINDOW:, :] = kn.astype(bf16)
    vt_s[:, WINDOW:] = zt[ATTN_WIDTH:, :].astype(bf16)
    ub_s[HIST_ROWS:, :] = u

    row0 = max(n_valid, WINDOW) - WINDOW
    ko_ref[...] = kn[row0:row0 + WINDOW, :]
    vo_ref[...] = v[row0:row0 + WINDOW, :]
    uo_ref[...] = u[n_valid - HIST_ROWS:n_valid, :]

    for hd in range(N_HEADS):
        qh = zt[hd * HEAD_DIM:(hd + 1) * HEAD_DIM, :]
        ss = jnp.sum(qh * qh, axis=0, keepdims=True)
        qn = qh * (lax.rsqrt(ss * (1.0 / HEAD_DIM) + EPS) * ATTN_SCALE) * qg_ref[...]
        qt_s[hd * HEAD_DIM:(hd + 1) * HEAD_DIM, :] = qn.astype(bf16)

    pos = pos0 + s * tile + lax.broadcasted_iota(jnp.int32, (tile, 1), 0)
    for g, w in enumerate(POOL_WINDOWS):
        e = ub_s[:, g * POOL_GROUP:(g + 1) * POOL_GROUP]
        acc = e
        for lvl in range(g + 1):
            acc = acc + pltpu.roll(acc, 2 ** lvl, axis=0)
        inv_cnt = 1.0 / jnp.minimum(pos + 1, w).astype(f32)
        d = (acc[HIST_ROWS:, :] * inv_cnt - e[HIST_ROWS:, :]).astype(bf16)
        y = jnp.dot(d, wpool_ref[g], preferred_element_type=f32)
        y = y * pscale_ref[:, g * POOL_GROUP:(g + 1) * POOL_GROUP]
        mix_s[:, g * POOL_GROUP:(g + 1) * POOL_GROUP] = y.astype(bf16)

    zeros_q = jnp.zeros((HEAD_DIM, GROUP * PAIR), bf16)
    for p in range(tile // PAIR):
        k_band = kb_s[p * PAIR:p * PAIR + BAND, :]
        o_parts = []
        for h in range(N_KV_HEADS):
            qcat = jnp.concatenate(
                [qt_s[(h * GROUP + g) * HEAD_DIM:(h * GROUP + g + 1) * HEAD_DIM,
                      p * PAIR:(p + 1) * PAIR] for g in range(GROUP)], axis=1)
            rhs = jnp.concatenate([qcat, zeros_q] if h == 0 else [zeros_q, qcat], axis=0)
            st = jnp.dot(k_band, rhs, preferred_element_type=f32) + bias_ref[h]
            if mask_first and p == 0:
                krow = lax.broadcasted_iota(jnp.int32, (BAND, 1), 0)
                st = jnp.where((krow >= WINDOW) | (s > 0), st, NEG_INF)
            sink = sink_ref[h:h + 1, :]
            m = jnp.maximum(jnp.max(st, axis=0, keepdims=True), sink)
            ex = jnp.exp(st - m)
            den = jnp.sum(ex, axis=0, keepdims=True) + jnp.exp(sink - m)
            v_band = vt_s[h * HEAD_DIM:(h + 1) * HEAD_DIM, p * PAIR:p * PAIR + BAND]
            ot = jnp.dot(v_band, ex.astype(bf16), preferred_element_type=f32) / den
            for g in range(GROUP):
                o_parts.append(ot[:, g * PAIR:(g + 1) * PAIR])
        o_all = jnp.concatenate(o_parts, axis=0)
        mix_s[p * PAIR:(p + 1) * PAIR, POOL_WIDTH:] = jnp.transpose(o_all).astype(bf16)

    x1 = x + jnp.dot(mix_s[...], wout_ref[...], preferred_element_type=f32)
    x1_ref[...] = x1

    hn = x1 * lax.rsqrt(jnp.mean(x1 * x1, axis=-1, keepdims=True) + EPS) * g2_ref[...]
    hp_ref[...] = _pack_bf16_pair(hn[:, 0:D_MODEL // 2], hn[:, D_MODEL // 2:])
    h_hi = hn.astype(bf16)
    h_lo = (hn - h_hi.astype(f32)).astype(bf16)
    logits = (jnp.dot(h_hi, wrh_ref[...], preferred_element_type=f32)
              + jnp.dot(h_lo, wrh_ref[...], preferred_element_type=f32)
              + jnp.dot(h_hi, wrl_ref[...], preferred_element_type=f32))
    lt = jnp.transpose(logits)[0:N_EXPERTS, :] + br_ref[...]

    eidx = lax.broadcasted_iota(jnp.int32, (N_EXPERTS, tile), 0).astype(f32)
    vals, hots = [], []
    for j in range(TOP_K):
        m = jnp.max(lt, axis=0, keepdims=True)
        sel = jnp.min(jnp.where(lt == m, eidx, float(N_EXPERTS)), axis=0, keepdims=True)
        hot = eidx == sel
        lt = jnp.where(hot, -jnp.inf, lt)
        idx_ref[j:j + 1, :] = sel.astype(jnp.int32)
        vals.append(m)
        hots.append(hot)
    exps = [jnp.exp(vv - vals[0]) for vv in vals]
    esum = exps[0] + exps[1] + exps[2] + exps[3]
    for j in range(TOP_K):
        gate_ref[j:j + 1, :] = exps[j] / esum

    chosen_f = sum(jnp.where(hot, 1.0, 0.0) for hot in hots)
    if n_valid < tile:
        lane = lax.broadcasted_iota(jnp.int32, (N_EXPERTS, tile), 1)
        chosen_f = jnp.where(lane < n_valid, chosen_f, 0.0)
    before = jnp.dot(chosen_f.astype(bf16), tri_ref[...], preferred_element_type=f32)
    base = before + cnt_s[:, 0:1]
    for j in range(TOP_K):
        rank_ref[j:j + 1, :] = jnp.sum(jnp.where(hots[j], base, 0.0), axis=0,
                                       keepdims=True).astype(jnp.int32)
    cnt_new = cnt_s[...] + jnp.sum(chosen_f, axis=1, keepdims=True)
    cnt_s[...] = cnt_new
    cnt_ref[...] = cnt_new


def _mixer_call(x, k_hist, v_hist, u_hist, cnt_in, consts, *, tile, n_valid, pos0, mask_first):
    nb, seq, _ = x.shape
    n_tiles = seq // tile
    f32 = jnp.float32

    def full(a):
        nd = a.ndim
        return pl.BlockSpec(a.shape, lambda b, s, _nd=nd: (0,) * _nd)

    in_specs = [
        pl.BlockSpec((None, tile, D_MODEL), lambda b, s: (b, s, 0)),
        pl.BlockSpec((None, WINDOW, KV_WIDTH), lambda b, s: (b, 0, 0)),
        pl.BlockSpec((None, WINDOW, KV_WIDTH), lambda b, s: (b, 0, 0)),
        pl.BlockSpec((None, HIST_ROWS, POOL_WIDTH), lambda b, s: (b, 0, 0)),
        full(cnt_in),
    ] + [full(c) for c in consts]
    out_shape = [
        jax.ShapeDtypeStruct((nb, seq, D_MODEL), f32),
        jax.ShapeDtypeStruct((nb, seq, D_MODEL // 2), jnp.uint32),
        jax.ShapeDtypeStruct((nb, TOP_K, seq), jnp.int32),
        jax.ShapeDtypeStruct((nb, TOP_K, seq), jnp.int32),
        jax.ShapeDtypeStruct((nb, TOP_K, seq), f32),
        jax.ShapeDtypeStruct((N_EXPERTS, 128), f32),
        jax.ShapeDtypeStruct((nb, WINDOW, KV_WIDTH), f32),
        jax.ShapeDtypeStruct((nb, WINDOW, KV_WIDTH), f32),
        jax.ShapeDtypeStruct((nb, HIST_ROWS, POOL_WIDTH), f32),
    ]
    out_specs = [
        pl.BlockSpec((None, tile, D_MODEL), lambda b, s: (b, s, 0)),
        pl.BlockSpec((None, tile, D_MODEL // 2), lambda b, s: (b, s, 0)),
        pl.BlockSpec((None, TOP_K, tile), lambda b, s: (b, 0, s)),
        pl.BlockSpec((None, TOP_K, tile), lambda b, s: (b, 0, s)),
        pl.BlockSpec((None, TOP_K, tile), lambda b, s: (b, 0, s)),
        pl.BlockSpec((N_EXPERTS, 128), lambda b, s: (0, 0)),
        pl.BlockSpec((None, WINDOW, KV_WIDTH), lambda b, s: (b, 0, 0)),
        pl.BlockSpec((None, WINDOW, KV_WIDTH), lambda b, s: (b, 0, 0)),
        pl.BlockSpec((None, HIST_ROWS, POOL_WIDTH), lambda b, s: (b, 0, 0)),
    ]
    scratch = [
        pltpu.VMEM((ATTN_WIDTH, tile), jnp.bfloat16),
        pltpu.VMEM((WINDOW + tile, KV_WIDTH), jnp.bfloat16),
        pltpu.VMEM((KV_WIDTH, WINDOW + tile), jnp.bfloat16),
        pltpu.VMEM((HIST_ROWS + tile, POOL_WIDTH), f32),
        pltpu.VMEM((tile, D_MODEL), jnp.bfloat16),
        pltpu.VMEM((N_EXPERTS, 128), f32),
    ]
    kern = functools.partial(_mixer_kernel, tile=tile, n_valid=n_valid, pos0=pos0,
                             mask_first=mask_first)
    return pl.pallas_call(
        kern,
        grid=(nb, n_tiles),
        in_specs=in_specs,
        out_specs=out_specs,
        out_shape=out_shape,
        scratch_shapes=scratch,
        compiler_params=pltpu.CompilerParams(
            dimension_semantics=("arbitrary", "arbitrary"),
            vmem_limit_bytes=VMEM_LIMIT),
        name="mixer",
    )(x, k_hist, v_hist, u_hist, cnt_in, *consts)


def _expert_kernel(be_ref, nu_ref, xs_ref, wg_ref, bg_ref, wu_ref, bu_ref, wd_ref, bd_ref, ys_ref):
    i = pl.program_id(0)

    @pl.when(i < nu_ref[0])
    def _():
        f32 = jnp.float32
        half = D_MODEL // 2
        xa, xb = _unpack_bf16_pair(xs_ref[...])
        a = (jnp.dot(xa, wg_ref[0:half, :], preferred_element_type=f32)
             + jnp.dot(xb, wg_ref[half:, :], preferred_element_type=f32) + bg_ref[...])
        bb = (jnp.dot(xa, wu_ref[0:half, :], preferred_element_type=f32)
              + jnp.dot(xb, wu_ref[half:, :], preferred_element_type=f32) + bu_ref[...])
        a = jnp.minimum(a, SWIGLU_LIMIT)
        bb = jnp.clip(bb, -SWIGLU_LIMIT, SWIGLU_LIMIT)
        act = a * (1.0 / (1.0 + jnp.exp(-SWIGLU_ALPHA * a))) * (bb + 1.0)
        ys_ref[...] = (jnp.dot(act.astype(jnp.bfloat16), wd_ref[...], preferred_element_type=f32)
                       + bd_ref[...])

    @pl.when(i >= nu_ref[0])
    def _():
        ys_ref[...] = jnp.zeros_like(ys_ref)


def _expert_call(block_e, n_used, xs, wg, bg, wu, bu, wd, bd):
    n_slots = xs.shape[0]
    n_blocks = n_slots // EXPERT_BLOCK
    w_spec = pl.BlockSpec((None, D_MODEL, D_MODEL), lambda i, be, nu: (be[i], 0, 0))
    b_spec = pl.BlockSpec((None, 1, D_MODEL), lambda i, be, nu: (be[i], 0, 0))
    grid_spec = pltpu.PrefetchScalarGridSpec(
        num_scalar_prefetch=2,
        grid=(n_blocks,),
        in_specs=[pl.BlockSpec((EXPERT_BLOCK, D_MODEL // 2), lambda i, be, nu: (i, 0)),
                  w_spec, b_spec, w_spec, b_spec, w_spec, b_spec],
        out_specs=pl.BlockSpec((EXPERT_BLOCK, D_MODEL), lambda i, be, nu: (i, 0)),
    )
    return pl.pallas_call(
        _expert_kernel,
        grid_spec=grid_spec,
        out_shape=jax.ShapeDtypeStruct((n_slots, D_MODEL), jnp.float32),
        compiler_params=pltpu.CompilerParams(
            dimension_semantics=("arbitrary",),
            vmem_limit_bytes=VMEM_LIMIT),
        name="experts",
    )(block_e, n_used, xs, wg, bg, wu, bu, wd, bd)


def _combine_kernel(x1_ref, g_ref, y0_ref, y1_ref, y2_ref, y3_ref, o_ref):
    g = g_ref[...]
    o_ref[...] = (x1_ref[...] + g[:, 0:1] * y0_ref[...] + g[:, 1:2] * y1_ref[...]
                  + g[:, 2:3] * y2_ref[...] + g[:, 3:4] * y3_ref[...])


def _combine_call(x1, gates, picked, n_tok, tok0, tile):
    n = x1.shape[0]
    base = [(j * n_tok + tok0) // tile for j in range(TOP_K)]
    y_specs = [pl.BlockSpec((tile, D_MODEL), lambda i, _o=o: (_o + i, 0)) for o in base]
    return pl.pallas_call(
        _combine_kernel,
        grid=(n // tile,),
        in_specs=[pl.BlockSpec((tile, D_MODEL), lambda i: (i, 0)),
                  pl.BlockSpec((tile, TOP_K), lambda i: (i, 0))] + y_specs,
        out_specs=pl.BlockSpec((tile, D_MODEL), lambda i: (i, 0)),
        out_shape=jax.ShapeDtypeStruct((n, D_MODEL), jnp.float32),
        compiler_params=pltpu.CompilerParams(dimension_semantics=("arbitrary",)),
        name="combine",
    )(x1, gates, picked, picked, picked, picked)


def _gather_rows(table, idx):
    n = idx.shape[0]
    width = table.shape[1]
    per_worker = n // SC_WORKERS
    n_chunks = per_worker // SC_CHUNK
    mesh = plsc.VectorSubcoreMesh(core_axis_name="c", subcore_axis_name="s")

    @functools.partial(
        pl.kernel, mesh=mesh,
        out_type=jax.ShapeDtypeStruct((n, width), table.dtype),
        scratch_types=[pltpu.VMEM((SC_CHUNK,), jnp.int32),
                       pltpu.VMEM((SC_CHUNK, width), table.dtype),
                       pltpu.SemaphoreType.DMA],
    )
    def gather(table_hbm, idx_hbm, out_hbm, idx_v, rows_v, sem):
        wid = lax.axis_index("s") * 2 + lax.axis_index("c")
        base = wid * per_worker

        @pl.loop(0, n_chunks)
        def _(i):
            off = base + i * SC_CHUNK
            pltpu.sync_copy(idx_hbm.at[pl.ds(off, SC_CHUNK)], idx_v)
            pltpu.async_copy(table_hbm.at[idx_v], rows_v, sem).wait()
            pltpu.sync_copy(rows_v, out_hbm.at[pl.ds(off, SC_CHUNK)])

    return gather(table, idx)


def _t5_bucket_np(rel):
    half = NUM_BUCKETS // 2
    max_exact = half // 2
    n = np.abs(rel)
    nf = np.maximum(n, 1).astype(np.float32)
    large = max_exact + (np.log(nf / max_exact) / math.log(MAX_DISTANCE / max_exact)
                         * (half - max_exact)).astype(np.int32)
    large = np.minimum(large, half - 1)
    return np.where(rel > 0, half, 0) + np.where(n < max_exact, n, large)


def _bias_tables(rel_bias, visible):
    kap = np.arange(BAND)[:, None]
    rho = np.arange(PAIR)[None, :]
    bucket = _t5_bucket_np(kap - WINDOW - rho)
    tab = rel_bias.astype(jnp.float32)[bucket]
    tab = jnp.where(jnp.asarray(visible)[:, :, None], tab, NEG_INF)
    tab = jnp.transpose(tab, (2, 0, 1)).reshape(N_KV_HEADS, GROUP, BAND, PAIR)
    return jnp.transpose(tab, (0, 2, 1, 3)).reshape(N_KV_HEADS, BAND, GROUP * PAIR)


def _mixer_consts(l, norm1_g, w_in, q_norm_g, k_norm_g, rel_bias, sinks, w_pool, pool_scale,
                  w_out, norm2_g, w_router, b_router, visible, tile):
    f32, bf16 = jnp.float32, jnp.bfloat16
    q_off, k_off, v_off = POOL_WIDTH, POOL_WIDTH + ATTN_WIDTH, POOL_WIDTH + ATTN_WIDTH + KV_WIDTH
    w = w_in[l]
    w_ukv = jnp.concatenate([w[:, :q_off], w[:, k_off:]], axis=1).astype(bf16)
    w_qvt = jnp.transpose(jnp.concatenate([w[:, q_off:k_off], w[:, v_off:]], axis=1)).astype(bf16)
    lane_head = np.arange(KV_WIDTH) // HEAD_DIM
    blockdiag = jnp.asarray(lane_head[:, None] == lane_head[None, :], bf16)
    sink_rows = jnp.repeat(sinks[l].astype(f32).reshape(N_KV_HEADS, GROUP), PAIR, axis=1)
    wr = jnp.pad(w_router[l].astype(f32), ((0, 0), (0, 128 - N_EXPERTS)))
    wr_hi = wr.astype(bf16)
    wr_lo = (wr - wr_hi.astype(f32)).astype(bf16)
    tri = jnp.asarray(np.arange(tile)[:, None] < np.arange(tile)[None, :], bf16)
    return [
        norm1_g[l].reshape(1, D_MODEL).astype(f32), w_ukv, w_qvt,
        q_norm_g[l].reshape(HEAD_DIM, 1).astype(f32),
        jnp.tile(k_norm_g[l].astype(f32), N_KV_HEADS).reshape(1, KV_WIDTH),
        blockdiag, _bias_tables(rel_bias, visible), sink_rows,
        w_pool[l].astype(bf16), pool_scale[l].reshape(1, POOL_WIDTH).astype(f32),
        w_out[l].astype(bf16), norm2_g[l].reshape(1, D_MODEL).astype(f32),
        wr_hi, wr_lo, b_router[l].reshape(N_EXPERTS, 1).astype(f32), tri,
    ]


def _visibility():
    kap = np.arange(BAND)[:, None]
    rho = np.arange(PAIR)[None, :]
    kc, qc = kap // CHUNK, rho // CHUNK
    prompt = (kc >= qc) & (kc <= qc + WINDOW // CHUNK)
    return prompt


def kernel(x_prompt, x_sample, cache_k, cache_v, state_pool, norm1_g, w_in, q_norm_g, k_norm_g,
           rel_bias, sinks, w_pool, pool_scale, w_out, norm2_g, w_router, b_router,
           w_gate, b_gate, w_up, b_up, w_down, b_down):
    f32, bf16 = jnp.float32, jnp.bfloat16
    depth = w_in.shape[0]
    nb, seq, _ = x_prompt.shape
    ndb, dseq, _ = x_sample.shape
    cache_len = cache_k.shape[2]
    assert seq % MIX_TILE == 0 and cache_len == WINDOW and HIST_ROWS <= dseq <= PAIR
    n_p, n_s = nb * seq, ndb * dseq
    n_tok = n_p + n_s
    assert n_p % COMBINE_TILE == 0 and n_tok % COMBINE_TILE == 0 and n_s % 8 == 0

    vis_prompt = _visibility()
    vis_sample = np.broadcast_to(np.arange(BAND)[:, None] < WINDOW + dseq, (BAND, PAIR))

    n_assign = n_tok * TOP_K
    gather_quant = SC_WORKERS * SC_CHUNK
    n_blocks = -(-(n_assign + N_EXPERTS * (EXPERT_BLOCK - 1)) // EXPERT_BLOCK)
    n_blocks = -(-n_blocks // (gather_quant // EXPERT_BLOCK)) * (gather_quant // EXPERT_BLOCK)
    n_slots = n_blocks * EXPERT_BLOCK
    n_pick = -(-n_assign // gather_quant) * gather_quant

    xp, xs = x_prompt, x_sample
    outs = [[] for _ in range(6)]
    for l in range(depth):
        wl = (l, norm1_g, w_in, q_norm_g, k_norm_g, rel_bias, sinks, w_pool, pool_scale, w_out,
              norm2_g, w_router, b_router)
        zk = jnp.zeros((nb, WINDOW, KV_WIDTH), f32)
        zu = jnp.zeros((nb, HIST_ROWS, POOL_WIDTH), f32)
        cnt0 = jnp.zeros((N_EXPERTS, 128), f32)
        (xp1, hp_p, idx_p, rank_p, gate_p, cnt_p, k_p, v_p, u_p) = _mixer_call(
            xp, zk, zk, zu, cnt0, _mixer_consts(*wl, vis_prompt, MIX_TILE),
            tile=MIX_TILE, n_valid=MIX_TILE, pos0=0, mask_first=True)
        xs_pad = jnp.pad(xs, ((0, 0), (0, PAIR - dseq), (0, 0)))
        uh = jnp.pad(state_pool[l], ((0, 0), (HIST_ROWS - POOL_HIST, 0), (0, 0)))
        (xs1, hp_s, idx_s, rank_s, gate_s, cnt_all, k_s, v_s, u_s) = _mixer_call(
            xs_pad, cache_k[l].reshape(ndb, WINDOW, KV_WIDTH),
            cache_v[l].reshape(ndb, WINDOW, KV_WIDTH), uh, cnt_p,
            _mixer_consts(*wl, vis_sample, PAIR),
            tile=PAIR, n_valid=dseq, pos0=PAST_LEN, mask_first=False)

        counts = cnt_all[:, 0].astype(jnp.int32)
        pcounts = (counts + EXPERT_BLOCK - 1) // EXPERT_BLOCK * EXPERT_BLOCK
        pend = jnp.cumsum(pcounts)
        pstart = pend - pcounts
        idx_all = jnp.concatenate([jnp.transpose(idx_p, (1, 0, 2)).reshape(TOP_K, n_p),
                                   jnp.transpose(idx_s[:, :, :dseq], (1, 0, 2)).reshape(TOP_K, n_s)], axis=1)
        rank_all = jnp.concatenate([jnp.transpose(rank_p, (1, 0, 2)).reshape(TOP_K, n_p),
                                    jnp.transpose(rank_s[:, :, :dseq], (1, 0, 2)).reshape(TOP_K, n_s)], axis=1)
        gate_all = jnp.concatenate([jnp.transpose(gate_p, (1, 0, 2)).reshape(TOP_K, n_p),
                                    jnp.transpose(gate_s[:, :, :dseq], (1, 0, 2)).reshape(TOP_K, n_s)], axis=1)
        dest = pstart[idx_all] + rank_all
        hrow_s = n_p + (jnp.arange(ndb)[:, None] * PAIR + jnp.arange(dseq)[None, :]).reshape(-1)
        hrow = jnp.concatenate([jnp.arange(n_p), hrow_s]).astype(jnp.int32)
        src_row = jnp.zeros((n_slots,), jnp.int32).at[dest.reshape(-1)].set(jnp.tile(hrow, TOP_K))
        block_e = jnp.minimum(
            jnp.searchsorted(pend, jnp.arange(n_blocks) * EXPERT_BLOCK, side='right'),
            N_EXPERTS - 1).astype(jnp.int32)
        n_used = (pend[-1] // EXPERT_BLOCK).astype(jnp.int32).reshape(1)

        h_all = jnp.concatenate([hp_p.reshape(n_p, D_MODEL // 2),
                                 hp_s.reshape(ndb * PAIR, D_MODEL // 2)], axis=0)
        x_sorted = _gather_rows(h_all, src_row)
        y_sorted = _expert_call(
            block_e, n_used, x_sorted,
            w_gate[l].astype(bf16), b_gate[l].reshape(N_EXPERTS, 1, D_MODEL).astype(f32),
            w_up[l].astype(bf16), b_up[l].reshape(N_EXPERTS, 1, D_MODEL).astype(f32),
            w_down[l].astype(bf16), b_down[l].reshape(N_EXPERTS, 1, D_MODEL).astype(f32))
        pick = jnp.pad(dest.reshape(-1), (0, n_pick - n_assign))
        picked = _gather_rows(y_sorted, pick)
        gates_t = jnp.transpose(gate_all)
        xp = _combine_call(xp1.reshape(n_p, D_MODEL), gates_t[:n_p], picked, n_tok, 0,
                           COMBINE_TILE).reshape(nb, seq, D_MODEL)
        xs1_rows = xs1[:, :dseq].reshape(n_s, D_MODEL)
        xs = _combine_call(xs1_rows, gates_t[n_p:], picked, n_tok, n_p,
                           COMBINE_TILE).reshape(ndb, dseq, D_MODEL)

        outs[0].append(k_p.reshape(nb, WINDOW, N_KV_HEADS, HEAD_DIM))
        outs[1].append(v_p.reshape(nb, WINDOW, N_KV_HEADS, HEAD_DIM))
        outs[2].append(u_p[:, HIST_ROWS - POOL_HIST:])
        outs[3].append(k_s[:, :dseq].reshape(ndb, dseq, N_KV_HEADS, HEAD_DIM))
        outs[4].append(v_s[:, :dseq].reshape(ndb, dseq, N_KV_HEADS, HEAD_DIM))
        outs[5].append(u_s[:, HIST_ROWS - POOL_HIST:])
    return (xp, xs) + tuple(jnp.stack(o) for o in outs)
```

```python
import functools
import math

import numpy as np
import jax
import jax.numpy as jnp
from jax import lax
from jax.experimental import pallas as pl
from jax.experimental.pallas import tpu as pltpu
from jax.experimental.pallas import tpu_sc as plsc

D_MODEL = 1024
CHUNK = 64
POOL_WIDTH = 512
POOL_WINDOWS = (2, 4, 8, 16)
POOL_GROUP = 128
POOL_HIST = 15
ATTN_WIDTH = 512
HEAD_DIM = 64
N_HEADS = 8
N_KV_HEADS = 2
GROUP = 4
KV_WIDTH = 128
WINDOW = 128
NUM_BUCKETS = 32
MAX_DISTANCE = 128
PAST_LEN = 2048
N_EXPERTS = 32
TOP_K = 4
SWIGLU_LIMIT = 7.0
SWIGLU_ALPHA = 1.702
EPS = 1e-5
NEG_INF = -1e30
ATTN_SCALE = HEAD_DIM ** -0.5

PAIR = 2 * CHUNK
BAND = PAIR + WINDOW
HIST_ROWS = 16
MIX_TILE = 512
EXPERT_BLOCK = 512
COMBINE_TILE = 256
SC_WORKERS = 32
SC_CHUNK = 64
VMEM_LIMIT = 56 * 1024 * 1024


def _pack_bf16_pair(a, b):
    ab = lax.bitcast_convert_type(a.astype(jnp.bfloat16).astype(jnp.float32), jnp.uint32)
    bb = lax.bitcast_convert_type(b.astype(jnp.bfloat16).astype(jnp.float32), jnp.uint32)
    return (ab >> 16) | (bb & jnp.uint32(0xFFFF0000))


def _unpack_bf16_pair(w):
    a = lax.bitcast_convert_type(w << 16, jnp.float32).astype(jnp.bfloat16)
    b = lax.bitcast_convert_type(w & jnp.uint32(0xFFFF0000), jnp.float32).astype(jnp.bfloat16)
    return a, b


def _mixer_kernel(hbuf_ref, x_ref, kh_ref, vh_ref, uh_ref, cnt_in_ref,
                  g1_ref, wukv_ref, wqvt_ref, qg_ref, kg_ref, bd_ref, bias_ref, sink_ref,
                  wpool_ref, pscale_ref, wout_ref, g2_ref, wrh_ref, wrl_ref, br_ref, tri_ref,
                  x1_ref, hp_ref, idx_ref, rank_ref, gate_ref, cnt_ref, ko_ref, vo_ref, uo_ref,
                  qt_s, kb_s, vt_s, ub_s, mix_s, cnt_s,
                  *, tile, n_valid, pos0, mask_first):
    del hbuf_ref
    b = pl.program_id(0)
    s = pl.program_id(1)
    bf16 = jnp.bfloat16
    f32 = jnp.float32

    @pl.when((b == 0) & (s == 0))
    def _():
        cnt_s[...] = cnt_in_ref[...]

    @pl.when(s == 0)
    def _():
        kb_s[0:WINDOW, :] = kh_ref[...].astype(bf16)
        vt_s[:, 0:WINDOW] = jnp.transpose(vh_ref[...]).astype(bf16)
        ub_s[0:HIST_ROWS, :] = uh_ref[...]

    @pl.when(s > 0)
    def _():
        kb_s[0:WINDOW, :] = kb_s[tile:tile + WINDOW, :]
        vt_s[:, 0:WINDOW] = vt_s[:, tile:tile + WINDOW]
        ub_s[0:HIST_ROWS, :] = ub_s[tile:tile + HIST_ROWS, :]

    x = x_ref[...]
    xn = (x * lax.rsqrt(jnp.mean(x * x, axis=-1, keepdims=True) + EPS) * g1_ref[...]).astype(bf16)
    z = jnp.dot(xn, wukv_ref[...], preferred_element_type=f32)
    zt = lax.dot_general(wqvt_ref[...], xn, (((1,), (1,)), ((), ())),
                         preferred_element_type=f32)
    u = z[:, 0:POOL_WIDTH]
    kz = z[:, POOL_WIDTH:POOL_WIDTH + KV_WIDTH]
    v = z[:, POOL_WIDTH + KV_WIDTH:]

    ksq = kz * kz
    kss = jnp.dot(ksq.astype(bf16), bd_ref[...], preferred_element_type=f32)
    kn = kz * lax.rsqrt(kss * (1.0 / HEAD_DIM) + EPS) * kg_ref[...]
    kb_s[WINDOW:, :] = kn.astype(bf16)
    vt_s[:, WINDOW:] = zt[ATTN_WIDTH:, :].astype(bf16)
    ub_s[HIST_ROWS:, :] = u

    row0 = max(n_valid, WINDOW) - WINDOW
    ko_ref[...] = kn[row0:row0 + WINDOW, :]
    vo_ref[...] = v[row0:row0 + WINDOW, :]
    uo_ref[...] = u[n_valid - HIST_ROWS:n_valid, :]

    for hd in range(N_HEADS):
        qh = zt[hd * HEAD_DIM:(hd + 1) * HEAD_DIM, :]
        ss = jnp.sum(qh * qh, axis=0, keepdims=True)
        qn = qh * (lax.rsqrt(ss * (1.0 / HEAD_DIM) + EPS) * ATTN_SCALE) * qg_ref[...]
        qt_s[hd * HEAD_DIM:(hd + 1) * HEAD_DIM, :] = qn.astype(bf16)

    pos = pos0 + s * tile + lax.broadcasted_iota(jnp.int32, (tile, 1), 0)
    for g, w in enumerate(POOL_WINDOWS):
        e = ub_s[:, g * POOL_GROUP:(g + 1) * POOL_GROUP]
        acc = e
        for lvl in range(g + 1):
            acc = acc + pltpu.roll(acc, 2 ** lvl, axis=0)
        inv_cnt = 1.0 / jnp.minimum(pos + 1, w).astype(f32)
        d = (acc[HIST_ROWS:, :] * inv_cnt - e[HIST_ROWS:, :]).astype(bf16)
        y = jnp.dot(d, wpool_ref[g], preferred_element_type=f32)
        y = y * pscale_ref[:, g * POOL_GROUP:(g + 1) * POOL_GROUP]
        mix_s[:, g * POOL_GROUP:(g + 1) * POOL_GROUP] = y.astype(bf16)

    zeros_q = jnp.zeros((HEAD_DIM, GROUP * PAIR), bf16)
    for p in range(tile // PAIR):
        k_band = kb_s[p * PAIR:p * PAIR + BAND, :]
        o_parts = []
        for h in range(N_KV_HEADS):
            qcat = jnp.concatenate(
                [qt_s[(h * GROUP + g) * HEAD_DIM:(h * GROUP + g + 1) * HEAD_DIM,
                      p * PAIR:(p + 1) * PAIR] for g in range(GROUP)], axis=1)
            rhs = jnp.concatenate([qcat, zeros_q] if h == 0 else [zeros_q, qcat], axis=0)
            st = jnp.dot(k_band, rhs, preferred_element_type=f32) + bias_ref[h]
            if mask_first and p == 0:
                krow = lax.broadcasted_iota(jnp.int32, (BAND, 1), 0)
                st = jnp.where((krow >= WINDOW) | (s > 0), st, NEG_INF)
            sink = sink_ref[h:h + 1, :]
            m = jnp.maximum(jnp.max(st, axis=0, keepdims=True), sink)
            ex = jnp.exp(st - m)
            den = jnp.sum(ex, axis=0, keepdims=True) + jnp.exp(sink - m)
            v_band = vt_s[h * HEAD_DIM:(h + 1) * HEAD_DIM, p * PAIR:p * PAIR + BAND]
            ot = jnp.dot(v_band, ex.astype(bf16), preferred_element_type=f32) / den
            for g in range(GROUP):
                o_parts.append(ot[:, g * PAIR:(g + 1) * PAIR])
        o_all = jnp.concatenate(o_parts, axis=0)
        mix_s[p * PAIR:(p + 1) * PAIR, POOL_WIDTH:] = jnp.transpose(o_all).astype(bf16)

    x1 = x + jnp.dot(mix_s[...], wout_ref[...], preferred_element_type=f32)
    x1_ref[...] = x1

    hn = x1 * lax.rsqrt(jnp.mean(x1 * x1, axis=-1, keepdims=True) + EPS) * g2_ref[...]
    hp_ref[...] = _pack_bf16_pair(hn[0:n_valid, 0:D_MODEL // 2], hn[0:n_valid, D_MODEL // 2:])
    h_hi = hn.astype(bf16)
    h_lo = (hn - h_hi.astype(f32)).astype(bf16)
    logits = (jnp.dot(h_hi, wrh_ref[...], preferred_element_type=f32)
              + jnp.dot(h_lo, wrh_ref[...], preferred_element_type=f32)
              + jnp.dot(h_hi, wrl_ref[...], preferred_element_type=f32))
    lt = jnp.transpose(logits)[0:N_EXPERTS, :] + br_ref[...]

    eidx = lax.broadcasted_iota(jnp.int32, (N_EXPERTS, tile), 0).astype(f32)
    vals, hots = [], []
    for j in range(TOP_K):
        m = jnp.max(lt, axis=0, keepdims=True)
        sel = jnp.min(jnp.where(lt == m, eidx, float(N_EXPERTS)), axis=0, keepdims=True)
        hot = eidx == sel
        lt = jnp.where(hot, -jnp.inf, lt)
        idx_ref[j:j + 1, :] = sel.astype(jnp.int32)
        vals.append(m)
        hots.append(hot)
    exps = [jnp.exp(vv - vals[0]) for vv in vals]
    esum = exps[0] + exps[1] + exps[2] + exps[3]
    for j in range(TOP_K):
        gate_ref[j:j + 1, :] = exps[j] / esum

    chosen_f = sum(jnp.where(hot, 1.0, 0.0) for hot in hots)
    if n_valid < tile:
        lane = lax.broadcasted_iota(jnp.int32, (N_EXPERTS, tile), 1)
        chosen_f = jnp.where(lane < n_valid, chosen_f, 0.0)
    before = jnp.dot(chosen_f.astype(bf16), tri_ref[...], preferred_element_type=f32)
    base = before + cnt_s[:, 0:1]
    for j in range(TOP_K):
        rank_ref[j:j + 1, :] = jnp.sum(jnp.where(hots[j], base, 0.0), axis=0,
                                       keepdims=True).astype(jnp.int32)
    cnt_new = cnt_s[...] + jnp.sum(chosen_f, axis=1, keepdims=True)
    cnt_s[...] = cnt_new
    cnt_ref[...] = cnt_new


def _mixer_call(h_buf, x, k_hist, v_hist, u_hist, cnt_in, consts, *, tile, n_valid, pos0,
                mask_first, h_rows, h_row0):
    nb, seq, _ = x.shape
    n_tiles = seq // tile
    f32 = jnp.float32
    aliased = h_buf.shape == (h_rows, D_MODEL // 2)
    hblk0 = h_row0 // n_valid
    assert h_row0 % n_valid == 0 and n_valid % 8 == 0

    def full(a):
        nd = a.ndim
        return pl.BlockSpec(a.shape, lambda b, s, _nd=nd: (0,) * _nd)

    in_specs = [
        pl.BlockSpec(memory_space=pl.ANY),
        pl.BlockSpec((None, tile, D_MODEL), lambda b, s: (b, s, 0)),
        pl.BlockSpec((None, WINDOW, KV_WIDTH), lambda b, s: (b, 0, 0)),
        pl.BlockSpec((None, WINDOW, KV_WIDTH), lambda b, s: (b, 0, 0)),
        pl.BlockSpec((None, HIST_ROWS, POOL_WIDTH), lambda b, s: (b, 0, 0)),
        full(cnt_in),
    ] + [full(c) for c in consts]
    out_shape = [
        jax.ShapeDtypeStruct((nb, seq, D_MODEL), f32),
        jax.ShapeDtypeStruct((h_rows, D_MODEL // 2), jnp.uint32),
        jax.ShapeDtypeStruct((nb, TOP_K, seq), jnp.int32),
        jax.ShapeDtypeStruct((nb, TOP_K, seq), jnp.int32),
        jax.ShapeDtypeStruct((nb, TOP_K, seq), f32),
        jax.ShapeDtypeStruct((N_EXPERTS, 128), f32),
        jax.ShapeDtypeStruct((nb, WINDOW, KV_WIDTH), f32),
        jax.ShapeDtypeStruct((nb, WINDOW, KV_WIDTH), f32),
        jax.ShapeDtypeStruct((nb, HIST_ROWS, POOL_WIDTH), f32),
    ]
    out_specs = [
        pl.BlockSpec((None, tile, D_MODEL), lambda b, s: (b, s, 0)),
        pl.BlockSpec((n_valid, D_MODEL // 2), lambda b, s: (hblk0 + b * n_tiles + s, 0)),
        pl.BlockSpec((None, TOP_K, tile), lambda b, s: (b, 0, s)),
        pl.BlockSpec((None, TOP_K, tile), lambda b, s: (b, 0, s)),
        pl.BlockSpec((None, TOP_K, tile), lambda b, s: (b, 0, s)),
        pl.BlockSpec((N_EXPERTS, 128), lambda b, s: (0, 0)),
        pl.BlockSpec((None, WINDOW, KV_WIDTH), lambda b, s: (b, 0, 0)),
        pl.BlockSpec((None, WINDOW, KV_WIDTH), lambda b, s: (b, 0, 0)),
        pl.BlockSpec((None, HIST_ROWS, POOL_WIDTH), lambda b, s: (b, 0, 0)),
    ]
    scratch = [
        pltpu.VMEM((ATTN_WIDTH, tile), jnp.bfloat16),
        pltpu.VMEM((WINDOW + tile, KV_WIDTH), jnp.bfloat16),
        pltpu.VMEM((KV_WIDTH, WINDOW + tile), jnp.bfloat16),
        pltpu.VMEM((HIST_ROWS + tile, POOL_WIDTH), f32),
        pltpu.VMEM((tile, D_MODEL), jnp.bfloat16),
        pltpu.VMEM((N_EXPERTS, 128), f32),
    ]
    kern = functools.partial(_mixer_kernel, tile=tile, n_valid=n_valid, pos0=pos0,
                             mask_first=mask_first)
    return pl.pallas_call(
        kern,
        grid=(nb, n_tiles),
        in_specs=in_specs,
        out_specs=out_specs,
        out_shape=out_shape,
        scratch_shapes=scratch,
        input_output_aliases={0: 1} if aliased else {},
        compiler_params=pltpu.CompilerParams(
            dimension_semantics=("arbitrary", "arbitrary"),
            vmem_limit_bytes=VMEM_LIMIT),
        name="mixer",
    )(h_buf, x, k_hist, v_hist, u_hist, cnt_in, *consts)


def _expert_kernel(be_ref, nv_ref, xs_ref, wg_ref, bg_ref, wu_ref, bu_ref, wd_ref, bd_ref, ys_ref,
                   wg_s, wu_s, wd_s):
    i = pl.program_id(0)
    n_rows = nv_ref[i]
    bf16 = jnp.bfloat16

    @pl.when((n_rows > 0) & ((i == 0) | (be_ref[i] != be_ref[jnp.maximum(i - 1, 0)])))
    def _():
        wg_s[...] = wg_ref[...].astype(bf16)
        wu_s[...] = wu_ref[...].astype(bf16)
        wd_s[...] = wd_ref[...].astype(bf16)

    @pl.when(n_rows > 0)
    def _():
        f32 = jnp.float32
        half = D_MODEL // 2
        row = lax.broadcasted_iota(jnp.int32, (EXPERT_BLOCK, 1), 0)
        words = jnp.where(row < n_rows, xs_ref[...], jnp.uint32(0))
        xa, xb = _unpack_bf16_pair(words)
        a = (jnp.dot(xa, wg_s[0:half, :], preferred_element_type=f32)
             + jnp.dot(xb, wg_s[half:, :], preferred_element_type=f32) + bg_ref[...])
        bb = (jnp.dot(xa, wu_s[0:half, :], preferred_element_type=f32)
              + jnp.dot(xb, wu_s[half:, :], preferred_element_type=f32) + bu_ref[...])
        a = jnp.minimum(a, SWIGLU_LIMIT)
        bb = jnp.clip(bb, -SWIGLU_LIMIT, SWIGLU_LIMIT)
        act = a * (1.0 / (1.0 + jnp.exp(-SWIGLU_ALPHA * a))) * (bb + 1.0)
        ys_ref[...] = (jnp.dot(act.astype(bf16), wd_s[...], preferred_element_type=f32)
                       + bd_ref[...])

    @pl.when(n_rows == 0)
    def _():
        ys_ref[...] = jnp.zeros_like(ys_ref)


def _expert_call(block_e, block_rows, xs, wg, bg, wu, bu, wd, bd):
    n_slots = xs.shape[0]
    n_blocks = n_slots // EXPERT_BLOCK
    w_spec = pl.BlockSpec((None, D_MODEL, D_MODEL), lambda i, be, nv: (be[i], 0, 0))
    b_spec = pl.BlockSpec((None, 1, D_MODEL), lambda i, be, nv: (be[i], 0, 0))
    grid_spec = pltpu.PrefetchScalarGridSpec(
        num_scalar_prefetch=2,
        grid=(n_blocks,),
        in_specs=[pl.BlockSpec((EXPERT_BLOCK, D_MODEL // 2), lambda i, be, nv: (i, 0)),
                  w_spec, b_spec, w_spec, b_spec, w_spec, b_spec],
        out_specs=pl.BlockSpec((EXPERT_BLOCK, D_MODEL), lambda i, be, nv: (i, 0)),
        scratch_shapes=[pltpu.VMEM((D_MODEL, D_MODEL), jnp.bfloat16)] * 3,
    )
    return pl.pallas_call(
        _expert_kernel,
        grid_spec=grid_spec,
        out_shape=jax.ShapeDtypeStruct((n_slots, D_MODEL), jnp.float32),
        compiler_params=pltpu.CompilerParams(
            dimension_semantics=("arbitrary",),
            vmem_limit_bytes=VMEM_LIMIT),
        name="experts",
    )(block_e, block_rows, xs, wg, bg, wu, bu, wd, bd)


def _combine_kernel(x1_ref, g_ref, y0_ref, y1_ref, y2_ref, y3_ref, o_ref):
    g = g_ref[...]
    o_ref[...] = (x1_ref[...] + g[:, 0:1] * y0_ref[...] + g[:, 1:2] * y1_ref[...]
                  + g[:, 2:3] * y2_ref[...] + g[:, 3:4] * y3_ref[...])


def _combine_call(x1, gates, picked, n_tok, tok0, tile):
    n = x1.shape[0]
    base = [(j * n_tok + tok0) // tile for j in range(TOP_K)]
    y_specs = [pl.BlockSpec((tile, D_MODEL), lambda i, _o=o: (_o + i, 0)) for o in base]
    return pl.pallas_call(
        _combine_kernel,
        grid=(n // tile,),
        in_specs=[pl.BlockSpec((tile, D_MODEL), lambda i: (i, 0)),
                  pl.BlockSpec((tile, TOP_K), lambda i: (i, 0))] + y_specs,
        out_specs=pl.BlockSpec((tile, D_MODEL), lambda i: (i, 0)),
        out_shape=jax.ShapeDtypeStruct((n, D_MODEL), jnp.float32),
        compiler_params=pltpu.CompilerParams(dimension_semantics=("arbitrary",)),
        name="combine",
    )(x1, gates, picked, picked, picked, picked)


def _gather_rows(table, idx):
    n = idx.shape[0]
    width = table.shape[1]
    per_worker = n // SC_WORKERS
    n_chunks = per_worker // SC_CHUNK
    mesh = plsc.VectorSubcoreMesh(core_axis_name="c", subcore_axis_name="s")

    @functools.partial(
        pl.kernel, mesh=mesh,
        out_type=jax.ShapeDtypeStruct((n, width), table.dtype),
        scratch_types=[pltpu.VMEM((SC_CHUNK,), jnp.int32),
                       pltpu.VMEM((SC_CHUNK, width), table.dtype),
                       pltpu.SemaphoreType.DMA],
    )
    def gather(table_hbm, idx_hbm, out_hbm, idx_v, rows_v, sem):
        wid = lax.axis_index("s") * 2 + lax.axis_index("c")
        base = wid * per_worker

        @pl.loop(0, n_chunks)
        def _(i):
            off = base + i * SC_CHUNK
            pltpu.sync_copy(idx_hbm.at[pl.ds(off, SC_CHUNK)], idx_v)
            pltpu.async_copy(table_hbm.at[idx_v], rows_v, sem).wait()
            pltpu.sync_copy(rows_v, out_hbm.at[pl.ds(off, SC_CHUNK)])

    return gather(table, idx)


def _scatter_rows(src, dest, n_out):
    n, width = src.shape
    n_chunks = n // SC_CHUNK
    per_worker = -(-n_chunks // SC_WORKERS)
    mesh = plsc.VectorSubcoreMesh(core_axis_name="c", subcore_axis_name="s")

    @functools.partial(
        pl.kernel, mesh=mesh,
        out_type=jax.ShapeDtypeStruct((n_out, width), src.dtype),
        scratch_types=[pltpu.VMEM((SC_CHUNK,), jnp.int32)] * TOP_K
        + [pltpu.VMEM((SC_CHUNK, width), src.dtype)],
    )
    def scatter(src_hbm, dest_hbm, out_hbm, i0, i1, i2, i3, rows_v):
        wid = lax.axis_index("s") * 2 + lax.axis_index("c")

        @pl.loop(0, per_worker)
        def _(i):
            c = i * SC_WORKERS + wid

            @pl.when(c < n_chunks)
            def _():
                off = c * SC_CHUNK
                pltpu.sync_copy(src_hbm.at[pl.ds(off, SC_CHUNK)], rows_v)
                for j, idx_v in enumerate((i0, i1, i2, i3)):
                    pltpu.sync_copy(dest_hbm.at[pl.ds(j * n + off, SC_CHUNK)], idx_v)
                for idx_v in (i0, i1, i2, i3):
                    pltpu.sync_copy(rows_v, out_hbm.at[idx_v])

    return scatter(src, dest)


def _t5_bucket_np(rel):
    half = NUM_BUCKETS // 2
    max_exact = half // 2
    n = np.abs(rel)
    nf = np.maximum(n, 1).astype(np.float32)
    large = max_exact + (np.log(nf / max_exact) / math.log(MAX_DISTANCE / max_exact)
                         * (half - max_exact)).astype(np.int32)
    large = np.minimum(large, half - 1)
    return np.where(rel > 0, half, 0) + np.where(n < max_exact, n, large)


def _bias_tables(rel_bias, visible):
    kap = np.arange(BAND)[:, None]
    rho = np.arange(PAIR)[None, :]
    bucket = _t5_bucket_np(kap - WINDOW - rho)
    rb = rel_bias.astype(jnp.float32)
    tab = jnp.zeros((BAND, PAIR, N_HEADS), jnp.float32)
    for bkt in range(NUM_BUCKETS):
        tab = jnp.where(jnp.asarray(bucket == bkt)[:, :, None], rb[bkt], tab)
    tab = jnp.where(jnp.asarray(visible)[:, :, None], tab, NEG_INF)
    tab = jnp.transpose(tab, (2, 0, 1)).reshape(N_KV_HEADS, GROUP, BAND, PAIR)
    return jnp.transpose(tab, (0, 2, 1, 3)).reshape(N_KV_HEADS, BAND, GROUP * PAIR)


def _mixer_consts(l, norm1_g, w_in, q_norm_g, k_norm_g, rel_bias, sinks, w_pool, pool_scale,
                  w_out, norm2_g, w_router, b_router, visible, tile):
    f32, bf16 = jnp.float32, jnp.bfloat16
    q_off, k_off, v_off = POOL_WIDTH, POOL_WIDTH + ATTN_WIDTH, POOL_WIDTH + ATTN_WIDTH + KV_WIDTH
    w = w_in[l]
    w_ukv = jnp.concatenate([w[:, :q_off], w[:, k_off:]], axis=1).astype(bf16)
    w_qvt = jnp.transpose(jnp.concatenate([w[:, q_off:k_off], w[:, v_off:]], axis=1)).astype(bf16)
    lane_head = np.arange(KV_WIDTH) // HEAD_DIM
    blockdiag = jnp.asarray(lane_head[:, None] == lane_head[None, :], bf16)
    sink_rows = jnp.repeat(sinks[l].astype(f32).reshape(N_KV_HEADS, GROUP), PAIR, axis=1)
    wr = jnp.pad(w_router[l].astype(f32), ((0, 0), (0, 128 - N_EXPERTS)))
    wr_hi = wr.astype(bf16)
    wr_lo = (wr - wr_hi.astype(f32)).astype(bf16)
    tri = jnp.asarray(np.arange(tile)[:, None] < np.arange(tile)[None, :], bf16)
    return [
        norm1_g[l].reshape(1, D_MODEL).astype(f32), w_ukv, w_qvt,
        q_norm_g[l].reshape(HEAD_DIM, 1).astype(f32),
        jnp.tile(k_norm_g[l].astype(f32), N_KV_HEADS).reshape(1, KV_WIDTH),
        blockdiag, _bias_tables(rel_bias, visible), sink_rows,
        w_pool[l].astype(bf16), pool_scale[l].reshape(1, POOL_WIDTH).astype(f32),
        w_out[l].astype(bf16), norm2_g[l].reshape(1, D_MODEL).astype(f32),
        wr_hi, wr_lo, b_router[l].reshape(N_EXPERTS, 1).astype(f32), tri,
    ]


def _visibility():
    kap = np.arange(BAND)[:, None]
    rho = np.arange(PAIR)[None, :]
    kc, qc = kap // CHUNK, rho // CHUNK
    prompt = (kc >= qc) & (kc <= qc + WINDOW // CHUNK)
    return prompt


def kernel(x_prompt, x_sample, cache_k, cache_v, state_pool, norm1_g, w_in, q_norm_g, k_norm_g,
           rel_bias, sinks, w_pool, pool_scale, w_out, norm2_g, w_router, b_router,
           w_gate, b_gate, w_up, b_up, w_down, b_down):
    f32, bf16 = jnp.float32, jnp.bfloat16
    depth = w_in.shape[0]
    nb, seq, _ = x_prompt.shape
    ndb, dseq, _ = x_sample.shape
    cache_len = cache_k.shape[2]
    assert seq % MIX_TILE == 0 and cache_len == WINDOW and HIST_ROWS <= dseq <= PAIR
    n_p, n_s = nb * seq, ndb * dseq
    n_tok = n_p + n_s
    assert n_p % COMBINE_TILE == 0 and n_tok % COMBINE_TILE == 0 and n_s % 8 == 0

    vis_prompt = _visibility()
    vis_sample = np.broadcast_to(np.arange(BAND)[:, None] < WINDOW + dseq, (BAND, PAIR))

    n_assign = n_tok * TOP_K
    gather_quant = SC_WORKERS * SC_CHUNK
    n_blocks = -(-(n_assign + N_EXPERTS * (EXPERT_BLOCK - 1)) // EXPERT_BLOCK)
    n_blocks = -(-n_blocks // (gather_quant // EXPERT_BLOCK)) * (gather_quant // EXPERT_BLOCK)
    n_slots = n_blocks * EXPERT_BLOCK
    n_pick = -(-n_assign // gather_quant) * gather_quant

    xp, xs = x_prompt, x_sample
    outs = [[] for _ in range(6)]
    e_ids = jnp.arange(N_EXPERTS, dtype=jnp.int32)
    for l in range(depth):
        wl = (l, norm1_g, w_in, q_norm_g, k_norm_g, rel_bias, sinks, w_pool, pool_scale, w_out,
              norm2_g, w_router, b_router)
        xs_pad = jnp.pad(xs, ((0, 0), (0, PAIR - dseq), (0, 0)))
        uh = jnp.pad(state_pool[l], ((0, 0), (HIST_ROWS - POOL_HIST, 0), (0, 0)))
        (xs1, h_all, idx_s, rank_s, gate_s, cnt_s, k_s, v_s, u_s) = _mixer_call(
            jnp.zeros((8, 128), jnp.uint32), xs_pad, cache_k[l].reshape(ndb, WINDOW, KV_WIDTH),
            cache_v[l].reshape(ndb, WINDOW, KV_WIDTH), uh, jnp.zeros((N_EXPERTS, 128), f32),
            _mixer_consts(*wl, vis_sample, PAIR),
            tile=PAIR, n_valid=dseq, pos0=PAST_LEN, mask_first=False, h_rows=n_tok, h_row0=n_p)
        zk = jnp.zeros((nb, WINDOW, KV_WIDTH), f32)
        zu = jnp.zeros((nb, HIST_ROWS, POOL_WIDTH), f32)
        (xp1, h_all, idx_p, rank_p, gate_p, cnt_all, k_p, v_p, u_p) = _mixer_call(
            h_all, xp, zk, zk, zu, cnt_s, _mixer_consts(*wl, vis_prompt, MIX_TILE),
            tile=MIX_TILE, n_valid=MIX_TILE, pos0=0, mask_first=True, h_rows=n_tok, h_row0=0)

        counts = cnt_all[:, 0].astype(jnp.int32)
        pcounts = (counts + EXPERT_BLOCK - 1) // EXPERT_BLOCK * EXPERT_BLOCK
        pend = jnp.cumsum(pcounts)
        pstart = pend - pcounts

        def per_token(a_p, a_s):
            return jnp.concatenate(
                [jnp.transpose(a_p, (1, 0, 2)).reshape(TOP_K, n_p),
                 jnp.transpose(a_s[:, :, :dseq], (1, 0, 2)).reshape(TOP_K, n_s)], axis=1)

        idx_all = per_token(idx_p, idx_s)
        rank_all = per_token(rank_p, rank_s)
        gate_all = per_token(gate_p, gate_s)
        dest = rank_all + jnp.sum(
            jnp.where(idx_all[None] == e_ids[:, None, None], pstart[:, None, None], 0), axis=0)
        dest = dest.reshape(-1)
        blk0 = jnp.arange(n_blocks, dtype=jnp.int32) * EXPERT_BLOCK
        block_e = jnp.minimum(jnp.sum((pend[None, :] <= blk0[:, None]).astype(jnp.int32), axis=1),
                              N_EXPERTS - 1)
        block_rows = jnp.clip(jnp.sum(jnp.where(block_e[:, None] == e_ids[None, :],
                                                 (pstart + counts)[None, :], 0), axis=1) - blk0,
                              0, EXPERT_BLOCK).astype(jnp.int32)

        x_sorted = _scatter_rows(h_all, dest, n_slots)
        y_sorted = _expert_call(
            block_e, block_rows, x_sorted,
            w_gate[l], b_gate[l].reshape(N_EXPERTS, 1, D_MODEL).astype(f32),
            w_up[l], b_up[l].reshape(N_EXPERTS, 1, D_MODEL).astype(f32),
            w_down[l], b_down[l].reshape(N_EXPERTS, 1, D_MODEL).astype(f32))
        picked = _gather_rows(y_sorted, jnp.pad(dest, (0, n_pick - n_assign)))
        gates_t = jnp.transpose(gate_all)
        xp = _combine_call(xp1.reshape(n_p, D_MODEL), gates_t[:n_p], picked, n_tok, 0,
                           COMBINE_TILE).reshape(nb, seq, D_MODEL)
        xs1_rows = xs1[:, :dseq].reshape(n_s, D_MODEL)
        xs = _combine_call(xs1_rows, gates_t[n_p:], picked, n_tok, n_p,
                           COMBINE_TILE).reshape(ndb, dseq, D_MODEL)

        outs[0].append(k_p.reshape(nb, WINDOW, N_KV_HEADS, HEAD_DIM))
        outs[1].append(v_p.reshape(nb, WINDOW, N_KV_HEADS, HEAD_DIM))
        outs[2].append(u_p[:, HIST_ROWS - POOL_HIST:])
        outs[3].append(k_s[:, :dseq].reshape(ndb, dseq, N_KV_HEADS, HEAD_DIM))
        outs[4].append(v_s[:, :dseq].reshape(ndb, dseq, N_KV_HEADS, HEAD_DIM))
        outs[5].append(u_s[:, HIST_ROWS - POOL_HIST:])
    return (xp, xs) + tuple(jnp.stack(o) for o in outs)
```

```python
import functools
import math

import numpy as np
import jax
import jax.numpy as jnp
from jax import lax
from jax.experimental import pallas as pl
from jax.experimental.pallas import tpu as pltpu
from jax.experimental.pallas import tpu_sc as plsc

D_MODEL = 1024
CHUNK = 64
POOL_WIDTH = 512
POOL_WINDOWS = (2, 4, 8, 16)
POOL_GROUP = 128
POOL_HIST = 15
ATTN_WIDTH = 512
HEAD_DIM = 64
N_HEADS = 8
N_KV_HEADS = 2
GROUP = 4
KV_WIDTH = 128
WINDOW = 128
NUM_BUCKETS = 32
MAX_DISTANCE = 128
PAST_LEN = 2048
N_EXPERTS = 32
TOP_K = 4
SWIGLU_LIMIT = 7.0
SWIGLU_ALPHA = 1.702
EPS = 1e-5
NEG_INF = -1e30
ATTN_SCALE = HEAD_DIM ** -0.5

PAIR = 2 * CHUNK
BAND = PAIR + WINDOW
HIST_ROWS = 16
MIX_TILE = 512
EXPERT_BLOCK = 512
COMBINE_TILE = 256
SC_WORKERS = 32
SC_CHUNK = 64
VMEM_LIMIT = 56 * 1024 * 1024


def _pack_bf16_pair(a, b):
    ab = lax.bitcast_convert_type(a.astype(jnp.bfloat16).astype(jnp.float32), jnp.uint32)
    bb = lax.bitcast_convert_type(b.astype(jnp.bfloat16).astype(jnp.float32), jnp.uint32)
    return (ab >> 16) | (bb & jnp.uint32(0xFFFF0000))


def _unpack_bf16_pair(w):
    a = lax.bitcast_convert_type(w << 16, jnp.float32).astype(jnp.bfloat16)
    b = lax.bitcast_convert_type(w & jnp.uint32(0xFFFF0000), jnp.float32).astype(jnp.bfloat16)
    return a, b


def _mixer_kernel(x_ref, kh_ref, vh_ref, uh_ref, cnt_in_ref,
                  g1_ref, wukv_ref, wqvt_ref, qg_ref, kg_ref, bd_ref, bias_ref, sink_ref,
                  wpool_ref, pscale_ref, wout_ref, g2_ref, wrh_ref, wrl_ref, br_ref, tri_ref,
                  x1_ref, hp_ref, idx_ref, rank_ref, gate_ref, cnt_ref, ko_ref, vo_ref, uo_ref,
                  qt_s, kb_s, vt_s, ub_s, mix_s, cnt_s,
                  *, tile, n_valid, pos0, mask_first):
    b = pl.program_id(0)
    s = pl.program_id(1)
    bf16 = jnp.bfloat16
    f32 = jnp.float32

    @pl.when((b == 0) & (s == 0))
    def _():
        cnt_s[...] = cnt_in_ref[...]

    @pl.when(s == 0)
    def _():
        kb_s[0:WINDOW, :] = kh_ref[...].astype(bf16)
        vt_s[:, 0:WINDOW] = jnp.transpose(vh_ref[...]).astype(bf16)
        ub_s[0:HIST_ROWS, :] = uh_ref[...]

    @pl.when(s > 0)
    def _():
        kb_s[0:WINDOW, :] = kb_s[tile:tile + WINDOW, :]
        vt_s[:, 0:WINDOW] = vt_s[:, tile:tile + WINDOW]
        ub_s[0:HIST_ROWS, :] = ub_s[tile:tile + HIST_ROWS, :]

    x = x_ref[...]
    xn = (x * lax.rsqrt(jnp.mean(x * x, axis=-1, keepdims=True) + EPS) * g1_ref[...]).astype(bf16)
    z = jnp.dot(xn, wukv_ref[...], preferred_element_type=f32)
    zt = lax.dot_general(wqvt_ref[...], xn, (((1,), (1,)), ((), ())),
                         preferred_element_type=f32)
    u = z[:, 0:POOL_WIDTH]
    kz = z[:, POOL_WIDTH:POOL_WIDTH + KV_WIDTH]
    v = z[:, POOL_WIDTH + KV_WIDTH:]

    ksq = kz * kz
    kss = jnp.dot(ksq.astype(bf16), bd_ref[...], preferred_element_type=f32)
    kn = kz * lax.rsqrt(kss * (1.0 / HEAD_DIM) + EPS) * kg_ref[...]
    kb_s[WINDOW:, :] = kn.astype(bf16)
    vt_s[:, WINDOW:] = zt[ATTN_WIDTH:, :].astype(bf16)
    ub_s[HIST_ROWS:, :] = u

    row0 = max(n_valid, WINDOW) - WINDOW
    ko_ref[...] = kn[row0:row0 + WINDOW, :]
    vo_ref[...] = v[row0:row0 + WINDOW, :]
    uo_ref[...] = u[n_valid - HIST_ROWS:n_valid, :]

    for hd in range(N_HEADS):
        qh = zt[hd * HEAD_DIM:(hd + 1) * HEAD_DIM, :]
        ss = jnp.sum(qh * qh, axis=0, keepdims=True)
        qn = qh * (lax.rsqrt(ss * (1.0 / HEAD_DIM) + EPS) * ATTN_SCALE) * qg_ref[...]
        qt_s[hd * HEAD_DIM:(hd + 1) * HEAD_DIM, :] = qn.astype(bf16)

    pos = pos0 + s * tile + lax.broadcasted_iota(jnp.int32, (tile, 1), 0)
    for g, w in enumerate(POOL_WINDOWS):
        e = ub_s[:, g * POOL_GROUP:(g + 1) * POOL_GROUP]
        acc = e
        for lvl in range(g + 1):
            acc = acc + pltpu.roll(acc, 2 ** lvl, axis=0)
        inv_cnt = 1.0 / jnp.minimum(pos + 1, w).astype(f32)
        d = (acc[HIST_ROWS:, :] * inv_cnt - e[HIST_ROWS:, :]).astype(bf16)
        y = jnp.dot(d, wpool_ref[g], preferred_element_type=f32)
        y = y * pscale_ref[:, g * POOL_GROUP:(g + 1) * POOL_GROUP]
        mix_s[:, g * POOL_GROUP:(g + 1) * POOL_GROUP] = y.astype(bf16)

    zeros_q = jnp.zeros((HEAD_DIM, GROUP * PAIR), bf16)
    for p in range(tile // PAIR):
        k_band = kb_s[p * PAIR:p * PAIR + BAND, :]
        o_parts = []
        for h in range(N_KV_HEADS):
            qcat = jnp.concatenate(
                [qt_s[(h * GROUP + g) * HEAD_DIM:(h * GROUP + g + 1) * HEAD_DIM,
                      p * PAIR:(p + 1) * PAIR] for g in range(GROUP)], axis=1)
            rhs = jnp.concatenate([qcat, zeros_q] if h == 0 else [zeros_q, qcat], axis=0)
            st = jnp.dot(k_band, rhs, preferred_element_type=f32) + bias_ref[h]
            if mask_first and p == 0:
                krow = lax.broadcasted_iota(jnp.int32, (BAND, 1), 0)
                st = jnp.where((krow >= WINDOW) | (s > 0), st, NEG_INF)
            sink = sink_ref[h:h + 1, :]
            m = jnp.maximum(jnp.max(st, axis=0, keepdims=True), sink)
            ex = jnp.exp(st - m)
            den = jnp.sum(ex, axis=0, keepdims=True) + jnp.exp(sink - m)
            v_band = vt_s[h * HEAD_DIM:(h + 1) * HEAD_DIM, p * PAIR:p * PAIR + BAND]
            ot = jnp.dot(v_band, ex.astype(bf16), preferred_element_type=f32) / den
            for g in range(GROUP):
                o_parts.append(ot[:, g * PAIR:(g + 1) * PAIR])
        o_all = jnp.concatenate(o_parts, axis=0)
        mix_s[p * PAIR:(p + 1) * PAIR, POOL_WIDTH:] = jnp.transpose(o_all).astype(bf16)

    x1 = x + jnp.dot(mix_s[...], wout_ref[...], preferred_element_type=f32)
    x1_ref[...] = x1

    hn = x1 * lax.rsqrt(jnp.mean(x1 * x1, axis=-1, keepdims=True) + EPS) * g2_ref[...]
    hp_ref[...] = _pack_bf16_pair(hn[0:n_valid, 0:D_MODEL // 2], hn[0:n_valid, D_MODEL // 2:])
    h_hi = hn.astype(bf16)
    h_lo = (hn - h_hi.astype(f32)).astype(bf16)
    logits = (jnp.dot(h_hi, wrh_ref[...], preferred_element_type=f32)
              + jnp.dot(h_lo, wrh_ref[...], preferred_element_type=f32)
              + jnp.dot(h_hi, wrl_ref[...], preferred_element_type=f32))
    lt = jnp.transpose(logits)[0:N_EXPERTS, :] + br_ref[...]

    eidx = lax.broadcasted_iota(jnp.int32, (N_EXPERTS, tile), 0).astype(f32)
    vals, hots = [], []
    for j in range(TOP_K):
        m = jnp.max(lt, axis=0, keepdims=True)
        sel = jnp.min(jnp.where(lt == m, eidx, float(N_EXPERTS)), axis=0, keepdims=True)
        hot = eidx == sel
        lt = jnp.where(hot, -jnp.inf, lt)
        idx_ref[j:j + 1, :] = sel.astype(jnp.int32)
        vals.append(m)
        hots.append(hot)
    exps = [jnp.exp(vv - vals[0]) for vv in vals]
    esum = exps[0] + exps[1] + exps[2] + exps[3]
    for j in range(TOP_K):
        gate_ref[j:j + 1, :] = exps[j] / esum

    chosen_f = sum(jnp.where(hot, 1.0, 0.0) for hot in hots)
    if n_valid < tile:
        lane = lax.broadcasted_iota(jnp.int32, (N_EXPERTS, tile), 1)
        chosen_f = jnp.where(lane < n_valid, chosen_f, 0.0)
    before = jnp.dot(chosen_f.astype(bf16), tri_ref[...], preferred_element_type=f32)
    base = before + cnt_s[:, 0:1]
    for j in range(TOP_K):
        rank_ref[j:j + 1, :] = jnp.sum(jnp.where(hots[j], base, 0.0), axis=0,
                                       keepdims=True).astype(jnp.int32)
    cnt_new = cnt_s[...] + jnp.sum(chosen_f, axis=1, keepdims=True)
    cnt_s[...] = cnt_new
    cnt_ref[...] = cnt_new


def _mixer_call(x, k_hist, v_hist, u_hist, cnt_in, consts, *, tile, n_valid, pos0, mask_first):
    nb, seq, _ = x.shape
    n_tiles = seq // tile
    f32 = jnp.float32
    assert n_valid % 8 == 0

    def full(a):
        nd = a.ndim
        return pl.BlockSpec(a.shape, lambda b, s, _nd=nd: (0,) * _nd)

    in_specs = [
        pl.BlockSpec((None, tile, D_MODEL), lambda b, s: (b, s, 0)),
        pl.BlockSpec((None, WINDOW, KV_WIDTH), lambda b, s: (b, 0, 0)),
        pl.BlockSpec((None, WINDOW, KV_WIDTH), lambda b, s: (b, 0, 0)),
        pl.BlockSpec((None, HIST_ROWS, POOL_WIDTH), lambda b, s: (b, 0, 0)),
        full(cnt_in),
    ] + [full(c) for c in consts]
    out_shape = [
        jax.ShapeDtypeStruct((nb, seq, D_MODEL), f32),
        jax.ShapeDtypeStruct((nb * n_tiles * n_valid, D_MODEL // 2), jnp.uint32),
        jax.ShapeDtypeStruct((nb, TOP_K, seq), jnp.int32),
        jax.ShapeDtypeStruct((nb, TOP_K, seq), jnp.int32),
        jax.ShapeDtypeStruct((nb, TOP_K, seq), f32),
        jax.ShapeDtypeStruct((N_EXPERTS, 128), f32),
        jax.ShapeDtypeStruct((nb, WINDOW, KV_WIDTH), f32),
        jax.ShapeDtypeStruct((nb, WINDOW, KV_WIDTH), f32),
        jax.ShapeDtypeStruct((nb, HIST_ROWS, POOL_WIDTH), f32),
    ]
    out_specs = [
        pl.BlockSpec((None, tile, D_MODEL), lambda b, s: (b, s, 0)),
        pl.BlockSpec((n_valid, D_MODEL // 2), lambda b, s: (b * n_tiles + s, 0)),
        pl.BlockSpec((None, TOP_K, tile), lambda b, s: (b, 0, s)),
        pl.BlockSpec((None, TOP_K, tile), lambda b, s: (b, 0, s)),
        pl.BlockSpec((None, TOP_K, tile), lambda b, s: (b, 0, s)),
        pl.BlockSpec((N_EXPERTS, 128), lambda b, s: (0, 0)),
        pl.BlockSpec((None, WINDOW, KV_WIDTH), lambda b, s: (b, 0, 0)),
        pl.BlockSpec((None, WINDOW, KV_WIDTH), lambda b, s: (b, 0, 0)),
        pl.BlockSpec((None, HIST_ROWS, POOL_WIDTH), lambda b, s: (b, 0, 0)),
    ]
    scratch = [
        pltpu.VMEM((ATTN_WIDTH, tile), jnp.bfloat16),
        pltpu.VMEM((WINDOW + tile, KV_WIDTH), jnp.bfloat16),
        pltpu.VMEM((KV_WIDTH, WINDOW + tile), jnp.bfloat16),
        pltpu.VMEM((HIST_ROWS + tile, POOL_WIDTH), f32),
        pltpu.VMEM((tile, D_MODEL), jnp.bfloat16),
        pltpu.VMEM((N_EXPERTS, 128), f32),
    ]
    kern = functools.partial(_mixer_kernel, tile=tile, n_valid=n_valid, pos0=pos0,
                             mask_first=mask_first)
    return pl.pallas_call(
        kern,
        grid=(nb, n_tiles),
        in_specs=in_specs,
        out_specs=out_specs,
        out_shape=out_shape,
        scratch_shapes=scratch,
        compiler_params=pltpu.CompilerParams(
            dimension_semantics=("arbitrary", "arbitrary"),
            vmem_limit_bytes=VMEM_LIMIT),
        name="mixer",
    )(x, k_hist, v_hist, u_hist, cnt_in, *consts)


def _expert_kernel(be_ref, nv_ref, xs_ref, wg_ref, bg_ref, wu_ref, bu_ref, wd_ref, bd_ref, ys_ref,
                   wg_s, wu_s, wd_s):
    i = pl.program_id(0)
    n_rows = nv_ref[i]
    bf16 = jnp.bfloat16

    @pl.when((n_rows > 0) & ((i == 0) | (be_ref[i] != be_ref[jnp.maximum(i - 1, 0)])))
    def _():
        wg_s[...] = wg_ref[...].astype(bf16)
        wu_s[...] = wu_ref[...].astype(bf16)
        wd_s[...] = wd_ref[...].astype(bf16)

    @pl.when(n_rows > 0)
    def _():
        f32 = jnp.float32
        half = D_MODEL // 2
        row = lax.broadcasted_iota(jnp.int32, (EXPERT_BLOCK, 1), 0)
        words = jnp.where(row < n_rows, xs_ref[...], jnp.uint32(0))
        xa, xb = _unpack_bf16_pair(words)
        a = (jnp.dot(xa, wg_s[0:half, :], preferred_element_type=f32)
             + jnp.dot(xb, wg_s[half:, :], preferred_element_type=f32) + bg_ref[...])
        bb = (jnp.dot(xa, wu_s[0:half, :], preferred_element_type=f32)
              + jnp.dot(xb, wu_s[half:, :], preferred_element_type=f32) + bu_ref[...])
        a = jnp.minimum(a, SWIGLU_LIMIT)
        bb = jnp.clip(bb, -SWIGLU_LIMIT, SWIGLU_LIMIT)
        act = a * (1.0 / (1.0 + jnp.exp(-SWIGLU_ALPHA * a))) * (bb + 1.0)
        y = jnp.dot(act.astype(bf16), wd_s[...], preferred_element_type=f32) + bd_ref[...]
        ys_ref[...] = _pack_bf16_pair(y[:, 0:half], y[:, half:])

    @pl.when(n_rows == 0)
    def _():
        ys_ref[...] = jnp.zeros_like(ys_ref)


def _expert_call(block_e, block_rows, xs, wg, bg, wu, bu, wd, bd):
    n_slots = xs.shape[0]
    n_blocks = n_slots // EXPERT_BLOCK
    w_spec = pl.BlockSpec((None, D_MODEL, D_MODEL), lambda i, be, nv: (be[i], 0, 0))
    b_spec = pl.BlockSpec((None, 1, D_MODEL), lambda i, be, nv: (be[i], 0, 0))
    grid_spec = pltpu.PrefetchScalarGridSpec(
        num_scalar_prefetch=2,
        grid=(n_blocks,),
        in_specs=[pl.BlockSpec((EXPERT_BLOCK, D_MODEL // 2), lambda i, be, nv: (i, 0)),
                  w_spec, b_spec, w_spec, b_spec, w_spec, b_spec],
        out_specs=pl.BlockSpec((EXPERT_BLOCK, D_MODEL // 2), lambda i, be, nv: (i, 0)),
        scratch_shapes=[pltpu.VMEM((D_MODEL, D_MODEL), jnp.bfloat16)] * 3,
    )
    return pl.pallas_call(
        _expert_kernel,
        grid_spec=grid_spec,
        out_shape=jax.ShapeDtypeStruct((n_slots, D_MODEL // 2), jnp.uint32),
        compiler_params=pltpu.CompilerParams(
            dimension_semantics=("arbitrary",),
            vmem_limit_bytes=VMEM_LIMIT),
        name="experts",
    )(block_e, block_rows, xs, wg, bg, wu, bu, wd, bd)


def _combine_kernel(x1_ref, g_ref, y0_ref, y1_ref, y2_ref, y3_ref, o_ref):
    g = g_ref[...]
    half = D_MODEL // 2
    lo, hi = x1_ref[:, 0:half], x1_ref[:, half:]
    for j, y_ref in enumerate((y0_ref, y1_ref, y2_ref, y3_ref)):
        w = y_ref[...]
        gj = g[:, j:j + 1]
        lo = lo + gj * lax.bitcast_convert_type(w << 16, jnp.float32)
        hi = hi + gj * lax.bitcast_convert_type(w & jnp.uint32(0xFFFF0000), jnp.float32)
    o_ref[:, 0:half] = lo
    o_ref[:, half:] = hi


def _combine_call(x1, gates, picked, n_tok, tok0, tile):
    n = x1.shape[0]
    base = [(j * n_tok + tok0) // tile for j in range(TOP_K)]
    y_specs = [pl.BlockSpec((tile, D_MODEL // 2), lambda i, _o=o: (_o + i, 0)) for o in base]
    return pl.pallas_call(
        _combine_kernel,
        grid=(n // tile,),
        in_specs=[pl.BlockSpec((tile, D_MODEL), lambda i: (i, 0)),
                  pl.BlockSpec((tile, TOP_K), lambda i: (i, 0))] + y_specs,
        out_specs=pl.BlockSpec((tile, D_MODEL), lambda i: (i, 0)),
        out_shape=jax.ShapeDtypeStruct((n, D_MODEL), jnp.float32),
        compiler_params=pltpu.CompilerParams(dimension_semantics=("arbitrary",)),
        name="combine",
    )(x1, gates, picked, picked, picked, picked)


def _gather_rows(table, idx):
    n = idx.shape[0]
    width = table.shape[1]
    per_worker = n // SC_WORKERS
    n_chunks = per_worker // SC_CHUNK
    mesh = plsc.VectorSubcoreMesh(core_axis_name="c", subcore_axis_name="s")

    @functools.partial(
        pl.kernel, mesh=mesh,
        out_type=jax.ShapeDtypeStruct((n, width), table.dtype),
        scratch_types=[pltpu.VMEM((SC_CHUNK,), jnp.int32),
                       pltpu.VMEM((SC_CHUNK, width), table.dtype),
                       pltpu.SemaphoreType.DMA],
    )
    def gather(table_hbm, idx_hbm, out_hbm, idx_v, rows_v, sem):
        wid = lax.axis_index("s") * 2 + lax.axis_index("c")
        base = wid * per_worker

        @pl.loop(0, n_chunks)
        def _(i):
            off = base + i * SC_CHUNK
            pltpu.sync_copy(idx_hbm.at[pl.ds(off, SC_CHUNK)], idx_v)
            pltpu.async_copy(table_hbm.at[idx_v], rows_v, sem).wait()
            pltpu.sync_copy(rows_v, out_hbm.at[pl.ds(off, SC_CHUNK)])

    return gather(table, idx)


def _scatter_rows(src_a, src_b, dest, n_out):
    n_a, width = src_a.shape
    n_b = src_b.shape[0]
    n = n_a + n_b
    assert n_a % SC_CHUNK == 0 and n_b % SC_CHUNK == 0
    chunks_a = n_a // SC_CHUNK
    n_chunks = n // SC_CHUNK
    per_worker = -(-n_chunks // SC_WORKERS)
    mesh = plsc.VectorSubcoreMesh(core_axis_name="c", subcore_axis_name="s")

    @functools.partial(
        pl.kernel, mesh=mesh,
        out_type=jax.ShapeDtypeStruct((n_out, width), src_a.dtype),
        scratch_types=[pltpu.VMEM((SC_CHUNK,), jnp.int32)] * TOP_K
        + [pltpu.VMEM((SC_CHUNK, width), src_a.dtype)],
    )
    def scatter(a_hbm, b_hbm, dest_hbm, out_hbm, i0, i1, i2, i3, rows_v):
        wid = lax.axis_index("s") * 2 + lax.axis_index("c")

        def send(off):
            for j, idx_v in enumerate((i0, i1, i2, i3)):
                pltpu.sync_copy(dest_hbm.at[pl.ds(j * n + off, SC_CHUNK)], idx_v)
            for idx_v in (i0, i1, i2, i3):
                pltpu.sync_copy(rows_v, out_hbm.at[idx_v])

        @pl.loop(0, per_worker)
        def _(i):
            c = i * SC_WORKERS + wid

            @pl.when(c < chunks_a)
            def _():
                pltpu.sync_copy(a_hbm.at[pl.ds(c * SC_CHUNK, SC_CHUNK)], rows_v)
                send(c * SC_CHUNK)

            @pl.when((c >= chunks_a) & (c < n_chunks))
            def _():
                pltpu.sync_copy(b_hbm.at[pl.ds((c - chunks_a) * SC_CHUNK, SC_CHUNK)], rows_v)
                send(c * SC_CHUNK)

    return scatter(src_a, src_b, dest)


def _t5_bucket_np(rel):
    half = NUM_BUCKETS // 2
    max_exact = half // 2
    n = np.abs(rel)
    nf = np.maximum(n, 1).astype(np.float32)
    large = max_exact + (np.log(nf / max_exact) / math.log(MAX_DISTANCE / max_exact)
                         * (half - max_exact)).astype(np.int32)
    large = np.minimum(large, half - 1)
    return np.where(rel > 0, half, 0) + np.where(n < max_exact, n, large)


def _bias_tables(rel_bias, visible):
    kap = np.arange(BAND)[:, None]
    rho = np.arange(PAIR)[None, :]
    bucket = _t5_bucket_np(kap - WINDOW - rho)
    rb = rel_bias.astype(jnp.float32)
    tab = jnp.zeros((BAND, PAIR, N_HEADS), jnp.float32)
    for bkt in range(NUM_BUCKETS):
        tab = jnp.where(jnp.asarray(bucket == bkt)[:, :, None], rb[bkt], tab)
    tab = jnp.where(jnp.asarray(visible)[:, :, None], tab, NEG_INF)
    tab = jnp.transpose(tab, (2, 0, 1)).reshape(N_KV_HEADS, GROUP, BAND, PAIR)
    return jnp.transpose(tab, (0, 2, 1, 3)).reshape(N_KV_HEADS, BAND, GROUP * PAIR)


def _mixer_consts(l, norm1_g, w_in, q_norm_g, k_norm_g, rel_bias, sinks, w_pool, pool_scale,
                  w_out, norm2_g, w_router, b_router, visible, tile):
    f32, bf16 = jnp.float32, jnp.bfloat16
    q_off, k_off, v_off = POOL_WIDTH, POOL_WIDTH + ATTN_WIDTH, POOL_WIDTH + ATTN_WIDTH + KV_WIDTH
    w = w_in[l]
    w_ukv = jnp.concatenate([w[:, :q_off], w[:, k_off:]], axis=1).astype(bf16)
    w_qvt = jnp.transpose(jnp.concatenate([w[:, q_off:k_off], w[:, v_off:]], axis=1)).astype(bf16)
    lane_head = np.arange(KV_WIDTH) // HEAD_DIM
    blockdiag = jnp.asarray(lane_head[:, None] == lane_head[None, :], bf16)
    sink_rows = jnp.repeat(sinks[l].astype(f32).reshape(N_KV_HEADS, GROUP), PAIR, axis=1)
    wr = jnp.pad(w_router[l].astype(f32), ((0, 0), (0, 128 - N_EXPERTS)))
    wr_hi = wr.astype(bf16)
    wr_lo = (wr - wr_hi.astype(f32)).astype(bf16)
    tri = jnp.asarray(np.arange(tile)[:, None] < np.arange(tile)[None, :], bf16)
    return [
        norm1_g[l].reshape(1, D_MODEL).astype(f32), w_ukv, w_qvt,
        q_norm_g[l].reshape(HEAD_DIM, 1).astype(f32),
        jnp.tile(k_norm_g[l].astype(f32), N_KV_HEADS).reshape(1, KV_WIDTH),
        blockdiag, _bias_tables(rel_bias, visible), sink_rows,
        w_pool[l].astype(bf16), pool_scale[l].reshape(1, POOL_WIDTH).astype(f32),
        w_out[l].astype(bf16), norm2_g[l].reshape(1, D_MODEL).astype(f32),
        wr_hi, wr_lo, b_router[l].reshape(N_EXPERTS, 1).astype(f32), tri,
    ]


def _visibility():
    kap = np.arange(BAND)[:, None]
    rho = np.arange(PAIR)[None, :]
    kc, qc = kap // CHUNK, rho // CHUNK
    prompt = (kc >= qc) & (kc <= qc + WINDOW // CHUNK)
    return prompt


def kernel(x_prompt, x_sample, cache_k, cache_v, state_pool, norm1_g, w_in, q_norm_g, k_norm_g,
           rel_bias, sinks, w_pool, pool_scale, w_out, norm2_g, w_router, b_router,
           w_gate, b_gate, w_up, b_up, w_down, b_down):
    f32, bf16 = jnp.float32, jnp.bfloat16
    depth = w_in.shape[0]
    nb, seq, _ = x_prompt.shape
    ndb, dseq, _ = x_sample.shape
    cache_len = cache_k.shape[2]
    assert seq % MIX_TILE == 0 and cache_len == WINDOW and HIST_ROWS <= dseq <= PAIR
    n_p, n_s = nb * seq, ndb * dseq
    n_tok = n_p + n_s
    assert n_p % COMBINE_TILE == 0 and n_tok % COMBINE_TILE == 0 and n_s % 8 == 0

    vis_prompt = _visibility()
    vis_sample = np.broadcast_to(np.arange(BAND)[:, None] < WINDOW + dseq, (BAND, PAIR))

    n_assign = n_tok * TOP_K
    gather_quant = SC_WORKERS * SC_CHUNK
    n_blocks = -(-(n_assign + N_EXPERTS * (EXPERT_BLOCK - 1)) // EXPERT_BLOCK)
    n_blocks = -(-n_blocks // (gather_quant // EXPERT_BLOCK)) * (gather_quant // EXPERT_BLOCK)
    n_slots = n_blocks * EXPERT_BLOCK
    n_pick = -(-n_assign // gather_quant) * gather_quant

    xp, xs = x_prompt, x_sample
    outs = [[] for _ in range(6)]
    e_ids = jnp.arange(N_EXPERTS, dtype=jnp.int32)
    for l in range(depth):
        wl = (l, norm1_g, w_in, q_norm_g, k_norm_g, rel_bias, sinks, w_pool, pool_scale, w_out,
              norm2_g, w_router, b_router)
        xs_pad = jnp.pad(xs, ((0, 0), (0, PAIR - dseq), (0, 0)))
        uh = jnp.pad(state_pool[l], ((0, 0), (HIST_ROWS - POOL_HIST, 0), (0, 0)))
        (xs1, h_s, idx_s, rank_s, gate_s, cnt_s, k_s, v_s, u_s) = _mixer_call(
            xs_pad, cache_k[l].reshape(ndb, WINDOW, KV_WIDTH),
            cache_v[l].reshape(ndb, WINDOW, KV_WIDTH), uh, jnp.zeros((N_EXPERTS, 128), f32),
            _mixer_consts(*wl, vis_sample, PAIR),
            tile=PAIR, n_valid=dseq, pos0=PAST_LEN, mask_first=False)
        zk = jnp.zeros((nb, WINDOW, KV_WIDTH), f32)
        zu = jnp.zeros((nb, HIST_ROWS, POOL_WIDTH), f32)
        (xp1, h_p, idx_p, rank_p, gate_p, cnt_all, k_p, v_p, u_p) = _mixer_call(
            xp, zk, zk, zu, cnt_s, _mixer_consts(*wl, vis_prompt, MIX_TILE),
            tile=MIX_TILE, n_valid=MIX_TILE, pos0=0, mask_first=True)

        counts = cnt_all[:, 0].astype(jnp.int32)
        pcounts = (counts + EXPERT_BLOCK - 1) // EXPERT_BLOCK * EXPERT_BLOCK
        pend = jnp.cumsum(pcounts)
        pstart = pend - pcounts

        def per_token(a_p, a_s):
            return jnp.concatenate(
                [jnp.transpose(a_p, (1, 0, 2)).reshape(TOP_K, n_p),
                 jnp.transpose(a_s[:, :, :dseq], (1, 0, 2)).reshape(TOP_K, n_s)], axis=1)

        idx_all = per_token(idx_p, idx_s)
        rank_all = per_token(rank_p, rank_s)
        gate_all = per_token(gate_p, gate_s)
        dest = rank_all + jnp.sum(
            jnp.where(idx_all[None] == e_ids[:, None, None], pstart[:, None, None], 0), axis=0)
        dest = dest.reshape(-1)
        blk0 = jnp.arange(n_blocks, dtype=jnp.int32) * EXPERT_BLOCK
        block_e = jnp.minimum(jnp.sum((pend[None, :] <= blk0[:, None]).astype(jnp.int32), axis=1),
                              N_EXPERTS - 1)
        block_rows = jnp.clip(jnp.sum(jnp.where(block_e[:, None] == e_ids[None, :],
                                                 (pstart + counts)[None, :], 0), axis=1) - blk0,
                              0, EXPERT_BLOCK).astype(jnp.int32)

        x_sorted = _scatter_rows(h_p, h_s, dest, n_slots)
        y_sorted = _expert_call(
            block_e, block_rows, x_sorted,
            w_gate[l], b_gate[l].reshape(N_EXPERTS, 1, D_MODEL).astype(f32),
            w_up[l], b_up[l].reshape(N_EXPERTS, 1, D_MODEL).astype(f32),
            w_down[l], b_down[l].reshape(N_EXPERTS, 1, D_MODEL).astype(f32))
        picked = _gather_rows(y_sorted, jnp.pad(dest, (0, n_pick - n_assign)))
        gates_t = jnp.transpose(gate_all)
        xp = _combine_call(xp1.reshape(n_p, D_MODEL), gates_t[:n_p], picked, n_tok, 0,
                           COMBINE_TILE).reshape(nb, seq, D_MODEL)
        xs1_rows = xs1[:, :dseq].reshape(n_s, D_MODEL)
        xs = _combine_call(xs1_rows, gates_t[n_p:], picked, n_tok, n_p,
                           COMBINE_TILE).reshape(ndb, dseq, D_MODEL)

        outs[0].append(k_p.reshape(nb, WINDOW, N_KV_HEADS, HEAD_DIM))
        outs[1].append(v_p.reshape(nb, WINDOW, N_KV_HEADS, HEAD_DIM))
        outs[2].append(u_p[:, HIST_ROWS - POOL_HIST:])
        outs[3].append(k_s[:, :dseq].reshape(ndb, dseq, N_KV_HEADS, HEAD_DIM))
        outs[4].append(v_s[:, :dseq].reshape(ndb, dseq, N_KV_HEADS, HEAD_DIM))
        outs[5].append(u_s[:, HIST_ROWS - POOL_HIST:])
    return (xp, xs) + tuple(jnp.stack(o) for o in outs)
```

```python
import functools
import math

import numpy as np
import jax
import jax.numpy as jnp
from jax import lax
from jax.experimental import pallas as pl
from jax.experimental.pallas import tpu as pltpu
from jax.experimental.pallas import tpu_sc as plsc

D_MODEL = 1024
CHUNK = 64
POOL_WIDTH = 512
POOL_WINDOWS = (2, 4, 8, 16)
POOL_GROUP = 128
POOL_HIST = 15
ATTN_WIDTH = 512
HEAD_DIM = 64
N_HEADS = 8
N_KV_HEADS = 2
GROUP = 4
KV_WIDTH = 128
WINDOW = 128
NUM_BUCKETS = 32
MAX_DISTANCE = 128
PAST_LEN = 2048
N_EXPERTS = 32
TOP_K = 4
SWIGLU_LIMIT = 7.0
SWIGLU_ALPHA = 1.702
EPS = 1e-5
NEG_INF = -1e30
ATTN_SCALE = HEAD_DIM ** -0.5

PAIR = 2 * CHUNK
BAND = PAIR + WINDOW
HIST_ROWS = 16
GATE_COLS = 8
MIX_TILE = 512
EXPERT_BLOCK = 512
COMBINE_TILE = 256
SC_WORKERS = 32
SC_CHUNK = 64
VMEM_LIMIT = 56 * 1024 * 1024


def _pack_bf16_pair(a, b):
    ab = lax.bitcast_convert_type(a.astype(jnp.bfloat16).astype(jnp.float32), jnp.uint32)
    bb = lax.bitcast_convert_type(b.astype(jnp.bfloat16).astype(jnp.float32), jnp.uint32)
    return (ab >> 16) | (bb & jnp.uint32(0xFFFF0000))


def _unpack_bf16_pair(w):
    a = lax.bitcast_convert_type(w << 16, jnp.float32).astype(jnp.bfloat16)
    b = lax.bitcast_convert_type(w & jnp.uint32(0xFFFF0000), jnp.float32).astype(jnp.bfloat16)
    return a, b


def _mixer_kernel(x_ref, kh_ref, vh_ref, uh_ref, cnt_in_ref,
                  g1_ref, wukv_ref, wqvt_ref, qg_ref, kg_ref, bd_ref, bias_ref, sink_ref,
                  wpool_ref, pscale_ref, wout_ref, g2_ref, wrh_ref, wrl_ref, br_ref, tri_ref,
                  x1_ref, hp_ref, idx_ref, rank_ref, gate_ref, cnt_ref, ko_ref, vo_ref, uo_ref,
                  qt_s, kb_s, vt_s, ub_s, mix_s, cnt_s,
                  *, tile, n_valid, pos0, mask_first):
    b = pl.program_id(0)
    s = pl.program_id(1)
    bf16 = jnp.bfloat16
    f32 = jnp.float32

    @pl.when((b == 0) & (s == 0))
    def _():
        cnt_s[...] = cnt_in_ref[...]

    @pl.when(s == 0)
    def _():
        kb_s[0:WINDOW, :] = kh_ref[...].astype(bf16)
        vt_s[:, 0:WINDOW] = jnp.transpose(vh_ref[...]).astype(bf16)
        ub_s[0:HIST_ROWS, :] = uh_ref[...]

    @pl.when(s > 0)
    def _():
        kb_s[0:WINDOW, :] = kb_s[tile:tile + WINDOW, :]
        vt_s[:, 0:WINDOW] = vt_s[:, tile:tile + WINDOW]
        ub_s[0:HIST_ROWS, :] = ub_s[tile:tile + HIST_ROWS, :]

    x = x_ref[...]
    xn = (x * lax.rsqrt(jnp.mean(x * x, axis=-1, keepdims=True) + EPS) * g1_ref[...]).astype(bf16)
    z = jnp.dot(xn, wukv_ref[...], preferred_element_type=f32)
    zt = lax.dot_general(wqvt_ref[...], xn, (((1,), (1,)), ((), ())),
                         preferred_element_type=f32)
    u = z[:, 0:POOL_WIDTH]
    kz = z[:, POOL_WIDTH:POOL_WIDTH + KV_WIDTH]
    v = z[:, POOL_WIDTH + KV_WIDTH:]

    ksq = kz * kz
    kss = jnp.dot(ksq.astype(bf16), bd_ref[...], preferred_element_type=f32)
    kn = kz * lax.rsqrt(kss * (1.0 / HEAD_DIM) + EPS) * kg_ref[...]
    kb_s[WINDOW:, :] = kn.astype(bf16)
    vt_s[:, WINDOW:] = zt[ATTN_WIDTH:, :].astype(bf16)
    ub_s[HIST_ROWS:, :] = u

    row0 = max(n_valid, WINDOW) - WINDOW
    ko_ref[...] = kn[row0:row0 + WINDOW, :]
    vo_ref[...] = v[row0:row0 + WINDOW, :]
    uo_ref[...] = u[n_valid - HIST_ROWS:n_valid, :]

    for hd in range(N_HEADS):
        qh = zt[hd * HEAD_DIM:(hd + 1) * HEAD_DIM, :]
        ss = jnp.sum(qh * qh, axis=0, keepdims=True)
        qn = qh * (lax.rsqrt(ss * (1.0 / HEAD_DIM) + EPS) * ATTN_SCALE) * qg_ref[...]
        qt_s[hd * HEAD_DIM:(hd + 1) * HEAD_DIM, :] = qn.astype(bf16)

    pos = pos0 + s * tile + lax.broadcasted_iota(jnp.int32, (tile, 1), 0)
    for g, w in enumerate(POOL_WINDOWS):
        e = ub_s[:, g * POOL_GROUP:(g + 1) * POOL_GROUP]
        acc = e
        for lvl in range(g + 1):
            acc = acc + pltpu.roll(acc, 2 ** lvl, axis=0)
        inv_cnt = 1.0 / jnp.minimum(pos + 1, w).astype(f32)
        d = (acc[HIST_ROWS:, :] * inv_cnt - e[HIST_ROWS:, :]).astype(bf16)
        y = jnp.dot(d, wpool_ref[g], preferred_element_type=f32)
        y = y * pscale_ref[:, g * POOL_GROUP:(g + 1) * POOL_GROUP]
        mix_s[:, g * POOL_GROUP:(g + 1) * POOL_GROUP] = y.astype(bf16)

    zeros_q = jnp.zeros((HEAD_DIM, GROUP * PAIR), bf16)
    for p in range(tile // PAIR):
        k_band = kb_s[p * PAIR:p * PAIR + BAND, :]
        o_parts = []
        for h in range(N_KV_HEADS):
            qcat = jnp.concatenate(
                [qt_s[(h * GROUP + g) * HEAD_DIM:(h * GROUP + g + 1) * HEAD_DIM,
                      p * PAIR:(p + 1) * PAIR] for g in range(GROUP)], axis=1)
            rhs = jnp.concatenate([qcat, zeros_q] if h == 0 else [zeros_q, qcat], axis=0)
            st = jnp.dot(k_band, rhs, preferred_element_type=f32) + bias_ref[h]
            if mask_first and p == 0:
                krow = lax.broadcasted_iota(jnp.int32, (BAND, 1), 0)
                st = jnp.where((krow >= WINDOW) | (s > 0), st, NEG_INF)
            sink = sink_ref[h:h + 1, :]
            m = jnp.maximum(jnp.max(st, axis=0, keepdims=True), sink)
            ex = jnp.exp(st - m)
            den = jnp.sum(ex, axis=0, keepdims=True) + jnp.exp(sink - m)
            v_band = vt_s[h * HEAD_DIM:(h + 1) * HEAD_DIM, p * PAIR:p * PAIR + BAND]
            ot = jnp.dot(v_band, ex.astype(bf16), preferred_element_type=f32) / den
            for g in range(GROUP):
                o_parts.append(ot[:, g * PAIR:(g + 1) * PAIR])
        o_all = jnp.concatenate(o_parts, axis=0)
        mix_s[p * PAIR:(p + 1) * PAIR, POOL_WIDTH:] = jnp.transpose(o_all).astype(bf16)

    x1 = x + jnp.dot(mix_s[...], wout_ref[...], preferred_element_type=f32)
    x1_ref[...] = x1

    hn = x1 * lax.rsqrt(jnp.mean(x1 * x1, axis=-1, keepdims=True) + EPS) * g2_ref[...]
    hp_ref[...] = _pack_bf16_pair(hn[0:n_valid, 0:D_MODEL // 2], hn[0:n_valid, D_MODEL // 2:])
    h_hi = hn.astype(bf16)
    h_lo = (hn - h_hi.astype(f32)).astype(bf16)
    logits = (jnp.dot(h_hi, wrh_ref[...], preferred_element_type=f32)
              + jnp.dot(h_lo, wrh_ref[...], preferred_element_type=f32)
              + jnp.dot(h_hi, wrl_ref[...], preferred_element_type=f32))
    lt = jnp.transpose(logits)[0:N_EXPERTS, :] + br_ref[...]

    eidx = lax.broadcasted_iota(jnp.int32, (N_EXPERTS, tile), 0).astype(f32)
    vals, hots = [], []
    for j in range(TOP_K):
        m = jnp.max(lt, axis=0, keepdims=True)
        sel = jnp.min(jnp.where(lt == m, eidx, float(N_EXPERTS)), axis=0, keepdims=True)
        hot = eidx == sel
        lt = jnp.where(hot, -jnp.inf, lt)
        idx_ref[j:j + 1, :] = sel.astype(jnp.int32)
        vals.append(m)
        hots.append(hot)
    exps = [jnp.exp(vv - vals[0]) for vv in vals]
    esum = exps[0] + exps[1] + exps[2] + exps[3]
    grow = lax.broadcasted_iota(jnp.int32, (GATE_COLS, tile), 0)
    gmat = jnp.zeros((GATE_COLS, tile), f32)
    for j in range(TOP_K):
        gmat = jnp.where(grow == j, exps[j] / esum, gmat)
    gmat = jnp.concatenate([gmat, jnp.zeros((128 - GATE_COLS, tile), f32)], axis=0)
    gate_ref[...] = jnp.transpose(gmat)[0:n_valid, 0:GATE_COLS]

    chosen_f = sum(jnp.where(hot, 1.0, 0.0) for hot in hots)
    if n_valid < tile:
        lane = lax.broadcasted_iota(jnp.int32, (N_EXPERTS, tile), 1)
        chosen_f = jnp.where(lane < n_valid, chosen_f, 0.0)
    before = jnp.dot(chosen_f.astype(bf16), tri_ref[...], preferred_element_type=f32)
    base = before + cnt_s[:, 0:1]
    for j in range(TOP_K):
        rank_ref[j:j + 1, :] = jnp.sum(jnp.where(hots[j], base, 0.0), axis=0,
                                       keepdims=True).astype(jnp.int32)
    cnt_new = cnt_s[...] + jnp.sum(chosen_f, axis=1, keepdims=True)
    cnt_s[...] = cnt_new
    cnt_ref[...] = cnt_new


def _mixer_call(x, k_hist, v_hist, u_hist, cnt_in, consts, *, stream0, tile, n_valid, pos0,
                mask_first):
    seq = x.shape[1]
    nb = k_hist.shape[0]
    n_tiles = seq // tile
    f32 = jnp.float32
    assert n_valid % 8 == 0

    def full(a):
        nd = a.ndim
        return pl.BlockSpec(a.shape, lambda b, s, _nd=nd: (0,) * _nd)

    in_specs = [
        pl.BlockSpec((None, tile, D_MODEL), lambda b, s: (stream0 + b, s, 0)),
        pl.BlockSpec((None, WINDOW, KV_WIDTH), lambda b, s: (b, 0, 0)),
        pl.BlockSpec((None, WINDOW, KV_WIDTH), lambda b, s: (b, 0, 0)),
        pl.BlockSpec((None, HIST_ROWS, POOL_WIDTH), lambda b, s: (b, 0, 0)),
        full(cnt_in),
    ] + [full(c) for c in consts]
    out_shape = [
        jax.ShapeDtypeStruct((nb, seq, D_MODEL), f32),
        jax.ShapeDtypeStruct((nb * n_tiles * n_valid, D_MODEL // 2), jnp.uint32),
        jax.ShapeDtypeStruct((nb, TOP_K, seq), jnp.int32),
        jax.ShapeDtypeStruct((nb, TOP_K, seq), jnp.int32),
        jax.ShapeDtypeStruct((nb * n_tiles * n_valid, GATE_COLS), f32),
        jax.ShapeDtypeStruct((N_EXPERTS, 128), f32),
        jax.ShapeDtypeStruct((nb, WINDOW, KV_WIDTH), f32),
        jax.ShapeDtypeStruct((nb, WINDOW, KV_WIDTH), f32),
        jax.ShapeDtypeStruct((nb, HIST_ROWS, POOL_WIDTH), f32),
    ]
    out_specs = [
        pl.BlockSpec((None, tile, D_MODEL), lambda b, s: (b, s, 0)),
        pl.BlockSpec((n_valid, D_MODEL // 2), lambda b, s: (b * n_tiles + s, 0)),
        pl.BlockSpec((None, TOP_K, tile), lambda b, s: (b, 0, s)),
        pl.BlockSpec((None, TOP_K, tile), lambda b, s: (b, 0, s)),
        pl.BlockSpec((n_valid, GATE_COLS), lambda b, s: (b * n_tiles + s, 0)),
        pl.BlockSpec((N_EXPERTS, 128), lambda b, s: (0, 0)),
        pl.BlockSpec((None, WINDOW, KV_WIDTH), lambda b, s: (b, 0, 0)),
        pl.BlockSpec((None, WINDOW, KV_WIDTH), lambda b, s: (b, 0, 0)),
        pl.BlockSpec((None, HIST_ROWS, POOL_WIDTH), lambda b, s: (b, 0, 0)),
    ]
    scratch = [
        pltpu.VMEM((ATTN_WIDTH, tile), jnp.bfloat16),
        pltpu.VMEM((WINDOW + tile, KV_WIDTH), jnp.bfloat16),
        pltpu.VMEM((KV_WIDTH, WINDOW + tile), jnp.bfloat16),
        pltpu.VMEM((HIST_ROWS + tile, POOL_WIDTH), f32),
        pltpu.VMEM((tile, D_MODEL), jnp.bfloat16),
        pltpu.VMEM((N_EXPERTS, 128), f32),
    ]
    kern = functools.partial(_mixer_kernel, tile=tile, n_valid=n_valid, pos0=pos0,
                             mask_first=mask_first)
    return pl.pallas_call(
        kern,
        grid=(nb, n_tiles),
        in_specs=in_specs,
        out_specs=out_specs,
        out_shape=out_shape,
        scratch_shapes=scratch,
        compiler_params=pltpu.CompilerParams(
            dimension_semantics=("arbitrary", "arbitrary"),
            vmem_limit_bytes=VMEM_LIMIT),
        name="mixer",
    )(x, k_hist, v_hist, u_hist, cnt_in, *consts)


def _expert_kernel(be_ref, nv_ref, xs_ref, wg_ref, bg_ref, wu_ref, bu_ref, wd_ref, bd_ref, ys_ref,
                   wg_s, wu_s, wd_s):
    i = pl.program_id(0)
    n_rows = nv_ref[i]
    bf16 = jnp.bfloat16

    @pl.when((n_rows > 0) & ((i == 0) | (be_ref[i] != be_ref[jnp.maximum(i - 1, 0)])))
    def _():
        wg_s[...] = wg_ref[...].astype(bf16)
        wu_s[...] = wu_ref[...].astype(bf16)
        wd_s[...] = wd_ref[...].astype(bf16)

    @pl.when(n_rows > 0)
    def _():
        f32 = jnp.float32
        half = D_MODEL // 2
        row = lax.broadcasted_iota(jnp.int32, (EXPERT_BLOCK, 1), 0)
        words = jnp.where(row < n_rows, xs_ref[...], jnp.uint32(0))
        xa, xb = _unpack_bf16_pair(words)
        a = (jnp.dot(xa, wg_s[0:half, :], preferred_element_type=f32)
             + jnp.dot(xb, wg_s[half:, :], preferred_element_type=f32) + bg_ref[...])
        bb = (jnp.dot(xa, wu_s[0:half, :], preferred_element_type=f32)
              + jnp.dot(xb, wu_s[half:, :], preferred_element_type=f32) + bu_ref[...])
        a = jnp.minimum(a, SWIGLU_LIMIT)
        bb = jnp.clip(bb, -SWIGLU_LIMIT, SWIGLU_LIMIT)
        act = a * (1.0 / (1.0 + jnp.exp(-SWIGLU_ALPHA * a))) * (bb + 1.0)
        y = jnp.dot(act.astype(bf16), wd_s[...], preferred_element_type=f32) + bd_ref[...]
        ys_ref[...] = _pack_bf16_pair(y[:, 0:half], y[:, half:])

    @pl.when(n_rows == 0)
    def _():
        ys_ref[...] = jnp.zeros_like(ys_ref)


def _expert_call(block_e, block_rows, xs, wg, bg, wu, bu, wd, bd):
    n_slots = xs.shape[0]
    n_blocks = n_slots // EXPERT_BLOCK
    w_spec = pl.BlockSpec((None, D_MODEL, D_MODEL), lambda i, be, nv: (be[i], 0, 0))
    b_spec = pl.BlockSpec((None, 1, D_MODEL), lambda i, be, nv: (be[i], 0, 0))
    grid_spec = pltpu.PrefetchScalarGridSpec(
        num_scalar_prefetch=2,
        grid=(n_blocks,),
        in_specs=[pl.BlockSpec((EXPERT_BLOCK, D_MODEL // 2), lambda i, be, nv: (i, 0)),
                  w_spec, b_spec, w_spec, b_spec, w_spec, b_spec],
        out_specs=pl.BlockSpec((EXPERT_BLOCK, D_MODEL // 2), lambda i, be, nv: (i, 0)),
        scratch_shapes=[pltpu.VMEM((D_MODEL, D_MODEL), jnp.bfloat16)] * 3,
    )
    return pl.pallas_call(
        _expert_kernel,
        grid_spec=grid_spec,
        out_shape=jax.ShapeDtypeStruct((n_slots, D_MODEL // 2), jnp.uint32),
        compiler_params=pltpu.CompilerParams(
            dimension_semantics=("arbitrary",),
            vmem_limit_bytes=VMEM_LIMIT),
        name="experts",
    )(block_e, block_rows, xs, wg, bg, wu, bu, wd, bd)


def _combine_kernel(*refs, starts):
    o_ref = refs[-1]
    i = pl.program_id(0)
    half = D_MODEL // 2
    for k in range(len(starts) - 1):
        x1_ref, g_ref = refs[6 * k], refs[6 * k + 1]
        y_refs = refs[6 * k + 2:6 * k + 6]

        @pl.when((i >= starts[k]) & (i < starts[k + 1]))
        def _():
            g = g_ref[...]
            lo, hi = x1_ref[:, 0:half], x1_ref[:, half:]
            for j, y_ref in enumerate(y_refs):
                w = y_ref[...]
                gj = g[:, j:j + 1]
                lo = lo + gj * lax.bitcast_convert_type(w << 16, jnp.float32)
                hi = hi + gj * lax.bitcast_convert_type(w & jnp.uint32(0xFFFF0000), jnp.float32)
            o_ref[:, 0:half] = lo
            o_ref[:, half:] = hi


def _combine_call(groups, tile):
    starts = [0]
    for x1, _, _, _, _ in groups:
        starts.append(starts[-1] + x1.shape[0] // tile)
    in_specs, args = [], []
    for k, (x1, gates, picked, n_tok, tok0) in enumerate(groups):
        last = x1.shape[0] // tile - 1

        def local(i, _s=starts[k], _last=last):
            return jnp.clip(i - _s, 0, _last)

        in_specs += [pl.BlockSpec((tile, D_MODEL), lambda i, _f=local: (_f(i), 0)),
                     pl.BlockSpec((tile, GATE_COLS), lambda i, _f=local: (_f(i), 0))]
        for j in range(TOP_K):
            assert (j * n_tok + tok0) % tile == 0 and x1.shape[0] % tile == 0
            base = (j * n_tok + tok0) // tile
            in_specs.append(pl.BlockSpec((tile, D_MODEL // 2),
                                         lambda i, _f=local, _b=base: (_b + _f(i), 0)))
        args += [x1, gates, picked, picked, picked, picked]
    return pl.pallas_call(
        functools.partial(_combine_kernel, starts=tuple(starts)),
        grid=(starts[-1],),
        in_specs=in_specs,
        out_specs=pl.BlockSpec((tile, D_MODEL), lambda i: (i, 0)),
        out_shape=jax.ShapeDtypeStruct((starts[-1] * tile, D_MODEL), jnp.float32),
        compiler_params=pltpu.CompilerParams(dimension_semantics=("arbitrary",)),
        name="combine",
    )(*args)


def _gather_rows(table, idx):
    n = idx.shape[0]
    width = table.shape[1]
    per_worker = n // SC_WORKERS
    n_chunks = per_worker // SC_CHUNK
    mesh = plsc.VectorSubcoreMesh(core_axis_name="c", subcore_axis_name="s")

    @functools.partial(
        pl.kernel, mesh=mesh,
        out_type=jax.ShapeDtypeStruct((n, width), table.dtype),
        scratch_types=[pltpu.VMEM((SC_CHUNK,), jnp.int32),
                       pltpu.VMEM((SC_CHUNK, width), table.dtype),
                       pltpu.SemaphoreType.DMA],
    )
    def gather(table_hbm, idx_hbm, out_hbm, idx_v, rows_v, sem):
        wid = lax.axis_index("s") * 2 + lax.axis_index("c")
        base = wid * per_worker

        @pl.loop(0, n_chunks)
        def _(i):
            off = base + i * SC_CHUNK
            pltpu.sync_copy(idx_hbm.at[pl.ds(off, SC_CHUNK)], idx_v)
            pltpu.async_copy(table_hbm.at[idx_v], rows_v, sem).wait()
            pltpu.sync_copy(rows_v, out_hbm.at[pl.ds(off, SC_CHUNK)])

    return gather(table, idx)


def _scatter_rows(srcs, dest, n_out):
    width = srcs[0].shape[1]
    starts = [0]
    for src in srcs:
        assert src.shape[0] % SC_CHUNK == 0
        starts.append(starts[-1] + src.shape[0] // SC_CHUNK)
    n_chunks = starts[-1]
    n = n_chunks * SC_CHUNK
    per_worker = -(-n_chunks // SC_WORKERS)
    mesh = plsc.VectorSubcoreMesh(core_axis_name="c", subcore_axis_name="s")

    @functools.partial(
        pl.kernel, mesh=mesh,
        out_type=jax.ShapeDtypeStruct((n_out, width), srcs[0].dtype),
        scratch_types=[pltpu.VMEM((SC_CHUNK,), jnp.int32)] * TOP_K
        + [pltpu.VMEM((SC_CHUNK, width), srcs[0].dtype)],
    )
    def scatter(*refs):
        src_hbms = refs[:len(srcs)]
        dest_hbm, out_hbm = refs[len(srcs)], refs[len(srcs) + 1]
        idx_vs, rows_v = refs[len(srcs) + 2:len(srcs) + 2 + TOP_K], refs[-1]
        wid = lax.axis_index("s") * 2 + lax.axis_index("c")

        @pl.loop(0, per_worker)
        def _(i):
            c = i * SC_WORKERS + wid
            for k, src_hbm in enumerate(src_hbms):

                @pl.when((c >= starts[k]) & (c < starts[k + 1]))
                def _():
                    pltpu.sync_copy(src_hbm.at[pl.ds((c - starts[k]) * SC_CHUNK, SC_CHUNK)], rows_v)
                    for j, idx_v in enumerate(idx_vs):
                        pltpu.sync_copy(dest_hbm.at[pl.ds(j * n + c * SC_CHUNK, SC_CHUNK)], idx_v)
                    for idx_v in idx_vs:
                        pltpu.sync_copy(rows_v, out_hbm.at[idx_v])

    return scatter(*srcs, dest)


def _t5_bucket_np(rel):
    half = NUM_BUCKETS // 2
    max_exact = half // 2
    n = np.abs(rel)
    nf = np.maximum(n, 1).astype(np.float32)
    large = max_exact + (np.log(nf / max_exact) / math.log(MAX_DISTANCE / max_exact)
                         * (half - max_exact)).astype(np.int32)
    large = np.minimum(large, half - 1)
    return np.where(rel > 0, half, 0) + np.where(n < max_exact, n, large)


def _bias_tables(rel_bias, visible):
    kap = np.arange(BAND)[:, None]
    rho = np.arange(PAIR)[None, :]
    bucket = _t5_bucket_np(kap - WINDOW - rho)
    rb = rel_bias.astype(jnp.float32)
    tab = jnp.zeros((BAND, PAIR, N_HEADS), jnp.float32)
    for bkt in range(NUM_BUCKETS):
        tab = jnp.where(jnp.asarray(bucket == bkt)[:, :, None], rb[bkt], tab)
    tab = jnp.where(jnp.asarray(visible)[:, :, None], tab, NEG_INF)
    tab = jnp.transpose(tab, (2, 0, 1)).reshape(N_KV_HEADS, GROUP, BAND, PAIR)
    return jnp.transpose(tab, (0, 2, 1, 3)).reshape(N_KV_HEADS, BAND, GROUP * PAIR)


def _mixer_consts(l, norm1_g, w_in, q_norm_g, k_norm_g, rel_bias, sinks, w_pool, pool_scale,
                  w_out, norm2_g, w_router, b_router, visible, tile):
    f32, bf16 = jnp.float32, jnp.bfloat16
    q_off, k_off, v_off = POOL_WIDTH, POOL_WIDTH + ATTN_WIDTH, POOL_WIDTH + ATTN_WIDTH + KV_WIDTH
    w = w_in[l]
    w_ukv = jnp.concatenate([w[:, :q_off], w[:, k_off:]], axis=1).astype(bf16)
    w_qvt = jnp.transpose(jnp.concatenate([w[:, q_off:k_off], w[:, v_off:]], axis=1)).astype(bf16)
    lane_head = np.arange(KV_WIDTH) // HEAD_DIM
    blockdiag = jnp.asarray(lane_head[:, None] == lane_head[None, :], bf16)
    sink_rows = jnp.repeat(sinks[l].astype(f32).reshape(N_KV_HEADS, GROUP), PAIR, axis=1)
    wr = jnp.pad(w_router[l].astype(f32), ((0, 0), (0, 128 - N_EXPERTS)))
    wr_hi = wr.astype(bf16)
    wr_lo = (wr - wr_hi.astype(f32)).astype(bf16)
    tri = jnp.asarray(np.arange(tile)[:, None] < np.arange(tile)[None, :], bf16)
    return [
        norm1_g[l].reshape(1, D_MODEL).astype(f32), w_ukv, w_qvt,
        q_norm_g[l].reshape(HEAD_DIM, 1).astype(f32),
        jnp.tile(k_norm_g[l].astype(f32), N_KV_HEADS).reshape(1, KV_WIDTH),
        blockdiag, _bias_tables(rel_bias, visible), sink_rows,
        w_pool[l].astype(bf16), pool_scale[l].reshape(1, POOL_WIDTH).astype(f32),
        w_out[l].astype(bf16), norm2_g[l].reshape(1, D_MODEL).astype(f32),
        wr_hi, wr_lo, b_router[l].reshape(N_EXPERTS, 1).astype(f32), tri,
    ]


def _visibility():
    kap = np.arange(BAND)[:, None]
    rho = np.arange(PAIR)[None, :]
    kc, qc = kap // CHUNK, rho // CHUNK
    prompt = (kc >= qc) & (kc <= qc + WINDOW // CHUNK)
    return prompt


def kernel(x_prompt, x_sample, cache_k, cache_v, state_pool, norm1_g, w_in, q_norm_g, k_norm_g,
           rel_bias, sinks, w_pool, pool_scale, w_out, norm2_g, w_router, b_router,
           w_gate, b_gate, w_up, b_up, w_down, b_down):
    f32, bf16 = jnp.float32, jnp.bfloat16
    depth = w_in.shape[0]
    nb, seq, _ = x_prompt.shape
    ndb, dseq, _ = x_sample.shape
    cache_len = cache_k.shape[2]
    assert seq % MIX_TILE == 0 and cache_len == WINDOW and HIST_ROWS <= dseq <= PAIR
    n_p, n_s = nb * seq, ndb * dseq
    n_tok = n_p + n_s
    assert n_p % COMBINE_TILE == 0 and n_tok % COMBINE_TILE == 0 and n_s % 8 == 0

    vis_prompt = _visibility()
    vis_sample = np.broadcast_to(np.arange(BAND)[:, None] < WINDOW + dseq, (BAND, PAIR))

    nb1 = nb * 3 // 4
    nb2 = nb - nb1
    assert nb1 > 0 and nb2 > 0 and (nb2 * seq) % COMBINE_TILE == 0

    xp, xs = x_prompt, x_sample
    outs = [[] for _ in range(6)]
    for l in range(depth):
        wl = (l, norm1_g, w_in, q_norm_g, k_norm_g, rel_bias, sinks, w_pool, pool_scale, w_out,
              norm2_g, w_router, b_router)
        consts_p = _mixer_consts(*wl, vis_prompt, MIX_TILE)
        moe_w = (w_gate[l], b_gate[l].reshape(N_EXPERTS, 1, D_MODEL).astype(f32),
                 w_up[l], b_up[l].reshape(N_EXPERTS, 1, D_MODEL).astype(f32),
                 w_down[l], b_down[l].reshape(N_EXPERTS, 1, D_MODEL).astype(f32))
        cnt0 = jnp.zeros((N_EXPERTS, 128), f32)

        def prompt_mixer(stream0, n_streams, cnt_in):
            zk = jnp.zeros((n_streams, WINDOW, KV_WIDTH), f32)
            zu = jnp.zeros((n_streams, HIST_ROWS, POOL_WIDTH), f32)
            return _mixer_call(xp, zk, zk, zu, cnt_in, consts_p, stream0=stream0, tile=MIX_TILE,
                               n_valid=MIX_TILE, pos0=0, mask_first=True)

        (x1_a, h_a, idx_a, rank_a, gate_a, cnt_a, k_a, v_a, u_a) = prompt_mixer(0, nb1, cnt0)
        picked_a = _moe_rows([h_a], [idx_a], [rank_a], [seq], cnt_a, moe_w)
        xs_pad = jnp.pad(xs, ((0, 0), (0, PAIR - dseq), (0, 0)))
        uh = jnp.pad(state_pool[l], ((0, 0), (HIST_ROWS - POOL_HIST, 0), (0, 0)))
        (xs1, h_s, idx_s, rank_s, gate_s, cnt_s, k_s, v_s, u_s) = _mixer_call(
            xs_pad, cache_k[l].reshape(ndb, WINDOW, KV_WIDTH),
            cache_v[l].reshape(ndb, WINDOW, KV_WIDTH), uh, cnt0,
            _mixer_consts(*wl, vis_sample, PAIR),
            stream0=0, tile=PAIR, n_valid=dseq, pos0=PAST_LEN, mask_first=False)
        (x1_b, h_b, idx_b, rank_b, gate_b, cnt_b, k_b, v_b, u_b) = prompt_mixer(nb1, nb2, cnt_s)
        picked_b = _moe_rows([h_b, h_s], [idx_b, idx_s], [rank_b, rank_s], [seq, dseq], cnt_b, moe_w)

        n_a, n_b = nb1 * seq, nb2 * seq
        xp = _combine_call(
            [(x1_a.reshape(n_a, D_MODEL), gate_a, picked_a, n_a, 0),
             (x1_b.reshape(n_b, D_MODEL), gate_b, picked_b, n_b + n_s, 0)],
            COMBINE_TILE).reshape(nb, seq, D_MODEL)
        xs = _combine_call(
            [(xs1[:, :dseq].reshape(n_s, D_MODEL), gate_s, picked_b, n_b + n_s, n_b)],
            COMBINE_TILE).reshape(ndb, dseq, D_MODEL)

        outs[0].append(jnp.concatenate([k_a, k_b]).reshape(nb, WINDOW, N_KV_HEADS, HEAD_DIM))
        outs[1].append(jnp.concatenate([v_a, v_b]).reshape(nb, WINDOW, N_KV_HEADS, HEAD_DIM))
        outs[2].append(jnp.concatenate([u_a, u_b])[:, HIST_ROWS - POOL_HIST:])
        outs[3].append(k_s[:, :dseq].reshape(ndb, dseq, N_KV_HEADS, HEAD_DIM))
        outs[4].append(v_s[:, :dseq].reshape(ndb, dseq, N_KV_HEADS, HEAD_DIM))
        outs[5].append(u_s[:, HIST_ROWS - POOL_HIST:])
    return (xp, xs) + tuple(jnp.stack(o) for o in outs)


def _moe_rows(hs, idxs, ranks, n_reals, cnt, moe_w):
    n_tok = sum(h.shape[0] for h in hs)
    n_assign = n_tok * TOP_K
    gather_quant = SC_WORKERS * SC_CHUNK
    n_blocks = -(-(n_assign + N_EXPERTS * (EXPERT_BLOCK - 1)) // EXPERT_BLOCK)
    n_blocks = -(-n_blocks // (gather_quant // EXPERT_BLOCK)) * (gather_quant // EXPERT_BLOCK)
    n_pick = -(-n_assign // gather_quant) * gather_quant
    e_ids = jnp.arange(N_EXPERTS, dtype=jnp.int32)

    def per_token(arrs):
        return jnp.concatenate(
            [jnp.transpose(a[:, :, :n], (1, 0, 2)).reshape(TOP_K, -1) for a, n in zip(arrs, n_reals)],
            axis=1)

    counts = cnt[:, 0].astype(jnp.int32)
    pcounts = (counts + EXPERT_BLOCK - 1) // EXPERT_BLOCK * EXPERT_BLOCK
    pend = jnp.cumsum(pcounts)
    pstart = pend - pcounts
    idx_all = per_token(idxs)
    dest = per_token(ranks) + jnp.sum(
        jnp.where(idx_all[None] == e_ids[:, None, None], pstart[:, None, None], 0), axis=0)
    dest = dest.reshape(-1)
    blk0 = jnp.arange(n_blocks, dtype=jnp.int32) * EXPERT_BLOCK
    block_e = jnp.minimum(jnp.sum((pend[None, :] <= blk0[:, None]).astype(jnp.int32), axis=1),
                          N_EXPERTS - 1)
    block_rows = jnp.clip(jnp.sum(jnp.where(block_e[:, None] == e_ids[None, :],
                                             (pstart + counts)[None, :], 0), axis=1) - blk0,
                          0, EXPERT_BLOCK).astype(jnp.int32)

    x_sorted = _scatter_rows(hs, dest, n_blocks * EXPERT_BLOCK)
    y_sorted = _expert_call(block_e, block_rows, x_sorted, *moe_w)
    return _gather_rows(y_sorted, jnp.pad(dest, (0, n_pick - n_assign)))
```

```python
import functools
import math

import numpy as np
import jax
import jax.numpy as jnp
from jax import lax
from jax.experimental import pallas as pl
from jax.experimental.pallas import tpu as pltpu
from jax.experimental.pallas import tpu_sc as plsc

D_MODEL = 1024
CHUNK = 64
POOL_WIDTH = 512
POOL_WINDOWS = (2, 4, 8, 16)
POOL_GROUP = 128
POOL_HIST = 15
ATTN_WIDTH = 512
HEAD_DIM = 64
N_HEADS = 8
N_KV_HEADS = 2
GROUP = 4
KV_WIDTH = 128
WINDOW = 128
NUM_BUCKETS = 32
MAX_DISTANCE = 128
PAST_LEN = 2048
N_EXPERTS = 32
TOP_K = 4
SWIGLU_LIMIT = 7.0
SWIGLU_ALPHA = 1.702
EPS = 1e-5
NEG_INF = -1e30
ATTN_SCALE = HEAD_DIM ** -0.5

PAIR = 2 * CHUNK
BAND = PAIR + WINDOW
HIST_ROWS = 16
GATE_COLS = 8
MIX_TILE = 512
EXPERT_BLOCK = 512
COMBINE_TILE = 256
SC_WORKERS = 32
SC_CHUNK = 64
VMEM_LIMIT = 56 * 1024 * 1024


def _pack_bf16_pair(a, b):
    ab = lax.bitcast_convert_type(a.astype(jnp.bfloat16).astype(jnp.float32), jnp.uint32)
    bb = lax.bitcast_convert_type(b.astype(jnp.bfloat16).astype(jnp.float32), jnp.uint32)
    return (ab >> 16) | (bb & jnp.uint32(0xFFFF0000))


def _unpack_bf16_pair(w):
    a = lax.bitcast_convert_type(w << 16, jnp.float32).astype(jnp.bfloat16)
    b = lax.bitcast_convert_type(w & jnp.uint32(0xFFFF0000), jnp.float32).astype(jnp.bfloat16)
    return a, b


def _mixer_kernel(x_ref, kh_ref, vh_ref, uh_ref, cnt_in_ref,
                  g1_ref, wukv_ref, wqvt_ref, qg_ref, kg_ref, bd_ref, bias_ref, sink_ref,
                  wpool_ref, pscale_ref, wout_ref, g2_ref, wr_ref, br_ref, tri_ref,
                  x1_ref, hp_ref, idx_ref, rank_ref, gate_ref, cnt_ref, ko_ref, vo_ref, uo_ref,
                  qt_s, kb_s, vt_s, ub_s, mix_s, cnt_s,
                  *, tile, n_valid, pos0, mask_first):
    b = pl.program_id(0)
    s = pl.program_id(1)
    bf16 = jnp.bfloat16
    f32 = jnp.float32

    @pl.when((b == 0) & (s == 0))
    def _():
        cnt_s[...] = cnt_in_ref[...]

    @pl.when(s == 0)
    def _():
        kb_s[0:WINDOW, :] = kh_ref[...].astype(bf16)
        vt_s[:, 0:WINDOW] = jnp.transpose(vh_ref[...]).astype(bf16)
        ub_s[0:HIST_ROWS, :] = uh_ref[...]

    @pl.when(s > 0)
    def _():
        kb_s[0:WINDOW, :] = kb_s[tile:tile + WINDOW, :]
        vt_s[:, 0:WINDOW] = vt_s[:, tile:tile + WINDOW]
        ub_s[0:HIST_ROWS, :] = ub_s[tile:tile + HIST_ROWS, :]

    x = x_ref[...]
    xn = (x * lax.rsqrt(jnp.mean(x * x, axis=-1, keepdims=True) + EPS) * g1_ref[...]).astype(bf16)
    z = jnp.dot(xn, wukv_ref[...], preferred_element_type=f32)
    zt = lax.dot_general(wqvt_ref[...], xn, (((1,), (1,)), ((), ())),
                         preferred_element_type=f32)
    u = z[:, 0:POOL_WIDTH]
    kz = z[:, POOL_WIDTH:POOL_WIDTH + KV_WIDTH]
    v = z[:, POOL_WIDTH + KV_WIDTH:]

    ksq = kz * kz
    kss = jnp.dot(ksq.astype(bf16), bd_ref[...], preferred_element_type=f32)
    kn = kz * lax.rsqrt(kss * (1.0 / HEAD_DIM) + EPS) * kg_ref[...]
    kb_s[WINDOW:, :] = kn.astype(bf16)
    vt_s[:, WINDOW:] = zt[ATTN_WIDTH:, :].astype(bf16)
    ub_s[HIST_ROWS:, :] = u

    row0 = max(n_valid, WINDOW) - WINDOW
    ko_ref[...] = kn[row0:row0 + WINDOW, :]
    vo_ref[...] = v[row0:row0 + WINDOW, :]
    uo_ref[...] = u[n_valid - HIST_ROWS:n_valid, :]

    for hd in range(N_HEADS):
        qh = zt[hd * HEAD_DIM:(hd + 1) * HEAD_DIM, :]
        ss = jnp.sum(qh * qh, axis=0, keepdims=True)
        qn = qh * (lax.rsqrt(ss * (1.0 / HEAD_DIM) + EPS) * ATTN_SCALE) * qg_ref[...]
        qt_s[hd * HEAD_DIM:(hd + 1) * HEAD_DIM, :] = qn.astype(bf16)

    pos = pos0 + s * tile + lax.broadcasted_iota(jnp.int32, (tile, 1), 0)
    for g, w in enumerate(POOL_WINDOWS):
        e = ub_s[:, g * POOL_GROUP:(g + 1) * POOL_GROUP]
        acc = e
        for lvl in range(g + 1):
            acc = acc + pltpu.roll(acc, 2 ** lvl, axis=0)
        inv_cnt = 1.0 / jnp.minimum(pos + 1, w).astype(f32)
        d = (acc[HIST_ROWS:, :] * inv_cnt - e[HIST_ROWS:, :]).astype(bf16)
        y = jnp.dot(d, wpool_ref[g], preferred_element_type=f32)
        y = y * pscale_ref[:, g * POOL_GROUP:(g + 1) * POOL_GROUP]
        mix_s[:, g * POOL_GROUP:(g + 1) * POOL_GROUP] = y.astype(bf16)

    zeros_q = jnp.zeros((HEAD_DIM, GROUP * PAIR), bf16)
    for p in range(tile // PAIR):
        k_band = kb_s[p * PAIR:p * PAIR + BAND, :]
        o_parts = []
        for h in range(N_KV_HEADS):
            qcat = jnp.concatenate(
                [qt_s[(h * GROUP + g) * HEAD_DIM:(h * GROUP + g + 1) * HEAD_DIM,
                      p * PAIR:(p + 1) * PAIR] for g in range(GROUP)], axis=1)
            rhs = jnp.concatenate([qcat, zeros_q] if h == 0 else [zeros_q, qcat], axis=0)
            st = jnp.dot(k_band, rhs, preferred_element_type=f32) + bias_ref[h]
            if mask_first and p == 0:
                krow = lax.broadcasted_iota(jnp.int32, (BAND, 1), 0)
                st = jnp.where((krow >= WINDOW) | (s > 0), st, NEG_INF)
            sink = sink_ref[h:h + 1, :]
            m = jnp.maximum(jnp.max(st, axis=0, keepdims=True), sink)
            ex = jnp.exp(st - m)
            den = jnp.sum(ex, axis=0, keepdims=True) + jnp.exp(sink - m)
            v_band = vt_s[h * HEAD_DIM:(h + 1) * HEAD_DIM, p * PAIR:p * PAIR + BAND]
            ot = jnp.dot(v_band, ex.astype(bf16), preferred_element_type=f32) / den
            for g in range(GROUP):
                o_parts.append(ot[:, g * PAIR:(g + 1) * PAIR])
        o_all = jnp.concatenate(o_parts, axis=0)
        mix_s[p * PAIR:(p + 1) * PAIR, POOL_WIDTH:] = jnp.transpose(o_all).astype(bf16)

    x1 = x + jnp.dot(mix_s[...], wout_ref[...], preferred_element_type=f32)
    x1_ref[...] = x1

    hn = x1 * lax.rsqrt(jnp.mean(x1 * x1, axis=-1, keepdims=True) + EPS) * g2_ref[...]
    hp_ref[...] = _pack_bf16_pair(hn[0:n_valid, 0:D_MODEL // 2], hn[0:n_valid, D_MODEL // 2:])
    h_hi = hn.astype(bf16)
    h_lo = (hn - h_hi.astype(f32)).astype(bf16)
    parts = jnp.transpose(jnp.dot(h_hi, wr_ref[...], preferred_element_type=f32)
                          + jnp.dot(h_lo, wr_ref[...], preferred_element_type=f32))
    lt = parts[0:N_EXPERTS, :] + parts[N_EXPERTS:2 * N_EXPERTS, :] + br_ref[...]

    eidx = lax.broadcasted_iota(jnp.int32, (N_EXPERTS, tile), 0).astype(f32)
    vals, hots = [], []
    for j in range(TOP_K):
        m = jnp.max(lt, axis=0, keepdims=True)
        sel = jnp.min(jnp.where(lt == m, eidx, float(N_EXPERTS)), axis=0, keepdims=True)
        hot = eidx == sel
        lt = jnp.where(hot, -jnp.inf, lt)
        idx_ref[j:j + 1, :] = sel.astype(jnp.int32)
        vals.append(m)
        hots.append(hot)
    exps = [jnp.exp(vv - vals[0]) for vv in vals]
    esum = exps[0] + exps[1] + exps[2] + exps[3]
    grow = lax.broadcasted_iota(jnp.int32, (GATE_COLS, tile), 0)
    gmat = jnp.zeros((GATE_COLS, tile), f32)
    for j in range(TOP_K):
        gmat = jnp.where(grow == j, exps[j] / esum, gmat)
    gmat = jnp.concatenate([gmat, jnp.zeros((128 - GATE_COLS, tile), f32)], axis=0)
    gate_ref[...] = jnp.transpose(gmat)[0:n_valid, 0:GATE_COLS]

    chosen_f = sum(jnp.where(hot, 1.0, 0.0) for hot in hots)
    if n_valid < tile:
        lane = lax.broadcasted_iota(jnp.int32, (N_EXPERTS, tile), 1)
        chosen_f = jnp.where(lane < n_valid, chosen_f, 0.0)
    before = jnp.dot(chosen_f.astype(bf16), tri_ref[...], preferred_element_type=f32)
    base = before + cnt_s[:, 0:1]
    for j in range(TOP_K):
        rank_ref[j:j + 1, :] = jnp.sum(jnp.where(hots[j], base, 0.0), axis=0,
                                       keepdims=True).astype(jnp.int32)
    cnt_new = cnt_s[...] + jnp.sum(chosen_f, axis=1, keepdims=True)
    cnt_s[...] = cnt_new
    cnt_ref[...] = cnt_new


def _mixer_call(x, k_hist, v_hist, u_hist, cnt_in, consts, *, stream0, tile, n_valid, pos0,
                mask_first):
    seq = x.shape[1]
    nb = k_hist.shape[0]
    n_tiles = seq // tile
    f32 = jnp.float32
    assert n_valid % 8 == 0

    def full(a):
        nd = a.ndim
        return pl.BlockSpec(a.shape, lambda b, s, _nd=nd: (0,) * _nd)

    in_specs = [
        pl.BlockSpec((None, tile, D_MODEL), lambda b, s: (stream0 + b, s, 0)),
        pl.BlockSpec((None, WINDOW, KV_WIDTH), lambda b, s: (b, 0, 0)),
        pl.BlockSpec((None, WINDOW, KV_WIDTH), lambda b, s: (b, 0, 0)),
        pl.BlockSpec((None, HIST_ROWS, POOL_WIDTH), lambda b, s: (b, 0, 0)),
        full(cnt_in),
    ] + [full(c) for c in consts]
    out_shape = [
        jax.ShapeDtypeStruct((nb, seq, D_MODEL), f32),
        jax.ShapeDtypeStruct((nb * n_tiles * n_valid, D_MODEL // 2), jnp.uint32),
        jax.ShapeDtypeStruct((nb, TOP_K, seq), jnp.int32),
        jax.ShapeDtypeStruct((nb, TOP_K, seq), jnp.int32),
        jax.ShapeDtypeStruct((nb * n_tiles * n_valid, GATE_COLS), f32),
        jax.ShapeDtypeStruct((N_EXPERTS, 128), f32),
        jax.ShapeDtypeStruct((nb, WINDOW, KV_WIDTH), f32),
        jax.ShapeDtypeStruct((nb, WINDOW, KV_WIDTH), f32),
        jax.ShapeDtypeStruct((nb, HIST_ROWS, POOL_WIDTH), f32),
    ]
    out_specs = [
        pl.BlockSpec((None, tile, D_MODEL), lambda b, s: (b, s, 0)),
        pl.BlockSpec((n_valid, D_MODEL // 2), lambda b, s: (b * n_tiles + s, 0)),
        pl.BlockSpec((None, TOP_K, tile), lambda b, s: (b, 0, s)),
        pl.BlockSpec((None, TOP_K, tile), lambda b, s: (b, 0, s)),
        pl.BlockSpec((n_valid, GATE_COLS), lambda b, s: (b * n_tiles + s, 0)),
        pl.BlockSpec((N_EXPERTS, 128), lambda b, s: (0, 0)),
        pl.BlockSpec((None, WINDOW, KV_WIDTH), lambda b, s: (b, 0, 0)),
        pl.BlockSpec((None, WINDOW, KV_WIDTH), lambda b, s: (b, 0, 0)),
        pl.BlockSpec((None, HIST_ROWS, POOL_WIDTH), lambda b, s: (b, 0, 0)),
    ]
    scratch = [
        pltpu.VMEM((ATTN_WIDTH, tile), jnp.bfloat16),
        pltpu.VMEM((WINDOW + tile, KV_WIDTH), jnp.bfloat16),
        pltpu.VMEM((KV_WIDTH, WINDOW + tile), jnp.bfloat16),
        pltpu.VMEM((HIST_ROWS + tile, POOL_WIDTH), f32),
        pltpu.VMEM((tile, D_MODEL), jnp.bfloat16),
        pltpu.VMEM((N_EXPERTS, 128), f32),
    ]
    kern = functools.partial(_mixer_kernel, tile=tile, n_valid=n_valid, pos0=pos0,
                             mask_first=mask_first)
    return pl.pallas_call(
        kern,
        grid=(nb, n_tiles),
        in_specs=in_specs,
        out_specs=out_specs,
        out_shape=out_shape,
        scratch_shapes=scratch,
        compiler_params=pltpu.CompilerParams(
            dimension_semantics=("arbitrary", "arbitrary"),
            vmem_limit_bytes=VMEM_LIMIT),
        name="mixer",
    )(x, k_hist, v_hist, u_hist, cnt_in, *consts)


def _expert_kernel(be_ref, nv_ref, xs_ref, wg_ref, bg_ref, wu_ref, bu_ref, wd_ref, bd_ref, ys_ref,
                   wg_s, wu_s, wd_s):
    i = pl.program_id(0)
    n_rows = nv_ref[i]
    bf16 = jnp.bfloat16

    @pl.when((n_rows > 0) & ((i == 0) | (be_ref[i] != be_ref[jnp.maximum(i - 1, 0)])))
    def _():
        wg_s[...] = wg_ref[...].astype(bf16)
        wu_s[...] = wu_ref[...].astype(bf16)
        wd_s[...] = wd_ref[...].astype(bf16)

    @pl.when(n_rows > 0)
    def _():
        f32 = jnp.float32
        half = D_MODEL // 2
        row = lax.broadcasted_iota(jnp.int32, (EXPERT_BLOCK, 1), 0)
        words = jnp.where(row < n_rows, xs_ref[...], jnp.uint32(0))
        xa, xb = _unpack_bf16_pair(words)
        a = (jnp.dot(xa, wg_s[0:half, :], preferred_element_type=f32)
             + jnp.dot(xb, wg_s[half:, :], preferred_element_type=f32) + bg_ref[...])
        bb = (jnp.dot(xa, wu_s[0:half, :], preferred_element_type=f32)
              + jnp.dot(xb, wu_s[half:, :], preferred_element_type=f32) + bu_ref[...])
        a = jnp.minimum(a, SWIGLU_LIMIT)
        bb = jnp.clip(bb, -SWIGLU_LIMIT, SWIGLU_LIMIT)
        act = a * (1.0 / (1.0 + jnp.exp(-SWIGLU_ALPHA * a))) * (bb + 1.0)
        y = jnp.dot(act.astype(bf16), wd_s[...], preferred_element_type=f32) + bd_ref[...]
        ys_ref[...] = _pack_bf16_pair(y[:, 0:half], y[:, half:])

    @pl.when(n_rows == 0)
    def _():
        ys_ref[...] = jnp.zeros_like(ys_ref)


def _expert_call(block_e, block_rows, xs, wg, bg, wu, bu, wd, bd):
    n_slots = xs.shape[0]
    n_blocks = n_slots // EXPERT_BLOCK
    w_spec = pl.BlockSpec((None, D_MODEL, D_MODEL), lambda i, be, nv: (be[i], 0, 0))
    b_spec = pl.BlockSpec((None, 1, D_MODEL), lambda i, be, nv: (be[i], 0, 0))
    grid_spec = pltpu.PrefetchScalarGridSpec(
        num_scalar_prefetch=2,
        grid=(n_blocks,),
        in_specs=[pl.BlockSpec((EXPERT_BLOCK, D_MODEL // 2), lambda i, be, nv: (i, 0)),
                  w_spec, b_spec, w_spec, b_spec, w_spec, b_spec],
        out_specs=pl.BlockSpec((EXPERT_BLOCK, D_MODEL // 2), lambda i, be, nv: (i, 0)),
        scratch_shapes=[pltpu.VMEM((D_MODEL, D_MODEL), jnp.bfloat16)] * 3,
    )
    return pl.pallas_call(
        _expert_kernel,
        grid_spec=grid_spec,
        out_shape=jax.ShapeDtypeStruct((n_slots, D_MODEL // 2), jnp.uint32),
        compiler_params=pltpu.CompilerParams(
            dimension_semantics=("arbitrary",),
            vmem_limit_bytes=VMEM_LIMIT),
        name="experts",
    )(block_e, block_rows, xs, wg, bg, wu, bu, wd, bd)


def _combine_kernel(*refs, starts):
    o_ref = refs[-1]
    i = pl.program_id(0)
    half = D_MODEL // 2
    for k in range(len(starts) - 1):
        x1_ref, g_ref = refs[6 * k], refs[6 * k + 1]
        y_refs = refs[6 * k + 2:6 * k + 6]

        @pl.when((i >= starts[k]) & (i < starts[k + 1]))
        def _():
            g = g_ref[...]
            lo, hi = x1_ref[:, 0:half], x1_ref[:, half:]
            for j, y_ref in enumerate(y_refs):
                w = y_ref[...]
                gj = g[:, j:j + 1]
                lo = lo + gj * lax.bitcast_convert_type(w << 16, jnp.float32)
                hi = hi + gj * lax.bitcast_convert_type(w & jnp.uint32(0xFFFF0000), jnp.float32)
            o_ref[:, 0:half] = lo
            o_ref[:, half:] = hi


def _combine_call(groups, tile):
    starts = [0]
    for x1, _, _, _, _ in groups:
        starts.append(starts[-1] + x1.shape[0] // tile)
    in_specs, args = [], []
    for k, (x1, gates, picked, n_tok, tok0) in enumerate(groups):
        last = x1.shape[0] // tile - 1

        def local(i, _s=starts[k], _last=last):
            return jnp.clip(i - _s, 0, _last)

        in_specs += [pl.BlockSpec((tile, D_MODEL), lambda i, _f=local: (_f(i), 0)),
                     pl.BlockSpec((tile, GATE_COLS), lambda i, _f=local: (_f(i), 0))]
        for j in range(TOP_K):
            assert (j * n_tok + tok0) % tile == 0 and x1.shape[0] % tile == 0
            base = (j * n_tok + tok0) // tile
            in_specs.append(pl.BlockSpec((tile, D_MODEL // 2),
                                         lambda i, _f=local, _b=base: (_b + _f(i), 0)))
        args += [x1, gates, picked, picked, picked, picked]
    return pl.pallas_call(
        functools.partial(_combine_kernel, starts=tuple(starts)),
        grid=(starts[-1],),
        in_specs=in_specs,
        out_specs=pl.BlockSpec((tile, D_MODEL), lambda i: (i, 0)),
        out_shape=jax.ShapeDtypeStruct((starts[-1] * tile, D_MODEL), jnp.float32),
        compiler_params=pltpu.CompilerParams(dimension_semantics=("arbitrary",)),
        name="combine",
    )(*args)


def _gather_rows(table, idx):
    n = idx.shape[0]
    width = table.shape[1]
    per_worker = n // SC_WORKERS
    n_chunks = per_worker // SC_CHUNK
    mesh = plsc.VectorSubcoreMesh(core_axis_name="c", subcore_axis_name="s")

    @functools.partial(
        pl.kernel, mesh=mesh,
        out_type=jax.ShapeDtypeStruct((n, width), table.dtype),
        scratch_types=[pltpu.VMEM((SC_CHUNK,), jnp.int32),
                       pltpu.VMEM((SC_CHUNK, width), table.dtype),
                       pltpu.SemaphoreType.DMA],
        cost_estimate=pl.CostEstimate(flops=0, transcendentals=0, bytes_accessed=8 * n * width),
    )
    def gather(table_hbm, idx_hbm, out_hbm, idx_v, rows_v, sem):
        wid = lax.axis_index("s") * 2 + lax.axis_index("c")
        base = wid * per_worker

        @pl.loop(0, n_chunks)
        def _(i):
            off = base + i * SC_CHUNK
            pltpu.sync_copy(idx_hbm.at[pl.ds(off, SC_CHUNK)], idx_v)
            pltpu.async_copy(table_hbm.at[idx_v], rows_v, sem).wait()
            pltpu.sync_copy(rows_v, out_hbm.at[pl.ds(off, SC_CHUNK)])

    return gather(table, idx)


def _scatter_rows(srcs, dest, n_out):
    width = srcs[0].shape[1]
    starts = [0]
    for src in srcs:
        assert src.shape[0] % SC_CHUNK == 0
        starts.append(starts[-1] + src.shape[0] // SC_CHUNK)
    n_chunks = starts[-1]
    n = n_chunks * SC_CHUNK
    per_worker = -(-n_chunks // SC_WORKERS)
    mesh = plsc.VectorSubcoreMesh(core_axis_name="c", subcore_axis_name="s")

    @functools.partial(
        pl.kernel, mesh=mesh,
        out_type=jax.ShapeDtypeStruct((n_out, width), srcs[0].dtype),
        scratch_types=[pltpu.VMEM((SC_CHUNK,), jnp.int32)] * TOP_K
        + [pltpu.VMEM((SC_CHUNK, width), srcs[0].dtype)],
        cost_estimate=pl.CostEstimate(flops=0, transcendentals=0,
                                      bytes_accessed=4 * (1 + TOP_K) * n * width),
    )
    def scatter(*refs):
        src_hbms = refs[:len(srcs)]
        dest_hbm, out_hbm = refs[len(srcs)], refs[len(srcs) + 1]
        idx_vs, rows_v = refs[len(srcs) + 2:len(srcs) + 2 + TOP_K], refs[-1]
        wid = lax.axis_index("s") * 2 + lax.axis_index("c")

        @pl.loop(0, per_worker)
        def _(i):
            c = i * SC_WORKERS + wid
            for k, src_hbm in enumerate(src_hbms):

                @pl.when((c >= starts[k]) & (c < starts[k + 1]))
                def _():
                    pltpu.sync_copy(src_hbm.at[pl.ds((c - starts[k]) * SC_CHUNK, SC_CHUNK)], rows_v)
                    for j, idx_v in enumerate(idx_vs):
                        pltpu.sync_copy(dest_hbm.at[pl.ds(j * n + c * SC_CHUNK, SC_CHUNK)], idx_v)
                    for idx_v in idx_vs:
                        pltpu.sync_copy(rows_v, out_hbm.at[idx_v])

    return scatter(*srcs, dest)


def _t5_bucket_np(rel):
    half = NUM_BUCKETS // 2
    max_exact = half // 2
    n = np.abs(rel)
    nf = np.maximum(n, 1).astype(np.float32)
    large = max_exact + (np.log(nf / max_exact) / math.log(MAX_DISTANCE / max_exact)
                         * (half - max_exact)).astype(np.int32)
    large = np.minimum(large, half - 1)
    return np.where(rel > 0, half, 0) + np.where(n < max_exact, n, large)


def _bias_tables(rel_bias, visible):
    kap = np.arange(BAND)[:, None]
    rho = np.arange(PAIR)[None, :]
    bucket = _t5_bucket_np(kap - WINDOW - rho)
    rb = rel_bias.astype(jnp.float32)
    tab = jnp.zeros((BAND, PAIR, N_HEADS), jnp.float32)
    for bkt in range(NUM_BUCKETS):
        tab = jnp.where(jnp.asarray(bucket == bkt)[:, :, None], rb[bkt], tab)
    tab = jnp.where(jnp.asarray(visible)[:, :, None], tab, NEG_INF)
    tab = jnp.transpose(tab, (2, 0, 1)).reshape(N_KV_HEADS, GROUP, BAND, PAIR)
    return jnp.transpose(tab, (0, 2, 1, 3)).reshape(N_KV_HEADS, BAND, GROUP * PAIR)


def _mixer_consts(l, norm1_g, w_in, q_norm_g, k_norm_g, rel_bias, sinks, w_pool, pool_scale,
                  w_out, norm2_g, w_router, b_router, visible, tile):
    f32, bf16 = jnp.float32, jnp.bfloat16
    q_off, k_off, v_off = POOL_WIDTH, POOL_WIDTH + ATTN_WIDTH, POOL_WIDTH + ATTN_WIDTH + KV_WIDTH
    w = w_in[l]
    w_ukv = jnp.concatenate([w[:, :q_off], w[:, k_off:]], axis=1).astype(bf16)
    w_qvt = jnp.transpose(jnp.concatenate([w[:, q_off:k_off], w[:, v_off:]], axis=1)).astype(bf16)
    lane_head = np.arange(KV_WIDTH) // HEAD_DIM
    blockdiag = jnp.asarray(lane_head[:, None] == lane_head[None, :], bf16)
    sink_rows = jnp.repeat(sinks[l].astype(f32).reshape(N_KV_HEADS, GROUP), PAIR, axis=1)
    wr = w_router[l].astype(f32)
    wr_hi = wr.astype(bf16)
    wr_lo = (wr - wr_hi.astype(f32)).astype(bf16)
    wr_parts = jnp.pad(jnp.concatenate([wr_hi, wr_lo], axis=1), ((0, 0), (0, 128 - 2 * N_EXPERTS)))
    tri = jnp.asarray(np.arange(tile)[:, None] < np.arange(tile)[None, :], bf16)
    return [
        norm1_g[l].reshape(1, D_MODEL).astype(f32), w_ukv, w_qvt,
        q_norm_g[l].reshape(HEAD_DIM, 1).astype(f32),
        jnp.tile(k_norm_g[l].astype(f32), N_KV_HEADS).reshape(1, KV_WIDTH),
        blockdiag, _bias_tables(rel_bias, visible), sink_rows,
        w_pool[l].astype(bf16), pool_scale[l].reshape(1, POOL_WIDTH).astype(f32),
        w_out[l].astype(bf16), norm2_g[l].reshape(1, D_MODEL).astype(f32),
        wr_parts, b_router[l].reshape(N_EXPERTS, 1).astype(f32), tri,
    ]


def _visibility():
    kap = np.arange(BAND)[:, None]
    rho = np.arange(PAIR)[None, :]
    kc, qc = kap // CHUNK, rho // CHUNK
    prompt = (kc >= qc) & (kc <= qc + WINDOW // CHUNK)
    return prompt


def kernel(x_prompt, x_sample, cache_k, cache_v, state_pool, norm1_g, w_in, q_norm_g, k_norm_g,
           rel_bias, sinks, w_pool, pool_scale, w_out, norm2_g, w_router, b_router,
           w_gate, b_gate, w_up, b_up, w_down, b_down):
    f32, bf16 = jnp.float32, jnp.bfloat16
    depth = w_in.shape[0]
    nb, seq, _ = x_prompt.shape
    ndb, dseq, _ = x_sample.shape
    cache_len = cache_k.shape[2]
    assert seq % MIX_TILE == 0 and cache_len == WINDOW and HIST_ROWS <= dseq <= PAIR
    n_p, n_s = nb * seq, ndb * dseq
    n_tok = n_p + n_s
    assert n_p % COMBINE_TILE == 0 and n_tok % COMBINE_TILE == 0 and n_s % 8 == 0

    vis_prompt = _visibility()
    vis_sample = np.broadcast_to(np.arange(BAND)[:, None] < WINDOW + dseq, (BAND, PAIR))

    nb1 = nb // 4
    nb2 = nb - nb1
    assert nb1 > 0 and nb2 > 0 and (nb2 * seq) % COMBINE_TILE == 0

    xp, xs = x_prompt, x_sample
    outs = [[] for _ in range(6)]
    for l in range(depth):
        wl = (l, norm1_g, w_in, q_norm_g, k_norm_g, rel_bias, sinks, w_pool, pool_scale, w_out,
              norm2_g, w_router, b_router)
        consts_p = _mixer_consts(*wl, vis_prompt, MIX_TILE)
        moe_w = (w_gate[l], b_gate[l].reshape(N_EXPERTS, 1, D_MODEL).astype(f32),
                 w_up[l], b_up[l].reshape(N_EXPERTS, 1, D_MODEL).astype(f32),
                 w_down[l], b_down[l].reshape(N_EXPERTS, 1, D_MODEL).astype(f32))
        cnt0 = jnp.zeros((N_EXPERTS, 128), f32)

        def prompt_mixer(stream0, n_streams, cnt_in):
            zk = jnp.zeros((n_streams, WINDOW, KV_WIDTH), f32)
            zu = jnp.zeros((n_streams, HIST_ROWS, POOL_WIDTH), f32)
            return _mixer_call(xp, zk, zk, zu, cnt_in, consts_p, stream0=stream0, tile=MIX_TILE,
                               n_valid=MIX_TILE, pos0=0, mask_first=True)

        (x1_a, h_a, idx_a, rank_a, gate_a, cnt_a, k_a, v_a, u_a) = prompt_mixer(0, nb1, cnt0)
        picked_a = _moe_rows([h_a], [idx_a], [rank_a], [seq], cnt_a, moe_w)
        xs_pad = jnp.pad(xs, ((0, 0), (0, PAIR - dseq), (0, 0)))
        uh = jnp.pad(state_pool[l], ((0, 0), (HIST_ROWS - POOL_HIST, 0), (0, 0)))
        (xs1, h_s, idx_s, rank_s, gate_s, cnt_s, k_s, v_s, u_s) = _mixer_call(
            xs_pad, cache_k[l].reshape(ndb, WINDOW, KV_WIDTH),
            cache_v[l].reshape(ndb, WINDOW, KV_WIDTH), uh, cnt0,
            _mixer_consts(*wl, vis_sample, PAIR),
            stream0=0, tile=PAIR, n_valid=dseq, pos0=PAST_LEN, mask_first=False)
        (x1_b, h_b, idx_b, rank_b, gate_b, cnt_b, k_b, v_b, u_b) = prompt_mixer(nb1, nb2, cnt_s)
        picked_b = _moe_rows([h_b, h_s], [idx_b, idx_s], [rank_b, rank_s], [seq, dseq], cnt_b, moe_w)

        n_a, n_b = nb1 * seq, nb2 * seq
        xp = _combine_call(
            [(x1_a.reshape(n_a, D_MODEL), gate_a, picked_a, n_a, 0),
             (x1_b.reshape(n_b, D_MODEL), gate_b, picked_b, n_b + n_s, 0)],
            COMBINE_TILE).reshape(nb, seq, D_MODEL)
        xs = _combine_call(
            [(xs1[:, :dseq].reshape(n_s, D_MODEL), gate_s, picked_b, n_b + n_s, n_b)],
            COMBINE_TILE).reshape(ndb, dseq, D_MODEL)

        outs[0].append(jnp.concatenate([k_a, k_b]).reshape(nb, WINDOW, N_KV_HEADS, HEAD_DIM))
        outs[1].append(jnp.concatenate([v_a, v_b]).reshape(nb, WINDOW, N_KV_HEADS, HEAD_DIM))
        outs[2].append(jnp.concatenate([u_a, u_b])[:, HIST_ROWS - POOL_HIST:])
        outs[3].append(k_s[:, :dseq].reshape(ndb, dseq, N_KV_HEADS, HEAD_DIM))
        outs[4].append(v_s[:, :dseq].reshape(ndb, dseq, N_KV_HEADS, HEAD_DIM))
        outs[5].append(u_s[:, HIST_ROWS - POOL_HIST:])
    return (xp, xs) + tuple(jnp.stack(o) for o in outs)


def _moe_rows(hs, idxs, ranks, n_reals, cnt, moe_w):
    n_tok = sum(h.shape[0] for h in hs)
    n_assign = n_tok * TOP_K
    gather_quant = SC_WORKERS * SC_CHUNK
    n_blocks = -(-(n_assign + N_EXPERTS * (EXPERT_BLOCK - 1)) // EXPERT_BLOCK)
    n_blocks = -(-n_blocks // (gather_quant // EXPERT_BLOCK)) * (gather_quant // EXPERT_BLOCK)
    n_pick = -(-n_assign // gather_quant) * gather_quant
    e_ids = jnp.arange(N_EXPERTS, dtype=jnp.int32)

    def per_token(arrs):
        return jnp.concatenate(
            [jnp.transpose(a[:, :, :n], (1, 0, 2)).reshape(TOP_K, -1) for a, n in zip(arrs, n_reals)],
            axis=1)

    counts = cnt[:, 0].astype(jnp.int32)
    pcounts = (counts + EXPERT_BLOCK - 1) // EXPERT_BLOCK * EXPERT_BLOCK
    pend = jnp.cumsum(pcounts)
    pstart = pend - pcounts
    idx_all = per_token(idxs)
    dest = per_token(ranks) + jnp.sum(
        jnp.where(idx_all[None] == e_ids[:, None, None], pstart[:, None, None], 0), axis=0)
    dest = dest.reshape(-1)
    blk0 = jnp.arange(n_blocks, dtype=jnp.int32) * EXPERT_BLOCK
    block_e = jnp.minimum(jnp.sum((pend[None, :] <= blk0[:, None]).astype(jnp.int32), axis=1),
                          N_EXPERTS - 1)
    block_rows = jnp.clip(jnp.sum(jnp.where(block_e[:, None] == e_ids[None, :],
                                             (pstart + counts)[None, :], 0), axis=1) - blk0,
                          0, EXPERT_BLOCK).astype(jnp.int32)

    x_sorted = _scatter_rows(hs, dest, n_blocks * EXPERT_BLOCK)
    y_sorted = _expert_call(block_e, block_rows, x_sorted, *moe_w)
    return _gather_rows(y_sorted, jnp.pad(dest, (0, n_pick - n_assign)))
```

```python
import functools
import math

import numpy as np
import jax
import jax.numpy as jnp
from jax import lax
from jax.experimental import pallas as pl
from jax.experimental.pallas import tpu as pltpu
from jax.experimental.pallas import tpu_sc as plsc

D_MODEL = 1024
CHUNK = 64
POOL_WIDTH = 512
POOL_WINDOWS = (2, 4, 8, 16)
POOL_GROUP = 128
POOL_HIST = 15
ATTN_WIDTH = 512
HEAD_DIM = 64
N_HEADS = 8
N_KV_HEADS = 2
GROUP = 4
KV_WIDTH = 128
WINDOW = 128
NUM_BUCKETS = 32
MAX_DISTANCE = 128
PAST_LEN = 2048
N_EXPERTS = 32
TOP_K = 4
SWIGLU_LIMIT = 7.0
SWIGLU_ALPHA = 1.702
EPS = 1e-5
NEG_INF = -1e30
ATTN_SCALE = HEAD_DIM ** -0.5

PAIR = 2 * CHUNK
BAND = PAIR + WINDOW
HIST_ROWS = 16
GATE_COLS = 8
MIX_TILE = 512
EXPERT_BLOCK = 512
COMBINE_TILE = 256
SC_WORKERS = 32
SC_CHUNK = 64
VMEM_LIMIT = 56 * 1024 * 1024


def _pack_bf16_pair(a, b):
    ab = lax.bitcast_convert_type(a.astype(jnp.bfloat16).astype(jnp.float32), jnp.uint32)
    bb = lax.bitcast_convert_type(b.astype(jnp.bfloat16).astype(jnp.float32), jnp.uint32)
    return (ab >> 16) | (bb & jnp.uint32(0xFFFF0000))


def _unpack_bf16_pair(w):
    a = lax.bitcast_convert_type(w << 16, jnp.float32).astype(jnp.bfloat16)
    b = lax.bitcast_convert_type(w & jnp.uint32(0xFFFF0000), jnp.float32).astype(jnp.bfloat16)
    return a, b


def _mixer_kernel(x_ref, kh_ref, vh_ref, uh_ref, cnt_in_ref,
                  g1_ref, wukv_ref, wqvt_ref, qg_ref, kg_ref, bd_ref, bias_ref, sink_ref,
                  wpool_ref, pscale_ref, wout_ref, g2_ref, wr_ref, br_ref, tri_ref,
                  x1_ref, hp_ref, idx_ref, rank_ref, gate_ref, cnt_ref, ko_ref, vo_ref, uo_ref,
                  qt_s, kb_s, vt_s, ub_s, mix_s, cnt_s,
                  *, tile, n_valid, pos0, mask_first):
    b = pl.program_id(0)
    s = pl.program_id(1)
    bf16 = jnp.bfloat16
    f32 = jnp.float32

    @pl.when((b == 0) & (s == 0))
    def _():
        cnt_s[...] = cnt_in_ref[...]

    @pl.when(s == 0)
    def _():
        kb_s[0:WINDOW, :] = kh_ref[...].astype(bf16)
        vt_s[:, 0:WINDOW] = jnp.transpose(vh_ref[...]).astype(bf16)
        ub_s[0:HIST_ROWS, :] = uh_ref[...]

    @pl.when(s > 0)
    def _():
        kb_s[0:WINDOW, :] = kb_s[tile:tile + WINDOW, :]
        vt_s[:, 0:WINDOW] = vt_s[:, tile:tile + WINDOW]
        ub_s[0:HIST_ROWS, :] = ub_s[tile:tile + HIST_ROWS, :]

    x = x_ref[...]
    xn = (x * lax.rsqrt(jnp.mean(x * x, axis=-1, keepdims=True) + EPS) * g1_ref[...]).astype(bf16)
    z = jnp.dot(xn, wukv_ref[...], preferred_element_type=f32)
    zt = lax.dot_general(wqvt_ref[...], xn, (((1,), (1,)), ((), ())),
                         preferred_element_type=f32)
    u = z[:, 0:POOL_WIDTH]
    kz = z[:, POOL_WIDTH:POOL_WIDTH + KV_WIDTH]
    v = z[:, POOL_WIDTH + KV_WIDTH:]

    ksq = kz * kz
    kss = jnp.dot(ksq.astype(bf16), bd_ref[...], preferred_element_type=f32)
    kn = kz * lax.rsqrt(kss * (1.0 / HEAD_DIM) + EPS) * kg_ref[...]
    kb_s[WINDOW:, :] = kn.astype(bf16)
    vt_s[:, WINDOW:] = zt[ATTN_WIDTH:, :].astype(bf16)
    ub_s[HIST_ROWS:, :] = u

    row0 = max(n_valid, WINDOW) - WINDOW
    ko_ref[...] = kn[row0:row0 + WINDOW, :]
    vo_ref[...] = v[row0:row0 + WINDOW, :]
    uo_ref[...] = u[n_valid - HIST_ROWS:n_valid, :]

    for hd in range(N_HEADS):
        qh = zt[hd * HEAD_DIM:(hd + 1) * HEAD_DIM, :]
        ss = jnp.sum(qh * qh, axis=0, keepdims=True)
        qn = qh * (lax.rsqrt(ss * (1.0 / HEAD_DIM) + EPS) * ATTN_SCALE) * qg_ref[...]
        qt_s[hd * HEAD_DIM:(hd + 1) * HEAD_DIM, :] = qn.astype(bf16)

    pos = pos0 + s * tile + lax.broadcasted_iota(jnp.int32, (tile, 1), 0)
    for g, w in enumerate(POOL_WINDOWS):
        e = ub_s[:, g * POOL_GROUP:(g + 1) * POOL_GROUP]
        acc = e
        for lvl in range(g + 1):
            acc = acc + pltpu.roll(acc, 2 ** lvl, axis=0)
        inv_cnt = 1.0 / jnp.minimum(pos + 1, w).astype(f32)
        d = (acc[HIST_ROWS:, :] * inv_cnt - e[HIST_ROWS:, :]).astype(bf16)
        y = jnp.dot(d, wpool_ref[g], preferred_element_type=f32)
        y = y * pscale_ref[:, g * POOL_GROUP:(g + 1) * POOL_GROUP]
        mix_s[:, g * POOL_GROUP:(g + 1) * POOL_GROUP] = y.astype(bf16)

    zeros_q = jnp.zeros((HEAD_DIM, GROUP * PAIR), bf16)
    for p in range(tile // PAIR):
        k_band = kb_s[p * PAIR:p * PAIR + BAND, :]
        o_parts = []
        for h in range(N_KV_HEADS):
            qcat = jnp.concatenate(
                [qt_s[(h * GROUP + g) * HEAD_DIM:(h * GROUP + g + 1) * HEAD_DIM,
                      p * PAIR:(p + 1) * PAIR] for g in range(GROUP)], axis=1)
            rhs = jnp.concatenate([qcat, zeros_q] if h == 0 else [zeros_q, qcat], axis=0)
            st = jnp.dot(k_band, rhs, preferred_element_type=f32) + bias_ref[h]
            if mask_first and p == 0:
                krow = lax.broadcasted_iota(jnp.int32, (BAND, 1), 0)
                st = jnp.where((krow >= WINDOW) | (s > 0), st, NEG_INF)
            sink = sink_ref[h:h + 1, :]
            m = jnp.maximum(jnp.max(st, axis=0, keepdims=True), sink)
            ex = jnp.exp(st - m)
            den = jnp.sum(ex, axis=0, keepdims=True) + jnp.exp(sink - m)
            v_band = vt_s[h * HEAD_DIM:(h + 1) * HEAD_DIM, p * PAIR:p * PAIR + BAND]
            ot = jnp.dot(v_band, ex.astype(bf16), preferred_element_type=f32) / den
            for g in range(GROUP):
                o_parts.append(ot[:, g * PAIR:(g + 1) * PAIR])
        o_all = jnp.concatenate(o_parts, axis=0)
        mix_s[p * PAIR:(p + 1) * PAIR, POOL_WIDTH:] = jnp.transpose(o_all).astype(bf16)

    x1 = x + jnp.dot(mix_s[...], wout_ref[...], preferred_element_type=f32)
    x1_ref[...] = x1

    hn = x1 * lax.rsqrt(jnp.mean(x1 * x1, axis=-1, keepdims=True) + EPS) * g2_ref[...]
    hp_ref[...] = _pack_bf16_pair(hn[0:n_valid, 0:D_MODEL // 2], hn[0:n_valid, D_MODEL // 2:])
    h_hi = hn.astype(bf16)
    h_lo = (hn - h_hi.astype(f32)).astype(bf16)
    parts = jnp.transpose(jnp.dot(h_hi, wr_ref[...], preferred_element_type=f32)
                          + jnp.dot(h_lo, wr_ref[...], preferred_element_type=f32))
    lt = parts[0:N_EXPERTS, :] + parts[N_EXPERTS:2 * N_EXPERTS, :] + br_ref[...]

    eidx = lax.broadcasted_iota(jnp.int32, (N_EXPERTS, tile), 0).astype(f32)
    vals, hots = [], []
    for j in range(TOP_K):
        m = jnp.max(lt, axis=0, keepdims=True)
        sel = jnp.min(jnp.where(lt == m, eidx, float(N_EXPERTS)), axis=0, keepdims=True)
        hot = eidx == sel
        lt = jnp.where(hot, -jnp.inf, lt)
        idx_ref[j:j + 1, :] = sel.astype(jnp.int32)
        vals.append(m)
        hots.append(hot)
    exps = [jnp.exp(vv - vals[0]) for vv in vals]
    esum = exps[0] + exps[1] + exps[2] + exps[3]
    grow = lax.broadcasted_iota(jnp.int32, (GATE_COLS, tile), 0)
    gmat = jnp.zeros((GATE_COLS, tile), f32)
    for j in range(TOP_K):
        gmat = jnp.where(grow == j, exps[j] / esum, gmat)
    gmat = jnp.concatenate([gmat, jnp.zeros((128 - GATE_COLS, tile), f32)], axis=0)
    gate_ref[...] = jnp.transpose(gmat)[0:n_valid, 0:GATE_COLS]

    chosen_f = sum(jnp.where(hot, 1.0, 0.0) for hot in hots)
    if n_valid < tile:
        lane = lax.broadcasted_iota(jnp.int32, (N_EXPERTS, tile), 1)
        chosen_f = jnp.where(lane < n_valid, chosen_f, 0.0)
    before = jnp.dot(chosen_f.astype(bf16), tri_ref[...], preferred_element_type=f32)
    base = before + cnt_s[:, 0:1]
    for j in range(TOP_K):
        rank_ref[j:j + 1, :] = jnp.sum(jnp.where(hots[j], base, 0.0), axis=0,
                                       keepdims=True).astype(jnp.int32)
    cnt_new = cnt_s[...] + jnp.sum(chosen_f, axis=1, keepdims=True)
    cnt_s[...] = cnt_new
    cnt_ref[...] = cnt_new


def _mixer_call(x, k_hist, v_hist, u_hist, cnt_in, consts, *, stream0, tile, n_valid, pos0,
                mask_first):
    seq = x.shape[1]
    nb = k_hist.shape[0]
    n_tiles = seq // tile
    f32 = jnp.float32
    assert n_valid % 8 == 0

    def full(a):
        nd = a.ndim
        return pl.BlockSpec(a.shape, lambda b, s, _nd=nd: (0,) * _nd)

    in_specs = [
        pl.BlockSpec((None, tile, D_MODEL), lambda b, s: (stream0 + b, s, 0)),
        pl.BlockSpec((None, WINDOW, KV_WIDTH), lambda b, s: (b, 0, 0)),
        pl.BlockSpec((None, WINDOW, KV_WIDTH), lambda b, s: (b, 0, 0)),
        pl.BlockSpec((None, HIST_ROWS, POOL_WIDTH), lambda b, s: (b, 0, 0)),
        full(cnt_in),
    ] + [full(c) for c in consts]
    out_shape = [
        jax.ShapeDtypeStruct((nb, seq, D_MODEL), f32),
        jax.ShapeDtypeStruct((nb * n_tiles * n_valid, D_MODEL // 2), jnp.uint32),
        jax.ShapeDtypeStruct((nb, TOP_K, seq), jnp.int32),
        jax.ShapeDtypeStruct((nb, TOP_K, seq), jnp.int32),
        jax.ShapeDtypeStruct((nb * n_tiles * n_valid, GATE_COLS), f32),
        jax.ShapeDtypeStruct((N_EXPERTS, 128), f32),
        jax.ShapeDtypeStruct((nb, WINDOW, KV_WIDTH), f32),
        jax.ShapeDtypeStruct((nb, WINDOW, KV_WIDTH), f32),
        jax.ShapeDtypeStruct((nb, HIST_ROWS, POOL_WIDTH), f32),
    ]
    out_specs = [
        pl.BlockSpec((None, tile, D_MODEL), lambda b, s: (b, s, 0)),
        pl.BlockSpec((n_valid, D_MODEL // 2), lambda b, s: (b * n_tiles + s, 0)),
        pl.BlockSpec((None, TOP_K, tile), lambda b, s: (b, 0, s)),
        pl.BlockSpec((None, TOP_K, tile), lambda b, s: (b, 0, s)),
        pl.BlockSpec((n_valid, GATE_COLS), lambda b, s: (b * n_tiles + s, 0)),
        pl.BlockSpec((N_EXPERTS, 128), lambda b, s: (0, 0)),
        pl.BlockSpec((None, WINDOW, KV_WIDTH), lambda b, s: (b, 0, 0)),
        pl.BlockSpec((None, WINDOW, KV_WIDTH), lambda b, s: (b, 0, 0)),
        pl.BlockSpec((None, HIST_ROWS, POOL_WIDTH), lambda b, s: (b, 0, 0)),
    ]
    scratch = [
        pltpu.VMEM((ATTN_WIDTH, tile), jnp.bfloat16),
        pltpu.VMEM((WINDOW + tile, KV_WIDTH), jnp.bfloat16),
        pltpu.VMEM((KV_WIDTH, WINDOW + tile), jnp.bfloat16),
        pltpu.VMEM((HIST_ROWS + tile, POOL_WIDTH), f32),
        pltpu.VMEM((tile, D_MODEL), jnp.bfloat16),
        pltpu.VMEM((N_EXPERTS, 128), f32),
    ]
    kern = functools.partial(_mixer_kernel, tile=tile, n_valid=n_valid, pos0=pos0,
                             mask_first=mask_first)
    return pl.pallas_call(
        kern,
        grid=(nb, n_tiles),
        in_specs=in_specs,
        out_specs=out_specs,
        out_shape=out_shape,
        scratch_shapes=scratch,
        compiler_params=pltpu.CompilerParams(
            dimension_semantics=("arbitrary", "arbitrary"),
            vmem_limit_bytes=VMEM_LIMIT),
        name="mixer",
    )(x, k_hist, v_hist, u_hist, cnt_in, *consts)


def _expert_kernel(be_ref, nv_ref, nx_ref, xs_ref, wg_hbm, bg_ref, wu_hbm, bu_ref, wd_hbm, bd_ref,
                   ys_ref, stage_s, wg_s, wu_s, wd_s, sems):
    i = pl.program_id(0)
    n_rows = nv_ref[i]
    expert = be_ref[i]
    bf16 = jnp.bfloat16

    def weight_copies(e):
        return [pltpu.make_async_copy(w_hbm.at[e], stage_s.at[k], sems.at[k])
                for k, w_hbm in enumerate((wg_hbm, wu_hbm, wd_hbm))]

    @pl.when((i == 0) & (n_rows > 0))
    def _():
        for copy in weight_copies(expert):
            copy.start()

    @pl.when((n_rows > 0) & ((i == 0) | (expert != be_ref[jnp.maximum(i - 1, 0)])))
    def _():
        for copy in weight_copies(expert):
            copy.wait()
        wg_s[...] = stage_s[0].astype(bf16)
        wu_s[...] = stage_s[1].astype(bf16)
        wd_s[...] = stage_s[2].astype(bf16)

        @pl.when(nx_ref[i] != expert)
        def _():
            for copy in weight_copies(nx_ref[i]):
                copy.start()

    def ffn(rows):
        f32 = jnp.float32
        half = D_MODEL // 2
        row = lax.broadcasted_iota(jnp.int32, (rows, 1), 0)
        words = jnp.where(row < n_rows, xs_ref[0:rows, :], jnp.uint32(0))
        xa, xb = _unpack_bf16_pair(words)
        a = (jnp.dot(xa, wg_s[0:half, :], preferred_element_type=f32)
             + jnp.dot(xb, wg_s[half:, :], preferred_element_type=f32) + bg_ref[...])
        bb = (jnp.dot(xa, wu_s[0:half, :], preferred_element_type=f32)
              + jnp.dot(xb, wu_s[half:, :], preferred_element_type=f32) + bu_ref[...])
        a = jnp.minimum(a, SWIGLU_LIMIT)
        bb = jnp.clip(bb, -SWIGLU_LIMIT, SWIGLU_LIMIT)
        act = a * (1.0 / (1.0 + jnp.exp(-SWIGLU_ALPHA * a))) * (bb + 1.0)
        y = jnp.dot(act.astype(bf16), wd_s[...], preferred_element_type=f32) + bd_ref[...]
        ys_ref[0:rows, :] = _pack_bf16_pair(y[:, 0:half], y[:, half:])
        if rows < EXPERT_BLOCK:
            ys_ref[rows:, :] = jnp.zeros((EXPERT_BLOCK - rows, half), jnp.uint32)

    @pl.when(n_rows > EXPERT_BLOCK // 2)
    def _():
        ffn(EXPERT_BLOCK)

    @pl.when((n_rows > 0) & (n_rows <= EXPERT_BLOCK // 2))
    def _():
        ffn(EXPERT_BLOCK // 2)

    @pl.when(n_rows == 0)
    def _():
        ys_ref[...] = jnp.zeros_like(ys_ref)


def _expert_call(block_e, block_rows, block_next, xs, wg, bg, wu, bu, wd, bd):
    n_slots = xs.shape[0]
    n_blocks = n_slots // EXPERT_BLOCK
    w_spec = pl.BlockSpec(memory_space=pl.ANY)
    b_spec = pl.BlockSpec((None, 1, D_MODEL), lambda i, be, nv, nx: (be[i], 0, 0))
    grid_spec = pltpu.PrefetchScalarGridSpec(
        num_scalar_prefetch=3,
        grid=(n_blocks,),
        in_specs=[pl.BlockSpec((EXPERT_BLOCK, D_MODEL // 2), lambda i, be, nv, nx: (i, 0)),
                  w_spec, b_spec, w_spec, b_spec, w_spec, b_spec],
        out_specs=pl.BlockSpec((EXPERT_BLOCK, D_MODEL // 2), lambda i, be, nv, nx: (i, 0)),
        scratch_shapes=[pltpu.VMEM((3, D_MODEL, D_MODEL), jnp.float32)]
        + [pltpu.VMEM((D_MODEL, D_MODEL), jnp.bfloat16)] * 3
        + [pltpu.SemaphoreType.DMA((3,))],
    )
    return pl.pallas_call(
        _expert_kernel,
        grid_spec=grid_spec,
        out_shape=jax.ShapeDtypeStruct((n_slots, D_MODEL // 2), jnp.uint32),
        compiler_params=pltpu.CompilerParams(
            dimension_semantics=("arbitrary",),
            vmem_limit_bytes=VMEM_LIMIT),
        name="experts",
    )(block_e, block_rows, block_next, xs, wg, bg, wu, bu, wd, bd)


def _combine_kernel(*refs, starts):
    o_ref = refs[-1]
    i = pl.program_id(0)
    half = D_MODEL // 2
    for k in range(len(starts) - 1):
        x1_ref, g_ref = refs[6 * k], refs[6 * k + 1]
        y_refs = refs[6 * k + 2:6 * k + 6]

        @pl.when((i >= starts[k]) & (i < starts[k + 1]))
        def _():
            g = g_ref[...]
            lo, hi = x1_ref[:, 0:half], x1_ref[:, half:]
            for j, y_ref in enumerate(y_refs):
                w = y_ref[...]
                gj = g[:, j:j + 1]
                lo = lo + gj * lax.bitcast_convert_type(w << 16, jnp.float32)
                hi = hi + gj * lax.bitcast_convert_type(w & jnp.uint32(0xFFFF0000), jnp.float32)
            o_ref[:, 0:half] = lo
            o_ref[:, half:] = hi


def _combine_call(groups, tile):
    starts = [0]
    for x1, _, _, _, _ in groups:
        starts.append(starts[-1] + x1.shape[0] // tile)
    in_specs, args = [], []
    for k, (x1, gates, picked, n_tok, tok0) in enumerate(groups):
        last = x1.shape[0] // tile - 1

        def local(i, _s=starts[k], _last=last):
            return jnp.clip(i - _s, 0, _last)

        in_specs += [pl.BlockSpec((tile, D_MODEL), lambda i, _f=local: (_f(i), 0)),
                     pl.BlockSpec((tile, GATE_COLS), lambda i, _f=local: (_f(i), 0))]
        for j in range(TOP_K):
            assert (j * n_tok + tok0) % tile == 0 and x1.shape[0] % tile == 0
            base = (j * n_tok + tok0) // tile
            in_specs.append(pl.BlockSpec((tile, D_MODEL // 2),
                                         lambda i, _f=local, _b=base: (_b + _f(i), 0)))
        args += [x1, gates, picked, picked, picked, picked]
    return pl.pallas_call(
        functools.partial(_combine_kernel, starts=tuple(starts)),
        grid=(starts[-1],),
        in_specs=in_specs,
        out_specs=pl.BlockSpec((tile, D_MODEL), lambda i: (i, 0)),
        out_shape=jax.ShapeDtypeStruct((starts[-1] * tile, D_MODEL), jnp.float32),
        compiler_params=pltpu.CompilerParams(dimension_semantics=("arbitrary",)),
        name="combine",
    )(*args)


def _gather_rows(table, idx):
    n = idx.shape[0]
    width = table.shape[1]
    per_worker = n // SC_WORKERS
    n_chunks = per_worker // SC_CHUNK
    mesh = plsc.VectorSubcoreMesh(core_axis_name="c", subcore_axis_name="s")

    @functools.partial(
        pl.kernel, mesh=mesh,
        out_type=jax.ShapeDtypeStruct((n, width), table.dtype),
        scratch_types=[pltpu.VMEM((SC_CHUNK,), jnp.int32),
                       pltpu.VMEM((SC_CHUNK, width), table.dtype),
                       pltpu.SemaphoreType.DMA],
        cost_estimate=pl.CostEstimate(flops=0, transcendentals=0, bytes_accessed=8 * n * width),
    )
    def gather(table_hbm, idx_hbm, out_hbm, idx_v, rows_v, sem):
        wid = lax.axis_index("s") * 2 + lax.axis_index("c")
        base = wid * per_worker

        @pl.loop(0, n_chunks)
        def _(i):
            off = base + i * SC_CHUNK
            pltpu.sync_copy(idx_hbm.at[pl.ds(off, SC_CHUNK)], idx_v)
            pltpu.async_copy(table_hbm.at[idx_v], rows_v, sem).wait()
            pltpu.sync_copy(rows_v, out_hbm.at[pl.ds(off, SC_CHUNK)])

    return gather(table, idx)


def _scatter_rows(srcs, dest, n_out):
    width = srcs[0].shape[1]
    starts = [0]
    for src in srcs:
        assert src.shape[0] % SC_CHUNK == 0
        starts.append(starts[-1] + src.shape[0] // SC_CHUNK)
    n_chunks = starts[-1]
    n = n_chunks * SC_CHUNK
    per_worker = -(-n_chunks // SC_WORKERS)
    mesh = plsc.VectorSubcoreMesh(core_axis_name="c", subcore_axis_name="s")

    @functools.partial(
        pl.kernel, mesh=mesh,
        out_type=jax.ShapeDtypeStruct((n_out, width), srcs[0].dtype),
        scratch_types=[pltpu.VMEM((SC_CHUNK,), jnp.int32)] * TOP_K
        + [pltpu.VMEM((SC_CHUNK, width), srcs[0].dtype)],
        cost_estimate=pl.CostEstimate(flops=0, transcendentals=0,
                                      bytes_accessed=4 * (1 + TOP_K) * n * width),
    )
    def scatter(*refs):
        src_hbms = refs[:len(srcs)]
        dest_hbm, out_hbm = refs[len(srcs)], refs[len(srcs) + 1]
        idx_vs, rows_v = refs[len(srcs) + 2:len(srcs) + 2 + TOP_K], refs[-1]
        wid = lax.axis_index("s") * 2 + lax.axis_index("c")

        @pl.loop(0, per_worker)
        def _(i):
            c = i * SC_WORKERS + wid
            for k, src_hbm in enumerate(src_hbms):

                @pl.when((c >= starts[k]) & (c < starts[k + 1]))
                def _():
                    pltpu.sync_copy(src_hbm.at[pl.ds((c - starts[k]) * SC_CHUNK, SC_CHUNK)], rows_v)
                    for j, idx_v in enumerate(idx_vs):
                        pltpu.sync_copy(dest_hbm.at[pl.ds(j * n + c * SC_CHUNK, SC_CHUNK)], idx_v)
                    for idx_v in idx_vs:
                        pltpu.sync_copy(rows_v, out_hbm.at[idx_v])

    return scatter(*srcs, dest)


def _t5_bucket_np(rel):
    half = NUM_BUCKETS // 2
    max_exact = half // 2
    n = np.abs(rel)
    nf = np.maximum(n, 1).astype(np.float32)
    large = max_exact + (np.log(nf / max_exact) / math.log(MAX_DISTANCE / max_exact)
                         * (half - max_exact)).astype(np.int32)
    large = np.minimum(large, half - 1)
    return np.where(rel > 0, half, 0) + np.where(n < max_exact, n, large)


def _bias_tables(rel_bias, visible):
    kap = np.arange(BAND)[:, None]
    rho = np.arange(PAIR)[None, :]
    bucket = _t5_bucket_np(kap - WINDOW - rho)
    rb = rel_bias.astype(jnp.float32)
    tab = jnp.zeros((BAND, PAIR, N_HEADS), jnp.float32)
    for bkt in range(NUM_BUCKETS):
        tab = jnp.where(jnp.asarray(bucket == bkt)[:, :, None], rb[bkt], tab)
    tab = jnp.where(jnp.asarray(visible)[:, :, None], tab, NEG_INF)
    tab = jnp.transpose(tab, (2, 0, 1)).reshape(N_KV_HEADS, GROUP, BAND, PAIR)
    return jnp.transpose(tab, (0, 2, 1, 3)).reshape(N_KV_HEADS, BAND, GROUP * PAIR)


def _mixer_consts(l, norm1_g, w_in, q_norm_g, k_norm_g, rel_bias, sinks, w_pool, pool_scale,
                  w_out, norm2_g, w_router, b_router, visible, tile):
    f32, bf16 = jnp.float32, jnp.bfloat16
    q_off, k_off, v_off = POOL_WIDTH, POOL_WIDTH + ATTN_WIDTH, POOL_WIDTH + ATTN_WIDTH + KV_WIDTH
    w = w_in[l]
    w_ukv = jnp.concatenate([w[:, :q_off], w[:, k_off:]], axis=1).astype(bf16)
    w_qvt = jnp.transpose(jnp.concatenate([w[:, q_off:k_off], w[:, v_off:]], axis=1)).astype(bf16)
    lane_head = np.arange(KV_WIDTH) // HEAD_DIM
    blockdiag = jnp.asarray(lane_head[:, None] == lane_head[None, :], bf16)
    sink_rows = jnp.repeat(sinks[l].astype(f32).reshape(N_KV_HEADS, GROUP), PAIR, axis=1)
    wr = w_router[l].astype(f32)
    wr_hi = wr.astype(bf16)
    wr_lo = (wr - wr_hi.astype(f32)).astype(bf16)
    wr_parts = jnp.pad(jnp.concatenate([wr_hi, wr_lo], axis=1), ((0, 0), (0, 128 - 2 * N_EXPERTS)))
    tri = jnp.asarray(np.arange(tile)[:, None] < np.arange(tile)[None, :], bf16)
    return [
        norm1_g[l].reshape(1, D_MODEL).astype(f32), w_ukv, w_qvt,
        q_norm_g[l].reshape(HEAD_DIM, 1).astype(f32),
        jnp.tile(k_norm_g[l].astype(f32), N_KV_HEADS).reshape(1, KV_WIDTH),
        blockdiag, _bias_tables(rel_bias, visible), sink_rows,
        w_pool[l].astype(bf16), pool_scale[l].reshape(1, POOL_WIDTH).astype(f32),
        w_out[l].astype(bf16), norm2_g[l].reshape(1, D_MODEL).astype(f32),
        wr_parts, b_router[l].reshape(N_EXPERTS, 1).astype(f32), tri,
    ]


def _visibility():
    kap = np.arange(BAND)[:, None]
    rho = np.arange(PAIR)[None, :]
    kc, qc = kap // CHUNK, rho // CHUNK
    prompt = (kc >= qc) & (kc <= qc + WINDOW // CHUNK)
    return prompt


def kernel(x_prompt, x_sample, cache_k, cache_v, state_pool, norm1_g, w_in, q_norm_g, k_norm_g,
           rel_bias, sinks, w_pool, pool_scale, w_out, norm2_g, w_router, b_router,
           w_gate, b_gate, w_up, b_up, w_down, b_down):
    f32, bf16 = jnp.float32, jnp.bfloat16
    depth = w_in.shape[0]
    nb, seq, _ = x_prompt.shape
    ndb, dseq, _ = x_sample.shape
    cache_len = cache_k.shape[2]
    assert seq % MIX_TILE == 0 and cache_len == WINDOW and HIST_ROWS <= dseq <= PAIR
    n_p, n_s = nb * seq, ndb * dseq
    n_tok = n_p + n_s
    assert n_p % COMBINE_TILE == 0 and n_tok % COMBINE_TILE == 0 and n_s % 8 == 0

    vis_prompt = _visibility()
    vis_sample = np.broadcast_to(np.arange(BAND)[:, None] < WINDOW + dseq, (BAND, PAIR))

    nb1 = nb // 4
    nb2 = nb - nb1
    assert nb1 > 0 and nb2 > 0 and (nb2 * seq) % COMBINE_TILE == 0

    xp, xs = x_prompt, x_sample
    outs = [[] for _ in range(6)]
    for l in range(depth):
        wl = (l, norm1_g, w_in, q_norm_g, k_norm_g, rel_bias, sinks, w_pool, pool_scale, w_out,
              norm2_g, w_router, b_router)
        consts_p = _mixer_consts(*wl, vis_prompt, MIX_TILE)
        moe_w = (w_gate[l], b_gate[l].reshape(N_EXPERTS, 1, D_MODEL).astype(f32),
                 w_up[l], b_up[l].reshape(N_EXPERTS, 1, D_MODEL).astype(f32),
                 w_down[l], b_down[l].reshape(N_EXPERTS, 1, D_MODEL).astype(f32))
        cnt0 = jnp.zeros((N_EXPERTS, 128), f32)

        def prompt_mixer(stream0, n_streams, cnt_in):
            zk = jnp.zeros((n_streams, WINDOW, KV_WIDTH), f32)
            zu = jnp.zeros((n_streams, HIST_ROWS, POOL_WIDTH), f32)
            return _mixer_call(xp, zk, zk, zu, cnt_in, consts_p, stream0=stream0, tile=MIX_TILE,
                               n_valid=MIX_TILE, pos0=0, mask_first=True)

        (x1_a, h_a, idx_a, rank_a, gate_a, cnt_a, k_a, v_a, u_a) = prompt_mixer(0, nb1, cnt0)
        picked_a = _moe_rows([h_a], [idx_a], [rank_a], [seq], cnt_a, moe_w)
        xs_pad = jnp.pad(xs, ((0, 0), (0, PAIR - dseq), (0, 0)))
        uh = jnp.pad(state_pool[l], ((0, 0), (HIST_ROWS - POOL_HIST, 0), (0, 0)))
        (xs1, h_s, idx_s, rank_s, gate_s, cnt_s, k_s, v_s, u_s) = _mixer_call(
            xs_pad, cache_k[l].reshape(ndb, WINDOW, KV_WIDTH),
            cache_v[l].reshape(ndb, WINDOW, KV_WIDTH), uh, cnt0,
            _mixer_consts(*wl, vis_sample, PAIR),
            stream0=0, tile=PAIR, n_valid=dseq, pos0=PAST_LEN, mask_first=False)
        (x1_b, h_b, idx_b, rank_b, gate_b, cnt_b, k_b, v_b, u_b) = prompt_mixer(nb1, nb2, cnt_s)
        picked_b = _moe_rows([h_b, h_s], [idx_b, idx_s], [rank_b, rank_s], [seq, dseq], cnt_b, moe_w)

        n_a, n_b = nb1 * seq, nb2 * seq
        xp = _combine_call(
            [(x1_a.reshape(n_a, D_MODEL), gate_a, picked_a, n_a, 0),
             (x1_b.reshape(n_b, D_MODEL), gate_b, picked_b, n_b + n_s, 0)],
            COMBINE_TILE).reshape(nb, seq, D_MODEL)
        xs = _combine_call(
            [(xs1[:, :dseq].reshape(n_s, D_MODEL), gate_s, picked_b, n_b + n_s, n_b)],
            COMBINE_TILE).reshape(ndb, dseq, D_MODEL)

        outs[0].append(jnp.concatenate([k_a, k_b]).reshape(nb, WINDOW, N_KV_HEADS, HEAD_DIM))
        outs[1].append(jnp.concatenate([v_a, v_b]).reshape(nb, WINDOW, N_KV_HEADS, HEAD_DIM))
        outs[2].append(jnp.concatenate([u_a, u_b])[:, HIST_ROWS - POOL_HIST:])
        outs[3].append(k_s[:, :dseq].reshape(ndb, dseq, N_KV_HEADS, HEAD_DIM))
        outs[4].append(v_s[:, :dseq].reshape(ndb, dseq, N_KV_HEADS, HEAD_DIM))
        outs[5].append(u_s[:, HIST_ROWS - POOL_HIST:])
    return (xp, xs) + tuple(jnp.stack(o) for o in outs)


def _moe_rows(hs, idxs, ranks, n_reals, cnt, moe_w):
    n_tok = sum(h.shape[0] for h in hs)
    n_assign = n_tok * TOP_K
    gather_quant = SC_WORKERS * SC_CHUNK
    n_blocks = -(-(n_assign + N_EXPERTS * (EXPERT_BLOCK - 1)) // EXPERT_BLOCK)
    n_blocks = -(-n_blocks // (gather_quant // EXPERT_BLOCK)) * (gather_quant // EXPERT_BLOCK)
    n_pick = -(-n_assign // gather_quant) * gather_quant
    e_ids = jnp.arange(N_EXPERTS, dtype=jnp.int32)

    def per_token(arrs):
        return jnp.concatenate(
            [jnp.transpose(a[:, :, :n], (1, 0, 2)).reshape(TOP_K, -1) for a, n in zip(arrs, n_reals)],
            axis=1)

    counts = cnt[:, 0].astype(jnp.int32)
    pcounts = (counts + EXPERT_BLOCK - 1) // EXPERT_BLOCK * EXPERT_BLOCK
    pend = jnp.cumsum(pcounts)
    pstart = pend - pcounts
    idx_all = per_token(idxs)
    dest = per_token(ranks) + jnp.sum(
        jnp.where(idx_all[None] == e_ids[:, None, None], pstart[:, None, None], 0), axis=0)
    dest = dest.reshape(-1)
    blk0 = jnp.arange(n_blocks, dtype=jnp.int32) * EXPERT_BLOCK
    block_e = jnp.minimum(jnp.sum((pend[None, :] <= blk0[:, None]).astype(jnp.int32), axis=1),
                          N_EXPERTS - 1)
    block_rows = jnp.clip(jnp.sum(jnp.where(block_e[:, None] == e_ids[None, :],
                                             (pstart + counts)[None, :], 0), axis=1) - blk0,
                          0, EXPERT_BLOCK).astype(jnp.int32)

    later = (counts > 0)[None, :] & (e_ids[None, :] > e_ids[:, None])
    next_e = jnp.min(jnp.where(later, e_ids[None, :], N_EXPERTS), axis=1)
    next_e = jnp.where(next_e == N_EXPERTS, e_ids, next_e)
    block_next = jnp.sum(jnp.where(block_e[:, None] == e_ids[None, :], next_e[None, :], 0),
                         axis=1).astype(jnp.int32)

    x_sorted = _scatter_rows(hs, dest, n_blocks * EXPERT_BLOCK)
    y_sorted = _expert_call(block_e, block_rows, block_next, x_sorted, *moe_w)
    return _gather_rows(y_sorted, jnp.pad(dest, (0, n_pick - n_assign)))
```

```python
import functools
import math

import numpy as np
import jax
import jax.numpy as jnp
from jax import lax
from jax.experimental import pallas as pl
from jax.experimental.pallas import tpu as pltpu
from jax.experimental.pallas import tpu_sc as plsc

D_MODEL = 1024
CHUNK = 64
POOL_WIDTH = 512
POOL_WINDOWS = (2, 4, 8, 16)
POOL_GROUP = 128
POOL_HIST = 15
ATTN_WIDTH = 512
HEAD_DIM = 64
N_HEADS = 8
N_KV_HEADS = 2
GROUP = 4
KV_WIDTH = 128
WINDOW = 128
NUM_BUCKETS = 32
MAX_DISTANCE = 128
PAST_LEN = 2048
N_EXPERTS = 32
TOP_K = 4
SWIGLU_LIMIT = 7.0
SWIGLU_ALPHA = 1.702
EPS = 1e-5
NEG_INF = -1e30
ATTN_SCALE = HEAD_DIM ** -0.5

PAIR = 2 * CHUNK
BAND = PAIR + WINDOW
HIST_ROWS = 16
GATE_COLS = 8
MIX_TILE = 512
EXPERT_BLOCK = 1024
COMBINE_TILE = 256
SC_WORKERS = 32
SC_CHUNK = 64
VMEM_LIMIT = 56 * 1024 * 1024


def _pack_bf16_pair(a, b):
    ab = lax.bitcast_convert_type(a.astype(jnp.bfloat16).astype(jnp.float32), jnp.uint32)
    bb = lax.bitcast_convert_type(b.astype(jnp.bfloat16).astype(jnp.float32), jnp.uint32)
    return (ab >> 16) | (bb & jnp.uint32(0xFFFF0000))


def _unpack_bf16_pair(w):
    a = lax.bitcast_convert_type(w << 16, jnp.float32).astype(jnp.bfloat16)
    b = lax.bitcast_convert_type(w & jnp.uint32(0xFFFF0000), jnp.float32).astype(jnp.bfloat16)
    return a, b


def _mixer_kernel(x_ref, kh_ref, vh_ref, uh_ref, cnt_in_ref,
                  g1_ref, wukv_ref, wqvt_ref, qg_ref, kg_ref, bd_ref, bias_ref, sink_ref,
                  wpool_ref, pscale_ref, wout_ref, g2_ref, wr_ref, br_ref, tri_ref,
                  x1_ref, hp_ref, idx_ref, rank_ref, gate_ref, cnt_ref, ko_ref, vo_ref, uo_ref,
                  qt_s, kb_s, vt_s, ub_s, mix_s, cnt_s,
                  *, tile, n_valid, pos0, mask_first):
    b = pl.program_id(0)
    s = pl.program_id(1)
    bf16 = jnp.bfloat16
    f32 = jnp.float32

    @pl.when((b == 0) & (s == 0))
    def _():
        cnt_s[...] = cnt_in_ref[...]

    @pl.when(s == 0)
    def _():
        kb_s[0:WINDOW, :] = kh_ref[...].astype(bf16)
        vt_s[:, 0:WINDOW] = jnp.transpose(vh_ref[...]).astype(bf16)
        ub_s[0:HIST_ROWS, :] = uh_ref[...]

    @pl.when(s > 0)
    def _():
        kb_s[0:WINDOW, :] = kb_s[tile:tile + WINDOW, :]
        vt_s[:, 0:WINDOW] = vt_s[:, tile:tile + WINDOW]
        ub_s[0:HIST_ROWS, :] = ub_s[tile:tile + HIST_ROWS, :]

    x = x_ref[...]
    xn = (x * lax.rsqrt(jnp.mean(x * x, axis=-1, keepdims=True) + EPS) * g1_ref[...]).astype(bf16)
    z = jnp.dot(xn, wukv_ref[...], preferred_element_type=f32)
    zt = lax.dot_general(wqvt_ref[...], xn, (((1,), (1,)), ((), ())),
                         preferred_element_type=f32)
    u = z[:, 0:POOL_WIDTH]
    kz = z[:, POOL_WIDTH:POOL_WIDTH + KV_WIDTH]
    v = z[:, POOL_WIDTH + KV_WIDTH:]

    ksq = kz * kz
    kss = jnp.dot(ksq.astype(bf16), bd_ref[...], preferred_element_type=f32)
    kn = kz * lax.rsqrt(kss * (1.0 / HEAD_DIM) + EPS) * kg_ref[...]
    kb_s[WINDOW:, :] = kn.astype(bf16)
    vt_s[:, WINDOW:] = zt[ATTN_WIDTH:, :].astype(bf16)
    ub_s[HIST_ROWS:, :] = u

    row0 = max(n_valid, WINDOW) - WINDOW
    ko_ref[...] = kn[row0:row0 + WINDOW, :]
    vo_ref[...] = v[row0:row0 + WINDOW, :]
    uo_ref[...] = u[n_valid - HIST_ROWS:n_valid, :]

    for hd in range(N_HEADS):
        qh = zt[hd * HEAD_DIM:(hd + 1) * HEAD_DIM, :]
        ss = jnp.sum(qh * qh, axis=0, keepdims=True)
        qn = qh * (lax.rsqrt(ss * (1.0 / HEAD_DIM) + EPS) * ATTN_SCALE) * qg_ref[...]
        qt_s[hd * HEAD_DIM:(hd + 1) * HEAD_DIM, :] = qn.astype(bf16)

    pos = pos0 + s * tile + lax.broadcasted_iota(jnp.int32, (tile, 1), 0)
    for g, w in enumerate(POOL_WINDOWS):
        e = ub_s[:, g * POOL_GROUP:(g + 1) * POOL_GROUP]
        acc = e
        for lvl in range(g + 1):
            acc = acc + pltpu.roll(acc, 2 ** lvl, axis=0)
        inv_cnt = 1.0 / jnp.minimum(pos + 1, w).astype(f32)
        d = (acc[HIST_ROWS:, :] * inv_cnt - e[HIST_ROWS:, :]).astype(bf16)
        y = jnp.dot(d, wpool_ref[g], preferred_element_type=f32)
        y = y * pscale_ref[:, g * POOL_GROUP:(g + 1) * POOL_GROUP]
        mix_s[:, g * POOL_GROUP:(g + 1) * POOL_GROUP] = y.astype(bf16)

    zeros_q = jnp.zeros((HEAD_DIM, GROUP * PAIR), bf16)
    for p in range(tile // PAIR):
        k_band = kb_s[p * PAIR:p * PAIR + BAND, :]
        o_parts = []
        for h in range(N_KV_HEADS):
            qcat = jnp.concatenate(
                [qt_s[(h * GROUP + g) * HEAD_DIM:(h * GROUP + g + 1) * HEAD_DIM,
                      p * PAIR:(p + 1) * PAIR] for g in range(GROUP)], axis=1)
            rhs = jnp.concatenate([qcat, zeros_q] if h == 0 else [zeros_q, qcat], axis=0)
            st = jnp.dot(k_band, rhs, preferred_element_type=f32) + bias_ref[h]
            if mask_first and p == 0:
                krow = lax.broadcasted_iota(jnp.int32, (BAND, 1), 0)
                st = jnp.where((krow >= WINDOW) | (s > 0), st, NEG_INF)
            sink = sink_ref[h:h + 1, :]
            m = jnp.maximum(jnp.max(st, axis=0, keepdims=True), sink)
            ex = jnp.exp(st - m)
            den = jnp.sum(ex, axis=0, keepdims=True) + jnp.exp(sink - m)
            v_band = vt_s[h * HEAD_DIM:(h + 1) * HEAD_DIM, p * PAIR:p * PAIR + BAND]
            ot = jnp.dot(v_band, ex.astype(bf16), preferred_element_type=f32) / den
            for g in range(GROUP):
                o_parts.append(ot[:, g * PAIR:(g + 1) * PAIR])
        o_all = jnp.concatenate(o_parts, axis=0)
        mix_s[p * PAIR:(p + 1) * PAIR, POOL_WIDTH:] = jnp.transpose(o_all).astype(bf16)

    x1 = x + jnp.dot(mix_s[...], wout_ref[...], preferred_element_type=f32)
    x1_ref[...] = x1

    hn = x1 * lax.rsqrt(jnp.mean(x1 * x1, axis=-1, keepdims=True) + EPS) * g2_ref[...]
    hp_ref[...] = _pack_bf16_pair(hn[0:n_valid, 0:D_MODEL // 2], hn[0:n_valid, D_MODEL // 2:])
    h_hi = hn.astype(bf16)
    h_lo = (hn - h_hi.astype(f32)).astype(bf16)
    parts = jnp.transpose(jnp.dot(h_hi, wr_ref[...], preferred_element_type=f32)
                          + jnp.dot(h_lo, wr_ref[...], preferred_element_type=f32))
    lt = parts[0:N_EXPERTS, :] + parts[N_EXPERTS:2 * N_EXPERTS, :] + br_ref[...]

    eidx = lax.broadcasted_iota(jnp.int32, (N_EXPERTS, tile), 0).astype(f32)
    vals, hots = [], []
    for j in range(TOP_K):
        m = jnp.max(lt, axis=0, keepdims=True)
        sel = jnp.min(jnp.where(lt == m, eidx, float(N_EXPERTS)), axis=0, keepdims=True)
        hot = eidx == sel
        lt = jnp.where(hot, -jnp.inf, lt)
        idx_ref[j:j + 1, :] = sel.astype(jnp.int32)
        vals.append(m)
        hots.append(hot)
    exps = [jnp.exp(vv - vals[0]) for vv in vals]
    esum = exps[0] + exps[1] + exps[2] + exps[3]
    grow = lax.broadcasted_iota(jnp.int32, (GATE_COLS, tile), 0)
    gmat = jnp.zeros((GATE_COLS, tile), f32)
    for j in range(TOP_K):
        gmat = jnp.where(grow == j, exps[j] / esum, gmat)
    gmat = jnp.concatenate([gmat, jnp.zeros((128 - GATE_COLS, tile), f32)], axis=0)
    gate_ref[...] = jnp.transpose(gmat)[0:n_valid, 0:GATE_COLS]

    chosen_f = sum(jnp.where(hot, 1.0, 0.0) for hot in hots)
    if n_valid < tile:
        lane = lax.broadcasted_iota(jnp.int32, (N_EXPERTS, tile), 1)
        chosen_f = jnp.where(lane < n_valid, chosen_f, 0.0)
    before = jnp.dot(chosen_f.astype(bf16), tri_ref[...], preferred_element_type=f32)
    base = before + cnt_s[:, 0:1]
    for j in range(TOP_K):
        rank_ref[j:j + 1, :] = jnp.sum(jnp.where(hots[j], base, 0.0), axis=0,
                                       keepdims=True).astype(jnp.int32)
    cnt_new = cnt_s[...] + jnp.sum(chosen_f, axis=1, keepdims=True)
    cnt_s[...] = cnt_new
    cnt_ref[...] = cnt_new


def _mixer_call(x, k_hist, v_hist, u_hist, cnt_in, consts, *, stream0, tile, n_valid, pos0,
                mask_first):
    seq = x.shape[1]
    nb = k_hist.shape[0]
    n_tiles = seq // tile
    f32 = jnp.float32
    assert n_valid % 8 == 0

    def full(a):
        nd = a.ndim
        return pl.BlockSpec(a.shape, lambda b, s, _nd=nd: (0,) * _nd)

    in_specs = [
        pl.BlockSpec((None, tile, D_MODEL), lambda b, s: (stream0 + b, s, 0)),
        pl.BlockSpec((None, WINDOW, KV_WIDTH), lambda b, s: (b, 0, 0)),
        pl.BlockSpec((None, WINDOW, KV_WIDTH), lambda b, s: (b, 0, 0)),
        pl.BlockSpec((None, HIST_ROWS, POOL_WIDTH), lambda b, s: (b, 0, 0)),
        full(cnt_in),
    ] + [full(c) for c in consts]
    out_shape = [
        jax.ShapeDtypeStruct((nb, seq, D_MODEL), f32),
        jax.ShapeDtypeStruct((nb * n_tiles * n_valid, D_MODEL // 2), jnp.uint32),
        jax.ShapeDtypeStruct((nb, TOP_K, seq), jnp.int32),
        jax.ShapeDtypeStruct((nb, TOP_K, seq), jnp.int32),
        jax.ShapeDtypeStruct((nb * n_tiles * n_valid, GATE_COLS), f32),
        jax.ShapeDtypeStruct((N_EXPERTS, 128), f32),
        jax.ShapeDtypeStruct((nb, WINDOW, KV_WIDTH), f32),
        jax.ShapeDtypeStruct((nb, WINDOW, KV_WIDTH), f32),
        jax.ShapeDtypeStruct((nb, HIST_ROWS, POOL_WIDTH), f32),
    ]
    out_specs = [
        pl.BlockSpec((None, tile, D_MODEL), lambda b, s: (b, s, 0)),
        pl.BlockSpec((n_valid, D_MODEL // 2), lambda b, s: (b * n_tiles + s, 0)),
        pl.BlockSpec((None, TOP_K, tile), lambda b, s: (b, 0, s)),
        pl.BlockSpec((None, TOP_K, tile), lambda b, s: (b, 0, s)),
        pl.BlockSpec((n_valid, GATE_COLS), lambda b, s: (b * n_tiles + s, 0)),
        pl.BlockSpec((N_EXPERTS, 128), lambda b, s: (0, 0)),
        pl.BlockSpec((None, WINDOW, KV_WIDTH), lambda b, s: (b, 0, 0)),
        pl.BlockSpec((None, WINDOW, KV_WIDTH), lambda b, s: (b, 0, 0)),
        pl.BlockSpec((None, HIST_ROWS, POOL_WIDTH), lambda b, s: (b, 0, 0)),
    ]
    scratch = [
        pltpu.VMEM((ATTN_WIDTH, tile), jnp.bfloat16),
        pltpu.VMEM((WINDOW + tile, KV_WIDTH), jnp.bfloat16),
        pltpu.VMEM((KV_WIDTH, WINDOW + tile), jnp.bfloat16),
        pltpu.VMEM((HIST_ROWS + tile, POOL_WIDTH), f32),
        pltpu.VMEM((tile, D_MODEL), jnp.bfloat16),
        pltpu.VMEM((N_EXPERTS, 128), f32),
    ]
    kern = functools.partial(_mixer_kernel, tile=tile, n_valid=n_valid, pos0=pos0,
                             mask_first=mask_first)
    return pl.pallas_call(
        kern,
        grid=(nb, n_tiles),
        in_specs=in_specs,
        out_specs=out_specs,
        out_shape=out_shape,
        scratch_shapes=scratch,
        compiler_params=pltpu.CompilerParams(
            dimension_semantics=("arbitrary", "arbitrary"),
            vmem_limit_bytes=VMEM_LIMIT),
        name="mixer",
    )(x, k_hist, v_hist, u_hist, cnt_in, *consts)


def _expert_kernel(be_ref, nv_ref, nx_ref, xs_ref, wg_hbm, bg_ref, wu_hbm, bu_ref, wd_hbm, bd_ref,
                   ys_ref, stage_s, wg_s, wu_s, wd_s, sems):
    i = pl.program_id(0)
    n_rows = nv_ref[i]
    expert = be_ref[i]
    bf16 = jnp.bfloat16

    def weight_copies(e):
        return [pltpu.make_async_copy(w_hbm.at[e], stage_s.at[k], sems.at[k])
                for k, w_hbm in enumerate((wg_hbm, wu_hbm, wd_hbm))]

    @pl.when((i == 0) & (n_rows > 0))
    def _():
        for copy in weight_copies(expert):
            copy.start()

    @pl.when((n_rows > 0) & ((i == 0) | (expert != be_ref[jnp.maximum(i - 1, 0)])))
    def _():
        for copy in weight_copies(expert):
            copy.wait()
        wg_s[...] = stage_s[0].astype(bf16)
        wu_s[...] = stage_s[1].astype(bf16)
        wd_s[...] = stage_s[2].astype(bf16)

        @pl.when(nx_ref[i] != expert)
        def _():
            for copy in weight_copies(nx_ref[i]):
                copy.start()

    def ffn(rows):
        f32 = jnp.float32
        half = D_MODEL // 2
        row = lax.broadcasted_iota(jnp.int32, (rows, 1), 0)
        words = jnp.where(row < n_rows, xs_ref[0:rows, :], jnp.uint32(0))
        xa, xb = _unpack_bf16_pair(words)
        a = (jnp.dot(xa, wg_s[0:half, :], preferred_element_type=f32)
             + jnp.dot(xb, wg_s[half:, :], preferred_element_type=f32) + bg_ref[...])
        bb = (jnp.dot(xa, wu_s[0:half, :], preferred_element_type=f32)
              + jnp.dot(xb, wu_s[half:, :], preferred_element_type=f32) + bu_ref[...])
        a = jnp.minimum(a, SWIGLU_LIMIT)
        bb = jnp.clip(bb, -SWIGLU_LIMIT, SWIGLU_LIMIT)
        act = a * (1.0 / (1.0 + jnp.exp(-SWIGLU_ALPHA * a))) * (bb + 1.0)
        y = jnp.dot(act.astype(bf16), wd_s[...], preferred_element_type=f32) + bd_ref[...]
        ys_ref[0:rows, :] = _pack_bf16_pair(y[:, 0:half], y[:, half:])
        if rows < EXPERT_BLOCK:
            ys_ref[rows:, :] = jnp.zeros((EXPERT_BLOCK - rows, half), jnp.uint32)

    quarter = EXPERT_BLOCK // 4
    for q in range(1, 5):
        @pl.when((n_rows > (q - 1) * quarter) & (n_rows <= q * quarter))
        def _(q=q):
            ffn(q * quarter)

    @pl.when(n_rows == 0)
    def _():
        ys_ref[...] = jnp.zeros_like(ys_ref)


def _expert_call(block_e, block_rows, block_next, xs, wg, bg, wu, bu, wd, bd):
    n_slots = xs.shape[0]
    n_blocks = n_slots // EXPERT_BLOCK
    w_spec = pl.BlockSpec(memory_space=pl.ANY)
    b_spec = pl.BlockSpec((None, 1, D_MODEL), lambda i, be, nv, nx: (be[i], 0, 0))
    grid_spec = pltpu.PrefetchScalarGridSpec(
        num_scalar_prefetch=3,
        grid=(n_blocks,),
        in_specs=[pl.BlockSpec((EXPERT_BLOCK, D_MODEL // 2), lambda i, be, nv, nx: (i, 0)),
                  w_spec, b_spec, w_spec, b_spec, w_spec, b_spec],
        out_specs=pl.BlockSpec((EXPERT_BLOCK, D_MODEL // 2), lambda i, be, nv, nx: (i, 0)),
        scratch_shapes=[pltpu.VMEM((3, D_MODEL, D_MODEL), jnp.float32)]
        + [pltpu.VMEM((D_MODEL, D_MODEL), jnp.bfloat16)] * 3
        + [pltpu.SemaphoreType.DMA((3,))],
    )
    return pl.pallas_call(
        _expert_kernel,
        grid_spec=grid_spec,
        out_shape=jax.ShapeDtypeStruct((n_slots, D_MODEL // 2), jnp.uint32),
        compiler_params=pltpu.CompilerParams(
            dimension_semantics=("arbitrary",),
            vmem_limit_bytes=VMEM_LIMIT),
        name="experts",
    )(block_e, block_rows, block_next, xs, wg, bg, wu, bu, wd, bd)


def _combine_kernel(out_buf_ref, x1_ref, g_ref, y0_ref, y1_ref, y2_ref, y3_ref, o_ref):
    del out_buf_ref
    g = g_ref[...]
    half = D_MODEL // 2
    lo, hi = x1_ref[:, 0:half], x1_ref[:, half:]
    for j, y_ref in enumerate((y0_ref, y1_ref, y2_ref, y3_ref)):
        w = y_ref[...]
        gj = g[:, j:j + 1]
        lo = lo + gj * lax.bitcast_convert_type(w << 16, jnp.float32)
        hi = hi + gj * lax.bitcast_convert_type(w & jnp.uint32(0xFFFF0000), jnp.float32)
    o_ref[:, 0:half] = lo
    o_ref[:, half:] = hi


def _combine_call(out_buf, x1, gates, picked, n_tok, tok0, *, out_rows, row0, tile):
    n = x1.shape[0]
    aliased = out_buf.shape == (out_rows, D_MODEL)
    assert n % tile == 0 and row0 % tile == 0
    y_specs = []
    for j in range(TOP_K):
        assert (j * n_tok + tok0) % tile == 0
        base = (j * n_tok + tok0) // tile
        y_specs.append(pl.BlockSpec((tile, D_MODEL // 2), lambda i, _b=base: (_b + i, 0)))
    return pl.pallas_call(
        _combine_kernel,
        grid=(n // tile,),
        in_specs=[pl.BlockSpec(memory_space=pl.ANY),
                  pl.BlockSpec((tile, D_MODEL), lambda i: (i, 0)),
                  pl.BlockSpec((tile, GATE_COLS), lambda i: (i, 0))] + y_specs,
        out_specs=pl.BlockSpec((tile, D_MODEL), lambda i: (row0 // tile + i, 0)),
        out_shape=jax.ShapeDtypeStruct((out_rows, D_MODEL), jnp.float32),
        input_output_aliases={0: 0} if aliased else {},
        compiler_params=pltpu.CompilerParams(dimension_semantics=("arbitrary",)),
        name="combine",
    )(out_buf, x1, gates, picked, picked, picked, picked)


def _gather_rows(table, idx):
    n = idx.shape[0]
    width = table.shape[1]
    per_worker = n // SC_WORKERS
    n_chunks = per_worker // SC_CHUNK
    mesh = plsc.VectorSubcoreMesh(core_axis_name="c", subcore_axis_name="s")

    @functools.partial(
        pl.kernel, mesh=mesh,
        out_type=jax.ShapeDtypeStruct((n, width), table.dtype),
        scratch_types=[pltpu.VMEM((SC_CHUNK,), jnp.int32),
                       pltpu.VMEM((SC_CHUNK, width), table.dtype),
                       pltpu.SemaphoreType.DMA],
        cost_estimate=pl.CostEstimate(flops=0, transcendentals=0, bytes_accessed=8 * n * width),
    )
    def gather(table_hbm, idx_hbm, out_hbm, idx_v, rows_v, sem):
        wid = lax.axis_index("s") * 2 + lax.axis_index("c")
        base = wid * per_worker

        @pl.loop(0, n_chunks)
        def _(i):
            off = base + i * SC_CHUNK
            pltpu.sync_copy(idx_hbm.at[pl.ds(off, SC_CHUNK)], idx_v)
            pltpu.async_copy(table_hbm.at[idx_v], rows_v, sem).wait()
            pltpu.sync_copy(rows_v, out_hbm.at[pl.ds(off, SC_CHUNK)])

    return gather(table, idx)


def _scatter_rows(srcs, dest, n_out):
    width = srcs[0].shape[1]
    starts = [0]
    for src in srcs:
        assert src.shape[0] % SC_CHUNK == 0
        starts.append(starts[-1] + src.shape[0] // SC_CHUNK)
    n_chunks = starts[-1]
    n = n_chunks * SC_CHUNK
    per_worker = -(-n_chunks // SC_WORKERS)
    mesh = plsc.VectorSubcoreMesh(core_axis_name="c", subcore_axis_name="s")

    @functools.partial(
        pl.kernel, mesh=mesh,
        out_type=jax.ShapeDtypeStruct((n_out, width), srcs[0].dtype),
        scratch_types=[pltpu.VMEM((SC_CHUNK,), jnp.int32)] * TOP_K
        + [pltpu.VMEM((SC_CHUNK, width), srcs[0].dtype)],
        cost_estimate=pl.CostEstimate(flops=0, transcendentals=0,
                                      bytes_accessed=4 * (1 + TOP_K) * n * width),
    )
    def scatter(*refs):
        src_hbms = refs[:len(srcs)]
        dest_hbm, out_hbm = refs[len(srcs)], refs[len(srcs) + 1]
        idx_vs, rows_v = refs[len(srcs) + 2:len(srcs) + 2 + TOP_K], refs[-1]
        wid = lax.axis_index("s") * 2 + lax.axis_index("c")

        @pl.loop(0, per_worker)
        def _(i):
            c = i * SC_WORKERS + wid
            for k, src_hbm in enumerate(src_hbms):

                @pl.when((c >= starts[k]) & (c < starts[k + 1]))
                def _():
                    pltpu.sync_copy(src_hbm.at[pl.ds((c - starts[k]) * SC_CHUNK, SC_CHUNK)], rows_v)
                    for j, idx_v in enumerate(idx_vs):
                        pltpu.sync_copy(dest_hbm.at[pl.ds(j * n + c * SC_CHUNK, SC_CHUNK)], idx_v)
                    for idx_v in idx_vs:
                        pltpu.sync_copy(rows_v, out_hbm.at[idx_v])

    return scatter(*srcs, dest)


def _t5_bucket_np(rel):
    half = NUM_BUCKETS // 2
    max_exact = half // 2
    n = np.abs(rel)
    nf = np.maximum(n, 1).astype(np.float32)
    large = max_exact + (np.log(nf / max_exact) / math.log(MAX_DISTANCE / max_exact)
                         * (half - max_exact)).astype(np.int32)
    large = np.minimum(large, half - 1)
    return np.where(rel > 0, half, 0) + np.where(n < max_exact, n, large)


def _bias_tables(rel_bias, visible):
    kap = np.arange(BAND)[:, None]
    rho = np.arange(PAIR)[None, :]
    bucket = _t5_bucket_np(kap - WINDOW - rho)
    rb = rel_bias.astype(jnp.float32)
    tab = jnp.zeros((BAND, PAIR, N_HEADS), jnp.float32)
    for bkt in range(NUM_BUCKETS):
        tab = jnp.where(jnp.asarray(bucket == bkt)[:, :, None], rb[bkt], tab)
    tab = jnp.where(jnp.asarray(visible)[:, :, None], tab, NEG_INF)
    tab = jnp.transpose(tab, (2, 0, 1)).reshape(N_KV_HEADS, GROUP, BAND, PAIR)
    return jnp.transpose(tab, (0, 2, 1, 3)).reshape(N_KV_HEADS, BAND, GROUP * PAIR)


def _mixer_consts(l, norm1_g, w_in, q_norm_g, k_norm_g, rel_bias, sinks, w_pool, pool_scale,
                  w_out, norm2_g, w_router, b_router, visible, tile):
    f32, bf16 = jnp.float32, jnp.bfloat16
    q_off, k_off, v_off = POOL_WIDTH, POOL_WIDTH + ATTN_WIDTH, POOL_WIDTH + ATTN_WIDTH + KV_WIDTH
    w = w_in[l]
    w_ukv = jnp.concatenate([w[:, :q_off], w[:, k_off:]], axis=1).astype(bf16)
    w_qvt = jnp.transpose(jnp.concatenate([w[:, q_off:k_off], w[:, v_off:]], axis=1)).astype(bf16)
    lane_head = np.arange(KV_WIDTH) // HEAD_DIM
    blockdiag = jnp.asarray(lane_head[:, None] == lane_head[None, :], bf16)
    sink_rows = jnp.repeat(sinks[l].astype(f32).reshape(N_KV_HEADS, GROUP), PAIR, axis=1)
    wr = w_router[l].astype(f32)
    wr_hi = wr.astype(bf16)
    wr_lo = (wr - wr_hi.astype(f32)).astype(bf16)
    wr_parts = jnp.pad(jnp.concatenate([wr_hi, wr_lo], axis=1), ((0, 0), (0, 128 - 2 * N_EXPERTS)))
    tri = jnp.asarray(np.arange(tile)[:, None] < np.arange(tile)[None, :], bf16)
    return [
        norm1_g[l].reshape(1, D_MODEL).astype(f32), w_ukv, w_qvt,
        q_norm_g[l].reshape(HEAD_DIM, 1).astype(f32),
        jnp.tile(k_norm_g[l].astype(f32), N_KV_HEADS).reshape(1, KV_WIDTH),
        blockdiag, _bias_tables(rel_bias, visible), sink_rows,
        w_pool[l].astype(bf16), pool_scale[l].reshape(1, POOL_WIDTH).astype(f32),
        w_out[l].astype(bf16), norm2_g[l].reshape(1, D_MODEL).astype(f32),
        wr_parts, b_router[l].reshape(N_EXPERTS, 1).astype(f32), tri,
    ]


def _visibility():
    kap = np.arange(BAND)[:, None]
    rho = np.arange(PAIR)[None, :]
    kc, qc = kap // CHUNK, rho // CHUNK
    prompt = (kc >= qc) & (kc <= qc + WINDOW // CHUNK)
    return prompt


def kernel(x_prompt, x_sample, cache_k, cache_v, state_pool, norm1_g, w_in, q_norm_g, k_norm_g,
           rel_bias, sinks, w_pool, pool_scale, w_out, norm2_g, w_router, b_router,
           w_gate, b_gate, w_up, b_up, w_down, b_down):
    f32, bf16 = jnp.float32, jnp.bfloat16
    depth = w_in.shape[0]
    nb, seq, _ = x_prompt.shape
    ndb, dseq, _ = x_sample.shape
    cache_len = cache_k.shape[2]
    assert seq % MIX_TILE == 0 and cache_len == WINDOW and HIST_ROWS <= dseq <= PAIR
    n_p, n_s = nb * seq, ndb * dseq
    n_tok = n_p + n_s
    assert n_p % COMBINE_TILE == 0 and n_tok % COMBINE_TILE == 0 and n_s % 8 == 0

    vis_prompt = _visibility()
    vis_sample = np.broadcast_to(np.arange(BAND)[:, None] < WINDOW + dseq, (BAND, PAIR))

    nb1 = nb // 4
    nb2 = nb - nb1
    assert nb1 > 0 and nb2 > 0 and (nb2 * seq) % COMBINE_TILE == 0

    xp, xs = x_prompt, x_sample
    outs = [[] for _ in range(6)]
    for l in range(depth):
        wl = (l, norm1_g, w_in, q_norm_g, k_norm_g, rel_bias, sinks, w_pool, pool_scale, w_out,
              norm2_g, w_router, b_router)
        consts_p = _mixer_consts(*wl, vis_prompt, MIX_TILE)
        moe_w = (w_gate[l], b_gate[l].reshape(N_EXPERTS, 1, D_MODEL).astype(f32),
                 w_up[l], b_up[l].reshape(N_EXPERTS, 1, D_MODEL).astype(f32),
                 w_down[l], b_down[l].reshape(N_EXPERTS, 1, D_MODEL).astype(f32))
        cnt0 = jnp.zeros((N_EXPERTS, 128), f32)

        def prompt_mixer(stream0, n_streams, cnt_in):
            zk = jnp.zeros((n_streams, WINDOW, KV_WIDTH), f32)
            zu = jnp.zeros((n_streams, HIST_ROWS, POOL_WIDTH), f32)
            return _mixer_call(xp, zk, zk, zu, cnt_in, consts_p, stream0=stream0, tile=MIX_TILE,
                               n_valid=MIX_TILE, pos0=0, mask_first=True)

        (x1_a, h_a, idx_a, rank_a, gate_a, cnt_a, k_a, v_a, u_a) = prompt_mixer(0, nb1, cnt0)
        picked_a = _moe_rows([h_a], [idx_a], [rank_a], [seq], cnt_a, moe_w)
        xs_pad = jnp.pad(xs, ((0, 0), (0, PAIR - dseq), (0, 0)))
        uh = jnp.pad(state_pool[l], ((0, 0), (HIST_ROWS - POOL_HIST, 0), (0, 0)))
        (xs1, h_s, idx_s, rank_s, gate_s, cnt_s, k_s, v_s, u_s) = _mixer_call(
            xs_pad, cache_k[l].reshape(ndb, WINDOW, KV_WIDTH),
            cache_v[l].reshape(ndb, WINDOW, KV_WIDTH), uh, cnt0,
            _mixer_consts(*wl, vis_sample, PAIR),
            stream0=0, tile=PAIR, n_valid=dseq, pos0=PAST_LEN, mask_first=False)
        (x1_b, h_b, idx_b, rank_b, gate_b, cnt_b, k_b, v_b, u_b) = prompt_mixer(nb1, nb2, cnt_s)
        picked_b = _moe_rows([h_b, h_s], [idx_b, idx_s], [rank_b, rank_s], [seq, dseq], cnt_b, moe_w)

        n_a, n_b = nb1 * seq, nb2 * seq
        no_buf = jnp.zeros((8, 128), f32)
        xp_rows = _combine_call(no_buf, x1_b.reshape(n_b, D_MODEL), gate_b, picked_b, n_b + n_s, 0,
                                out_rows=n_p, row0=n_a, tile=COMBINE_TILE)
        xp_rows = _combine_call(xp_rows, x1_a.reshape(n_a, D_MODEL), gate_a, picked_a, n_a, 0,
                                out_rows=n_p, row0=0, tile=COMBINE_TILE)
        xp = xp_rows.reshape(nb, seq, D_MODEL)
        xs = _combine_call(no_buf, xs1[:, :dseq].reshape(n_s, D_MODEL), gate_s, picked_b, n_b + n_s,
                           n_b, out_rows=n_s, row0=0, tile=COMBINE_TILE).reshape(ndb, dseq, D_MODEL)

        outs[0].append(jnp.concatenate([k_a, k_b]).reshape(nb, WINDOW, N_KV_HEADS, HEAD_DIM))
        outs[1].append(jnp.concatenate([v_a, v_b]).reshape(nb, WINDOW, N_KV_HEADS, HEAD_DIM))
        outs[2].append(jnp.concatenate([u_a, u_b])[:, HIST_ROWS - POOL_HIST:])
        outs[3].append(k_s[:, :dseq].reshape(ndb, dseq, N_KV_HEADS, HEAD_DIM))
        outs[4].append(v_s[:, :dseq].reshape(ndb, dseq, N_KV_HEADS, HEAD_DIM))
        outs[5].append(u_s[:, HIST_ROWS - POOL_HIST:])
    return (xp, xs) + tuple(jnp.stack(o) for o in outs)


def _moe_rows(hs, idxs, ranks, n_reals, cnt, moe_w):
    n_tok = sum(h.shape[0] for h in hs)
    n_assign = n_tok * TOP_K
    gather_quant = SC_WORKERS * SC_CHUNK
    n_blocks = -(-(n_assign + N_EXPERTS * (EXPERT_BLOCK - 1)) // EXPERT_BLOCK)
    n_blocks = -(-n_blocks // (gather_quant // EXPERT_BLOCK)) * (gather_quant // EXPERT_BLOCK)
    n_pick = -(-n_assign // gather_quant) * gather_quant
    e_ids = jnp.arange(N_EXPERTS, dtype=jnp.int32)

    def per_token(arrs):
        return jnp.concatenate(
            [jnp.transpose(a[:, :, :n], (1, 0, 2)).reshape(TOP_K, -1) for a, n in zip(arrs, n_reals)],
            axis=1)

    counts = cnt[:, 0].astype(jnp.int32)
    pcounts = (counts + EXPERT_BLOCK - 1) // EXPERT_BLOCK * EXPERT_BLOCK
    pend = jnp.cumsum(pcounts)
    pstart = pend - pcounts
    idx_all = per_token(idxs)
    dest = per_token(ranks) + jnp.sum(
        jnp.where(idx_all[None] == e_ids[:, None, None], pstart[:, None, None], 0), axis=0)
    dest = dest.reshape(-1)
    blk0 = jnp.arange(n_blocks, dtype=jnp.int32) * EXPERT_BLOCK
    block_e = jnp.minimum(jnp.sum((pend[None, :] <= blk0[:, None]).astype(jnp.int32), axis=1),
                          N_EXPERTS - 1)
    block_rows = jnp.clip(jnp.sum(jnp.where(block_e[:, None] == e_ids[None, :],
                                             (pstart + counts)[None, :], 0), axis=1) - blk0,
                          0, EXPERT_BLOCK).astype(jnp.int32)

    later = (counts > 0)[None, :] & (e_ids[None, :] > e_ids[:, None])
    next_e = jnp.min(jnp.where(later, e_ids[None, :], N_EXPERTS), axis=1)
    next_e = jnp.where(next_e == N_EXPERTS, e_ids, next_e)
    block_next = jnp.sum(jnp.where(block_e[:, None] == e_ids[None, :], next_e[None, :], 0),
                         axis=1).astype(jnp.int32)

    x_sorted = _scatter_rows(hs, dest, n_blocks * EXPERT_BLOCK)
    y_sorted = _expert_call(block_e, block_rows, block_next, x_sorted, *moe_w)
    return _gather_rows(y_sorted, jnp.pad(dest, (0, n_pick - n_assign)))
```

```python
import functools
import math

import numpy as np
import jax
import jax.numpy as jnp
from jax import lax
from jax.experimental import pallas as pl
from jax.experimental.pallas import tpu as pltpu
from jax.experimental.pallas import tpu_sc as plsc

D_MODEL = 1024
CHUNK = 64
POOL_WIDTH = 512
POOL_WINDOWS = (2, 4, 8, 16)
POOL_GROUP = 128
POOL_HIST = 15
ATTN_WIDTH = 512
HEAD_DIM = 64
N_HEADS = 8
N_KV_HEADS = 2
GROUP = 4
KV_WIDTH = 128
WINDOW = 128
NUM_BUCKETS = 32
MAX_DISTANCE = 128
PAST_LEN = 2048
N_EXPERTS = 32
TOP_K = 4
SWIGLU_LIMIT = 7.0
SWIGLU_ALPHA = 1.702
EPS = 1e-5
NEG_INF = -1e30
ATTN_SCALE = HEAD_DIM ** -0.5

PAIR = 2 * CHUNK
BAND = PAIR + WINDOW
HIST_ROWS = 16
GATE_COLS = 8
MIX_TILE = 1024
EXPERT_BLOCK = 1024
FFN_CHUNK = 256
COMBINE_TILE = 256
SC_WORKERS = 32
SC_CHUNK = 64
VMEM_LIMIT = 56 * 1024 * 1024


def _pack_bf16_pair(a, b):
    ab = lax.bitcast_convert_type(a.astype(jnp.bfloat16).astype(jnp.float32), jnp.uint32)
    bb = lax.bitcast_convert_type(b.astype(jnp.bfloat16).astype(jnp.float32), jnp.uint32)
    return (ab >> 16) | (bb & jnp.uint32(0xFFFF0000))


def _unpack_bf16_pair(w):
    a = lax.bitcast_convert_type(w << 16, jnp.float32).astype(jnp.bfloat16)
    b = lax.bitcast_convert_type(w & jnp.uint32(0xFFFF0000), jnp.float32).astype(jnp.bfloat16)
    return a, b


def _mixer_kernel(x_ref, kh_ref, vh_ref, uh_ref, cnt_in_ref,
                  g1_ref, wukv_ref, wqvt_ref, qg_ref, kg_ref, bd_ref, bias_ref, sink_ref,
                  wpool_ref, pscale_ref, wout_ref, g2_ref, wr_ref, br_ref, tri_ref,
                  x1_ref, hp_ref, idx_ref, rank_ref, gate_ref, cnt_ref, ko_ref, vo_ref, uo_ref,
                  qt_s, kb_s, vt_s, ub_s, mix_s, cnt_s,
                  *, tile, n_valid, pos0, mask_first):
    b = pl.program_id(0)
    s = pl.program_id(1)
    bf16 = jnp.bfloat16
    f32 = jnp.float32

    @pl.when((b == 0) & (s == 0))
    def _():
        cnt_s[...] = cnt_in_ref[...]

    @pl.when(s == 0)
    def _():
        kb_s[0:WINDOW, :] = kh_ref[...].astype(bf16)
        vt_s[:, 0:WINDOW] = jnp.transpose(vh_ref[...]).astype(bf16)
        ub_s[0:HIST_ROWS, :] = uh_ref[...]

    @pl.when(s > 0)
    def _():
        kb_s[0:WINDOW, :] = kb_s[tile:tile + WINDOW, :]
        vt_s[:, 0:WINDOW] = vt_s[:, tile:tile + WINDOW]
        ub_s[0:HIST_ROWS, :] = ub_s[tile:tile + HIST_ROWS, :]

    x = x_ref[...]
    xn = (x * lax.rsqrt(jnp.mean(x * x, axis=-1, keepdims=True) + EPS) * g1_ref[...]).astype(bf16)
    z = jnp.dot(xn, wukv_ref[...], preferred_element_type=f32)
    zt = lax.dot_general(wqvt_ref[...], xn, (((1,), (1,)), ((), ())),
                         preferred_element_type=f32)
    u = z[:, 0:POOL_WIDTH]
    kz = z[:, POOL_WIDTH:POOL_WIDTH + KV_WIDTH]
    v = z[:, POOL_WIDTH + KV_WIDTH:]

    ksq = kz * kz
    kss = jnp.dot(ksq.astype(bf16), bd_ref[...], preferred_element_type=f32)
    kn = kz * lax.rsqrt(kss * (1.0 / HEAD_DIM) + EPS) * kg_ref[...]
    kb_s[WINDOW:, :] = kn.astype(bf16)
    vt_s[:, WINDOW:] = zt[ATTN_WIDTH:, :].astype(bf16)
    ub_s[HIST_ROWS:, :] = u

    row0 = max(n_valid, WINDOW) - WINDOW
    ko_ref[...] = kn[row0:row0 + WINDOW, :]
    vo_ref[...] = v[row0:row0 + WINDOW, :]
    uo_ref[...] = u[n_valid - HIST_ROWS:n_valid, :]

    for hd in range(N_HEADS):
        qh = zt[hd * HEAD_DIM:(hd + 1) * HEAD_DIM, :]
        ss = jnp.sum(qh * qh, axis=0, keepdims=True)
        qn = qh * (lax.rsqrt(ss * (1.0 / HEAD_DIM) + EPS) * ATTN_SCALE) * qg_ref[...]
        qt_s[hd * HEAD_DIM:(hd + 1) * HEAD_DIM, :] = qn.astype(bf16)

    pos = pos0 + s * tile + lax.broadcasted_iota(jnp.int32, (tile, 1), 0)
    for g, w in enumerate(POOL_WINDOWS):
        e = ub_s[:, g * POOL_GROUP:(g + 1) * POOL_GROUP]
        acc = e
        for lvl in range(g + 1):
            acc = acc + pltpu.roll(acc, 2 ** lvl, axis=0)
        inv_cnt = 1.0 / jnp.minimum(pos + 1, w).astype(f32)
        d = (acc[HIST_ROWS:, :] * inv_cnt - e[HIST_ROWS:, :]).astype(bf16)
        y = jnp.dot(d, wpool_ref[g], preferred_element_type=f32)
        y = y * pscale_ref[:, g * POOL_GROUP:(g + 1) * POOL_GROUP]
        mix_s[:, g * POOL_GROUP:(g + 1) * POOL_GROUP] = y.astype(bf16)

    zeros_q = jnp.zeros((HEAD_DIM, GROUP * PAIR), bf16)
    for p in range(tile // PAIR):
        k_band = kb_s[p * PAIR:p * PAIR + BAND, :]
        o_parts = []
        for h in range(N_KV_HEADS):
            qcat = jnp.concatenate(
                [qt_s[(h * GROUP + g) * HEAD_DIM:(h * GROUP + g + 1) * HEAD_DIM,
                      p * PAIR:(p + 1) * PAIR] for g in range(GROUP)], axis=1)
            rhs = jnp.concatenate([qcat, zeros_q] if h == 0 else [zeros_q, qcat], axis=0)
            st = jnp.dot(k_band, rhs, preferred_element_type=f32) + bias_ref[h]
            if mask_first and p == 0:
                krow = lax.broadcasted_iota(jnp.int32, (BAND, 1), 0)
                st = jnp.where((krow >= WINDOW) | (s > 0), st, NEG_INF)
            sink = sink_ref[h:h + 1, :]
            m = jnp.maximum(jnp.max(st, axis=0, keepdims=True), sink)
            ex = jnp.exp(st - m)
            den = jnp.sum(ex, axis=0, keepdims=True) + jnp.exp(sink - m)
            v_band = vt_s[h * HEAD_DIM:(h + 1) * HEAD_DIM, p * PAIR:p * PAIR + BAND]
            ot = jnp.dot(v_band, ex.astype(bf16), preferred_element_type=f32) / den
            for g in range(GROUP):
                o_parts.append(ot[:, g * PAIR:(g + 1) * PAIR])
        o_all = jnp.concatenate(o_parts, axis=0)
        mix_s[p * PAIR:(p + 1) * PAIR, POOL_WIDTH:] = jnp.transpose(o_all).astype(bf16)

    x1 = x + jnp.dot(mix_s[...], wout_ref[...], preferred_element_type=f32)
    x1_ref[...] = x1

    hn = x1 * lax.rsqrt(jnp.mean(x1 * x1, axis=-1, keepdims=True) + EPS) * g2_ref[...]
    hp_ref[...] = _pack_bf16_pair(hn[0:n_valid, 0:D_MODEL // 2], hn[0:n_valid, D_MODEL // 2:])
    h_hi = hn.astype(bf16)
    h_lo = (hn - h_hi.astype(f32)).astype(bf16)
    parts = jnp.transpose(jnp.dot(h_hi, wr_ref[...], preferred_element_type=f32)
                          + jnp.dot(h_lo, wr_ref[...], preferred_element_type=f32))
    lt = parts[0:N_EXPERTS, :] + parts[N_EXPERTS:2 * N_EXPERTS, :] + br_ref[...]

    eidx = lax.broadcasted_iota(jnp.int32, (N_EXPERTS, tile), 0).astype(f32)
    vals, hots = [], []
    for j in range(TOP_K):
        m = jnp.max(lt, axis=0, keepdims=True)
        sel = jnp.min(jnp.where(lt == m, eidx, float(N_EXPERTS)), axis=0, keepdims=True)
        hot = eidx == sel
        lt = jnp.where(hot, -jnp.inf, lt)
        idx_ref[j:j + 1, :] = sel.astype(jnp.int32)
        vals.append(m)
        hots.append(hot)
    exps = [jnp.exp(vv - vals[0]) for vv in vals]
    esum = exps[0] + exps[1] + exps[2] + exps[3]
    grow = lax.broadcasted_iota(jnp.int32, (GATE_COLS, tile), 0)
    gmat = jnp.zeros((GATE_COLS, tile), f32)
    for j in range(TOP_K):
        gmat = jnp.where(grow == j, exps[j] / esum, gmat)
    gmat = jnp.concatenate([gmat, jnp.zeros((128 - GATE_COLS, tile), f32)], axis=0)
    gate_ref[...] = jnp.transpose(gmat)[0:n_valid, 0:GATE_COLS]

    chosen_f = sum(jnp.where(hot, 1.0, 0.0) for hot in hots)
    if n_valid < tile:
        lane = lax.broadcasted_iota(jnp.int32, (N_EXPERTS, tile), 1)
        chosen_f = jnp.where(lane < n_valid, chosen_f, 0.0)
    before = jnp.dot(chosen_f.astype(bf16), tri_ref[...], preferred_element_type=f32)
    base = before + cnt_s[:, 0:1]
    for j in range(TOP_K):
        rank_ref[j:j + 1, :] = jnp.sum(jnp.where(hots[j], base, 0.0), axis=0,
                                       keepdims=True).astype(jnp.int32)
    cnt_new = cnt_s[...] + jnp.sum(chosen_f, axis=1, keepdims=True)
    cnt_s[...] = cnt_new
    cnt_ref[...] = cnt_new


def _mixer_call(x, k_hist, v_hist, u_hist, cnt_in, consts, *, stream0, tile, n_valid, pos0,
                mask_first):
    seq = x.shape[1]
    nb = k_hist.shape[0]
    n_tiles = seq // tile
    f32 = jnp.float32
    assert n_valid % 8 == 0

    def full(a):
        nd = a.ndim
        return pl.BlockSpec(a.shape, lambda b, s, _nd=nd: (0,) * _nd)

    in_specs = [
        pl.BlockSpec((None, tile, D_MODEL), lambda b, s: (stream0 + b, s, 0)),
        pl.BlockSpec((None, WINDOW, KV_WIDTH), lambda b, s: (b, 0, 0)),
        pl.BlockSpec((None, WINDOW, KV_WIDTH), lambda b, s: (b, 0, 0)),
        pl.BlockSpec((None, HIST_ROWS, POOL_WIDTH), lambda b, s: (b, 0, 0)),
        full(cnt_in),
    ] + [full(c) for c in consts]
    out_shape = [
        jax.ShapeDtypeStruct((nb, seq, D_MODEL), f32),
        jax.ShapeDtypeStruct((nb * n_tiles * n_valid, D_MODEL // 2), jnp.uint32),
        jax.ShapeDtypeStruct((nb, TOP_K, seq), jnp.int32),
        jax.ShapeDtypeStruct((nb, TOP_K, seq), jnp.int32),
        jax.ShapeDtypeStruct((nb * n_tiles * n_valid, GATE_COLS), f32),
        jax.ShapeDtypeStruct((N_EXPERTS, 128), f32),
        jax.ShapeDtypeStruct((nb, WINDOW, KV_WIDTH), f32),
        jax.ShapeDtypeStruct((nb, WINDOW, KV_WIDTH), f32),
        jax.ShapeDtypeStruct((nb, HIST_ROWS, POOL_WIDTH), f32),
    ]
    out_specs = [
        pl.BlockSpec((None, tile, D_MODEL), lambda b, s: (b, s, 0)),
        pl.BlockSpec((n_valid, D_MODEL // 2), lambda b, s: (b * n_tiles + s, 0)),
        pl.BlockSpec((None, TOP_K, tile), lambda b, s: (b, 0, s)),
        pl.BlockSpec((None, TOP_K, tile), lambda b, s: (b, 0, s)),
        pl.BlockSpec((n_valid, GATE_COLS), lambda b, s: (b * n_tiles + s, 0)),
        pl.BlockSpec((N_EXPERTS, 128), lambda b, s: (0, 0)),
        pl.BlockSpec((None, WINDOW, KV_WIDTH), lambda b, s: (b, 0, 0)),
        pl.BlockSpec((None, WINDOW, KV_WIDTH), lambda b, s: (b, 0, 0)),
        pl.BlockSpec((None, HIST_ROWS, POOL_WIDTH), lambda b, s: (b, 0, 0)),
    ]
    scratch = [
        pltpu.VMEM((ATTN_WIDTH, tile), jnp.bfloat16),
        pltpu.VMEM((WINDOW + tile, KV_WIDTH), jnp.bfloat16),
        pltpu.VMEM((KV_WIDTH, WINDOW + tile), jnp.bfloat16),
        pltpu.VMEM((HIST_ROWS + tile, POOL_WIDTH), f32),
        pltpu.VMEM((tile, D_MODEL), jnp.bfloat16),
        pltpu.VMEM((N_EXPERTS, 128), f32),
    ]
    kern = functools.partial(_mixer_kernel, tile=tile, n_valid=n_valid, pos0=pos0,
                             mask_first=mask_first)
    return pl.pallas_call(
        kern,
        grid=(nb, n_tiles),
        in_specs=in_specs,
        out_specs=out_specs,
        out_shape=out_shape,
        scratch_shapes=scratch,
        compiler_params=pltpu.CompilerParams(
            dimension_semantics=("arbitrary", "arbitrary"),
            vmem_limit_bytes=VMEM_LIMIT),
        name="mixer",
    )(x, k_hist, v_hist, u_hist, cnt_in, *consts)


def _expert_kernel(be_ref, nv_ref, nx_ref, xs_ref, wg_hbm, bg_ref, wu_hbm, bu_ref, wd_hbm, bd_ref,
                   ys_ref, stage_s, wg_s, wu_s, wd_s, sems):
    i = pl.program_id(0)
    n_rows = nv_ref[i]
    expert = be_ref[i]
    bf16 = jnp.bfloat16

    def weight_copies(e):
        return [pltpu.make_async_copy(w_hbm.at[e], stage_s.at[k], sems.at[k])
                for k, w_hbm in enumerate((wg_hbm, wu_hbm, wd_hbm))]

    @pl.when((i == 0) & (n_rows > 0))
    def _():
        for copy in weight_copies(expert):
            copy.start()

    @pl.when((n_rows > 0) & ((i == 0) | (expert != be_ref[jnp.maximum(i - 1, 0)])))
    def _():
        for copy in weight_copies(expert):
            copy.wait()
        wg_s[...] = stage_s[0].astype(bf16)
        wu_s[...] = stage_s[1].astype(bf16)
        wd_s[...] = stage_s[2].astype(bf16)

        @pl.when(nx_ref[i] != expert)
        def _():
            for copy in weight_copies(nx_ref[i]):
                copy.start()

    def ffn(rows):
        f32 = jnp.float32
        half = D_MODEL // 2
        for r0 in range(0, rows, FFN_CHUNK):
            row = r0 + lax.broadcasted_iota(jnp.int32, (FFN_CHUNK, 1), 0)
            words = jnp.where(row < n_rows, xs_ref[r0:r0 + FFN_CHUNK, :],
                              jnp.uint32(0))
            xa, xb = _unpack_bf16_pair(words)
            a = (jnp.dot(xa, wg_s[0:half, :], preferred_element_type=f32)
                 + jnp.dot(xb, wg_s[half:, :], preferred_element_type=f32) + bg_ref[...])
            bb = (jnp.dot(xa, wu_s[0:half, :], preferred_element_type=f32)
                  + jnp.dot(xb, wu_s[half:, :], preferred_element_type=f32) + bu_ref[...])
            a = jnp.minimum(a, SWIGLU_LIMIT)
            bb = jnp.clip(bb, -SWIGLU_LIMIT, SWIGLU_LIMIT)
            act = a * (1.0 / (1.0 + jnp.exp(-SWIGLU_ALPHA * a))) * (bb + 1.0)
            y = jnp.dot(act.astype(bf16), wd_s[...], preferred_element_type=f32) + bd_ref[...]
            ys_ref[r0:r0 + FFN_CHUNK, :] = _pack_bf16_pair(y[:, 0:half], y[:, half:])
        if rows < EXPERT_BLOCK:
            ys_ref[rows:, :] = jnp.zeros((EXPERT_BLOCK - rows, half), jnp.uint32)

    quarter = EXPERT_BLOCK // 4
    for q in range(1, 5):
        @pl.when((n_rows > (q - 1) * quarter) & (n_rows <= q * quarter))
        def _(q=q):
            ffn(q * quarter)

    @pl.when(n_rows == 0)
    def _():
        ys_ref[...] = jnp.zeros_like(ys_ref)


def _expert_call(block_e, block_rows, block_next, xs, wg, bg, wu, bu, wd, bd):
    n_slots = xs.shape[0]
    n_blocks = n_slots // EXPERT_BLOCK
    w_spec = pl.BlockSpec(memory_space=pl.ANY)
    b_spec = pl.BlockSpec((None, 1, D_MODEL), lambda i, be, nv, nx: (be[i], 0, 0))
    grid_spec = pltpu.PrefetchScalarGridSpec(
        num_scalar_prefetch=3,
        grid=(n_blocks,),
        in_specs=[pl.BlockSpec((EXPERT_BLOCK, D_MODEL // 2), lambda i, be, nv, nx: (i, 0)),
                  w_spec, b_spec, w_spec, b_spec, w_spec, b_spec],
        out_specs=pl.BlockSpec((EXPERT_BLOCK, D_MODEL // 2), lambda i, be, nv, nx: (i, 0)),
        scratch_shapes=[pltpu.VMEM((3, D_MODEL, D_MODEL), jnp.float32)]
        + [pltpu.VMEM((D_MODEL, D_MODEL), jnp.bfloat16)] * 3
        + [pltpu.SemaphoreType.DMA((3,))],
    )
    return pl.pallas_call(
        _expert_kernel,
        grid_spec=grid_spec,
        out_shape=jax.ShapeDtypeStruct((n_slots, D_MODEL // 2), jnp.uint32),
        compiler_params=pltpu.CompilerParams(
            dimension_semantics=("arbitrary",),
            vmem_limit_bytes=VMEM_LIMIT),
        name="experts",
    )(block_e, block_rows, block_next, xs, wg, bg, wu, bu, wd, bd)


def _combine_kernel(out_buf_ref, x1_ref, g_ref, y0_ref, y1_ref, y2_ref, y3_ref, o_ref):
    del out_buf_ref
    g = g_ref[...]
    half = D_MODEL // 2
    lo, hi = x1_ref[:, 0:half], x1_ref[:, half:]
    for j, y_ref in enumerate((y0_ref, y1_ref, y2_ref, y3_ref)):
        w = y_ref[...]
        gj = g[:, j:j + 1]
        lo = lo + gj * lax.bitcast_convert_type(w << 16, jnp.float32)
        hi = hi + gj * lax.bitcast_convert_type(w & jnp.uint32(0xFFFF0000), jnp.float32)
    o_ref[:, 0:half] = lo
    o_ref[:, half:] = hi


def _combine_call(out_buf, x1, gates, picked, n_tok, tok0, *, out_rows, row0, tile):
    n = x1.shape[0]
    aliased = out_buf.shape == (out_rows, D_MODEL)
    assert n % tile == 0 and row0 % tile == 0
    y_specs = []
    for j in range(TOP_K):
        assert (j * n_tok + tok0) % tile == 0
        base = (j * n_tok + tok0) // tile
        y_specs.append(pl.BlockSpec((tile, D_MODEL // 2), lambda i, _b=base: (_b + i, 0)))
    return pl.pallas_call(
        _combine_kernel,
        grid=(n // tile,),
        in_specs=[pl.BlockSpec(memory_space=pl.ANY),
                  pl.BlockSpec((tile, D_MODEL), lambda i: (i, 0)),
                  pl.BlockSpec((tile, GATE_COLS), lambda i: (i, 0))] + y_specs,
        out_specs=pl.BlockSpec((tile, D_MODEL), lambda i: (row0 // tile + i, 0)),
        out_shape=jax.ShapeDtypeStruct((out_rows, D_MODEL), jnp.float32),
        input_output_aliases={0: 0} if aliased else {},
        compiler_params=pltpu.CompilerParams(dimension_semantics=("arbitrary",)),
        name="combine",
    )(out_buf, x1, gates, picked, picked, picked, picked)


def _gather_rows(table, idx):
    n = idx.shape[0]
    width = table.shape[1]
    per_worker = n // SC_WORKERS
    n_chunks = per_worker // SC_CHUNK
    mesh = plsc.VectorSubcoreMesh(core_axis_name="c", subcore_axis_name="s")

    @functools.partial(
        pl.kernel, mesh=mesh,
        out_type=jax.ShapeDtypeStruct((n, width), table.dtype),
        scratch_types=[pltpu.VMEM((SC_CHUNK,), jnp.int32),
                       pltpu.VMEM((SC_CHUNK, width), table.dtype),
                       pltpu.SemaphoreType.DMA],
        cost_estimate=pl.CostEstimate(flops=0, transcendentals=0, bytes_accessed=8 * n * width),
    )
    def gather(table_hbm, idx_hbm, out_hbm, idx_v, rows_v, sem):
        wid = lax.axis_index("s") * 2 + lax.axis_index("c")
        base = wid * per_worker

        @pl.loop(0, n_chunks)
        def _(i):
            off = base + i * SC_CHUNK
            pltpu.sync_copy(idx_hbm.at[pl.ds(off, SC_CHUNK)], idx_v)
            pltpu.async_copy(table_hbm.at[idx_v], rows_v, sem).wait()
            pltpu.sync_copy(rows_v, out_hbm.at[pl.ds(off, SC_CHUNK)])

    return gather(table, idx)


def _scatter_rows(srcs, dest, n_out):
    width = srcs[0].shape[1]
    starts = [0]
    for src in srcs:
        assert src.shape[0] % SC_CHUNK == 0
        starts.append(starts[-1] + src.shape[0] // SC_CHUNK)
    n_chunks = starts[-1]
    n = n_chunks * SC_CHUNK
    per_worker = -(-n_chunks // SC_WORKERS)
    mesh = plsc.VectorSubcoreMesh(core_axis_name="c", subcore_axis_name="s")

    @functools.partial(
        pl.kernel, mesh=mesh,
        out_type=jax.ShapeDtypeStruct((n_out, width), srcs[0].dtype),
        scratch_types=[pltpu.VMEM((SC_CHUNK,), jnp.int32)] * TOP_K
        + [pltpu.VMEM((SC_CHUNK, width), srcs[0].dtype)],
        cost_estimate=pl.CostEstimate(flops=0, transcendentals=0,
                                      bytes_accessed=4 * (1 + TOP_K) * n * width),
    )
    def scatter(*refs):
        src_hbms = refs[:len(srcs)]
        dest_hbm, out_hbm = refs[len(srcs)], refs[len(srcs) + 1]
        idx_vs, rows_v = refs[len(srcs) + 2:len(srcs) + 2 + TOP_K], refs[-1]
        wid = lax.axis_index("s") * 2 + lax.axis_index("c")

        @pl.loop(0, per_worker)
        def _(i):
            c = i * SC_WORKERS + wid
            for k, src_hbm in enumerate(src_hbms):

                @pl.when((c >= starts[k]) & (c < starts[k + 1]))
                def _():
                    pltpu.sync_copy(src_hbm.at[pl.ds((c - starts[k]) * SC_CHUNK, SC_CHUNK)], rows_v)
                    for j, idx_v in enumerate(idx_vs):
                        pltpu.sync_copy(dest_hbm.at[pl.ds(j * n + c * SC_CHUNK, SC_CHUNK)], idx_v)
                    for idx_v in idx_vs:
                        pltpu.sync_copy(rows_v, out_hbm.at[idx_v])

    return scatter(*srcs, dest)


def _t5_bucket_np(rel):
    half = NUM_BUCKETS // 2
    max_exact = half // 2
    n = np.abs(rel)
    nf = np.maximum(n, 1).astype(np.float32)
    large = max_exact + (np.log(nf / max_exact) / math.log(MAX_DISTANCE / max_exact)
                         * (half - max_exact)).astype(np.int32)
    large = np.minimum(large, half - 1)
    return np.where(rel > 0, half, 0) + np.where(n < max_exact, n, large)


def _bias_tables(rel_bias, visible):
    kap = np.arange(BAND)[:, None]
    rho = np.arange(PAIR)[None, :]
    bucket = _t5_bucket_np(kap - WINDOW - rho)
    rb = rel_bias.astype(jnp.float32)
    tab = jnp.zeros((BAND, PAIR, N_HEADS), jnp.float32)
    for bkt in range(NUM_BUCKETS):
        tab = jnp.where(jnp.asarray(bucket == bkt)[:, :, None], rb[bkt], tab)
    tab = jnp.where(jnp.asarray(visible)[:, :, None], tab, NEG_INF)
    tab = jnp.transpose(tab, (2, 0, 1)).reshape(N_KV_HEADS, GROUP, BAND, PAIR)
    return jnp.transpose(tab, (0, 2, 1, 3)).reshape(N_KV_HEADS, BAND, GROUP * PAIR)


def _mixer_consts(l, norm1_g, w_in, q_norm_g, k_norm_g, rel_bias, sinks, w_pool, pool_scale,
                  w_out, norm2_g, w_router, b_router, visible, tile):
    f32, bf16 = jnp.float32, jnp.bfloat16
    q_off, k_off, v_off = POOL_WIDTH, POOL_WIDTH + ATTN_WIDTH, POOL_WIDTH + ATTN_WIDTH + KV_WIDTH
    w = w_in[l]
    w_ukv = jnp.concatenate([w[:, :q_off], w[:, k_off:]], axis=1).astype(bf16)
    w_qvt = jnp.transpose(jnp.concatenate([w[:, q_off:k_off], w[:, v_off:]], axis=1)).astype(bf16)
    lane_head = np.arange(KV_WIDTH) // HEAD_DIM
    blockdiag = jnp.asarray(lane_head[:, None] == lane_head[None, :], bf16)
    sink_rows = jnp.repeat(sinks[l].astype(f32).reshape(N_KV_HEADS, GROUP), PAIR, axis=1)
    wr = w_router[l].astype(f32)
    wr_hi = wr.astype(bf16)
    wr_lo = (wr - wr_hi.astype(f32)).astype(bf16)
    wr_parts = jnp.pad(jnp.concatenate([wr_hi, wr_lo], axis=1), ((0, 0), (0, 128 - 2 * N_EXPERTS)))
    tri = jnp.asarray(np.arange(tile)[:, None] < np.arange(tile)[None, :], bf16)
    return [
        norm1_g[l].reshape(1, D_MODEL).astype(f32), w_ukv, w_qvt,
        q_norm_g[l].reshape(HEAD_DIM, 1).astype(f32),
        jnp.tile(k_norm_g[l].astype(f32), N_KV_HEADS).reshape(1, KV_WIDTH),
        blockdiag, _bias_tables(rel_bias, visible), sink_rows,
        w_pool[l].astype(bf16), pool_scale[l].reshape(1, POOL_WIDTH).astype(f32),
        w_out[l].astype(bf16), norm2_g[l].reshape(1, D_MODEL).astype(f32),
        wr_parts, b_router[l].reshape(N_EXPERTS, 1).astype(f32), tri,
    ]


def _visibility():
    kap = np.arange(BAND)[:, None]
    rho = np.arange(PAIR)[None, :]
    kc, qc = kap // CHUNK, rho // CHUNK
    prompt = (kc >= qc) & (kc <= qc + WINDOW // CHUNK)
    return prompt


def kernel(x_prompt, x_sample, cache_k, cache_v, state_pool, norm1_g, w_in, q_norm_g, k_norm_g,
           rel_bias, sinks, w_pool, pool_scale, w_out, norm2_g, w_router, b_router,
           w_gate, b_gate, w_up, b_up, w_down, b_down):
    f32, bf16 = jnp.float32, jnp.bfloat16
    depth = w_in.shape[0]
    nb, seq, _ = x_prompt.shape
    ndb, dseq, _ = x_sample.shape
    cache_len = cache_k.shape[2]
    assert seq % MIX_TILE == 0 and cache_len == WINDOW and HIST_ROWS <= dseq <= PAIR
    n_p, n_s = nb * seq, ndb * dseq
    n_tok = n_p + n_s
    assert n_p % COMBINE_TILE == 0 and n_tok % COMBINE_TILE == 0 and n_s % 8 == 0

    vis_prompt = _visibility()
    vis_sample = np.broadcast_to(np.arange(BAND)[:, None] < WINDOW + dseq, (BAND, PAIR))

    nb1 = nb // 4
    nb2 = nb - nb1
    assert nb1 > 0 and nb2 > 0 and (nb2 * seq) % COMBINE_TILE == 0

    xp, xs = x_prompt, x_sample
    outs = [[] for _ in range(6)]
    for l in range(depth):
        wl = (l, norm1_g, w_in, q_norm_g, k_norm_g, rel_bias, sinks, w_pool, pool_scale, w_out,
              norm2_g, w_router, b_router)
        consts_p = _mixer_consts(*wl, vis_prompt, MIX_TILE)
        moe_w = (w_gate[l], b_gate[l].reshape(N_EXPERTS, 1, D_MODEL).astype(f32),
                 w_up[l], b_up[l].reshape(N_EXPERTS, 1, D_MODEL).astype(f32),
                 w_down[l], b_down[l].reshape(N_EXPERTS, 1, D_MODEL).astype(f32))
        cnt0 = jnp.zeros((N_EXPERTS, 128), f32)

        def prompt_mixer(stream0, n_streams, cnt_in):
            zk = jnp.zeros((n_streams, WINDOW, KV_WIDTH), f32)
            zu = jnp.zeros((n_streams, HIST_ROWS, POOL_WIDTH), f32)
            return _mixer_call(xp, zk, zk, zu, cnt_in, consts_p, stream0=stream0, tile=MIX_TILE,
                               n_valid=MIX_TILE, pos0=0, mask_first=True)

        (x1_a, h_a, idx_a, rank_a, gate_a, cnt_a, k_a, v_a, u_a) = prompt_mixer(0, nb1, cnt0)
        picked_a = _moe_rows([h_a], [idx_a], [rank_a], [seq], cnt_a, moe_w)
        xs_pad = jnp.pad(xs, ((0, 0), (0, PAIR - dseq), (0, 0)))
        uh = jnp.pad(state_pool[l], ((0, 0), (HIST_ROWS - POOL_HIST, 0), (0, 0)))
        (xs1, h_s, idx_s, rank_s, gate_s, cnt_s, k_s, v_s, u_s) = _mixer_call(
            xs_pad, cache_k[l].reshape(ndb, WINDOW, KV_WIDTH),
            cache_v[l].reshape(ndb, WINDOW, KV_WIDTH), uh, cnt0,
            _mixer_consts(*wl, vis_sample, PAIR),
            stream0=0, tile=PAIR, n_valid=dseq, pos0=PAST_LEN, mask_first=False)
        (x1_b, h_b, idx_b, rank_b, gate_b, cnt_b, k_b, v_b, u_b) = prompt_mixer(nb1, nb2, cnt_s)
        picked_b = _moe_rows([h_b, h_s], [idx_b, idx_s], [rank_b, rank_s], [seq, dseq], cnt_b, moe_w)

        n_a, n_b = nb1 * seq, nb2 * seq
        no_buf = jnp.zeros((8, 128), f32)
        xp_rows = _combine_call(no_buf, x1_b.reshape(n_b, D_MODEL), gate_b, picked_b, n_b + n_s, 0,
                                out_rows=n_p, row0=n_a, tile=COMBINE_TILE)
        xp_rows = _combine_call(xp_rows, x1_a.reshape(n_a, D_MODEL), gate_a, picked_a, n_a, 0,
                                out_rows=n_p, row0=0, tile=COMBINE_TILE)
        xp = xp_rows.reshape(nb, seq, D_MODEL)
        xs = _combine_call(no_buf, xs1[:, :dseq].reshape(n_s, D_MODEL), gate_s, picked_b, n_b + n_s,
                           n_b, out_rows=n_s, row0=0, tile=COMBINE_TILE).reshape(ndb, dseq, D_MODEL)

        outs[0].append(jnp.concatenate([k_a, k_b]).reshape(nb, WINDOW, N_KV_HEADS, HEAD_DIM))
        outs[1].append(jnp.concatenate([v_a, v_b]).reshape(nb, WINDOW, N_KV_HEADS, HEAD_DIM))
        outs[2].append(jnp.concatenate([u_a, u_b])[:, HIST_ROWS - POOL_HIST:])
        outs[3].append(k_s[:, :dseq].reshape(ndb, dseq, N_KV_HEADS, HEAD_DIM))
        outs[4].append(v_s[:, :dseq].reshape(ndb, dseq, N_KV_HEADS, HEAD_DIM))
        outs[5].append(u_s[:, HIST_ROWS - POOL_HIST:])
    return (xp, xs) + tuple(jnp.stack(o) for o in outs)


def _moe_rows(hs, idxs, ranks, n_reals, cnt, moe_w):
    n_tok = sum(h.shape[0] for h in hs)
    n_assign = n_tok * TOP_K
    gather_quant = SC_WORKERS * SC_CHUNK
    n_blocks = -(-(n_assign + N_EXPERTS * (EXPERT_BLOCK - 1)) // EXPERT_BLOCK)
    n_blocks = -(-n_blocks // (gather_quant // EXPERT_BLOCK)) * (gather_quant // EXPERT_BLOCK)
    n_pick = -(-n_assign // gather_quant) * gather_quant
    e_ids = jnp.arange(N_EXPERTS, dtype=jnp.int32)

    def per_token(arrs):
        return jnp.concatenate(
            [jnp.transpose(a[:, :, :n], (1, 0, 2)).reshape(TOP_K, -1) for a, n in zip(arrs, n_reals)],
            axis=1)

    counts = cnt[:, 0].astype(jnp.int32)
    pcounts = (counts + EXPERT_BLOCK - 1) // EXPERT_BLOCK * EXPERT_BLOCK
    pend = jnp.cumsum(pcounts)
    pstart = pend - pcounts
    idx_all = per_token(idxs)
    dest = per_token(ranks) + jnp.sum(
        jnp.where(idx_all[None] == e_ids[:, None, None], pstart[:, None, None], 0), axis=0)
    dest = dest.reshape(-1)
    blk0 = jnp.arange(n_blocks, dtype=jnp.int32) * EXPERT_BLOCK
    block_e = jnp.minimum(jnp.sum((pend[None, :] <= blk0[:, None]).astype(jnp.int32), axis=1),
                          N_EXPERTS - 1)
    block_rows = jnp.clip(jnp.sum(jnp.where(block_e[:, None] == e_ids[None, :],
                                             (pstart + counts)[None, :], 0), axis=1) - blk0,
                          0, EXPERT_BLOCK).astype(jnp.int32)

    later = (counts > 0)[None, :] & (e_ids[None, :] > e_ids[:, None])
    next_e = jnp.min(jnp.where(later, e_ids[None, :], N_EXPERTS), axis=1)
    next_e = jnp.where(next_e == N_EXPERTS, e_ids, next_e)
    block_next = jnp.sum(jnp.where(block_e[:, None] == e_ids[None, :], next_e[None, :], 0),
                         axis=1).astype(jnp.int32)

    x_sorted = _scatter_rows(hs, dest, n_blocks * EXPERT_BLOCK)
    y_sorted = _expert_call(block_e, block_rows, block_next, x_sorted, *moe_w)
    return _gather_rows(y_sorted, jnp.pad(dest, (0, n_pick - n_assign)))
```

```python
import functools
import math

import numpy as np
import jax
import jax.numpy as jnp
from jax import lax
from jax.experimental import pallas as pl
from jax.experimental.pallas import tpu as pltpu
from jax.experimental.pallas import tpu_sc as plsc

D_MODEL = 1024
CHUNK = 64
POOL_WIDTH = 512
POOL_WINDOWS = (2, 4, 8, 16)
POOL_GROUP = 128
POOL_HIST = 15
ATTN_WIDTH = 512
HEAD_DIM = 64
N_HEADS = 8
N_KV_HEADS = 2
GROUP = 4
KV_WIDTH = 128
WINDOW = 128
NUM_BUCKETS = 32
MAX_DISTANCE = 128
PAST_LEN = 2048
N_EXPERTS = 32
TOP_K = 4
SWIGLU_LIMIT = 7.0
SWIGLU_ALPHA = 1.702
EPS = 1e-5
NEG_INF = -1e30
ATTN_SCALE = HEAD_DIM ** -0.5

PAIR = 2 * CHUNK
BAND = PAIR + WINDOW
HIST_ROWS = 16
GATE_COLS = 8
MIX_TILE = 1024
MIX_SUB = 1024
EXPERT_BLOCK = 1024
FFN_CHUNK = 256
COMBINE_TILE = 1024
SC_WORKERS = 32
SC_CHUNK = 64
VMEM_LIMIT = 56 * 1024 * 1024


def _pack_bf16_pair(a, b):
    ab = lax.bitcast_convert_type(a.astype(jnp.bfloat16).astype(jnp.float32), jnp.uint32)
    bb = lax.bitcast_convert_type(b.astype(jnp.bfloat16).astype(jnp.float32), jnp.uint32)
    return (ab >> 16) | (bb & jnp.uint32(0xFFFF0000))


def _unpack_bf16_pair(w):
    a = lax.bitcast_convert_type(w << 16, jnp.float32).astype(jnp.bfloat16)
    b = lax.bitcast_convert_type(w & jnp.uint32(0xFFFF0000), jnp.float32).astype(jnp.bfloat16)
    return a, b


def _mixer_kernel(x_ref, kh_ref, vh_ref, uh_ref, cnt_in_ref,
                  g1_ref, wukv_ref, wqvt_ref, qg_ref, kg_ref, bd_ref, bias_ref, sink_ref,
                  wpool_ref, pscale_ref, wout_ref, g2_ref, wr_ref, br_ref, tri_ref,
                  x1_ref, hp_ref, idx_ref, rank_ref, gate_ref, cnt_ref, ko_ref, vo_ref, uo_ref,
                  qt_s, kb_s, vt_s, ub_s, mix_s, cnt_s,
                  *, tile, sub, n_valid, pos0, mask_first):
    b = pl.program_id(0)
    s = pl.program_id(1)
    bf16 = jnp.bfloat16
    f32 = jnp.float32

    @pl.when((b == 0) & (s == 0))
    def _():
        cnt_s[...] = cnt_in_ref[...]

    @pl.when(s == 0)
    def _():
        kb_s[0:WINDOW, :] = kh_ref[...].astype(bf16)
        vt_s[:, 0:WINDOW] = jnp.transpose(vh_ref[...]).astype(bf16)
        ub_s[0:HIST_ROWS, :] = uh_ref[...]

    @pl.when(s > 0)
    def _():
        kb_s[0:WINDOW, :] = kb_s[tile:tile + WINDOW, :]
        vt_s[:, 0:WINDOW] = vt_s[:, tile:tile + WINDOW]
        ub_s[0:HIST_ROWS, :] = ub_s[tile:tile + HIST_ROWS, :]

    for r0 in range(0, tile, sub):
        _mixer_rows(r0, s, x_ref, g1_ref, wukv_ref, wqvt_ref, qg_ref, kg_ref, bd_ref, bias_ref,
                    sink_ref, wpool_ref, pscale_ref, wout_ref, g2_ref, wr_ref, br_ref, tri_ref,
                    x1_ref, hp_ref, idx_ref, rank_ref, gate_ref, cnt_ref, ko_ref, vo_ref, uo_ref,
                    qt_s, kb_s, vt_s, ub_s, mix_s, cnt_s,
                    tile=tile, sub=sub, n_valid=n_valid, pos0=pos0, mask_first=mask_first)


def _mixer_rows(r0, s, x_ref, g1_ref, wukv_ref, wqvt_ref, qg_ref, kg_ref, bd_ref, bias_ref,
                sink_ref, wpool_ref, pscale_ref, wout_ref, g2_ref, wr_ref, br_ref, tri_ref,
                x1_ref, hp_ref, idx_ref, rank_ref, gate_ref, cnt_ref, ko_ref, vo_ref, uo_ref,
                qt_s, kb_s, vt_s, ub_s, mix_s, cnt_s, *, tile, sub, n_valid, pos0, mask_first):
    bf16 = jnp.bfloat16
    f32 = jnp.float32
    rows = slice(r0, r0 + sub)

    x = x_ref[rows, :]
    xn = (x * lax.rsqrt(jnp.mean(x * x, axis=-1, keepdims=True) + EPS) * g1_ref[...]).astype(bf16)
    z = jnp.dot(xn, wukv_ref[...], preferred_element_type=f32)
    zt = lax.dot_general(wqvt_ref[...], xn, (((1,), (1,)), ((), ())),
                         preferred_element_type=f32)
    u = z[:, 0:POOL_WIDTH]
    kz = z[:, POOL_WIDTH:POOL_WIDTH + KV_WIDTH]
    v = z[:, POOL_WIDTH + KV_WIDTH:]

    ksq = kz * kz
    kss = jnp.dot(ksq.astype(bf16), bd_ref[...], preferred_element_type=f32)
    kn = kz * lax.rsqrt(kss * (1.0 / HEAD_DIM) + EPS) * kg_ref[...]
    kb_s[WINDOW + r0:WINDOW + r0 + sub, :] = kn.astype(bf16)
    vt_s[:, WINDOW + r0:WINDOW + r0 + sub] = zt[ATTN_WIDTH:, :].astype(bf16)
    ub_s[HIST_ROWS + r0:HIST_ROWS + r0 + sub, :] = u

    if r0 + sub == tile:
        row0 = max(n_valid, WINDOW) - WINDOW - r0
        ko_ref[...] = kn[row0:row0 + WINDOW, :]
        vo_ref[...] = v[row0:row0 + WINDOW, :]
        uo_ref[...] = u[n_valid - HIST_ROWS - r0:n_valid - r0, :]

    for hd in range(N_HEADS):
        qh = zt[hd * HEAD_DIM:(hd + 1) * HEAD_DIM, :]
        ss = jnp.sum(qh * qh, axis=0, keepdims=True)
        qn = qh * (lax.rsqrt(ss * (1.0 / HEAD_DIM) + EPS) * ATTN_SCALE) * qg_ref[...]
        qt_s[hd * HEAD_DIM:(hd + 1) * HEAD_DIM, rows] = qn.astype(bf16)

    pos = pos0 + s * tile + r0 + lax.broadcasted_iota(jnp.int32, (sub, 1), 0)
    for g, w in enumerate(POOL_WINDOWS):
        e = ub_s[r0:r0 + HIST_ROWS + sub, g * POOL_GROUP:(g + 1) * POOL_GROUP]
        acc = e
        for lvl in range(g + 1):
            acc = acc + pltpu.roll(acc, 2 ** lvl, axis=0)
        inv_cnt = 1.0 / jnp.minimum(pos + 1, w).astype(f32)
        d = (acc[HIST_ROWS:, :] * inv_cnt - e[HIST_ROWS:, :]).astype(bf16)
        y = jnp.dot(d, wpool_ref[g], preferred_element_type=f32)
        y = y * pscale_ref[:, g * POOL_GROUP:(g + 1) * POOL_GROUP]
        mix_s[rows, g * POOL_GROUP:(g + 1) * POOL_GROUP] = y.astype(bf16)

    zeros_q = jnp.zeros((HEAD_DIM, GROUP * PAIR), bf16)
    for p in range(r0 // PAIR, (r0 + sub) // PAIR):
        k_band = kb_s[p * PAIR:p * PAIR + BAND, :]
        o_parts = []
        for h in range(N_KV_HEADS):
            qcat = jnp.concatenate(
                [qt_s[(h * GROUP + g) * HEAD_DIM:(h * GROUP + g + 1) * HEAD_DIM,
                      p * PAIR:(p + 1) * PAIR] for g in range(GROUP)], axis=1)
            rhs = jnp.concatenate([qcat, zeros_q] if h == 0 else [zeros_q, qcat], axis=0)
            st = jnp.dot(k_band, rhs, preferred_element_type=f32) + bias_ref[h]
            if mask_first and p == 0:
                krow = lax.broadcasted_iota(jnp.int32, (BAND, 1), 0)
                st = jnp.where((krow >= WINDOW) | (s > 0), st, NEG_INF)
            sink = sink_ref[h:h + 1, :]
            m = jnp.maximum(jnp.max(st, axis=0, keepdims=True), sink)
            ex = jnp.exp(st - m)
            den = jnp.sum(ex, axis=0, keepdims=True) + jnp.exp(sink - m)
            v_band = vt_s[h * HEAD_DIM:(h + 1) * HEAD_DIM, p * PAIR:p * PAIR + BAND]
            ot = jnp.dot(v_band, ex.astype(bf16), preferred_element_type=f32) / den
            for g in range(GROUP):
                o_parts.append(ot[:, g * PAIR:(g + 1) * PAIR])
        o_all = jnp.concatenate(o_parts, axis=0)
        mix_s[p * PAIR:(p + 1) * PAIR, POOL_WIDTH:] = jnp.transpose(o_all).astype(bf16)

    x1 = x + jnp.dot(mix_s[rows, :], wout_ref[...], preferred_element_type=f32)
    x1_ref[rows, :] = x1

    n_real = min(sub, n_valid - r0)
    hn = x1 * lax.rsqrt(jnp.mean(x1 * x1, axis=-1, keepdims=True) + EPS) * g2_ref[...]
    hp_ref[r0:r0 + n_real, :] = _pack_bf16_pair(hn[0:n_real, 0:D_MODEL // 2],
                                                hn[0:n_real, D_MODEL // 2:])
    h_hi = hn.astype(bf16)
    h_lo = (hn - h_hi.astype(f32)).astype(bf16)
    parts = jnp.transpose(jnp.dot(h_hi, wr_ref[...], preferred_element_type=f32)
                          + jnp.dot(h_lo, wr_ref[...], preferred_element_type=f32))
    lt = parts[0:N_EXPERTS, :] + parts[N_EXPERTS:2 * N_EXPERTS, :] + br_ref[...]

    eidx = lax.broadcasted_iota(jnp.int32, (N_EXPERTS, sub), 0).astype(f32)
    vals, hots = [], []
    for j in range(TOP_K):
        m = jnp.max(lt, axis=0, keepdims=True)
        sel = jnp.min(jnp.where(lt == m, eidx, float(N_EXPERTS)), axis=0, keepdims=True)
        hot = eidx == sel
        lt = jnp.where(hot, -jnp.inf, lt)
        idx_ref[j:j + 1, rows] = sel.astype(jnp.int32)
        vals.append(m)
        hots.append(hot)
    exps = [jnp.exp(vv - vals[0]) for vv in vals]
    esum = exps[0] + exps[1] + exps[2] + exps[3]
    grow = lax.broadcasted_iota(jnp.int32, (GATE_COLS, sub), 0)
    gmat = jnp.zeros((GATE_COLS, sub), f32)
    for j in range(TOP_K):
        gmat = jnp.where(grow == j, exps[j] / esum, gmat)
    gmat = jnp.concatenate([gmat, jnp.zeros((128 - GATE_COLS, sub), f32)], axis=0)
    gate_ref[r0:r0 + n_real, :] = jnp.transpose(gmat)[0:n_real, 0:GATE_COLS]

    chosen_f = sum(jnp.where(hot, 1.0, 0.0) for hot in hots)
    if n_real < sub:
        lane = lax.broadcasted_iota(jnp.int32, (N_EXPERTS, sub), 1)
        chosen_f = jnp.where(lane < n_real, chosen_f, 0.0)
    before = jnp.dot(chosen_f.astype(bf16), tri_ref[...], preferred_element_type=f32)
    base = before + cnt_s[:, 0:1]
    for j in range(TOP_K):
        rank_ref[j:j + 1, rows] = jnp.sum(jnp.where(hots[j], base, 0.0), axis=0,
                                          keepdims=True).astype(jnp.int32)
    cnt_new = cnt_s[...] + jnp.sum(chosen_f, axis=1, keepdims=True)
    cnt_s[...] = cnt_new
    cnt_ref[...] = cnt_new


def _mixer_call(x, k_hist, v_hist, u_hist, cnt_in, consts, *, stream0, tile, sub, n_valid, pos0,
                mask_first):
    seq = x.shape[1]
    nb = k_hist.shape[0]
    n_tiles = seq // tile
    f32 = jnp.float32
    assert n_valid % 8 == 0 and tile % sub == 0 and sub % PAIR == 0
    assert n_valid == tile or sub == tile

    def full(a):
        nd = a.ndim
        return pl.BlockSpec(a.shape, lambda b, s, _nd=nd: (0,) * _nd)

    in_specs = [
        pl.BlockSpec((None, tile, D_MODEL), lambda b, s: (stream0 + b, s, 0)),
        pl.BlockSpec((None, WINDOW, KV_WIDTH), lambda b, s: (b, 0, 0)),
        pl.BlockSpec((None, WINDOW, KV_WIDTH), lambda b, s: (b, 0, 0)),
        pl.BlockSpec((None, HIST_ROWS, POOL_WIDTH), lambda b, s: (b, 0, 0)),
        full(cnt_in),
    ] + [full(c) for c in consts]
    out_shape = [
        jax.ShapeDtypeStruct((nb, seq, D_MODEL), f32),
        jax.ShapeDtypeStruct((nb * n_tiles * n_valid, D_MODEL // 2), jnp.uint32),
        jax.ShapeDtypeStruct((nb, TOP_K, seq), jnp.int32),
        jax.ShapeDtypeStruct((nb, TOP_K, seq), jnp.int32),
        jax.ShapeDtypeStruct((nb * n_tiles * n_valid, GATE_COLS), f32),
        jax.ShapeDtypeStruct((N_EXPERTS, 128), f32),
        jax.ShapeDtypeStruct((nb, WINDOW, KV_WIDTH), f32),
        jax.ShapeDtypeStruct((nb, WINDOW, KV_WIDTH), f32),
        jax.ShapeDtypeStruct((nb, HIST_ROWS, POOL_WIDTH), f32),
    ]
    out_specs = [
        pl.BlockSpec((None, tile, D_MODEL), lambda b, s: (b, s, 0)),
        pl.BlockSpec((n_valid, D_MODEL // 2), lambda b, s: (b * n_tiles + s, 0)),
        pl.BlockSpec((None, TOP_K, tile), lambda b, s: (b, 0, s)),
        pl.BlockSpec((None, TOP_K, tile), lambda b, s: (b, 0, s)),
        pl.BlockSpec((n_valid, GATE_COLS), lambda b, s: (b * n_tiles + s, 0)),
        pl.BlockSpec((N_EXPERTS, 128), lambda b, s: (0, 0)),
        pl.BlockSpec((None, WINDOW, KV_WIDTH), lambda b, s: (b, 0, 0)),
        pl.BlockSpec((None, WINDOW, KV_WIDTH), lambda b, s: (b, 0, 0)),
        pl.BlockSpec((None, HIST_ROWS, POOL_WIDTH), lambda b, s: (b, 0, 0)),
    ]
    scratch = [
        pltpu.VMEM((ATTN_WIDTH, tile), jnp.bfloat16),
        pltpu.VMEM((WINDOW + tile, KV_WIDTH), jnp.bfloat16),
        pltpu.VMEM((KV_WIDTH, WINDOW + tile), jnp.bfloat16),
        pltpu.VMEM((HIST_ROWS + tile, POOL_WIDTH), f32),
        pltpu.VMEM((tile, D_MODEL), jnp.bfloat16),
        pltpu.VMEM((N_EXPERTS, 128), f32),
    ]
    kern = functools.partial(_mixer_kernel, tile=tile, sub=sub, n_valid=n_valid, pos0=pos0,
                             mask_first=mask_first)
    return pl.pallas_call(
        kern,
        grid=(nb, n_tiles),
        in_specs=in_specs,
        out_specs=out_specs,
        out_shape=out_shape,
        scratch_shapes=scratch,
        compiler_params=pltpu.CompilerParams(
            dimension_semantics=("arbitrary", "arbitrary"),
            vmem_limit_bytes=VMEM_LIMIT),
        name="mixer",
    )(x, k_hist, v_hist, u_hist, cnt_in, *consts)


def _expert_kernel(be_ref, nv_ref, nx_ref, xs_ref, wg_hbm, bg_ref, wu_hbm, bu_ref, wd_hbm, bd_ref,
                   ys_ref, stage_s, wg_s, wu_s, wd_s, sems):
    i = pl.program_id(0)
    n_rows = nv_ref[i]
    expert = be_ref[i]
    bf16 = jnp.bfloat16

    def weight_copies(e):
        return [pltpu.make_async_copy(w_hbm.at[e], stage_s.at[k], sems.at[k])
                for k, w_hbm in enumerate((wg_hbm, wu_hbm, wd_hbm))]

    @pl.when((i == 0) & (n_rows > 0))
    def _():
        for copy in weight_copies(expert):
            copy.start()

    @pl.when((n_rows > 0) & ((i == 0) | (expert != be_ref[jnp.maximum(i - 1, 0)])))
    def _():
        for copy in weight_copies(expert):
            copy.wait()
        wg_s[...] = stage_s[0].astype(bf16)
        wu_s[...] = stage_s[1].astype(bf16)
        wd_s[...] = stage_s[2].astype(bf16)

        @pl.when(nx_ref[i] != expert)
        def _():
            for copy in weight_copies(nx_ref[i]):
                copy.start()

    def ffn(rows):
        f32 = jnp.float32
        half = D_MODEL // 2
        for r0 in range(0, rows, FFN_CHUNK):
            row = r0 + lax.broadcasted_iota(jnp.int32, (FFN_CHUNK, 1), 0)
            words = jnp.where(row < n_rows, xs_ref[r0:r0 + FFN_CHUNK, :],
                              jnp.uint32(0))
            xa, xb = _unpack_bf16_pair(words)
            a = (jnp.dot(xa, wg_s[0:half, :], preferred_element_type=f32)
                 + jnp.dot(xb, wg_s[half:, :], preferred_element_type=f32) + bg_ref[...])
            bb = (jnp.dot(xa, wu_s[0:half, :], preferred_element_type=f32)
                  + jnp.dot(xb, wu_s[half:, :], preferred_element_type=f32) + bu_ref[...])
            a = jnp.minimum(a, SWIGLU_LIMIT)
            bb = jnp.clip(bb, -SWIGLU_LIMIT, SWIGLU_LIMIT)
            act = a * (1.0 / (1.0 + jnp.exp(-SWIGLU_ALPHA * a))) * (bb + 1.0)
            y = jnp.dot(act.astype(bf16), wd_s[...], preferred_element_type=f32) + bd_ref[...]
            ys_ref[r0:r0 + FFN_CHUNK, :] = _pack_bf16_pair(y[:, 0:half], y[:, half:])
        if rows < EXPERT_BLOCK:
            ys_ref[rows:, :] = jnp.zeros((EXPERT_BLOCK - rows, half), jnp.uint32)

    quarter = EXPERT_BLOCK // 4
    for q in range(1, 5):
        @pl.when((n_rows > (q - 1) * quarter) & (n_rows <= q * quarter))
        def _(q=q):
            ffn(q * quarter)

    @pl.when(n_rows == 0)
    def _():
        ys_ref[...] = jnp.zeros_like(ys_ref)


def _expert_call(block_e, block_rows, block_next, xs, wg, bg, wu, bu, wd, bd):
    n_slots = xs.shape[0]
    n_blocks = n_slots // EXPERT_BLOCK
    w_spec = pl.BlockSpec(memory_space=pl.ANY)
    b_spec = pl.BlockSpec((None, 1, D_MODEL), lambda i, be, nv, nx: (be[i], 0, 0))
    grid_spec = pltpu.PrefetchScalarGridSpec(
        num_scalar_prefetch=3,
        grid=(n_blocks,),
        in_specs=[pl.BlockSpec((EXPERT_BLOCK, D_MODEL // 2), lambda i, be, nv, nx: (i, 0)),
                  w_spec, b_spec, w_spec, b_spec, w_spec, b_spec],
        out_specs=pl.BlockSpec((EXPERT_BLOCK, D_MODEL // 2), lambda i, be, nv, nx: (i, 0)),
        scratch_shapes=[pltpu.VMEM((3, D_MODEL, D_MODEL), jnp.float32)]
        + [pltpu.VMEM((D_MODEL, D_MODEL), jnp.bfloat16)] * 3
        + [pltpu.SemaphoreType.DMA((3,))],
    )
    return pl.pallas_call(
        _expert_kernel,
        grid_spec=grid_spec,
        out_shape=jax.ShapeDtypeStruct((n_slots, D_MODEL // 2), jnp.uint32),
        compiler_params=pltpu.CompilerParams(
            dimension_semantics=("arbitrary",),
            vmem_limit_bytes=VMEM_LIMIT),
        name="experts",
    )(block_e, block_rows, block_next, xs, wg, bg, wu, bu, wd, bd)


def _combine_kernel(out_buf_ref, x1_ref, g_ref, y0_ref, y1_ref, y2_ref, y3_ref, o_ref):
    del out_buf_ref
    g = g_ref[...]
    half = D_MODEL // 2
    lo, hi = x1_ref[:, 0:half], x1_ref[:, half:]
    for j, y_ref in enumerate((y0_ref, y1_ref, y2_ref, y3_ref)):
        w = y_ref[...]
        gj = g[:, j:j + 1]
        lo = lo + gj * lax.bitcast_convert_type(w << 16, jnp.float32)
        hi = hi + gj * lax.bitcast_convert_type(w & jnp.uint32(0xFFFF0000), jnp.float32)
    o_ref[:, 0:half] = lo
    o_ref[:, half:] = hi


def _combine_call(out_buf, x1, gates, picked, stride, tok0, *, out_rows, row0, tile):
    n = x1.shape[0]
    aliased = out_buf.shape == (out_rows, D_MODEL)
    assert n % tile == 0 and row0 % tile == 0
    y_specs = []
    for j in range(TOP_K):
        assert (j * stride + tok0) % tile == 0
        base = (j * stride + tok0) // tile
        y_specs.append(pl.BlockSpec((tile, D_MODEL // 2), lambda i, _b=base: (_b + i, 0)))
    return pl.pallas_call(
        _combine_kernel,
        grid=(n // tile,),
        in_specs=[pl.BlockSpec(memory_space=pl.ANY),
                  pl.BlockSpec((tile, D_MODEL), lambda i: (i, 0)),
                  pl.BlockSpec((tile, GATE_COLS), lambda i: (i, 0))] + y_specs,
        out_specs=pl.BlockSpec((tile, D_MODEL), lambda i: (row0 // tile + i, 0)),
        out_shape=jax.ShapeDtypeStruct((out_rows, D_MODEL), jnp.float32),
        input_output_aliases={0: 0} if aliased else {},
        compiler_params=pltpu.CompilerParams(dimension_semantics=("arbitrary",),
                                             vmem_limit_bytes=VMEM_LIMIT),
        name="combine",
    )(out_buf, x1, gates, picked, picked, picked, picked)


def _gather_rows(table, idx):
    n = idx.shape[0]
    width = table.shape[1]
    per_worker = n // SC_WORKERS
    n_chunks = per_worker // SC_CHUNK
    mesh = plsc.VectorSubcoreMesh(core_axis_name="c", subcore_axis_name="s")

    @functools.partial(
        pl.kernel, mesh=mesh,
        out_type=jax.ShapeDtypeStruct((n, width), table.dtype),
        scratch_types=[pltpu.VMEM((SC_CHUNK,), jnp.int32),
                       pltpu.VMEM((SC_CHUNK, width), table.dtype),
                       pltpu.SemaphoreType.DMA],
        cost_estimate=pl.CostEstimate(flops=0, transcendentals=0, bytes_accessed=8 * n * width),
    )
    def gather(table_hbm, idx_hbm, out_hbm, idx_v, rows_v, sem):
        wid = lax.axis_index("s") * 2 + lax.axis_index("c")
        base = wid * per_worker

        @pl.loop(0, n_chunks)
        def _(i):
            off = base + i * SC_CHUNK
            pltpu.sync_copy(idx_hbm.at[pl.ds(off, SC_CHUNK)], idx_v)
            pltpu.async_copy(table_hbm.at[idx_v], rows_v, sem).wait()
            pltpu.sync_copy(rows_v, out_hbm.at[pl.ds(off, SC_CHUNK)])

    return gather(table, idx)


def _scatter_rows(srcs, dest, n_out):
    width = srcs[0].shape[1]
    starts = [0]
    for src in srcs:
        assert src.shape[0] % SC_CHUNK == 0
        starts.append(starts[-1] + src.shape[0] // SC_CHUNK)
    n_chunks = starts[-1]
    n = n_chunks * SC_CHUNK
    per_worker = -(-n_chunks // SC_WORKERS)
    mesh = plsc.VectorSubcoreMesh(core_axis_name="c", subcore_axis_name="s")

    @functools.partial(
        pl.kernel, mesh=mesh,
        out_type=jax.ShapeDtypeStruct((n_out, width), srcs[0].dtype),
        scratch_types=[pltpu.VMEM((SC_CHUNK,), jnp.int32)] * TOP_K
        + [pltpu.VMEM((SC_CHUNK, width), srcs[0].dtype)],
        cost_estimate=pl.CostEstimate(flops=0, transcendentals=0,
                                      bytes_accessed=4 * (1 + TOP_K) * n * width),
    )
    def scatter(*refs):
        src_hbms = refs[:len(srcs)]
        dest_hbm, out_hbm = refs[len(srcs)], refs[len(srcs) + 1]
        idx_vs, rows_v = refs[len(srcs) + 2:len(srcs) + 2 + TOP_K], refs[-1]
        wid = lax.axis_index("s") * 2 + lax.axis_index("c")

        @pl.loop(0, per_worker)
        def _(i):
            c = i * SC_WORKERS + wid
            for k, src_hbm in enumerate(src_hbms):

                @pl.when((c >= starts[k]) & (c < starts[k + 1]))
                def _():
                    pltpu.sync_copy(src_hbm.at[pl.ds((c - starts[k]) * SC_CHUNK, SC_CHUNK)], rows_v)
                    for j, idx_v in enumerate(idx_vs):
                        pltpu.sync_copy(dest_hbm.at[pl.ds(j * n + c * SC_CHUNK, SC_CHUNK)], idx_v)
                    for idx_v in idx_vs:
                        pltpu.sync_copy(rows_v, out_hbm.at[idx_v])

    return scatter(*srcs, dest)


def _t5_bucket_np(rel):
    half = NUM_BUCKETS // 2
    max_exact = half // 2
    n = np.abs(rel)
    nf = np.maximum(n, 1).astype(np.float32)
    large = max_exact + (np.log(nf / max_exact) / math.log(MAX_DISTANCE / max_exact)
                         * (half - max_exact)).astype(np.int32)
    large = np.minimum(large, half - 1)
    return np.where(rel > 0, half, 0) + np.where(n < max_exact, n, large)


def _bias_tables(rel_bias, visible):
    kap = np.arange(BAND)[:, None]
    rho = np.arange(PAIR)[None, :]
    bucket = _t5_bucket_np(kap - WINDOW - rho)
    rb = rel_bias.astype(jnp.float32)
    tab = jnp.zeros((BAND, PAIR, N_HEADS), jnp.float32)
    for bkt in range(NUM_BUCKETS):
        tab = jnp.where(jnp.asarray(bucket == bkt)[:, :, None], rb[bkt], tab)
    tab = jnp.where(jnp.asarray(visible)[:, :, None], tab, NEG_INF)
    tab = jnp.transpose(tab, (2, 0, 1)).reshape(N_KV_HEADS, GROUP, BAND, PAIR)
    return jnp.transpose(tab, (0, 2, 1, 3)).reshape(N_KV_HEADS, BAND, GROUP * PAIR)


def _mixer_consts(l, norm1_g, w_in, q_norm_g, k_norm_g, rel_bias, sinks, w_pool, pool_scale,
                  w_out, norm2_g, w_router, b_router, visible, tile):
    f32, bf16 = jnp.float32, jnp.bfloat16
    q_off, k_off, v_off = POOL_WIDTH, POOL_WIDTH + ATTN_WIDTH, POOL_WIDTH + ATTN_WIDTH + KV_WIDTH
    w = w_in[l]
    w_ukv = jnp.concatenate([w[:, :q_off], w[:, k_off:]], axis=1).astype(bf16)
    w_qvt = jnp.transpose(jnp.concatenate([w[:, q_off:k_off], w[:, v_off:]], axis=1)).astype(bf16)
    lane_head = np.arange(KV_WIDTH) // HEAD_DIM
    blockdiag = jnp.asarray(lane_head[:, None] == lane_head[None, :], bf16)
    sink_rows = jnp.repeat(sinks[l].astype(f32).reshape(N_KV_HEADS, GROUP), PAIR, axis=1)
    wr = w_router[l].astype(f32)
    wr_hi = wr.astype(bf16)
    wr_lo = (wr - wr_hi.astype(f32)).astype(bf16)
    wr_parts = jnp.pad(jnp.concatenate([wr_hi, wr_lo], axis=1), ((0, 0), (0, 128 - 2 * N_EXPERTS)))
    tri = jnp.asarray(np.arange(tile)[:, None] < np.arange(tile)[None, :], bf16)
    return [
        norm1_g[l].reshape(1, D_MODEL).astype(f32), w_ukv, w_qvt,
        q_norm_g[l].reshape(HEAD_DIM, 1).astype(f32),
        jnp.tile(k_norm_g[l].astype(f32), N_KV_HEADS).reshape(1, KV_WIDTH),
        blockdiag, _bias_tables(rel_bias, visible), sink_rows,
        w_pool[l].astype(bf16), pool_scale[l].reshape(1, POOL_WIDTH).astype(f32),
        w_out[l].astype(bf16), norm2_g[l].reshape(1, D_MODEL).astype(f32),
        wr_parts, b_router[l].reshape(N_EXPERTS, 1).astype(f32), tri,
    ]


def _visibility():
    kap = np.arange(BAND)[:, None]
    rho = np.arange(PAIR)[None, :]
    kc, qc = kap // CHUNK, rho // CHUNK
    prompt = (kc >= qc) & (kc <= qc + WINDOW // CHUNK)
    return prompt


def kernel(x_prompt, x_sample, cache_k, cache_v, state_pool, norm1_g, w_in, q_norm_g, k_norm_g,
           rel_bias, sinks, w_pool, pool_scale, w_out, norm2_g, w_router, b_router,
           w_gate, b_gate, w_up, b_up, w_down, b_down):
    f32, bf16 = jnp.float32, jnp.bfloat16
    depth = w_in.shape[0]
    nb, seq, _ = x_prompt.shape
    ndb, dseq, _ = x_sample.shape
    cache_len = cache_k.shape[2]
    assert seq % MIX_TILE == 0 and cache_len == WINDOW and HIST_ROWS <= dseq <= PAIR
    n_p, n_s = nb * seq, ndb * dseq
    n_tok = n_p + n_s
    assert COMBINE_TILE % n_s == 0 and n_s % SC_CHUNK == 0

    vis_prompt = _visibility()
    vis_sample = np.broadcast_to(np.arange(BAND)[:, None] < WINDOW + dseq, (BAND, PAIR))

    nb1 = nb // 4
    nb2 = nb - nb1
    assert nb1 > 0 and nb2 > 0 and seq % COMBINE_TILE == 0

    xp, xs = x_prompt, x_sample
    outs = [[] for _ in range(6)]
    for l in range(depth):
        wl = (l, norm1_g, w_in, q_norm_g, k_norm_g, rel_bias, sinks, w_pool, pool_scale, w_out,
              norm2_g, w_router, b_router)
        consts_p = _mixer_consts(*wl, vis_prompt, MIX_SUB)
        moe_w = (w_gate[l], b_gate[l].reshape(N_EXPERTS, 1, D_MODEL).astype(f32),
                 w_up[l], b_up[l].reshape(N_EXPERTS, 1, D_MODEL).astype(f32),
                 w_down[l], b_down[l].reshape(N_EXPERTS, 1, D_MODEL).astype(f32))
        cnt0 = jnp.zeros((N_EXPERTS, 128), f32)

        def prompt_mixer(stream0, n_streams, cnt_in):
            zk = jnp.zeros((n_streams, WINDOW, KV_WIDTH), f32)
            zu = jnp.zeros((n_streams, HIST_ROWS, POOL_WIDTH), f32)
            return _mixer_call(xp, zk, zk, zu, cnt_in, consts_p, stream0=stream0, tile=MIX_TILE,
                               sub=MIX_SUB, n_valid=MIX_TILE, pos0=0, mask_first=True)

        (x1_a, h_a, idx_a, rank_a, gate_a, cnt_a, k_a, v_a, u_a) = prompt_mixer(0, nb1, cnt0)
        picked_a = _moe_rows([h_a], [idx_a], [rank_a], [seq], cnt_a, moe_w)
        xs_pad = jnp.pad(xs, ((0, 0), (0, PAIR - dseq), (0, 0)))
        uh = jnp.pad(state_pool[l], ((0, 0), (HIST_ROWS - POOL_HIST, 0), (0, 0)))
        (xs1, h_s, idx_s, rank_s, gate_s, cnt_s, k_s, v_s, u_s) = _mixer_call(
            xs_pad, cache_k[l].reshape(ndb, WINDOW, KV_WIDTH),
            cache_v[l].reshape(ndb, WINDOW, KV_WIDTH), uh, cnt0,
            _mixer_consts(*wl, vis_sample, PAIR),
            stream0=0, tile=PAIR, sub=PAIR, n_valid=dseq, pos0=PAST_LEN, mask_first=False)
        (x1_b, h_b, idx_b, rank_b, gate_b, cnt_b, k_b, v_b, u_b) = prompt_mixer(nb1, nb2, cnt_s)
        picked_b = _moe_rows([h_b, h_s], [idx_b, idx_s], [rank_b, rank_s], [seq, dseq], cnt_b, moe_w)

        n_a, n_b = nb1 * seq, nb2 * seq
        no_buf = jnp.zeros((8, 128), f32)
        stride_a, stride_b = _pick_stride(n_a), _pick_stride(n_b + n_s)
        xp_rows = _combine_call(no_buf, x1_b.reshape(n_b, D_MODEL), gate_b, picked_b, stride_b, 0,
                                out_rows=n_p, row0=n_a, tile=COMBINE_TILE)
        xp_rows = _combine_call(xp_rows, x1_a.reshape(n_a, D_MODEL), gate_a, picked_a, stride_a, 0,
                                out_rows=n_p, row0=0, tile=COMBINE_TILE)
        xp = xp_rows.reshape(nb, seq, D_MODEL)
        xs = _combine_call(no_buf, xs1[:, :dseq].reshape(n_s, D_MODEL), gate_s, picked_b, stride_b,
                           n_b, out_rows=n_s, row0=0, tile=n_s).reshape(ndb, dseq, D_MODEL)

        outs[0].append(jnp.concatenate([k_a, k_b]).reshape(nb, WINDOW, N_KV_HEADS, HEAD_DIM))
        outs[1].append(jnp.concatenate([v_a, v_b]).reshape(nb, WINDOW, N_KV_HEADS, HEAD_DIM))
        outs[2].append(jnp.concatenate([u_a, u_b])[:, HIST_ROWS - POOL_HIST:])
        outs[3].append(k_s[:, :dseq].reshape(ndb, dseq, N_KV_HEADS, HEAD_DIM))
        outs[4].append(v_s[:, :dseq].reshape(ndb, dseq, N_KV_HEADS, HEAD_DIM))
        outs[5].append(u_s[:, HIST_ROWS - POOL_HIST:])
    return (xp, xs) + tuple(jnp.stack(o) for o in outs)


def _moe_rows(hs, idxs, ranks, n_reals, cnt, moe_w):
    n_tok = sum(h.shape[0] for h in hs)
    n_assign = n_tok * TOP_K
    gather_quant = SC_WORKERS * SC_CHUNK
    n_blocks = -(-(n_assign + N_EXPERTS * (EXPERT_BLOCK - 1)) // EXPERT_BLOCK)
    n_blocks = -(-n_blocks // (gather_quant // EXPERT_BLOCK)) * (gather_quant // EXPERT_BLOCK)
    e_ids = jnp.arange(N_EXPERTS, dtype=jnp.int32)

    def per_token(arrs):
        return jnp.concatenate(
            [jnp.transpose(a[:, :, :n], (1, 0, 2)).reshape(TOP_K, -1) for a, n in zip(arrs, n_reals)],
            axis=1)

    counts = cnt[:, 0].astype(jnp.int32)
    pcounts = (counts + EXPERT_BLOCK - 1) // EXPERT_BLOCK * EXPERT_BLOCK
    pend = jnp.cumsum(pcounts)
    pstart = pend - pcounts
    idx_all = per_token(idxs)
    dest = per_token(ranks) + jnp.sum(
        jnp.where(idx_all[None] == e_ids[:, None, None], pstart[:, None, None], 0), axis=0)
    dest = dest.reshape(-1)
    blk0 = jnp.arange(n_blocks, dtype=jnp.int32) * EXPERT_BLOCK
    block_e = jnp.minimum(jnp.sum((pend[None, :] <= blk0[:, None]).astype(jnp.int32), axis=1),
                          N_EXPERTS - 1)
    block_rows = jnp.clip(jnp.sum(jnp.where(block_e[:, None] == e_ids[None, :],
                                             (pstart + counts)[None, :], 0), axis=1) - blk0,
                          0, EXPERT_BLOCK).astype(jnp.int32)

    later = (counts > 0)[None, :] & (e_ids[None, :] > e_ids[:, None])
    next_e = jnp.min(jnp.where(later, e_ids[None, :], N_EXPERTS), axis=1)
    next_e = jnp.where(next_e == N_EXPERTS, e_ids, next_e)
    block_next = jnp.sum(jnp.where(block_e[:, None] == e_ids[None, :], next_e[None, :], 0),
                         axis=1).astype(jnp.int32)

    x_sorted = _scatter_rows(hs, dest, n_blocks * EXPERT_BLOCK)
    y_sorted = _expert_call(block_e, block_rows, block_next, x_sorted, *moe_w)
    stride = _pick_stride(n_tok)
    n_pick = -(-(TOP_K * stride) // gather_quant) * gather_quant
    picks = jnp.pad(dest.reshape(TOP_K, n_tok), ((0, 0), (0, stride - n_tok))).reshape(-1)
    return _gather_rows(y_sorted, jnp.pad(picks, (0, n_pick - TOP_K * stride)))


def _pick_stride(n_tok):
    return -(-n_tok // COMBINE_TILE) * COMBINE_TILE
```

```python
import functools
import math

import numpy as np
import jax
import jax.numpy as jnp
from jax import lax
from jax.experimental import pallas as pl
from jax.experimental.pallas import tpu as pltpu
from jax.experimental.pallas import tpu_sc as plsc

D_MODEL = 1024
CHUNK = 64
POOL_WIDTH = 512
POOL_WINDOWS = (2, 4, 8, 16)
POOL_GROUP = 128
POOL_HIST = 15
ATTN_WIDTH = 512
HEAD_DIM = 64
N_HEADS = 8
N_KV_HEADS = 2
GROUP = 4
KV_WIDTH = 128
WINDOW = 128
NUM_BUCKETS = 32
MAX_DISTANCE = 128
PAST_LEN = 2048
N_EXPERTS = 32
TOP_K = 4
SWIGLU_LIMIT = 7.0
SWIGLU_ALPHA = 1.702
EPS = 1e-5
NEG_INF = -1e30
ATTN_SCALE = HEAD_DIM ** -0.5

PAIR = 2 * CHUNK
BAND = PAIR + WINDOW
HIST_ROWS = 16
GATE_COLS = 8
MIX_TILE = 1024
MIX_SUB = 1024
EXPERT_BLOCK = 1024
FFN_CHUNK = 256
COMBINE_TILE = 1024
SC_WORKERS = 32
SC_CHUNK = 64
VMEM_LIMIT = 56 * 1024 * 1024


def _pack_bf16_pair(a, b):
    ab = lax.bitcast_convert_type(a.astype(jnp.bfloat16).astype(jnp.float32), jnp.uint32)
    bb = lax.bitcast_convert_type(b.astype(jnp.bfloat16).astype(jnp.float32), jnp.uint32)
    return (ab >> 16) | (bb & jnp.uint32(0xFFFF0000))


def _unpack_bf16_pair(w):
    a = lax.bitcast_convert_type(w << 16, jnp.float32).astype(jnp.bfloat16)
    b = lax.bitcast_convert_type(w & jnp.uint32(0xFFFF0000), jnp.float32).astype(jnp.bfloat16)
    return a, b


def _mixer_kernel(x_ref, kh_ref, vh_ref, uh_ref, cnt_in_ref,
                  g1_ref, wukv_ref, wqvt_ref, qg_ref, kg_ref, bd_ref, bias_ref, sink_ref,
                  wpool_ref, pscale_ref, wout_ref, g2_ref, wr_ref, br_ref, tri_ref,
                  x1_ref, hp_ref, idx_ref, rank_ref, gate_ref, cnt_ref, ko_ref, vo_ref, uo_ref,
                  qt_s, kb_s, vt_s, ub_s, mix_s, cnt_s,
                  *, tile, sub, n_valid, pos0, mask_first):
    b = pl.program_id(0)
    s = pl.program_id(1)
    bf16 = jnp.bfloat16
    f32 = jnp.float32

    @pl.when((b == 0) & (s == 0))
    def _():
        cnt_s[...] = cnt_in_ref[...]

    @pl.when(s == 0)
    def _():
        kb_s[0:WINDOW, :] = kh_ref[...].astype(bf16)
        vt_s[:, 0:WINDOW] = jnp.transpose(vh_ref[...]).astype(bf16)
        ub_s[0:HIST_ROWS, :] = uh_ref[...]

    @pl.when(s > 0)
    def _():
        kb_s[0:WINDOW, :] = kb_s[tile:tile + WINDOW, :]
        vt_s[:, 0:WINDOW] = vt_s[:, tile:tile + WINDOW]
        ub_s[0:HIST_ROWS, :] = ub_s[tile:tile + HIST_ROWS, :]

    for r0 in range(0, tile, sub):
        _mixer_rows(r0, s, x_ref, g1_ref, wukv_ref, wqvt_ref, qg_ref, kg_ref, bd_ref, bias_ref,
                    sink_ref, wpool_ref, pscale_ref, wout_ref, g2_ref, wr_ref, br_ref, tri_ref,
                    x1_ref, hp_ref, idx_ref, rank_ref, gate_ref, cnt_ref, ko_ref, vo_ref, uo_ref,
                    qt_s, kb_s, vt_s, ub_s, mix_s, cnt_s,
                    tile=tile, sub=sub, n_valid=n_valid, pos0=pos0, mask_first=mask_first)


def _mixer_rows(r0, s, x_ref, g1_ref, wukv_ref, wqvt_ref, qg_ref, kg_ref, bd_ref, bias_ref,
                sink_ref, wpool_ref, pscale_ref, wout_ref, g2_ref, wr_ref, br_ref, tri_ref,
                x1_ref, hp_ref, idx_ref, rank_ref, gate_ref, cnt_ref, ko_ref, vo_ref, uo_ref,
                qt_s, kb_s, vt_s, ub_s, mix_s, cnt_s, *, tile, sub, n_valid, pos0, mask_first):
    bf16 = jnp.bfloat16
    f32 = jnp.float32
    rows = slice(r0, r0 + sub)

    x = x_ref[rows, :]
    xn = (x * lax.rsqrt(jnp.mean(x * x, axis=-1, keepdims=True) + EPS) * g1_ref[...]).astype(bf16)
    z = jnp.dot(xn, wukv_ref[...], preferred_element_type=f32)
    zt = lax.dot_general(wqvt_ref[...], xn, (((1,), (1,)), ((), ())),
                         preferred_element_type=f32)
    u = z[:, 0:POOL_WIDTH]
    kz = z[:, POOL_WIDTH:POOL_WIDTH + KV_WIDTH]
    v = z[:, POOL_WIDTH + KV_WIDTH:]

    ksq = kz * kz
    kss = jnp.dot(ksq.astype(bf16), bd_ref[...], preferred_element_type=f32)
    kn = kz * lax.rsqrt(kss * (1.0 / HEAD_DIM) + EPS) * kg_ref[...]
    kb_s[WINDOW + r0:WINDOW + r0 + sub, :] = kn.astype(bf16)
    vt_s[:, WINDOW + r0:WINDOW + r0 + sub] = zt[ATTN_WIDTH:, :].astype(bf16)
    ub_s[HIST_ROWS + r0:HIST_ROWS + r0 + sub, :] = u

    if r0 + sub == tile:
        row0 = max(n_valid, WINDOW) - WINDOW - r0
        ko_ref[...] = kn[row0:row0 + WINDOW, :]
        vo_ref[...] = v[row0:row0 + WINDOW, :]
        uo_ref[...] = u[n_valid - HIST_ROWS - r0:n_valid - r0, :]

    for hd in range(N_HEADS):
        qh = zt[hd * HEAD_DIM:(hd + 1) * HEAD_DIM, :]
        ss = jnp.sum(qh * qh, axis=0, keepdims=True)
        qn = qh * (lax.rsqrt(ss * (1.0 / HEAD_DIM) + EPS) * ATTN_SCALE) * qg_ref[...]
        qt_s[hd * HEAD_DIM:(hd + 1) * HEAD_DIM, rows] = qn.astype(bf16)

    pos = pos0 + s * tile + r0 + lax.broadcasted_iota(jnp.int32, (sub, 1), 0)
    for g, w in enumerate(POOL_WINDOWS):
        e = ub_s[r0:r0 + HIST_ROWS + sub, g * POOL_GROUP:(g + 1) * POOL_GROUP]
        acc = e
        for lvl in range(g + 1):
            acc = acc + pltpu.roll(acc, 2 ** lvl, axis=0)
        inv_cnt = 1.0 / jnp.minimum(pos + 1, w).astype(f32)
        d = (acc[HIST_ROWS:, :] * inv_cnt - e[HIST_ROWS:, :]).astype(bf16)
        y = jnp.dot(d, wpool_ref[g], preferred_element_type=f32)
        y = y * pscale_ref[:, g * POOL_GROUP:(g + 1) * POOL_GROUP]
        mix_s[rows, g * POOL_GROUP:(g + 1) * POOL_GROUP] = y.astype(bf16)

    zeros_q = jnp.zeros((HEAD_DIM, GROUP * PAIR), bf16)
    for p in range(r0 // PAIR, (r0 + sub) // PAIR):
        k_band = kb_s[p * PAIR:p * PAIR + BAND, :]
        o_parts = []
        for h in range(N_KV_HEADS):
            qcat = jnp.concatenate(
                [qt_s[(h * GROUP + g) * HEAD_DIM:(h * GROUP + g + 1) * HEAD_DIM,
                      p * PAIR:(p + 1) * PAIR] for g in range(GROUP)], axis=1)
            rhs = jnp.concatenate([qcat, zeros_q] if h == 0 else [zeros_q, qcat], axis=0)
            st = jnp.dot(k_band, rhs, preferred_element_type=f32) + bias_ref[h]
            if mask_first and p == 0:
                krow = lax.broadcasted_iota(jnp.int32, (BAND, 1), 0)
                st = jnp.where((krow >= WINDOW) | (s > 0), st, NEG_INF)
            sink = sink_ref[h:h + 1, :]
            m = jnp.maximum(jnp.max(st, axis=0, keepdims=True), sink)
            ex = jnp.exp(st - m)
            den = jnp.sum(ex, axis=0, keepdims=True) + jnp.exp(sink - m)
            v_band = vt_s[h * HEAD_DIM:(h + 1) * HEAD_DIM, p * PAIR:p * PAIR + BAND]
            ot = jnp.dot(v_band, ex.astype(bf16), preferred_element_type=f32) / den
            for g in range(GROUP):
                o_parts.append(ot[:, g * PAIR:(g + 1) * PAIR])
        o_all = jnp.concatenate(o_parts, axis=0)
        mix_s[p * PAIR:(p + 1) * PAIR, POOL_WIDTH:] = jnp.transpose(o_all).astype(bf16)

    x1 = x + jnp.dot(mix_s[rows, :], wout_ref[...], preferred_element_type=f32)
    x1_ref[rows, :] = x1

    n_real = min(sub, n_valid - r0)
    hn = x1 * lax.rsqrt(jnp.mean(x1 * x1, axis=-1, keepdims=True) + EPS) * g2_ref[...]
    hp_ref[r0:r0 + n_real, :] = _pack_bf16_pair(hn[0:n_real, 0:D_MODEL // 2],
                                                hn[0:n_real, D_MODEL // 2:])
    h_hi = hn.astype(bf16)
    h_lo = (hn - h_hi.astype(f32)).astype(bf16)
    parts = jnp.transpose(jnp.dot(h_hi, wr_ref[...], preferred_element_type=f32)
                          + jnp.dot(h_lo, wr_ref[...], preferred_element_type=f32))
    lt = parts[0:N_EXPERTS, :] + parts[N_EXPERTS:2 * N_EXPERTS, :] + br_ref[...]

    eidx = lax.broadcasted_iota(jnp.int32, (N_EXPERTS, sub), 0).astype(f32)
    vals, hots = [], []
    for j in range(TOP_K):
        m = jnp.max(lt, axis=0, keepdims=True)
        sel = jnp.min(jnp.where(lt == m, eidx, float(N_EXPERTS)), axis=0, keepdims=True)
        hot = eidx == sel
        lt = jnp.where(hot, -jnp.inf, lt)
        idx_ref[j:j + 1, rows] = sel.astype(jnp.int32)
        vals.append(m)
        hots.append(hot)
    exps = [jnp.exp(vv - vals[0]) for vv in vals]
    esum = exps[0] + exps[1] + exps[2] + exps[3]
    grow = lax.broadcasted_iota(jnp.int32, (GATE_COLS, sub), 0)
    gmat = jnp.zeros((GATE_COLS, sub), f32)
    for j in range(TOP_K):
        gmat = jnp.where(grow == j, exps[j] / esum, gmat)
    gmat = jnp.concatenate([gmat, jnp.zeros((128 - GATE_COLS, sub), f32)], axis=0)
    gate_ref[r0:r0 + n_real, :] = jnp.transpose(gmat)[0:n_real, 0:GATE_COLS]

    chosen_f = sum(jnp.where(hot, 1.0, 0.0) for hot in hots)
    if n_real < sub:
        lane = lax.broadcasted_iota(jnp.int32, (N_EXPERTS, sub), 1)
        chosen_f = jnp.where(lane < n_real, chosen_f, 0.0)
    before = jnp.dot(chosen_f.astype(bf16), tri_ref[...], preferred_element_type=f32)
    base = before + cnt_s[:, 0:1]
    for j in range(TOP_K):
        rank_ref[j:j + 1, rows] = jnp.sum(jnp.where(hots[j], base, 0.0), axis=0,
                                          keepdims=True).astype(jnp.int32)
    cnt_new = cnt_s[...] + jnp.sum(chosen_f, axis=1, keepdims=True)
    cnt_s[...] = cnt_new
    cnt_ref[...] = cnt_new


def _mixer_call(x, k_hist, v_hist, u_hist, cnt_in, consts, *, stream0, tile, sub, n_valid, pos0,
                mask_first):
    seq = x.shape[1]
    nb = k_hist.shape[0]
    n_tiles = seq // tile
    f32 = jnp.float32
    assert n_valid % 8 == 0 and tile % sub == 0 and sub % PAIR == 0
    assert n_valid == tile or sub == tile

    def full(a):
        nd = a.ndim
        return pl.BlockSpec(a.shape, lambda b, s, _nd=nd: (0,) * _nd)

    in_specs = [
        pl.BlockSpec((None, tile, D_MODEL), lambda b, s: (stream0 + b, s, 0)),
        pl.BlockSpec((None, WINDOW, KV_WIDTH), lambda b, s: (b, 0, 0)),
        pl.BlockSpec((None, WINDOW, KV_WIDTH), lambda b, s: (b, 0, 0)),
        pl.BlockSpec((None, HIST_ROWS, POOL_WIDTH), lambda b, s: (b, 0, 0)),
        full(cnt_in),
    ] + [full(c) for c in consts]
    out_shape = [
        jax.ShapeDtypeStruct((nb, seq, D_MODEL), f32),
        jax.ShapeDtypeStruct((nb * n_tiles * n_valid, D_MODEL // 2), jnp.uint32),
        jax.ShapeDtypeStruct((nb, TOP_K, seq), jnp.int32),
        jax.ShapeDtypeStruct((nb, TOP_K, seq), jnp.int32),
        jax.ShapeDtypeStruct((nb * n_tiles * n_valid, GATE_COLS), f32),
        jax.ShapeDtypeStruct((N_EXPERTS, 128), f32),
        jax.ShapeDtypeStruct((nb, WINDOW, KV_WIDTH), f32),
        jax.ShapeDtypeStruct((nb, WINDOW, KV_WIDTH), f32),
        jax.ShapeDtypeStruct((nb, HIST_ROWS, POOL_WIDTH), f32),
    ]
    out_specs = [
        pl.BlockSpec((None, tile, D_MODEL), lambda b, s: (b, s, 0)),
        pl.BlockSpec((n_valid, D_MODEL // 2), lambda b, s: (b * n_tiles + s, 0)),
        pl.BlockSpec((None, TOP_K, tile), lambda b, s: (b, 0, s)),
        pl.BlockSpec((None, TOP_K, tile), lambda b, s: (b, 0, s)),
        pl.BlockSpec((n_valid, GATE_COLS), lambda b, s: (b * n_tiles + s, 0)),
        pl.BlockSpec((N_EXPERTS, 128), lambda b, s: (0, 0)),
        pl.BlockSpec((None, WINDOW, KV_WIDTH), lambda b, s: (b, 0, 0)),
        pl.BlockSpec((None, WINDOW, KV_WIDTH), lambda b, s: (b, 0, 0)),
        pl.BlockSpec((None, HIST_ROWS, POOL_WIDTH), lambda b, s: (b, 0, 0)),
    ]
    scratch = [
        pltpu.VMEM((ATTN_WIDTH, tile), jnp.bfloat16),
        pltpu.VMEM((WINDOW + tile, KV_WIDTH), jnp.bfloat16),
        pltpu.VMEM((KV_WIDTH, WINDOW + tile), jnp.bfloat16),
        pltpu.VMEM((HIST_ROWS + tile, POOL_WIDTH), f32),
        pltpu.VMEM((tile, D_MODEL), jnp.bfloat16),
        pltpu.VMEM((N_EXPERTS, 128), f32),
    ]
    kern = functools.partial(_mixer_kernel, tile=tile, sub=sub, n_valid=n_valid, pos0=pos0,
                             mask_first=mask_first)
    return pl.pallas_call(
        kern,
        grid=(nb, n_tiles),
        in_specs=in_specs,
        out_specs=out_specs,
        out_shape=out_shape,
        scratch_shapes=scratch,
        compiler_params=pltpu.CompilerParams(
            dimension_semantics=("arbitrary", "arbitrary"),
            vmem_limit_bytes=VMEM_LIMIT),
        name="mixer",
    )(x, k_hist, v_hist, u_hist, cnt_in, *consts)


def _expert_kernel_f32(be_ref, nv_ref, nx_ref, lv_ref, xs_ref, wg_hbm, bg_ref, wu_hbm, bu_ref,
                       wd_hbm, bd_ref, ys_ref, wg16_hbm, wu16_hbm, wd16_hbm,
                       stage_s, wg_s, wu_s, wd_s, in_sems, out_sems, *, n_blocks):
    i = pl.program_id(0)
    expert = be_ref[i]
    bf16 = jnp.bfloat16

    def copies_in(e):
        return [pltpu.make_async_copy(w_hbm.at[e], stage_s.at[k], in_sems.at[k])
                for k, w_hbm in enumerate((wg_hbm, wu_hbm, wd_hbm))]

    def copies_out(e):
        return [pltpu.make_async_copy(w_s, w16_hbm.at[e], out_sems.at[k])
                for k, (w_s, w16_hbm) in enumerate(((wg_s, wg16_hbm), (wu_s, wu16_hbm),
                                                    (wd_s, wd16_hbm)))]

    @pl.when(i == 0)
    def _():
        for copy in copies_in(expert):
            copy.start()

    @pl.when((lv_ref[i] > 0) & ((i == 0) | (expert != be_ref[jnp.maximum(i - 1, 0)])))
    def _():
        for copy in copies_in(expert):
            copy.wait()

        @pl.when(i > 0)
        def _():
            for copy in copies_out(expert):
                copy.wait()

        wg_s[...] = stage_s[0].astype(bf16)
        wu_s[...] = stage_s[1].astype(bf16)
        wd_s[...] = stage_s[2].astype(bf16)
        for copy in copies_out(expert):
            copy.start()

        @pl.when(nx_ref[i] != expert)
        def _():
            for copy in copies_in(nx_ref[i]):
                copy.start()

    _ffn_block(nv_ref[i], xs_ref, ys_ref, wg_s, wu_s, wd_s, bg_ref, bu_ref, bd_ref)

    @pl.when(i == n_blocks - 1)
    def _():
        for copy in copies_out(expert):
            copy.wait()


def _expert_kernel_bf16(be_ref, nv_ref, xs_ref, wg_ref, bg_ref, wu_ref, bu_ref, wd_ref, bd_ref,
                        ys_ref):
    del be_ref
    _ffn_block(nv_ref[pl.program_id(0)], xs_ref, ys_ref, wg_ref, wu_ref, wd_ref,
               bg_ref, bu_ref, bd_ref)


def _ffn_block(n_rows, xs_ref, ys_ref, wg_s, wu_s, wd_s, bg_ref, bu_ref, bd_ref):
    bf16 = jnp.bfloat16

    def ffn(rows):
        f32 = jnp.float32
        half = D_MODEL // 2
        for r0 in range(0, rows, FFN_CHUNK):
            row = r0 + lax.broadcasted_iota(jnp.int32, (FFN_CHUNK, 1), 0)
            words = jnp.where(row < n_rows, xs_ref[r0:r0 + FFN_CHUNK, :],
                              jnp.uint32(0))
            xa, xb = _unpack_bf16_pair(words)
            a = (jnp.dot(xa, wg_s[0:half, :], preferred_element_type=f32)
                 + jnp.dot(xb, wg_s[half:, :], preferred_element_type=f32) + bg_ref[...])
            bb = (jnp.dot(xa, wu_s[0:half, :], preferred_element_type=f32)
                  + jnp.dot(xb, wu_s[half:, :], preferred_element_type=f32) + bu_ref[...])
            a = jnp.minimum(a, SWIGLU_LIMIT)
            bb = jnp.clip(bb, -SWIGLU_LIMIT, SWIGLU_LIMIT)
            act = a * (1.0 / (1.0 + jnp.exp(-SWIGLU_ALPHA * a))) * (bb + 1.0)
            y = jnp.dot(act.astype(bf16), wd_s[...], preferred_element_type=f32) + bd_ref[...]
            ys_ref[r0:r0 + FFN_CHUNK, :] = _pack_bf16_pair(y[:, 0:half], y[:, half:])
        if rows < EXPERT_BLOCK:
            ys_ref[rows:, :] = jnp.zeros((EXPERT_BLOCK - rows, half), jnp.uint32)

    quarter = EXPERT_BLOCK // 4
    for q in range(1, 5):
        @pl.when((n_rows > (q - 1) * quarter) & (n_rows <= q * quarter))
        def _(q=q):
            ffn(q * quarter)

    @pl.when(n_rows == 0)
    def _():
        ys_ref[...] = jnp.zeros_like(ys_ref)


def _expert_call_f32(block_e, block_rows, block_next, block_live, xs, wg, bg, wu, bu, wd, bd):
    n_slots = xs.shape[0]
    n_blocks = n_slots // EXPERT_BLOCK
    any_spec = pl.BlockSpec(memory_space=pl.ANY)
    b_spec = pl.BlockSpec((None, 1, D_MODEL), lambda i, be, nv, nx, lv: (be[i], 0, 0))
    x_spec = pl.BlockSpec((EXPERT_BLOCK, D_MODEL // 2), lambda i, be, nv, nx, lv: (i, 0))
    grid_spec = pltpu.PrefetchScalarGridSpec(
        num_scalar_prefetch=4,
        grid=(n_blocks,),
        in_specs=[x_spec, any_spec, b_spec, any_spec, b_spec, any_spec, b_spec],
        out_specs=[x_spec, any_spec, any_spec, any_spec],
        scratch_shapes=[pltpu.VMEM((3, D_MODEL, D_MODEL), jnp.float32)]
        + [pltpu.VMEM((D_MODEL, D_MODEL), jnp.bfloat16)] * 3
        + [pltpu.SemaphoreType.DMA((3,)), pltpu.SemaphoreType.DMA((3,))],
    )
    w16 = jax.ShapeDtypeStruct((N_EXPERTS, D_MODEL, D_MODEL), jnp.bfloat16)
    return pl.pallas_call(
        functools.partial(_expert_kernel_f32, n_blocks=n_blocks),
        grid_spec=grid_spec,
        out_shape=[jax.ShapeDtypeStruct((n_slots, D_MODEL // 2), jnp.uint32), w16, w16, w16],
        compiler_params=pltpu.CompilerParams(
            dimension_semantics=("arbitrary",),
            vmem_limit_bytes=VMEM_LIMIT),
        name="experts_f32",
    )(block_e, block_rows, block_next, block_live, xs, wg, bg, wu, bu, wd, bd)


def _expert_call_bf16(block_e, block_rows, xs, wg, bg, wu, bu, wd, bd):
    n_slots = xs.shape[0]
    w_spec = pl.BlockSpec((None, D_MODEL, D_MODEL), lambda i, be, nv: (be[i], 0, 0))
    b_spec = pl.BlockSpec((None, 1, D_MODEL), lambda i, be, nv: (be[i], 0, 0))
    x_spec = pl.BlockSpec((EXPERT_BLOCK, D_MODEL // 2), lambda i, be, nv: (i, 0))
    grid_spec = pltpu.PrefetchScalarGridSpec(
        num_scalar_prefetch=2,
        grid=(n_slots // EXPERT_BLOCK,),
        in_specs=[x_spec, w_spec, b_spec, w_spec, b_spec, w_spec, b_spec],
        out_specs=x_spec,
    )
    return pl.pallas_call(
        _expert_kernel_bf16,
        grid_spec=grid_spec,
        out_shape=jax.ShapeDtypeStruct((n_slots, D_MODEL // 2), jnp.uint32),
        compiler_params=pltpu.CompilerParams(
            dimension_semantics=("arbitrary",),
            vmem_limit_bytes=VMEM_LIMIT),
        name="experts_bf16",
    )(block_e, block_rows, xs, wg, bg, wu, bu, wd, bd)


def _combine_kernel(out_buf_ref, x1_ref, g_ref, y0_ref, y1_ref, y2_ref, y3_ref, o_ref):
    del out_buf_ref
    g = g_ref[...]
    half = D_MODEL // 2
    lo, hi = x1_ref[:, 0:half], x1_ref[:, half:]
    for j, y_ref in enumerate((y0_ref, y1_ref, y2_ref, y3_ref)):
        w = y_ref[...]
        gj = g[:, j:j + 1]
        lo = lo + gj * lax.bitcast_convert_type(w << 16, jnp.float32)
        hi = hi + gj * lax.bitcast_convert_type(w & jnp.uint32(0xFFFF0000), jnp.float32)
    o_ref[:, 0:half] = lo
    o_ref[:, half:] = hi


def _combine_call(out_buf, x1, gates, picked, stride, tok0, *, out_rows, row0, tile):
    n = x1.shape[0]
    aliased = out_buf.shape == (out_rows, D_MODEL)
    assert n % tile == 0 and row0 % tile == 0
    y_specs = []
    for j in range(TOP_K):
        assert (j * stride + tok0) % tile == 0
        base = (j * stride + tok0) // tile
        y_specs.append(pl.BlockSpec((tile, D_MODEL // 2), lambda i, _b=base: (_b + i, 0)))
    return pl.pallas_call(
        _combine_kernel,
        grid=(n // tile,),
        in_specs=[pl.BlockSpec(memory_space=pl.ANY),
                  pl.BlockSpec((tile, D_MODEL), lambda i: (i, 0)),
                  pl.BlockSpec((tile, GATE_COLS), lambda i: (i, 0))] + y_specs,
        out_specs=pl.BlockSpec((tile, D_MODEL), lambda i: (row0 // tile + i, 0)),
        out_shape=jax.ShapeDtypeStruct((out_rows, D_MODEL), jnp.float32),
        input_output_aliases={0: 0} if aliased else {},
        compiler_params=pltpu.CompilerParams(dimension_semantics=("arbitrary",),
                                             vmem_limit_bytes=VMEM_LIMIT),
        name="combine",
    )(out_buf, x1, gates, picked, picked, picked, picked)


def _gather_rows(table, idx):
    n = idx.shape[0]
    width = table.shape[1]
    per_worker = n // SC_WORKERS
    n_chunks = per_worker // SC_CHUNK
    mesh = plsc.VectorSubcoreMesh(core_axis_name="c", subcore_axis_name="s")

    @functools.partial(
        pl.kernel, mesh=mesh,
        out_type=jax.ShapeDtypeStruct((n, width), table.dtype),
        scratch_types=[pltpu.VMEM((SC_CHUNK,), jnp.int32),
                       pltpu.VMEM((SC_CHUNK, width), table.dtype),
                       pltpu.SemaphoreType.DMA],
        cost_estimate=pl.CostEstimate(flops=0, transcendentals=0, bytes_accessed=8 * n * width),
    )
    def gather(table_hbm, idx_hbm, out_hbm, idx_v, rows_v, sem):
        wid = lax.axis_index("s") * 2 + lax.axis_index("c")
        base = wid * per_worker

        @pl.loop(0, n_chunks)
        def _(i):
            off = base + i * SC_CHUNK
            pltpu.sync_copy(idx_hbm.at[pl.ds(off, SC_CHUNK)], idx_v)
            pltpu.async_copy(table_hbm.at[idx_v], rows_v, sem).wait()
            pltpu.sync_copy(rows_v, out_hbm.at[pl.ds(off, SC_CHUNK)])

    return gather(table, idx)


def _scatter_rows(srcs, dest, n_out):
    width = srcs[0].shape[1]
    starts = [0]
    for src in srcs:
        assert src.shape[0] % SC_CHUNK == 0
        starts.append(starts[-1] + src.shape[0] // SC_CHUNK)
    n_chunks = starts[-1]
    n = n_chunks * SC_CHUNK
    per_worker = -(-n_chunks // SC_WORKERS)
    mesh = plsc.VectorSubcoreMesh(core_axis_name="c", subcore_axis_name="s")

    @functools.partial(
        pl.kernel, mesh=mesh,
        out_type=jax.ShapeDtypeStruct((n_out, width), srcs[0].dtype),
        scratch_types=[pltpu.VMEM((SC_CHUNK,), jnp.int32)] * TOP_K
        + [pltpu.VMEM((SC_CHUNK, width), srcs[0].dtype)],
        cost_estimate=pl.CostEstimate(flops=0, transcendentals=0,
                                      bytes_accessed=4 * (1 + TOP_K) * n * width),
    )
    def scatter(*refs):
        src_hbms = refs[:len(srcs)]
        dest_hbm, out_hbm = refs[len(srcs)], refs[len(srcs) + 1]
        idx_vs, rows_v = refs[len(srcs) + 2:len(srcs) + 2 + TOP_K], refs[-1]
        wid = lax.axis_index("s") * 2 + lax.axis_index("c")

        @pl.loop(0, per_worker)
        def _(i):
            c = i * SC_WORKERS + wid
            for k, src_hbm in enumerate(src_hbms):

                @pl.when((c >= starts[k]) & (c < starts[k + 1]))
                def _():
                    pltpu.sync_copy(src_hbm.at[pl.ds((c - starts[k]) * SC_CHUNK, SC_CHUNK)], rows_v)
                    for j, idx_v in enumerate(idx_vs):
                        pltpu.sync_copy(dest_hbm.at[pl.ds(j * n + c * SC_CHUNK, SC_CHUNK)], idx_v)
                    for idx_v in idx_vs:
                        pltpu.sync_copy(rows_v, out_hbm.at[idx_v])

    return scatter(*srcs, dest)


def _t5_bucket_np(rel):
    half = NUM_BUCKETS // 2
    max_exact = half // 2
    n = np.abs(rel)
    nf = np.maximum(n, 1).astype(np.float32)
    large = max_exact + (np.log(nf / max_exact) / math.log(MAX_DISTANCE / max_exact)
                         * (half - max_exact)).astype(np.int32)
    large = np.minimum(large, half - 1)
    return np.where(rel > 0, half, 0) + np.where(n < max_exact, n, large)


def _bias_tables(rel_bias, visible):
    kap = np.arange(BAND)[:, None]
    rho = np.arange(PAIR)[None, :]
    bucket = _t5_bucket_np(kap - WINDOW - rho)
    rb = rel_bias.astype(jnp.float32)
    tab = jnp.zeros((BAND, PAIR, N_HEADS), jnp.float32)
    for bkt in range(NUM_BUCKETS):
        tab = jnp.where(jnp.asarray(bucket == bkt)[:, :, None], rb[bkt], tab)
    tab = jnp.where(jnp.asarray(visible)[:, :, None], tab, NEG_INF)
    tab = jnp.transpose(tab, (2, 0, 1)).reshape(N_KV_HEADS, GROUP, BAND, PAIR)
    return jnp.transpose(tab, (0, 2, 1, 3)).reshape(N_KV_HEADS, BAND, GROUP * PAIR)


def _mixer_consts(l, norm1_g, w_in, q_norm_g, k_norm_g, rel_bias, sinks, w_pool, pool_scale,
                  w_out, norm2_g, w_router, b_router, visible, tile):
    f32, bf16 = jnp.float32, jnp.bfloat16
    q_off, k_off, v_off = POOL_WIDTH, POOL_WIDTH + ATTN_WIDTH, POOL_WIDTH + ATTN_WIDTH + KV_WIDTH
    w = w_in[l]
    w_ukv = jnp.concatenate([w[:, :q_off], w[:, k_off:]], axis=1).astype(bf16)
    w_qvt = jnp.transpose(jnp.concatenate([w[:, q_off:k_off], w[:, v_off:]], axis=1)).astype(bf16)
    lane_head = np.arange(KV_WIDTH) // HEAD_DIM
    blockdiag = jnp.asarray(lane_head[:, None] == lane_head[None, :], bf16)
    sink_rows = jnp.repeat(sinks[l].astype(f32).reshape(N_KV_HEADS, GROUP), PAIR, axis=1)
    wr = w_router[l].astype(f32)
    wr_hi = wr.astype(bf16)
    wr_lo = (wr - wr_hi.astype(f32)).astype(bf16)
    wr_parts = jnp.pad(jnp.concatenate([wr_hi, wr_lo], axis=1), ((0, 0), (0, 128 - 2 * N_EXPERTS)))
    tri = jnp.asarray(np.arange(tile)[:, None] < np.arange(tile)[None, :], bf16)
    return [
        norm1_g[l].reshape(1, D_MODEL).astype(f32), w_ukv, w_qvt,
        q_norm_g[l].reshape(HEAD_DIM, 1).astype(f32),
        jnp.tile(k_norm_g[l].astype(f32), N_KV_HEADS).reshape(1, KV_WIDTH),
        blockdiag, _bias_tables(rel_bias, visible), sink_rows,
        w_pool[l].astype(bf16), pool_scale[l].reshape(1, POOL_WIDTH).astype(f32),
        w_out[l].astype(bf16), norm2_g[l].reshape(1, D_MODEL).astype(f32),
        wr_parts, b_router[l].reshape(N_EXPERTS, 1).astype(f32), tri,
    ]


def _visibility():
    kap = np.arange(BAND)[:, None]
    rho = np.arange(PAIR)[None, :]
    kc, qc = kap // CHUNK, rho // CHUNK
    prompt = (kc >= qc) & (kc <= qc + WINDOW // CHUNK)
    return prompt


def kernel(x_prompt, x_sample, cache_k, cache_v, state_pool, norm1_g, w_in, q_norm_g, k_norm_g,
           rel_bias, sinks, w_pool, pool_scale, w_out, norm2_g, w_router, b_router,
           w_gate, b_gate, w_up, b_up, w_down, b_down):
    f32, bf16 = jnp.float32, jnp.bfloat16
    depth = w_in.shape[0]
    nb, seq, _ = x_prompt.shape
    ndb, dseq, _ = x_sample.shape
    cache_len = cache_k.shape[2]
    assert seq % MIX_TILE == 0 and cache_len == WINDOW and HIST_ROWS <= dseq <= PAIR
    n_p, n_s = nb * seq, ndb * dseq
    n_tok = n_p + n_s
    assert COMBINE_TILE % n_s == 0 and n_s % SC_CHUNK == 0

    vis_prompt = _visibility()
    vis_sample = np.broadcast_to(np.arange(BAND)[:, None] < WINDOW + dseq, (BAND, PAIR))

    nb1 = nb // 4
    nb2 = nb - nb1
    assert nb1 > 0 and nb2 > 0 and seq % COMBINE_TILE == 0

    xp, xs = x_prompt, x_sample
    outs = [[] for _ in range(6)]
    for l in range(depth):
        wl = (l, norm1_g, w_in, q_norm_g, k_norm_g, rel_bias, sinks, w_pool, pool_scale, w_out,
              norm2_g, w_router, b_router)
        consts_p = _mixer_consts(*wl, vis_prompt, MIX_SUB)
        moe_b = [b.reshape(N_EXPERTS, 1, D_MODEL).astype(f32) for b in (b_gate[l], b_up[l], b_down[l])]
        cnt0 = jnp.zeros((N_EXPERTS, 128), f32)

        def prompt_mixer(stream0, n_streams, cnt_in):
            zk = jnp.zeros((n_streams, WINDOW, KV_WIDTH), f32)
            zu = jnp.zeros((n_streams, HIST_ROWS, POOL_WIDTH), f32)
            return _mixer_call(xp, zk, zk, zu, cnt_in, consts_p, stream0=stream0, tile=MIX_TILE,
                               sub=MIX_SUB, n_valid=MIX_TILE, pos0=0, mask_first=True)

        (x1_a, h_a, idx_a, rank_a, gate_a, cnt_a, k_a, v_a, u_a) = prompt_mixer(0, nb1, cnt0)
        xs_pad = jnp.pad(xs, ((0, 0), (0, PAIR - dseq), (0, 0)))
        uh = jnp.pad(state_pool[l], ((0, 0), (HIST_ROWS - POOL_HIST, 0), (0, 0)))
        (xs1, h_s, idx_s, rank_s, gate_s, cnt_s, k_s, v_s, u_s) = _mixer_call(
            xs_pad, cache_k[l].reshape(ndb, WINDOW, KV_WIDTH),
            cache_v[l].reshape(ndb, WINDOW, KV_WIDTH), uh, cnt0,
            _mixer_consts(*wl, vis_sample, PAIR),
            stream0=0, tile=PAIR, sub=PAIR, n_valid=dseq, pos0=PAST_LEN, mask_first=False)
        (x1_b, h_b, idx_b, rank_b, gate_b, cnt_b, k_b, v_b, u_b) = prompt_mixer(nb1, nb2, cnt_s)
        picked_b, w_bf16 = _moe_rows([h_b, h_s], [idx_b, idx_s], [rank_b, rank_s], [seq, dseq], cnt_b,
                                     moe_b, w_f32=(w_gate[l], w_up[l], w_down[l]))
        picked_a, _ = _moe_rows([h_a], [idx_a], [rank_a], [seq], cnt_a, moe_b, w_bf16=w_bf16)

        n_a, n_b = nb1 * seq, nb2 * seq
        no_buf = jnp.zeros((8, 128), f32)
        stride_a, stride_b = _pick_stride(n_a), _pick_stride(n_b + n_s)
        xp_rows = _combine_call(no_buf, x1_b.reshape(n_b, D_MODEL), gate_b, picked_b, stride_b, 0,
                                out_rows=n_p, row0=n_a, tile=COMBINE_TILE)
        xp_rows = _combine_call(xp_rows, x1_a.reshape(n_a, D_MODEL), gate_a, picked_a, stride_a, 0,
                                out_rows=n_p, row0=0, tile=COMBINE_TILE)
        xp = xp_rows.reshape(nb, seq, D_MODEL)
        xs = _combine_call(no_buf, xs1[:, :dseq].reshape(n_s, D_MODEL), gate_s, picked_b, stride_b,
                           n_b, out_rows=n_s, row0=0, tile=n_s).reshape(ndb, dseq, D_MODEL)

        outs[0].append(jnp.concatenate([k_a, k_b]).reshape(nb, WINDOW, N_KV_HEADS, HEAD_DIM))
        outs[1].append(jnp.concatenate([v_a, v_b]).reshape(nb, WINDOW, N_KV_HEADS, HEAD_DIM))
        outs[2].append(jnp.concatenate([u_a, u_b])[:, HIST_ROWS - POOL_HIST:])
        outs[3].append(k_s[:, :dseq].reshape(ndb, dseq, N_KV_HEADS, HEAD_DIM))
        outs[4].append(v_s[:, :dseq].reshape(ndb, dseq, N_KV_HEADS, HEAD_DIM))
        outs[5].append(u_s[:, HIST_ROWS - POOL_HIST:])
    return (xp, xs) + tuple(jnp.stack(o) for o in outs)


def _moe_rows(hs, idxs, ranks, n_reals, cnt, biases, w_f32=None, w_bf16=None):
    n_tok = sum(h.shape[0] for h in hs)
    n_assign = n_tok * TOP_K
    gather_quant = SC_WORKERS * SC_CHUNK
    n_blocks = n_assign // EXPERT_BLOCK + N_EXPERTS
    n_blocks = -(-n_blocks // (gather_quant // EXPERT_BLOCK)) * (gather_quant // EXPERT_BLOCK)
    e_ids = jnp.arange(N_EXPERTS, dtype=jnp.int32)

    def per_token(arrs):
        return jnp.concatenate(
            [jnp.transpose(a[:, :, :n], (1, 0, 2)).reshape(TOP_K, -1) for a, n in zip(arrs, n_reals)],
            axis=1)

    counts = cnt[:, 0].astype(jnp.int32)
    pcounts = (counts + EXPERT_BLOCK - 1) // EXPERT_BLOCK * EXPERT_BLOCK
    if w_bf16 is None:
        pcounts = jnp.maximum(pcounts, EXPERT_BLOCK)
    pend = jnp.cumsum(pcounts)
    pstart = pend - pcounts
    idx_all = per_token(idxs)
    dest = per_token(ranks) + jnp.sum(
        jnp.where(idx_all[None] == e_ids[:, None, None], pstart[:, None, None], 0), axis=0)
    dest = dest.reshape(-1)
    blk0 = jnp.arange(n_blocks, dtype=jnp.int32) * EXPERT_BLOCK
    block_e = jnp.minimum(jnp.sum((pend[None, :] <= blk0[:, None]).astype(jnp.int32), axis=1),
                          N_EXPERTS - 1)
    block_rows = jnp.clip(jnp.sum(jnp.where(block_e[:, None] == e_ids[None, :],
                                             (pstart + counts)[None, :], 0), axis=1) - blk0,
                          0, EXPERT_BLOCK).astype(jnp.int32)

    x_sorted = _scatter_rows(hs, dest, n_blocks * EXPERT_BLOCK)
    if w_bf16 is None:
        block_next = jnp.minimum(block_e + 1, N_EXPERTS - 1)
        block_live = (blk0 < pend[-1]).astype(jnp.int32)
        y_sorted, *w_bf16 = _expert_call_f32(
            block_e, block_rows, block_next, block_live, x_sorted,
            w_f32[0], biases[0], w_f32[1], biases[1], w_f32[2], biases[2])
    else:
        y_sorted = _expert_call_bf16(
            block_e, block_rows, x_sorted,
            w_bf16[0], biases[0], w_bf16[1], biases[1], w_bf16[2], biases[2])
    stride = _pick_stride(n_tok)
    n_pick = -(-(TOP_K * stride) // gather_quant) * gather_quant
    picks = jnp.pad(dest.reshape(TOP_K, n_tok), ((0, 0), (0, stride - n_tok))).reshape(-1)
    return _gather_rows(y_sorted, jnp.pad(picks, (0, n_pick - TOP_K * stride))), w_bf16


def _pick_stride(n_tok):
    return -(-n_tok // COMBINE_TILE) * COMBINE_TILE
```

```python
import functools
import math

import numpy as np
import jax
import jax.numpy as jnp
from jax import lax
from jax.experimental import pallas as pl
from jax.experimental.pallas import tpu as pltpu
from jax.experimental.pallas import tpu_sc as plsc

D_MODEL = 1024
CHUNK = 64
POOL_WIDTH = 512
POOL_WINDOWS = (2, 4, 8, 16)
POOL_GROUP = 128
POOL_HIST = 15
ATTN_WIDTH = 512
HEAD_DIM = 64
N_HEADS = 8
N_KV_HEADS = 2
GROUP = 4
KV_WIDTH = 128
WINDOW = 128
NUM_BUCKETS = 32
MAX_DISTANCE = 128
PAST_LEN = 2048
N_EXPERTS = 32
TOP_K = 4
SWIGLU_LIMIT = 7.0
SWIGLU_ALPHA = 1.702
EPS = 1e-5
NEG_INF = -1e30
ATTN_SCALE = HEAD_DIM ** -0.5

PAIR = 2 * CHUNK
BAND = PAIR + WINDOW
HIST_ROWS = 16
GATE_COLS = 8
MIX_TILE = 1024
MIX_SUB = 1024
EXPERT_BLOCK = 1024
FFN_CHUNK = 256
COMBINE_TILE = 1024
SC_WORKERS = 32
SC_CHUNK = 64
VMEM_LIMIT = 56 * 1024 * 1024


def _pack_bf16_pair(a, b):
    ab = lax.bitcast_convert_type(a.astype(jnp.bfloat16).astype(jnp.float32), jnp.uint32)
    bb = lax.bitcast_convert_type(b.astype(jnp.bfloat16).astype(jnp.float32), jnp.uint32)
    return (ab >> 16) | (bb & jnp.uint32(0xFFFF0000))


def _unpack_bf16_pair(w):
    a = lax.bitcast_convert_type(w << 16, jnp.float32).astype(jnp.bfloat16)
    b = lax.bitcast_convert_type(w & jnp.uint32(0xFFFF0000), jnp.float32).astype(jnp.bfloat16)
    return a, b


def _mixer_kernel(x_ref, kh_ref, vh_ref, uh_ref, cnt_in_ref,
                  g1_ref, wukv_ref, wqvt_ref, qg_ref, kg_ref, bd_ref, bias_ref, sink_ref,
                  wpool_ref, pscale_ref, wout_ref, g2_ref, wr_ref, br_ref, tri_ref,
                  x1_ref, hp_ref, idx_ref, rank_ref, gate_ref, cnt_ref, ko_ref, vo_ref, uo_ref,
                  qt_s, kb_s, vt_s, ub_s, mix_s, cnt_s,
                  *, tile, sub, n_valid, pos0, mask_first):
    b = pl.program_id(0)
    s = pl.program_id(1)
    bf16 = jnp.bfloat16
    f32 = jnp.float32

    @pl.when((b == 0) & (s == 0))
    def _():
        cnt_s[...] = cnt_in_ref[...]

    @pl.when(s == 0)
    def _():
        kb_s[0:WINDOW, :] = kh_ref[...].astype(bf16)
        vt_s[:, 0:WINDOW] = jnp.transpose(vh_ref[...]).astype(bf16)
        ub_s[0:HIST_ROWS, :] = uh_ref[...]

    @pl.when(s > 0)
    def _():
        kb_s[0:WINDOW, :] = kb_s[tile:tile + WINDOW, :]
        vt_s[:, 0:WINDOW] = vt_s[:, tile:tile + WINDOW]
        ub_s[0:HIST_ROWS, :] = ub_s[tile:tile + HIST_ROWS, :]

    for r0 in range(0, tile, sub):
        _mixer_rows(r0, s, x_ref, g1_ref, wukv_ref, wqvt_ref, qg_ref, kg_ref, bd_ref, bias_ref,
                    sink_ref, wpool_ref, pscale_ref, wout_ref, g2_ref, wr_ref, br_ref, tri_ref,
                    x1_ref, hp_ref, idx_ref, rank_ref, gate_ref, cnt_ref, ko_ref, vo_ref, uo_ref,
                    qt_s, kb_s, vt_s, ub_s, mix_s, cnt_s,
                    tile=tile, sub=sub, n_valid=n_valid, pos0=pos0, mask_first=mask_first)


def _mixer_rows(r0, s, x_ref, g1_ref, wukv_ref, wqvt_ref, qg_ref, kg_ref, bd_ref, bias_ref,
                sink_ref, wpool_ref, pscale_ref, wout_ref, g2_ref, wr_ref, br_ref, tri_ref,
                x1_ref, hp_ref, idx_ref, rank_ref, gate_ref, cnt_ref, ko_ref, vo_ref, uo_ref,
                qt_s, kb_s, vt_s, ub_s, mix_s, cnt_s, *, tile, sub, n_valid, pos0, mask_first):
    bf16 = jnp.bfloat16
    f32 = jnp.float32
    rows = slice(r0, r0 + sub)

    x = x_ref[rows, :]
    xn = (x * lax.rsqrt(jnp.mean(x * x, axis=-1, keepdims=True) + EPS) * g1_ref[...]).astype(bf16)
    z = jnp.dot(xn, wukv_ref[...], preferred_element_type=f32)
    zt = lax.dot_general(wqvt_ref[...], xn, (((1,), (1,)), ((), ())),
                         preferred_element_type=f32)
    u = z[:, 0:POOL_WIDTH]
    kz = z[:, POOL_WIDTH:POOL_WIDTH + KV_WIDTH]
    v = z[:, POOL_WIDTH + KV_WIDTH:]

    ksq = kz * kz
    kss = jnp.dot(ksq.astype(bf16), bd_ref[...], preferred_element_type=f32)
    kn = kz * lax.rsqrt(kss * (1.0 / HEAD_DIM) + EPS) * kg_ref[...]
    kb_s[WINDOW + r0:WINDOW + r0 + sub, :] = kn.astype(bf16)
    vt_s[:, WINDOW + r0:WINDOW + r0 + sub] = zt[ATTN_WIDTH:, :].astype(bf16)
    ub_s[HIST_ROWS + r0:HIST_ROWS + r0 + sub, :] = u

    if r0 + sub == tile:
        row0 = max(n_valid, WINDOW) - WINDOW - r0
        ko_ref[...] = kn[row0:row0 + WINDOW, :]
        vo_ref[...] = v[row0:row0 + WINDOW, :]
        uo_ref[...] = u[n_valid - HIST_ROWS - r0:n_valid - r0, :]

    for hd in range(N_HEADS):
        qh = zt[hd * HEAD_DIM:(hd + 1) * HEAD_DIM, :]
        ss = jnp.sum(qh * qh, axis=0, keepdims=True)
        qn = qh * (lax.rsqrt(ss * (1.0 / HEAD_DIM) + EPS) * ATTN_SCALE) * qg_ref[...]
        qt_s[hd * HEAD_DIM:(hd + 1) * HEAD_DIM, rows] = qn.astype(bf16)

    pos = pos0 + s * tile + r0 + lax.broadcasted_iota(jnp.int32, (sub, 1), 0)
    for g, w in enumerate(POOL_WINDOWS):
        e = ub_s[r0:r0 + HIST_ROWS + sub, g * POOL_GROUP:(g + 1) * POOL_GROUP]
        acc = e
        for lvl in range(g + 1):
            acc = acc + pltpu.roll(acc, 2 ** lvl, axis=0)
        inv_cnt = 1.0 / jnp.minimum(pos + 1, w).astype(f32)
        d = (acc[HIST_ROWS:, :] * inv_cnt - e[HIST_ROWS:, :]).astype(bf16)
        y = jnp.dot(d, wpool_ref[g], preferred_element_type=f32)
        y = y * pscale_ref[:, g * POOL_GROUP:(g + 1) * POOL_GROUP]
        mix_s[rows, g * POOL_GROUP:(g + 1) * POOL_GROUP] = y.astype(bf16)

    zeros_q = jnp.zeros((HEAD_DIM, GROUP * PAIR), bf16)
    for p in range(r0 // PAIR, (r0 + sub) // PAIR):
        k_band = kb_s[p * PAIR:p * PAIR + BAND, :]
        o_parts = []
        for h in range(N_KV_HEADS):
            qcat = jnp.concatenate(
                [qt_s[(h * GROUP + g) * HEAD_DIM:(h * GROUP + g + 1) * HEAD_DIM,
                      p * PAIR:(p + 1) * PAIR] for g in range(GROUP)], axis=1)
            rhs = jnp.concatenate([qcat, zeros_q] if h == 0 else [zeros_q, qcat], axis=0)
            st = jnp.dot(k_band, rhs, preferred_element_type=f32) + bias_ref[h]
            if mask_first and p == 0:
                krow = lax.broadcasted_iota(jnp.int32, (BAND, 1), 0)
                st = jnp.where((krow >= WINDOW) | (s > 0), st, NEG_INF)
            sink = sink_ref[h:h + 1, :]
            m = jnp.maximum(jnp.max(st, axis=0, keepdims=True), sink)
            ex = jnp.exp(st - m)
            den = jnp.sum(ex, axis=0, keepdims=True) + jnp.exp(sink - m)
            v_band = vt_s[h * HEAD_DIM:(h + 1) * HEAD_DIM, p * PAIR:p * PAIR + BAND]
            ot = jnp.dot(v_band, ex.astype(bf16), preferred_element_type=f32) / den
            for g in range(GROUP):
                o_parts.append(ot[:, g * PAIR:(g + 1) * PAIR])
        o_all = jnp.concatenate(o_parts, axis=0)
        mix_s[p * PAIR:(p + 1) * PAIR, POOL_WIDTH:] = jnp.transpose(o_all).astype(bf16)

    x1 = x + jnp.dot(mix_s[rows, :], wout_ref[...], preferred_element_type=f32)
    x1_ref[rows, :] = x1

    n_real = min(sub, n_valid - r0)
    hn = x1 * lax.rsqrt(jnp.mean(x1 * x1, axis=-1, keepdims=True) + EPS) * g2_ref[...]
    hp_ref[r0:r0 + n_real, :] = _pack_bf16_pair(hn[0:n_real, 0:D_MODEL // 2],
                                                hn[0:n_real, D_MODEL // 2:])
    h_hi = hn.astype(bf16)
    h_lo = (hn - h_hi.astype(f32)).astype(bf16)
    parts = jnp.transpose(jnp.dot(h_hi, wr_ref[...], preferred_element_type=f32)
                          + jnp.dot(h_lo, wr_ref[...], preferred_element_type=f32))
    lt = parts[0:N_EXPERTS, :] + parts[N_EXPERTS:2 * N_EXPERTS, :] + br_ref[...]

    eidx = lax.broadcasted_iota(jnp.int32, (N_EXPERTS, sub), 0).astype(f32)
    vals, hots = [], []
    for j in range(TOP_K):
        m = jnp.max(lt, axis=0, keepdims=True)
        sel = jnp.min(jnp.where(lt == m, eidx, float(N_EXPERTS)), axis=0, keepdims=True)
        hot = eidx == sel
        lt = jnp.where(hot, -jnp.inf, lt)
        idx_ref[j:j + 1, rows] = sel.astype(jnp.int32)
        vals.append(m)
        hots.append(hot)
    exps = [jnp.exp(vv - vals[0]) for vv in vals]
    esum = exps[0] + exps[1] + exps[2] + exps[3]
    grow = lax.broadcasted_iota(jnp.int32, (GATE_COLS, sub), 0)
    gmat = jnp.zeros((GATE_COLS, sub), f32)
    for j in range(TOP_K):
        gmat = jnp.where(grow == j, exps[j] / esum, gmat)
    gmat = jnp.concatenate([gmat, jnp.zeros((128 - GATE_COLS, sub), f32)], axis=0)
    gate_ref[r0:r0 + n_real, :] = jnp.transpose(gmat)[0:n_real, 0:GATE_COLS]

    chosen_f = sum(jnp.where(hot, 1.0, 0.0) for hot in hots)
    if n_real < sub:
        lane = lax.broadcasted_iota(jnp.int32, (N_EXPERTS, sub), 1)
        chosen_f = jnp.where(lane < n_real, chosen_f, 0.0)
    before = jnp.dot(chosen_f.astype(bf16), tri_ref[...], preferred_element_type=f32)
    base = before + cnt_s[:, 0:1]
    for j in range(TOP_K):
        rank_ref[j:j + 1, rows] = jnp.sum(jnp.where(hots[j], base, 0.0), axis=0,
                                          keepdims=True).astype(jnp.int32)
    cnt_new = cnt_s[...] + jnp.sum(chosen_f, axis=1, keepdims=True)
    cnt_s[...] = cnt_new
    cnt_ref[...] = cnt_new


def _mixer_call(x, k_hist, v_hist, u_hist, cnt_in, consts, *, stream0, tile, sub, n_valid, pos0,
                mask_first):
    seq = x.shape[1]
    nb = k_hist.shape[0]
    n_tiles = seq // tile
    f32 = jnp.float32
    assert n_valid % 8 == 0 and tile % sub == 0 and sub % PAIR == 0
    assert n_valid == tile or sub == tile

    def full(a):
        nd = a.ndim
        return pl.BlockSpec(a.shape, lambda b, s, _nd=nd: (0,) * _nd)

    in_specs = [
        pl.BlockSpec((None, tile, D_MODEL), lambda b, s: (stream0 + b, s, 0)),
        pl.BlockSpec((None, WINDOW, KV_WIDTH), lambda b, s: (b, 0, 0)),
        pl.BlockSpec((None, WINDOW, KV_WIDTH), lambda b, s: (b, 0, 0)),
        pl.BlockSpec((None, HIST_ROWS, POOL_WIDTH), lambda b, s: (b, 0, 0)),
        full(cnt_in),
    ] + [full(c) for c in consts]
    out_shape = [
        jax.ShapeDtypeStruct((nb, seq, D_MODEL), f32),
        jax.ShapeDtypeStruct((nb * n_tiles * n_valid, D_MODEL // 2), jnp.uint32),
        jax.ShapeDtypeStruct((nb, TOP_K, seq), jnp.int32),
        jax.ShapeDtypeStruct((nb, TOP_K, seq), jnp.int32),
        jax.ShapeDtypeStruct((nb * n_tiles * n_valid, GATE_COLS), f32),
        jax.ShapeDtypeStruct((N_EXPERTS, 128), f32),
        jax.ShapeDtypeStruct((nb, WINDOW, KV_WIDTH), f32),
        jax.ShapeDtypeStruct((nb, WINDOW, KV_WIDTH), f32),
        jax.ShapeDtypeStruct((nb, HIST_ROWS, POOL_WIDTH), f32),
    ]
    out_specs = [
        pl.BlockSpec((None, tile, D_MODEL), lambda b, s: (b, s, 0)),
        pl.BlockSpec((n_valid, D_MODEL // 2), lambda b, s: (b * n_tiles + s, 0)),
        pl.BlockSpec((None, TOP_K, tile), lambda b, s: (b, 0, s)),
        pl.BlockSpec((None, TOP_K, tile), lambda b, s: (b, 0, s)),
        pl.BlockSpec((n_valid, GATE_COLS), lambda b, s: (b * n_tiles + s, 0)),
        pl.BlockSpec((N_EXPERTS, 128), lambda b, s: (0, 0)),
        pl.BlockSpec((None, WINDOW, KV_WIDTH), lambda b, s: (b, 0, 0)),
        pl.BlockSpec((None, WINDOW, KV_WIDTH), lambda b, s: (b, 0, 0)),
        pl.BlockSpec((None, HIST_ROWS, POOL_WIDTH), lambda b, s: (b, 0, 0)),
    ]
    scratch = [
        pltpu.VMEM((ATTN_WIDTH, tile), jnp.bfloat16),
        pltpu.VMEM((WINDOW + tile, KV_WIDTH), jnp.bfloat16),
        pltpu.VMEM((KV_WIDTH, WINDOW + tile), jnp.bfloat16),
        pltpu.VMEM((HIST_ROWS + tile, POOL_WIDTH), f32),
        pltpu.VMEM((tile, D_MODEL), jnp.bfloat16),
        pltpu.VMEM((N_EXPERTS, 128), f32),
    ]
    kern = functools.partial(_mixer_kernel, tile=tile, sub=sub, n_valid=n_valid, pos0=pos0,
                             mask_first=mask_first)
    return pl.pallas_call(
        kern,
        grid=(nb, n_tiles),
        in_specs=in_specs,
        out_specs=out_specs,
        out_shape=out_shape,
        scratch_shapes=scratch,
        compiler_params=pltpu.CompilerParams(
            dimension_semantics=("arbitrary", "arbitrary"),
            vmem_limit_bytes=VMEM_LIMIT),
        name="mixer",
    )(x, k_hist, v_hist, u_hist, cnt_in, *consts)


def _expert_kernel_f32(be_ref, nv_ref, nx_ref, lv_ref, xs_ref, wg_hbm, bg_ref, wu_hbm, bu_ref,
                       wd_hbm, bd_ref, ys_ref, wg16_hbm, wu16_hbm, wd16_hbm,
                       stage_s, wg_s, wu_s, wd_s, in_sems, out_sems, *, n_blocks):
    i = pl.program_id(0)
    expert = be_ref[i]
    bf16 = jnp.bfloat16

    def copies_in(e):
        return [pltpu.make_async_copy(w_hbm.at[e], stage_s.at[k], in_sems.at[k])
                for k, w_hbm in enumerate((wg_hbm, wu_hbm, wd_hbm))]

    def copies_out(e):
        return [pltpu.make_async_copy(w_s, w16_hbm.at[e], out_sems.at[k])
                for k, (w_s, w16_hbm) in enumerate(((wg_s, wg16_hbm), (wu_s, wu16_hbm),
                                                    (wd_s, wd16_hbm)))]

    @pl.when(i == 0)
    def _():
        for copy in copies_in(expert):
            copy.start()

    @pl.when((lv_ref[i] > 0) & ((i == 0) | (expert != be_ref[jnp.maximum(i - 1, 0)])))
    def _():
        for copy in copies_in(expert):
            copy.wait()

        @pl.when(i > 0)
        def _():
            for copy in copies_out(expert):
                copy.wait()

        wg_s[...] = stage_s[0].astype(bf16)
        wu_s[...] = stage_s[1].astype(bf16)
        wd_s[...] = stage_s[2].astype(bf16)
        for copy in copies_out(expert):
            copy.start()

        @pl.when(nx_ref[i] != expert)
        def _():
            for copy in copies_in(nx_ref[i]):
                copy.start()

    _ffn_block(nv_ref[i], xs_ref, ys_ref, wg_s, wu_s, wd_s, bg_ref, bu_ref, bd_ref)

    @pl.when(i == n_blocks - 1)
    def _():
        for copy in copies_out(expert):
            copy.wait()


def _expert_kernel_bf16(be_ref, nv_ref, nx_ref, sl_ref, xs_ref, wg_hbm, bg_ref, wu_hbm, bu_ref,
                        wd_hbm, bd_ref, ys_ref, w_s, sems):
    i = pl.program_id(0)
    n_rows = nv_ref[i]
    expert = be_ref[i]
    slot = sl_ref[i]

    def copies(e, to_slot):
        return [pltpu.make_async_copy(w_hbm.at[e], w_s.at[to_slot, k], sems.at[to_slot, k])
                for k, w_hbm in enumerate((wg_hbm, wu_hbm, wd_hbm))]

    @pl.when((i == 0) & (n_rows > 0))
    def _():
        for copy in copies(expert, slot):
            copy.start()

    @pl.when((n_rows > 0) & ((i == 0) | (expert != be_ref[jnp.maximum(i - 1, 0)])))
    def _():
        for copy in copies(expert, slot):
            copy.wait()

        @pl.when(nx_ref[i] != expert)
        def _():
            for copy in copies(nx_ref[i], 1 - slot):
                copy.start()

    _ffn_block(n_rows, xs_ref, ys_ref, w_s.at[slot, 0], w_s.at[slot, 1], w_s.at[slot, 2],
               bg_ref, bu_ref, bd_ref)


def _ffn_block(n_rows, xs_ref, ys_ref, wg_s, wu_s, wd_s, bg_ref, bu_ref, bd_ref):
    bf16 = jnp.bfloat16

    def ffn(rows):
        f32 = jnp.float32
        half = D_MODEL // 2
        for r0 in range(0, rows, FFN_CHUNK):
            row = r0 + lax.broadcasted_iota(jnp.int32, (FFN_CHUNK, 1), 0)
            words = jnp.where(row < n_rows, xs_ref[r0:r0 + FFN_CHUNK, :],
                              jnp.uint32(0))
            xa, xb = _unpack_bf16_pair(words)
            a = (jnp.dot(xa, wg_s[0:half, :], preferred_element_type=f32)
                 + jnp.dot(xb, wg_s[half:, :], preferred_element_type=f32) + bg_ref[...])
            bb = (jnp.dot(xa, wu_s[0:half, :], preferred_element_type=f32)
                  + jnp.dot(xb, wu_s[half:, :], preferred_element_type=f32) + bu_ref[...])
            a = jnp.minimum(a, SWIGLU_LIMIT)
            bb = jnp.clip(bb, -SWIGLU_LIMIT, SWIGLU_LIMIT)
            act = a * (1.0 / (1.0 + jnp.exp(-SWIGLU_ALPHA * a))) * (bb + 1.0)
            y = jnp.dot(act.astype(bf16), wd_s[...], preferred_element_type=f32) + bd_ref[...]
            ys_ref[r0:r0 + FFN_CHUNK, :] = _pack_bf16_pair(y[:, 0:half], y[:, half:])
        if rows < EXPERT_BLOCK:
            ys_ref[rows:, :] = jnp.zeros((EXPERT_BLOCK - rows, half), jnp.uint32)

    quarter = EXPERT_BLOCK // 4
    for q in range(1, 5):
        @pl.when((n_rows > (q - 1) * quarter) & (n_rows <= q * quarter))
        def _(q=q):
            ffn(q * quarter)

    @pl.when(n_rows == 0)
    def _():
        ys_ref[...] = jnp.zeros_like(ys_ref)


def _expert_call_f32(block_e, block_rows, block_next, block_live, xs, wg, bg, wu, bu, wd, bd):
    n_slots = xs.shape[0]
    n_blocks = n_slots // EXPERT_BLOCK
    any_spec = pl.BlockSpec(memory_space=pl.ANY)
    b_spec = pl.BlockSpec((None, 1, D_MODEL), lambda i, be, nv, nx, lv: (be[i], 0, 0))
    x_spec = pl.BlockSpec((EXPERT_BLOCK, D_MODEL // 2), lambda i, be, nv, nx, lv: (i, 0))
    grid_spec = pltpu.PrefetchScalarGridSpec(
        num_scalar_prefetch=4,
        grid=(n_blocks,),
        in_specs=[x_spec, any_spec, b_spec, any_spec, b_spec, any_spec, b_spec],
        out_specs=[x_spec, any_spec, any_spec, any_spec],
        scratch_shapes=[pltpu.VMEM((3, D_MODEL, D_MODEL), jnp.float32)]
        + [pltpu.VMEM((D_MODEL, D_MODEL), jnp.bfloat16)] * 3
        + [pltpu.SemaphoreType.DMA((3,)), pltpu.SemaphoreType.DMA((3,))],
    )
    w16 = jax.ShapeDtypeStruct((N_EXPERTS, D_MODEL, D_MODEL), jnp.bfloat16)
    return pl.pallas_call(
        functools.partial(_expert_kernel_f32, n_blocks=n_blocks),
        grid_spec=grid_spec,
        out_shape=[jax.ShapeDtypeStruct((n_slots, D_MODEL // 2), jnp.uint32), w16, w16, w16],
        compiler_params=pltpu.CompilerParams(
            dimension_semantics=("arbitrary",),
            vmem_limit_bytes=VMEM_LIMIT),
        name="experts_f32",
    )(block_e, block_rows, block_next, block_live, xs, wg, bg, wu, bu, wd, bd)


def _expert_call_bf16(block_e, block_rows, block_next, block_slot, xs, wg, bg, wu, bu, wd, bd):
    n_slots = xs.shape[0]
    any_spec = pl.BlockSpec(memory_space=pl.ANY)
    b_spec = pl.BlockSpec((None, 1, D_MODEL), lambda i, be, nv, nx, sl: (be[i], 0, 0))
    x_spec = pl.BlockSpec((EXPERT_BLOCK, D_MODEL // 2), lambda i, be, nv, nx, sl: (i, 0))
    grid_spec = pltpu.PrefetchScalarGridSpec(
        num_scalar_prefetch=4,
        grid=(n_slots // EXPERT_BLOCK,),
        in_specs=[x_spec, any_spec, b_spec, any_spec, b_spec, any_spec, b_spec],
        out_specs=x_spec,
        scratch_shapes=[pltpu.VMEM((2, 3, D_MODEL, D_MODEL), jnp.bfloat16),
                        pltpu.SemaphoreType.DMA((2, 3))],
    )
    return pl.pallas_call(
        _expert_kernel_bf16,
        grid_spec=grid_spec,
        out_shape=jax.ShapeDtypeStruct((n_slots, D_MODEL // 2), jnp.uint32),
        compiler_params=pltpu.CompilerParams(
            dimension_semantics=("arbitrary",),
            vmem_limit_bytes=VMEM_LIMIT),
        name="experts_bf16",
    )(block_e, block_rows, block_next, block_slot, xs, wg, bg, wu, bu, wd, bd)


def _combine_kernel(out_buf_ref, x1_ref, g_ref, y0_ref, y1_ref, y2_ref, y3_ref, o_ref):
    del out_buf_ref
    g = g_ref[...]
    half = D_MODEL // 2
    lo, hi = x1_ref[:, 0:half], x1_ref[:, half:]
    for j, y_ref in enumerate((y0_ref, y1_ref, y2_ref, y3_ref)):
        w = y_ref[...]
        gj = g[:, j:j + 1]
        lo = lo + gj * lax.bitcast_convert_type(w << 16, jnp.float32)
        hi = hi + gj * lax.bitcast_convert_type(w & jnp.uint32(0xFFFF0000), jnp.float32)
    o_ref[:, 0:half] = lo
    o_ref[:, half:] = hi


def _combine_call(out_buf, x1, gates, picked, stride, tok0, *, out_rows, row0, tile):
    n = x1.shape[0]
    aliased = out_buf.shape == (out_rows, D_MODEL)
    assert n % tile == 0 and row0 % tile == 0
    y_specs = []
    for j in range(TOP_K):
        assert (j * stride + tok0) % tile == 0
        base = (j * stride + tok0) // tile
        y_specs.append(pl.BlockSpec((tile, D_MODEL // 2), lambda i, _b=base: (_b + i, 0)))
    return pl.pallas_call(
        _combine_kernel,
        grid=(n // tile,),
        in_specs=[pl.BlockSpec(memory_space=pl.ANY),
                  pl.BlockSpec((tile, D_MODEL), lambda i: (i, 0)),
                  pl.BlockSpec((tile, GATE_COLS), lambda i: (i, 0))] + y_specs,
        out_specs=pl.BlockSpec((tile, D_MODEL), lambda i: (row0 // tile + i, 0)),
        out_shape=jax.ShapeDtypeStruct((out_rows, D_MODEL), jnp.float32),
        input_output_aliases={0: 0} if aliased else {},
        compiler_params=pltpu.CompilerParams(dimension_semantics=("arbitrary",),
                                             vmem_limit_bytes=VMEM_LIMIT),
        name="combine",
    )(out_buf, x1, gates, picked, picked, picked, picked)


def _gather_rows(table, idx):
    n = idx.shape[0]
    width = table.shape[1]
    per_worker = n // SC_WORKERS
    n_chunks = per_worker // SC_CHUNK
    mesh = plsc.VectorSubcoreMesh(core_axis_name="c", subcore_axis_name="s")

    @functools.partial(
        pl.kernel, mesh=mesh,
        out_type=jax.ShapeDtypeStruct((n, width), table.dtype),
        scratch_types=[pltpu.VMEM((SC_CHUNK,), jnp.int32),
                       pltpu.VMEM((SC_CHUNK, width), table.dtype),
                       pltpu.SemaphoreType.DMA],
        cost_estimate=pl.CostEstimate(flops=0, transcendentals=0, bytes_accessed=8 * n * width),
    )
    def gather(table_hbm, idx_hbm, out_hbm, idx_v, rows_v, sem):
        wid = lax.axis_index("s") * 2 + lax.axis_index("c")
        base = wid * per_worker

        @pl.loop(0, n_chunks)
        def _(i):
            off = base + i * SC_CHUNK
            pltpu.sync_copy(idx_hbm.at[pl.ds(off, SC_CHUNK)], idx_v)
            pltpu.async_copy(table_hbm.at[idx_v], rows_v, sem).wait()
            pltpu.sync_copy(rows_v, out_hbm.at[pl.ds(off, SC_CHUNK)])

    return gather(table, idx)


def _scatter_rows(srcs, dest, n_out):
    width = srcs[0].shape[1]
    starts = [0]
    for src in srcs:
        assert src.shape[0] % SC_CHUNK == 0
        starts.append(starts[-1] + src.shape[0] // SC_CHUNK)
    n_chunks = starts[-1]
    n = n_chunks * SC_CHUNK
    per_worker = -(-n_chunks // SC_WORKERS)
    mesh = plsc.VectorSubcoreMesh(core_axis_name="c", subcore_axis_name="s")

    @functools.partial(
        pl.kernel, mesh=mesh,
        out_type=jax.ShapeDtypeStruct((n_out, width), srcs[0].dtype),
        scratch_types=[pltpu.VMEM((SC_CHUNK,), jnp.int32)] * TOP_K
        + [pltpu.VMEM((SC_CHUNK, width), srcs[0].dtype)],
        cost_estimate=pl.CostEstimate(flops=0, transcendentals=0,
                                      bytes_accessed=4 * (1 + TOP_K) * n * width),
    )
    def scatter(*refs):
        src_hbms = refs[:len(srcs)]
        dest_hbm, out_hbm = refs[len(srcs)], refs[len(srcs) + 1]
        idx_vs, rows_v = refs[len(srcs) + 2:len(srcs) + 2 + TOP_K], refs[-1]
        wid = lax.axis_index("s") * 2 + lax.axis_index("c")

        @pl.loop(0, per_worker)
        def _(i):
            c = i * SC_WORKERS + wid
            for k, src_hbm in enumerate(src_hbms):

                @pl.when((c >= starts[k]) & (c < starts[k + 1]))
                def _():
                    pltpu.sync_copy(src_hbm.at[pl.ds((c - starts[k]) * SC_CHUNK, SC_CHUNK)], rows_v)
                    for j, idx_v in enumerate(idx_vs):
                        pltpu.sync_copy(dest_hbm.at[pl.ds(j * n + c * SC_CHUNK, SC_CHUNK)], idx_v)
                    for idx_v in idx_vs:
                        pltpu.sync_copy(rows_v, out_hbm.at[idx_v])

    return scatter(*srcs, dest)


def _t5_bucket_np(rel):
    half = NUM_BUCKETS // 2
    max_exact = half // 2
    n = np.abs(rel)
    nf = np.maximum(n, 1).astype(np.float32)
    large = max_exact + (np.log(nf / max_exact) / math.log(MAX_DISTANCE / max_exact)
                         * (half - max_exact)).astype(np.int32)
    large = np.minimum(large, half - 1)
    return np.where(rel > 0, half, 0) + np.where(n < max_exact, n, large)


def _bias_tables(rel_bias, visible):
    kap = np.arange(BAND)[:, None]
    rho = np.arange(PAIR)[None, :]
    bucket = _t5_bucket_np(kap - WINDOW - rho)
    rb = rel_bias.astype(jnp.float32)
    tab = jnp.zeros((BAND, PAIR, N_HEADS), jnp.float32)
    for bkt in range(NUM_BUCKETS):
        tab = jnp.where(jnp.asarray(bucket == bkt)[:, :, None], rb[bkt], tab)
    tab = jnp.where(jnp.asarray(visible)[:, :, None], tab, NEG_INF)
    tab = jnp.transpose(tab, (2, 0, 1)).reshape(N_KV_HEADS, GROUP, BAND, PAIR)
    return jnp.transpose(tab, (0, 2, 1, 3)).reshape(N_KV_HEADS, BAND, GROUP * PAIR)


def _mixer_consts(l, norm1_g, w_in, q_norm_g, k_norm_g, rel_bias, sinks, w_pool, pool_scale,
                  w_out, norm2_g, w_router, b_router, visible, tile):
    f32, bf16 = jnp.float32, jnp.bfloat16
    q_off, k_off, v_off = POOL_WIDTH, POOL_WIDTH + ATTN_WIDTH, POOL_WIDTH + ATTN_WIDTH + KV_WIDTH
    w = w_in[l]
    w_ukv = jnp.concatenate([w[:, :q_off], w[:, k_off:]], axis=1).astype(bf16)
    w_qvt = jnp.transpose(jnp.concatenate([w[:, q_off:k_off], w[:, v_off:]], axis=1)).astype(bf16)
    lane_head = np.arange(KV_WIDTH) // HEAD_DIM
    blockdiag = jnp.asarray(lane_head[:, None] == lane_head[None, :], bf16)
    sink_rows = jnp.repeat(sinks[l].astype(f32).reshape(N_KV_HEADS, GROUP), PAIR, axis=1)
    wr = w_router[l].astype(f32)
    wr_hi = wr.astype(bf16)
    wr_lo = (wr - wr_hi.astype(f32)).astype(bf16)
    wr_parts = jnp.pad(jnp.concatenate([wr_hi, wr_lo], axis=1), ((0, 0), (0, 128 - 2 * N_EXPERTS)))
    tri = jnp.asarray(np.arange(tile)[:, None] < np.arange(tile)[None, :], bf16)
    return [
        norm1_g[l].reshape(1, D_MODEL).astype(f32), w_ukv, w_qvt,
        q_norm_g[l].reshape(HEAD_DIM, 1).astype(f32),
        jnp.tile(k_norm_g[l].astype(f32), N_KV_HEADS).reshape(1, KV_WIDTH),
        blockdiag, _bias_tables(rel_bias, visible), sink_rows,
        w_pool[l].astype(bf16), pool_scale[l].reshape(1, POOL_WIDTH).astype(f32),
        w_out[l].astype(bf16), norm2_g[l].reshape(1, D_MODEL).astype(f32),
        wr_parts, b_router[l].reshape(N_EXPERTS, 1).astype(f32), tri,
    ]


def _visibility():
    kap = np.arange(BAND)[:, None]
    rho = np.arange(PAIR)[None, :]
    kc, qc = kap // CHUNK, rho // CHUNK
    prompt = (kc >= qc) & (kc <= qc + WINDOW // CHUNK)
    return prompt


def kernel(x_prompt, x_sample, cache_k, cache_v, state_pool, norm1_g, w_in, q_norm_g, k_norm_g,
           rel_bias, sinks, w_pool, pool_scale, w_out, norm2_g, w_router, b_router,
           w_gate, b_gate, w_up, b_up, w_down, b_down):
    f32, bf16 = jnp.float32, jnp.bfloat16
    depth = w_in.shape[0]
    nb, seq, _ = x_prompt.shape
    ndb, dseq, _ = x_sample.shape
    cache_len = cache_k.shape[2]
    assert seq % MIX_TILE == 0 and cache_len == WINDOW and HIST_ROWS <= dseq <= PAIR
    n_p, n_s = nb * seq, ndb * dseq
    n_tok = n_p + n_s
    assert COMBINE_TILE % n_s == 0 and n_s % SC_CHUNK == 0

    vis_prompt = _visibility()
    vis_sample = np.broadcast_to(np.arange(BAND)[:, None] < WINDOW + dseq, (BAND, PAIR))

    nb1 = nb // 4
    nb2 = nb - nb1
    assert nb1 > 0 and nb2 > 0 and seq % COMBINE_TILE == 0

    xp, xs = x_prompt, x_sample
    outs = [[] for _ in range(6)]
    for l in range(depth):
        wl = (l, norm1_g, w_in, q_norm_g, k_norm_g, rel_bias, sinks, w_pool, pool_scale, w_out,
              norm2_g, w_router, b_router)
        consts_p = _mixer_consts(*wl, vis_prompt, MIX_SUB)
        moe_b = [b.reshape(N_EXPERTS, 1, D_MODEL).astype(f32) for b in (b_gate[l], b_up[l], b_down[l])]
        cnt0 = jnp.zeros((N_EXPERTS, 128), f32)

        def prompt_mixer(stream0, n_streams, cnt_in):
            zk = jnp.zeros((n_streams, WINDOW, KV_WIDTH), f32)
            zu = jnp.zeros((n_streams, HIST_ROWS, POOL_WIDTH), f32)
            return _mixer_call(xp, zk, zk, zu, cnt_in, consts_p, stream0=stream0, tile=MIX_TILE,
                               sub=MIX_SUB, n_valid=MIX_TILE, pos0=0, mask_first=True)

        (x1_a, h_a, idx_a, rank_a, gate_a, cnt_a, k_a, v_a, u_a) = prompt_mixer(0, nb1, cnt0)
        xs_pad = jnp.pad(xs, ((0, 0), (0, PAIR - dseq), (0, 0)))
        uh = jnp.pad(state_pool[l], ((0, 0), (HIST_ROWS - POOL_HIST, 0), (0, 0)))
        (xs1, h_s, idx_s, rank_s, gate_s, cnt_s, k_s, v_s, u_s) = _mixer_call(
            xs_pad, cache_k[l].reshape(ndb, WINDOW, KV_WIDTH),
            cache_v[l].reshape(ndb, WINDOW, KV_WIDTH), uh, cnt0,
            _mixer_consts(*wl, vis_sample, PAIR),
            stream0=0, tile=PAIR, sub=PAIR, n_valid=dseq, pos0=PAST_LEN, mask_first=False)
        (x1_b, h_b, idx_b, rank_b, gate_b, cnt_b, k_b, v_b, u_b) = prompt_mixer(nb1, nb2, cnt_s)
        picked_b, w_bf16 = _moe_rows([h_b, h_s], [idx_b, idx_s], [rank_b, rank_s], [seq, dseq], cnt_b,
                                     moe_b, w_f32=(w_gate[l], w_up[l], w_down[l]))
        picked_a, _ = _moe_rows([h_a], [idx_a], [rank_a], [seq], cnt_a, moe_b, w_bf16=w_bf16)

        n_a, n_b = nb1 * seq, nb2 * seq
        no_buf = jnp.zeros((8, 128), f32)
        stride_a, stride_b = _pick_stride(n_a), _pick_stride(n_b + n_s)
        xp_rows = _combine_call(no_buf, x1_b.reshape(n_b, D_MODEL), gate_b, picked_b, stride_b, 0,
                                out_rows=n_p, row0=n_a, tile=COMBINE_TILE)
        xp_rows = _combine_call(xp_rows, x1_a.reshape(n_a, D_MODEL), gate_a, picked_a, stride_a, 0,
                                out_rows=n_p, row0=0, tile=COMBINE_TILE)
        xp = xp_rows.reshape(nb, seq, D_MODEL)
        xs = _combine_call(no_buf, xs1[:, :dseq].reshape(n_s, D_MODEL), gate_s, picked_b, stride_b,
                           n_b, out_rows=n_s, row0=0, tile=n_s).reshape(ndb, dseq, D_MODEL)

        outs[0].append(jnp.concatenate([k_a, k_b]).reshape(nb, WINDOW, N_KV_HEADS, HEAD_DIM))
        outs[1].append(jnp.concatenate([v_a, v_b]).reshape(nb, WINDOW, N_KV_HEADS, HEAD_DIM))
        outs[2].append(jnp.concatenate([u_a, u_b])[:, HIST_ROWS - POOL_HIST:])
        outs[3].append(k_s[:, :dseq].reshape(ndb, dseq, N_KV_HEADS, HEAD_DIM))
        outs[4].append(v_s[:, :dseq].reshape(ndb, dseq, N_KV_HEADS, HEAD_DIM))
        outs[5].append(u_s[:, HIST_ROWS - POOL_HIST:])
    return (xp, xs) + tuple(jnp.stack(o) for o in outs)


def _moe_rows(hs, idxs, ranks, n_reals, cnt, biases, w_f32=None, w_bf16=None):
    n_tok = sum(h.shape[0] for h in hs)
    n_assign = n_tok * TOP_K
    gather_quant = SC_WORKERS * SC_CHUNK
    n_blocks = n_assign // EXPERT_BLOCK + N_EXPERTS
    n_blocks = -(-n_blocks // (gather_quant // EXPERT_BLOCK)) * (gather_quant // EXPERT_BLOCK)
    e_ids = jnp.arange(N_EXPERTS, dtype=jnp.int32)

    def per_token(arrs):
        return jnp.concatenate(
            [jnp.transpose(a[:, :, :n], (1, 0, 2)).reshape(TOP_K, -1) for a, n in zip(arrs, n_reals)],
            axis=1)

    counts = cnt[:, 0].astype(jnp.int32)
    pcounts = (counts + EXPERT_BLOCK - 1) // EXPERT_BLOCK * EXPERT_BLOCK
    if w_bf16 is None:
        pcounts = jnp.maximum(pcounts, EXPERT_BLOCK)
    pend = jnp.cumsum(pcounts)
    pstart = pend - pcounts
    idx_all = per_token(idxs)
    dest = per_token(ranks) + jnp.sum(
        jnp.where(idx_all[None] == e_ids[:, None, None], pstart[:, None, None], 0), axis=0)
    dest = dest.reshape(-1)
    blk0 = jnp.arange(n_blocks, dtype=jnp.int32) * EXPERT_BLOCK
    block_e = jnp.minimum(jnp.sum((pend[None, :] <= blk0[:, None]).astype(jnp.int32), axis=1),
                          N_EXPERTS - 1)
    block_rows = jnp.clip(jnp.sum(jnp.where(block_e[:, None] == e_ids[None, :],
                                             (pstart + counts)[None, :], 0), axis=1) - blk0,
                          0, EXPERT_BLOCK).astype(jnp.int32)

    x_sorted = _scatter_rows(hs, dest, n_blocks * EXPERT_BLOCK)
    if w_bf16 is None:
        block_next = jnp.minimum(block_e + 1, N_EXPERTS - 1)
        block_live = (blk0 < pend[-1]).astype(jnp.int32)
        y_sorted, *w_bf16 = _expert_call_f32(
            block_e, block_rows, block_next, block_live, x_sorted,
            w_f32[0], biases[0], w_f32[1], biases[1], w_f32[2], biases[2])
    else:
        has = counts > 0
        later = has[None, :] & (e_ids[None, :] > e_ids[:, None])
        next_e = jnp.min(jnp.where(later, e_ids[None, :], N_EXPERTS), axis=1)
        next_e = jnp.where(next_e == N_EXPERTS, e_ids, next_e)
        order = jnp.cumsum(has.astype(jnp.int32)) - 1
        table = jnp.stack([next_e, order % 2], axis=0)[:, None, :]
        block_next, block_slot = jnp.sum(
            jnp.where(block_e[None, :, None] == e_ids[None, None, :], table, 0), axis=2).astype(jnp.int32)
        y_sorted = _expert_call_bf16(
            block_e, block_rows, block_next, block_slot, x_sorted,
            w_bf16[0], biases[0], w_bf16[1], biases[1], w_bf16[2], biases[2])
    stride = _pick_stride(n_tok)
    n_pick = -(-(TOP_K * stride) // gather_quant) * gather_quant
    picks = jnp.pad(dest.reshape(TOP_K, n_tok), ((0, 0), (0, stride - n_tok))).reshape(-1)
    return _gather_rows(y_sorted, jnp.pad(picks, (0, n_pick - TOP_K * stride))), w_bf16


def _pick_stride(n_tok):
    return -(-n_tok // COMBINE_TILE) * COMBINE_TILE
```

```python
import functools
import math

import numpy as np
import jax
import jax.numpy as jnp
from jax import lax
from jax.experimental import pallas as pl
from jax.experimental.pallas import tpu as pltpu
from jax.experimental.pallas import tpu_sc as plsc

D_MODEL = 1024
CHUNK = 64
POOL_WIDTH = 512
POOL_WINDOWS = (2, 4, 8, 16)
POOL_GROUP = 128
POOL_HIST = 15
ATTN_WIDTH = 512
HEAD_DIM = 64
N_HEADS = 8
N_KV_HEADS = 2
GROUP = 4
KV_WIDTH = 128
WINDOW = 128
NUM_BUCKETS = 32
MAX_DISTANCE = 128
PAST_LEN = 2048
N_EXPERTS = 32
TOP_K = 4
SWIGLU_LIMIT = 7.0
SWIGLU_ALPHA = 1.702
EPS = 1e-5
NEG_INF = -1e30
ATTN_SCALE = HEAD_DIM ** -0.5

PAIR = 2 * CHUNK
BAND = PAIR + WINDOW
HIST_ROWS = 16
GATE_COLS = 8
MIX_TILE = 1024
MIX_SUB = 1024
EXPERT_BLOCK = 1024
FFN_CHUNK = 256
COMBINE_TILE = 1024
SC_WORKERS = 32
SC_CHUNK = 64
VMEM_LIMIT = 56 * 1024 * 1024


def _pack_bf16_pair(a, b):
    ab = lax.bitcast_convert_type(a.astype(jnp.bfloat16).astype(jnp.float32), jnp.uint32)
    bb = lax.bitcast_convert_type(b.astype(jnp.bfloat16).astype(jnp.float32), jnp.uint32)
    return (ab >> 16) | (bb & jnp.uint32(0xFFFF0000))


def _unpack_bf16_pair(w):
    a = lax.bitcast_convert_type(w << 16, jnp.float32).astype(jnp.bfloat16)
    b = lax.bitcast_convert_type(w & jnp.uint32(0xFFFF0000), jnp.float32).astype(jnp.bfloat16)
    return a, b


def _mixer_kernel(x_ref, kh_ref, vh_ref, uh_ref, cnt_in_ref,
                  g1_ref, wukv_ref, wqvt_ref, qg_ref, kg_ref, bd_ref, bias_ref, sink_ref,
                  wpool_ref, pscale_ref, wout_ref, g2_ref, wr_ref, br_ref, tri_ref,
                  x1_ref, hp_ref, idx_ref, rank_ref, gate_ref, cnt_ref, ko_ref, vo_ref, uo_ref,
                  qt_s, kb_s, vt_s, ub_s, mix_s, cnt_s,
                  *, tile, sub, n_valid, pos0, mask_first):
    b = pl.program_id(0)
    s = pl.program_id(1)
    bf16 = jnp.bfloat16
    f32 = jnp.float32

    @pl.when((b == 0) & (s == 0))
    def _():
        cnt_s[...] = cnt_in_ref[...]

    @pl.when(s == 0)
    def _():
        kb_s[0:WINDOW, :] = kh_ref[...].astype(bf16)
        vt_s[:, 0:WINDOW] = jnp.transpose(vh_ref[...]).astype(bf16)
        ub_s[0:HIST_ROWS, :] = uh_ref[...]

    @pl.when(s > 0)
    def _():
        kb_s[0:WINDOW, :] = kb_s[tile:tile + WINDOW, :]
        vt_s[:, 0:WINDOW] = vt_s[:, tile:tile + WINDOW]
        ub_s[0:HIST_ROWS, :] = ub_s[tile:tile + HIST_ROWS, :]

    for r0 in range(0, tile, sub):
        _mixer_rows(r0, s, x_ref, g1_ref, wukv_ref, wqvt_ref, qg_ref, kg_ref, bd_ref, bias_ref,
                    sink_ref, wpool_ref, pscale_ref, wout_ref, g2_ref, wr_ref, br_ref, tri_ref,
                    x1_ref, hp_ref, idx_ref, rank_ref, gate_ref, cnt_ref, ko_ref, vo_ref, uo_ref,
                    qt_s, kb_s, vt_s, ub_s, mix_s, cnt_s,
                    tile=tile, sub=sub, n_valid=n_valid, pos0=pos0, mask_first=mask_first)


def _mixer_rows(r0, s, x_ref, g1_ref, wukv_ref, wqvt_ref, qg_ref, kg_ref, bd_ref, bias_ref,
                sink_ref, wpool_ref, pscale_ref, wout_ref, g2_ref, wr_ref, br_ref, tri_ref,
                x1_ref, hp_ref, idx_ref, rank_ref, gate_ref, cnt_ref, ko_ref, vo_ref, uo_ref,
                qt_s, kb_s, vt_s, ub_s, mix_s, cnt_s, *, tile, sub, n_valid, pos0, mask_first):
    bf16 = jnp.bfloat16
    f32 = jnp.float32
    rows = slice(r0, r0 + sub)

    x = x_ref[rows, :]
    xn = (x * lax.rsqrt(jnp.mean(x * x, axis=-1, keepdims=True) + EPS) * g1_ref[...]).astype(bf16)
    z = jnp.dot(xn, wukv_ref[...], preferred_element_type=f32)
    zt = lax.dot_general(wqvt_ref[...], xn, (((1,), (1,)), ((), ())),
                         preferred_element_type=f32)
    u = z[:, 0:POOL_WIDTH]
    kz = z[:, POOL_WIDTH:POOL_WIDTH + KV_WIDTH]
    v = z[:, POOL_WIDTH + KV_WIDTH:]

    ksq = kz * kz
    kss = jnp.dot(ksq.astype(bf16), bd_ref[...], preferred_element_type=f32)
    kn = kz * lax.rsqrt(kss * (1.0 / HEAD_DIM) + EPS) * kg_ref[...]
    kb_s[WINDOW + r0:WINDOW + r0 + sub, :] = kn.astype(bf16)
    vt_s[:, WINDOW + r0:WINDOW + r0 + sub] = zt[ATTN_WIDTH:, :].astype(bf16)
    ub_s[HIST_ROWS + r0:HIST_ROWS + r0 + sub, :] = u

    if r0 + sub == tile:
        row0 = max(n_valid, WINDOW) - WINDOW - r0
        ko_ref[...] = kn[row0:row0 + WINDOW, :]
        vo_ref[...] = v[row0:row0 + WINDOW, :]
        uo_ref[...] = u[n_valid - HIST_ROWS - r0:n_valid - r0, :]

    for hd in range(N_HEADS):
        qh = zt[hd * HEAD_DIM:(hd + 1) * HEAD_DIM, :]
        ss = jnp.sum(qh * qh, axis=0, keepdims=True)
        qn = qh * (lax.rsqrt(ss * (1.0 / HEAD_DIM) + EPS) * ATTN_SCALE) * qg_ref[...]
        qt_s[hd * HEAD_DIM:(hd + 1) * HEAD_DIM, rows] = qn.astype(bf16)

    pos = pos0 + s * tile + r0 + lax.broadcasted_iota(jnp.int32, (sub, 1), 0)
    for g, w in enumerate(POOL_WINDOWS):
        e = ub_s[r0:r0 + HIST_ROWS + sub, g * POOL_GROUP:(g + 1) * POOL_GROUP]
        acc = e
        for lvl in range(g + 1):
            acc = acc + pltpu.roll(acc, 2 ** lvl, axis=0)
        inv_cnt = 1.0 / jnp.minimum(pos + 1, w).astype(f32)
        d = (acc[HIST_ROWS:, :] * inv_cnt - e[HIST_ROWS:, :]).astype(bf16)
        y = jnp.dot(d, wpool_ref[g], preferred_element_type=f32)
        y = y * pscale_ref[:, g * POOL_GROUP:(g + 1) * POOL_GROUP]
        mix_s[rows, g * POOL_GROUP:(g + 1) * POOL_GROUP] = y.astype(bf16)

    zeros_q = jnp.zeros((HEAD_DIM, GROUP * PAIR), bf16)
    for p in range(r0 // PAIR, (r0 + sub) // PAIR):
        k_band = kb_s[p * PAIR:p * PAIR + BAND, :]
        o_parts = []
        for h in range(N_KV_HEADS):
            qcat = jnp.concatenate(
                [qt_s[(h * GROUP + g) * HEAD_DIM:(h * GROUP + g + 1) * HEAD_DIM,
                      p * PAIR:(p + 1) * PAIR] for g in range(GROUP)], axis=1)
            rhs = jnp.concatenate([qcat, zeros_q] if h == 0 else [zeros_q, qcat], axis=0)
            st = jnp.dot(k_band, rhs, preferred_element_type=f32) + bias_ref[h]
            if mask_first and p == 0:
                krow = lax.broadcasted_iota(jnp.int32, (BAND, 1), 0)
                st = jnp.where((krow >= WINDOW) | (s > 0), st, NEG_INF)
            sink = sink_ref[h:h + 1, :]
            m = jnp.maximum(jnp.max(st, axis=0, keepdims=True), sink)
            ex = jnp.exp(st - m)
            den = jnp.sum(ex, axis=0, keepdims=True) + jnp.exp(sink - m)
            v_band = vt_s[h * HEAD_DIM:(h + 1) * HEAD_DIM, p * PAIR:p * PAIR + BAND]
            ot = jnp.dot(v_band, ex.astype(bf16), preferred_element_type=f32) / den
            for g in range(GROUP):
                o_parts.append(ot[:, g * PAIR:(g + 1) * PAIR])
        o_all = jnp.concatenate(o_parts, axis=0)
        mix_s[p * PAIR:(p + 1) * PAIR, POOL_WIDTH:] = jnp.transpose(o_all).astype(bf16)

    x1 = x + jnp.dot(mix_s[rows, :], wout_ref[...], preferred_element_type=f32)
    x1_ref[rows, :] = x1

    n_real = min(sub, n_valid - r0)
    hn = x1 * lax.rsqrt(jnp.mean(x1 * x1, axis=-1, keepdims=True) + EPS) * g2_ref[...]
    hp_ref[r0:r0 + n_real, :] = _pack_bf16_pair(hn[0:n_real, 0:D_MODEL // 2],
                                                hn[0:n_real, D_MODEL // 2:])
    h_hi = hn.astype(bf16)
    h_lo = (hn - h_hi.astype(f32)).astype(bf16)
    parts = jnp.transpose(jnp.dot(h_hi, wr_ref[...], preferred_element_type=f32)
                          + jnp.dot(h_lo, wr_ref[...], preferred_element_type=f32))
    lt = parts[0:N_EXPERTS, :] + parts[N_EXPERTS:2 * N_EXPERTS, :] + br_ref[...]

    eidx = lax.broadcasted_iota(jnp.int32, (N_EXPERTS, sub), 0).astype(f32)
    vals, hots = [], []
    for j in range(TOP_K):
        m = jnp.max(lt, axis=0, keepdims=True)
        sel = jnp.min(jnp.where(lt == m, eidx, float(N_EXPERTS)), axis=0, keepdims=True)
        hot = eidx == sel
        lt = jnp.where(hot, -jnp.inf, lt)
        idx_ref[j:j + 1, rows] = sel.astype(jnp.int32)
        vals.append(m)
        hots.append(hot)
    exps = [jnp.exp(vv - vals[0]) for vv in vals]
    esum = exps[0] + exps[1] + exps[2] + exps[3]
    grow = lax.broadcasted_iota(jnp.int32, (GATE_COLS, sub), 0)
    gmat = jnp.zeros((GATE_COLS, sub), f32)
    for j in range(TOP_K):
        gmat = jnp.where(grow == j, exps[j] / esum, gmat)
    gmat = jnp.concatenate([gmat, jnp.zeros((128 - GATE_COLS, sub), f32)], axis=0)
    gate_ref[r0:r0 + n_real, :] = jnp.transpose(gmat)[0:n_real, 0:GATE_COLS]

    chosen_f = sum(jnp.where(hot, 1.0, 0.0) for hot in hots)
    if n_real < sub:
        lane = lax.broadcasted_iota(jnp.int32, (N_EXPERTS, sub), 1)
        chosen_f = jnp.where(lane < n_real, chosen_f, 0.0)
    before = jnp.dot(chosen_f.astype(bf16), tri_ref[...], preferred_element_type=f32)
    base = before + cnt_s[:, 0:1]
    for j in range(TOP_K):
        rank_ref[j:j + 1, rows] = jnp.sum(jnp.where(hots[j], base, 0.0), axis=0,
                                          keepdims=True).astype(jnp.int32)
    cnt_new = cnt_s[...] + jnp.sum(chosen_f, axis=1, keepdims=True)
    cnt_s[...] = cnt_new
    cnt_ref[...] = cnt_new


def _mixer_call(x, k_hist, v_hist, u_hist, cnt_in, consts, *, stream0, tile, sub, n_valid, pos0,
                mask_first):
    seq = x.shape[1]
    nb = k_hist.shape[0]
    n_tiles = seq // tile
    f32 = jnp.float32
    assert n_valid % 8 == 0 and tile % sub == 0 and sub % PAIR == 0
    assert n_valid == tile or sub == tile

    def full(a):
        nd = a.ndim
        return pl.BlockSpec(a.shape, lambda b, s, _nd=nd: (0,) * _nd)

    in_specs = [
        pl.BlockSpec((None, tile, D_MODEL), lambda b, s: (stream0 + b, s, 0)),
        pl.BlockSpec((None, WINDOW, KV_WIDTH), lambda b, s: (b, 0, 0)),
        pl.BlockSpec((None, WINDOW, KV_WIDTH), lambda b, s: (b, 0, 0)),
        pl.BlockSpec((None, HIST_ROWS, POOL_WIDTH), lambda b, s: (b, 0, 0)),
        full(cnt_in),
    ] + [full(c) for c in consts]
    out_shape = [
        jax.ShapeDtypeStruct((nb, seq, D_MODEL), f32),
        jax.ShapeDtypeStruct((nb * n_tiles * n_valid, D_MODEL // 2), jnp.uint32),
        jax.ShapeDtypeStruct((nb, TOP_K, seq), jnp.int32),
        jax.ShapeDtypeStruct((nb, TOP_K, seq), jnp.int32),
        jax.ShapeDtypeStruct((nb * n_tiles * n_valid, GATE_COLS), f32),
        jax.ShapeDtypeStruct((N_EXPERTS, 128), f32),
        jax.ShapeDtypeStruct((nb, WINDOW, KV_WIDTH), f32),
        jax.ShapeDtypeStruct((nb, WINDOW, KV_WIDTH), f32),
        jax.ShapeDtypeStruct((nb, HIST_ROWS, POOL_WIDTH), f32),
    ]
    out_specs = [
        pl.BlockSpec((None, tile, D_MODEL), lambda b, s: (b, s, 0)),
        pl.BlockSpec((n_valid, D_MODEL // 2), lambda b, s: (b * n_tiles + s, 0)),
        pl.BlockSpec((None, TOP_K, tile), lambda b, s: (b, 0, s)),
        pl.BlockSpec((None, TOP_K, tile), lambda b, s: (b, 0, s)),
        pl.BlockSpec((n_valid, GATE_COLS), lambda b, s: (b * n_tiles + s, 0)),
        pl.BlockSpec((N_EXPERTS, 128), lambda b, s: (0, 0)),
        pl.BlockSpec((None, WINDOW, KV_WIDTH), lambda b, s: (b, 0, 0)),
        pl.BlockSpec((None, WINDOW, KV_WIDTH), lambda b, s: (b, 0, 0)),
        pl.BlockSpec((None, HIST_ROWS, POOL_WIDTH), lambda b, s: (b, 0, 0)),
    ]
    scratch = [
        pltpu.VMEM((ATTN_WIDTH, tile), jnp.bfloat16),
        pltpu.VMEM((WINDOW + tile, KV_WIDTH), jnp.bfloat16),
        pltpu.VMEM((KV_WIDTH, WINDOW + tile), jnp.bfloat16),
        pltpu.VMEM((HIST_ROWS + tile, POOL_WIDTH), f32),
        pltpu.VMEM((tile, D_MODEL), jnp.bfloat16),
        pltpu.VMEM((N_EXPERTS, 128), f32),
    ]
    kern = functools.partial(_mixer_kernel, tile=tile, sub=sub, n_valid=n_valid, pos0=pos0,
                             mask_first=mask_first)
    return pl.pallas_call(
        kern,
        grid=(nb, n_tiles),
        in_specs=in_specs,
        out_specs=out_specs,
        out_shape=out_shape,
        scratch_shapes=scratch,
        compiler_params=pltpu.CompilerParams(
            dimension_semantics=("arbitrary", "arbitrary"),
            vmem_limit_bytes=VMEM_LIMIT),
        name="mixer",
    )(x, k_hist, v_hist, u_hist, cnt_in, *consts)


def _expert_kernel_f32(be_ref, nv_ref, nx_ref, lv_ref, xs_ref, wg_hbm, bg_ref, wu_hbm, bu_ref,
                       wd_hbm, bd_ref, ys_ref, wg16_hbm, wu16_hbm, wd16_hbm,
                       stage_s, wg_s, wu_s, wd_s, in_sems, out_sems, *, n_blocks):
    i = pl.program_id(0)
    expert = be_ref[i]
    bf16 = jnp.bfloat16

    def copies_in(e):
        return [pltpu.make_async_copy(w_hbm.at[e], stage_s.at[k], in_sems.at[k])
                for k, w_hbm in enumerate((wg_hbm, wu_hbm, wd_hbm))]

    def copies_out(e):
        return [pltpu.make_async_copy(w_s, w16_hbm.at[e], out_sems.at[k])
                for k, (w_s, w16_hbm) in enumerate(((wg_s, wg16_hbm), (wu_s, wu16_hbm),
                                                    (wd_s, wd16_hbm)))]

    @pl.when(i == 0)
    def _():
        for copy in copies_in(expert):
            copy.start()

    @pl.when((lv_ref[i] > 0) & ((i == 0) | (expert != be_ref[jnp.maximum(i - 1, 0)])))
    def _():
        for copy in copies_in(expert):
            copy.wait()

        @pl.when(i > 0)
        def _():
            for copy in copies_out(expert):
                copy.wait()

        wg_s[...] = stage_s[0].astype(bf16)
        wu_s[...] = stage_s[1].astype(bf16)
        wd_s[...] = stage_s[2].astype(bf16)
        for copy in copies_out(expert):
            copy.start()

        @pl.when(nx_ref[i] != expert)
        def _():
            for copy in copies_in(nx_ref[i]):
                copy.start()

    _ffn_block(nv_ref[i], expert, xs_ref, ys_ref, wg_s, wu_s, wd_s, bg_ref, bu_ref, bd_ref)

    @pl.when(i == n_blocks - 1)
    def _():
        for copy in copies_out(expert):
            copy.wait()


def _expert_kernel_bf16(be_ref, nv_ref, nx_ref, sl_ref, xs_ref, wg_hbm, bg_ref, wu_hbm, bu_ref,
                        wd_hbm, bd_ref, ys_ref, w_s, sems):
    i = pl.program_id(0)
    n_rows = nv_ref[i]
    expert = be_ref[i]
    slot = sl_ref[i]

    def copies(e, to_slot):
        return [pltpu.make_async_copy(w_hbm.at[e], w_s.at[to_slot, k], sems.at[to_slot, k])
                for k, w_hbm in enumerate((wg_hbm, wu_hbm, wd_hbm))]

    @pl.when((i == 0) & (n_rows > 0))
    def _():
        for copy in copies(expert, slot):
            copy.start()

    @pl.when((n_rows > 0) & ((i == 0) | (expert != be_ref[jnp.maximum(i - 1, 0)])))
    def _():
        for copy in copies(expert, slot):
            copy.wait()

        @pl.when(nx_ref[i] != expert)
        def _():
            for copy in copies(nx_ref[i], 1 - slot):
                copy.start()

    _ffn_block(n_rows, expert, xs_ref, ys_ref, w_s.at[slot, 0], w_s.at[slot, 1], w_s.at[slot, 2],
               bg_ref, bu_ref, bd_ref)


def _ffn_block(n_rows, expert, xs_ref, ys_ref, wg_s, wu_s, wd_s, bg_all_ref, bu_all_ref, bd_all_ref):
    bf16 = jnp.bfloat16

    def ffn(rows):
        f32 = jnp.float32
        half = D_MODEL // 2
        bg_ref, bu_ref, bd_ref = (b.at[pl.ds(expert, 1), :]
                                  for b in (bg_all_ref, bu_all_ref, bd_all_ref))
        for r0 in range(0, rows, FFN_CHUNK):
            row = r0 + lax.broadcasted_iota(jnp.int32, (FFN_CHUNK, 1), 0)
            words = jnp.where(row < n_rows, xs_ref[r0:r0 + FFN_CHUNK, :],
                              jnp.uint32(0))
            xa, xb = _unpack_bf16_pair(words)
            a = (jnp.dot(xa, wg_s[0:half, :], preferred_element_type=f32)
                 + jnp.dot(xb, wg_s[half:, :], preferred_element_type=f32) + bg_ref[...])
            bb = (jnp.dot(xa, wu_s[0:half, :], preferred_element_type=f32)
                  + jnp.dot(xb, wu_s[half:, :], preferred_element_type=f32) + bu_ref[...])
            a = jnp.minimum(a, SWIGLU_LIMIT)
            bb = jnp.clip(bb, -SWIGLU_LIMIT, SWIGLU_LIMIT)
            act = a * (1.0 / (1.0 + jnp.exp(-SWIGLU_ALPHA * a))) * (bb + 1.0)
            y = jnp.dot(act.astype(bf16), wd_s[...], preferred_element_type=f32) + bd_ref[...]
            ys_ref[r0:r0 + FFN_CHUNK, :] = _pack_bf16_pair(y[:, 0:half], y[:, half:])
        if rows < EXPERT_BLOCK:
            ys_ref[rows:, :] = jnp.zeros((EXPERT_BLOCK - rows, half), jnp.uint32)

    quarter = EXPERT_BLOCK // 4
    for q in range(1, 5):
        @pl.when((n_rows > (q - 1) * quarter) & (n_rows <= q * quarter))
        def _(q=q):
            ffn(q * quarter)

    @pl.when(n_rows == 0)
    def _():
        ys_ref[...] = jnp.zeros_like(ys_ref)


def _expert_call_f32(block_e, block_rows, block_next, block_live, xs, wg, bg, wu, bu, wd, bd):
    n_slots = xs.shape[0]
    n_blocks = n_slots // EXPERT_BLOCK
    any_spec = pl.BlockSpec(memory_space=pl.ANY)
    b_spec = pl.BlockSpec((N_EXPERTS, D_MODEL), lambda i, be, nv, nx, lv: (0, 0))
    x_spec = pl.BlockSpec((EXPERT_BLOCK, D_MODEL // 2), lambda i, be, nv, nx, lv: (i, 0))
    grid_spec = pltpu.PrefetchScalarGridSpec(
        num_scalar_prefetch=4,
        grid=(n_blocks,),
        in_specs=[x_spec, any_spec, b_spec, any_spec, b_spec, any_spec, b_spec],
        out_specs=[x_spec, any_spec, any_spec, any_spec],
        scratch_shapes=[pltpu.VMEM((3, D_MODEL, D_MODEL), jnp.float32)]
        + [pltpu.VMEM((D_MODEL, D_MODEL), jnp.bfloat16)] * 3
        + [pltpu.SemaphoreType.DMA((3,)), pltpu.SemaphoreType.DMA((3,))],
    )
    w16 = jax.ShapeDtypeStruct((N_EXPERTS, D_MODEL, D_MODEL), jnp.bfloat16)
    return pl.pallas_call(
        functools.partial(_expert_kernel_f32, n_blocks=n_blocks),
        grid_spec=grid_spec,
        out_shape=[jax.ShapeDtypeStruct((n_slots, D_MODEL // 2), jnp.uint32), w16, w16, w16],
        compiler_params=pltpu.CompilerParams(
            dimension_semantics=("arbitrary",),
            vmem_limit_bytes=VMEM_LIMIT),
        name="experts_f32",
    )(block_e, block_rows, block_next, block_live, xs, wg, bg, wu, bu, wd, bd)


def _expert_call_bf16(block_e, block_rows, block_next, block_slot, xs, wg, bg, wu, bu, wd, bd):
    n_slots = xs.shape[0]
    any_spec = pl.BlockSpec(memory_space=pl.ANY)
    b_spec = pl.BlockSpec((N_EXPERTS, D_MODEL), lambda i, be, nv, nx, sl: (0, 0))
    x_spec = pl.BlockSpec((EXPERT_BLOCK, D_MODEL // 2), lambda i, be, nv, nx, sl: (i, 0))
    grid_spec = pltpu.PrefetchScalarGridSpec(
        num_scalar_prefetch=4,
        grid=(n_slots // EXPERT_BLOCK,),
        in_specs=[x_spec, any_spec, b_spec, any_spec, b_spec, any_spec, b_spec],
        out_specs=x_spec,
        scratch_shapes=[pltpu.VMEM((2, 3, D_MODEL, D_MODEL), jnp.bfloat16),
                        pltpu.SemaphoreType.DMA((2, 3))],
    )
    return pl.pallas_call(
        _expert_kernel_bf16,
        grid_spec=grid_spec,
        out_shape=jax.ShapeDtypeStruct((n_slots, D_MODEL // 2), jnp.uint32),
        compiler_params=pltpu.CompilerParams(
            dimension_semantics=("arbitrary",),
            vmem_limit_bytes=VMEM_LIMIT),
        name="experts_bf16",
    )(block_e, block_rows, block_next, block_slot, xs, wg, bg, wu, bu, wd, bd)


def _combine_kernel(out_buf_ref, x1_ref, g_ref, y0_ref, y1_ref, y2_ref, y3_ref, o_ref):
    del out_buf_ref
    g = g_ref[...]
    half = D_MODEL // 2
    lo, hi = x1_ref[:, 0:half], x1_ref[:, half:]
    for j, y_ref in enumerate((y0_ref, y1_ref, y2_ref, y3_ref)):
        w = y_ref[...]
        gj = g[:, j:j + 1]
        lo = lo + gj * lax.bitcast_convert_type(w << 16, jnp.float32)
        hi = hi + gj * lax.bitcast_convert_type(w & jnp.uint32(0xFFFF0000), jnp.float32)
    o_ref[:, 0:half] = lo
    o_ref[:, half:] = hi


def _combine_call(out_buf, x1, gates, picked, stride, tok0, *, out_rows, row0, tile):
    n = x1.shape[0]
    aliased = out_buf.shape == (out_rows, D_MODEL)
    assert n % tile == 0 and row0 % tile == 0
    y_specs = []
    for j in range(TOP_K):
        assert (j * stride + tok0) % tile == 0
        base = (j * stride + tok0) // tile
        y_specs.append(pl.BlockSpec((tile, D_MODEL // 2), lambda i, _b=base: (_b + i, 0)))
    return pl.pallas_call(
        _combine_kernel,
        grid=(n // tile,),
        in_specs=[pl.BlockSpec(memory_space=pl.ANY),
                  pl.BlockSpec((tile, D_MODEL), lambda i: (i, 0)),
                  pl.BlockSpec((tile, GATE_COLS), lambda i: (i, 0))] + y_specs,
        out_specs=pl.BlockSpec((tile, D_MODEL), lambda i: (row0 // tile + i, 0)),
        out_shape=jax.ShapeDtypeStruct((out_rows, D_MODEL), jnp.float32),
        input_output_aliases={0: 0} if aliased else {},
        compiler_params=pltpu.CompilerParams(dimension_semantics=("arbitrary",),
                                             vmem_limit_bytes=VMEM_LIMIT),
        name="combine",
    )(out_buf, x1, gates, picked, picked, picked, picked)


def _gather_rows(table, idx):
    n = idx.shape[0]
    width = table.shape[1]
    per_worker = n // SC_WORKERS
    n_chunks = per_worker // SC_CHUNK
    mesh = plsc.VectorSubcoreMesh(core_axis_name="c", subcore_axis_name="s")

    @functools.partial(
        pl.kernel, mesh=mesh,
        out_type=jax.ShapeDtypeStruct((n, width), table.dtype),
        scratch_types=[pltpu.VMEM((SC_CHUNK,), jnp.int32),
                       pltpu.VMEM((SC_CHUNK, width), table.dtype),
                       pltpu.SemaphoreType.DMA],
        cost_estimate=pl.CostEstimate(flops=0, transcendentals=0, bytes_accessed=8 * n * width),
    )
    def gather(table_hbm, idx_hbm, out_hbm, idx_v, rows_v, sem):
        wid = lax.axis_index("s") * 2 + lax.axis_index("c")
        base = wid * per_worker

        @pl.loop(0, n_chunks)
        def _(i):
            off = base + i * SC_CHUNK
            pltpu.sync_copy(idx_hbm.at[pl.ds(off, SC_CHUNK)], idx_v)
            pltpu.async_copy(table_hbm.at[idx_v], rows_v, sem).wait()
            pltpu.sync_copy(rows_v, out_hbm.at[pl.ds(off, SC_CHUNK)])

    return gather(table, idx)


def _scatter_rows(srcs, dest, n_out):
    width = srcs[0].shape[1]
    starts = [0]
    for src in srcs:
        assert src.shape[0] % SC_CHUNK == 0
        starts.append(starts[-1] + src.shape[0] // SC_CHUNK)
    n_chunks = starts[-1]
    n = n_chunks * SC_CHUNK
    per_worker = -(-n_chunks // SC_WORKERS)
    mesh = plsc.VectorSubcoreMesh(core_axis_name="c", subcore_axis_name="s")

    @functools.partial(
        pl.kernel, mesh=mesh,
        out_type=jax.ShapeDtypeStruct((n_out, width), srcs[0].dtype),
        scratch_types=[pltpu.VMEM((SC_CHUNK,), jnp.int32)] * TOP_K
        + [pltpu.VMEM((SC_CHUNK, width), srcs[0].dtype)],
        cost_estimate=pl.CostEstimate(flops=0, transcendentals=0,
                                      bytes_accessed=4 * (1 + TOP_K) * n * width),
    )
    def scatter(*refs):
        src_hbms = refs[:len(srcs)]
        dest_hbm, out_hbm = refs[len(srcs)], refs[len(srcs) + 1]
        idx_vs, rows_v = refs[len(srcs) + 2:len(srcs) + 2 + TOP_K], refs[-1]
        wid = lax.axis_index("s") * 2 + lax.axis_index("c")

        @pl.loop(0, per_worker)
        def _(i):
            c = i * SC_WORKERS + wid
            for k, src_hbm in enumerate(src_hbms):

                @pl.when((c >= starts[k]) & (c < starts[k + 1]))
                def _():
                    pltpu.sync_copy(src_hbm.at[pl.ds((c - starts[k]) * SC_CHUNK, SC_CHUNK)], rows_v)
                    for j, idx_v in enumerate(idx_vs):
                        pltpu.sync_copy(dest_hbm.at[pl.ds(j * n + c * SC_CHUNK, SC_CHUNK)], idx_v)
                    for idx_v in idx_vs:
                        pltpu.sync_copy(rows_v, out_hbm.at[idx_v])

    return scatter(*srcs, dest)


def _t5_bucket_np(rel):
    half = NUM_BUCKETS // 2
    max_exact = half // 2
    n = np.abs(rel)
    nf = np.maximum(n, 1).astype(np.float32)
    large = max_exact + (np.log(nf / max_exact) / math.log(MAX_DISTANCE / max_exact)
                         * (half - max_exact)).astype(np.int32)
    large = np.minimum(large, half - 1)
    return np.where(rel > 0, half, 0) + np.where(n < max_exact, n, large)


def _bias_tables(rel_bias, visible):
    kap = np.arange(BAND)[:, None]
    col = np.arange(GROUP * PAIR)[None, :]
    bucket = _t5_bucket_np(kap - WINDOW - col % PAIR)
    onehot = jnp.asarray(bucket[:, :, None] == np.arange(NUM_BUCKETS))
    seen = jnp.asarray(np.tile(visible, (1, GROUP)))
    rb = rel_bias.astype(jnp.float32)
    tabs = []
    for h in range(N_KV_HEADS):
        per_col = jnp.repeat(rb[:, h * GROUP:(h + 1) * GROUP], PAIR, axis=1)
        tab = jnp.sum(jnp.where(onehot, jnp.transpose(per_col)[None], 0.0), axis=-1)
        tabs.append(jnp.where(seen, tab, NEG_INF))
    return jnp.stack(tabs)


def _mixer_consts(l, norm1_g, w_in, q_norm_g, k_norm_g, rel_bias, sinks, w_pool, pool_scale,
                  w_out, norm2_g, w_router, b_router, visible, tile):
    f32, bf16 = jnp.float32, jnp.bfloat16
    q_off, k_off, v_off = POOL_WIDTH, POOL_WIDTH + ATTN_WIDTH, POOL_WIDTH + ATTN_WIDTH + KV_WIDTH
    w = w_in[l]
    w_ukv = jnp.concatenate([w[:, :q_off], w[:, k_off:]], axis=1).astype(bf16)
    w_qvt = jnp.transpose(jnp.concatenate([w[:, q_off:k_off], w[:, v_off:]], axis=1)).astype(bf16)
    lane_head = np.arange(KV_WIDTH) // HEAD_DIM
    blockdiag = jnp.asarray(lane_head[:, None] == lane_head[None, :], bf16)
    sink_rows = jnp.repeat(sinks[l].astype(f32).reshape(N_KV_HEADS, GROUP), PAIR, axis=1)
    wr = w_router[l].astype(f32)
    wr_hi = wr.astype(bf16)
    wr_lo = (wr - wr_hi.astype(f32)).astype(bf16)
    wr_parts = jnp.pad(jnp.concatenate([wr_hi, wr_lo], axis=1), ((0, 0), (0, 128 - 2 * N_EXPERTS)))
    tri = jnp.asarray(np.arange(tile)[:, None] < np.arange(tile)[None, :], bf16)
    return [
        norm1_g[l].reshape(1, D_MODEL).astype(f32), w_ukv, w_qvt,
        q_norm_g[l].reshape(HEAD_DIM, 1).astype(f32),
        jnp.tile(k_norm_g[l].astype(f32), N_KV_HEADS).reshape(1, KV_WIDTH),
        blockdiag, _bias_tables(rel_bias, visible), sink_rows,
        w_pool[l].astype(bf16), pool_scale[l].reshape(1, POOL_WIDTH).astype(f32),
        w_out[l].astype(bf16), norm2_g[l].reshape(1, D_MODEL).astype(f32),
        wr_parts, b_router[l].reshape(N_EXPERTS, 1).astype(f32), tri,
    ]


def _visibility():
    kap = np.arange(BAND)[:, None]
    rho = np.arange(PAIR)[None, :]
    kc, qc = kap // CHUNK, rho // CHUNK
    prompt = (kc >= qc) & (kc <= qc + WINDOW // CHUNK)
    return prompt


def kernel(x_prompt, x_sample, cache_k, cache_v, state_pool, norm1_g, w_in, q_norm_g, k_norm_g,
           rel_bias, sinks, w_pool, pool_scale, w_out, norm2_g, w_router, b_router,
           w_gate, b_gate, w_up, b_up, w_down, b_down):
    f32, bf16 = jnp.float32, jnp.bfloat16
    depth = w_in.shape[0]
    nb, seq, _ = x_prompt.shape
    ndb, dseq, _ = x_sample.shape
    cache_len = cache_k.shape[2]
    assert seq % MIX_TILE == 0 and cache_len == WINDOW and HIST_ROWS <= dseq <= PAIR
    n_p, n_s = nb * seq, ndb * dseq
    n_tok = n_p + n_s
    assert COMBINE_TILE % n_s == 0 and n_s % SC_CHUNK == 0

    vis_prompt = _visibility()
    vis_sample = np.broadcast_to(np.arange(BAND)[:, None] < WINDOW + dseq, (BAND, PAIR))

    nb1 = nb // 4
    nb2 = nb - nb1
    assert nb1 > 0 and nb2 > 0 and seq % COMBINE_TILE == 0

    xp, xs = x_prompt, x_sample
    outs = [[] for _ in range(6)]
    for l in range(depth):
        wl = (l, norm1_g, w_in, q_norm_g, k_norm_g, rel_bias, sinks, w_pool, pool_scale, w_out,
              norm2_g, w_router, b_router)
        consts_p = _mixer_consts(*wl, vis_prompt, MIX_SUB)
        moe_b = [b.astype(f32) for b in (b_gate[l], b_up[l], b_down[l])]
        cnt0 = jnp.zeros((N_EXPERTS, 128), f32)

        def prompt_mixer(stream0, n_streams, cnt_in):
            zk = jnp.zeros((n_streams, WINDOW, KV_WIDTH), f32)
            zu = jnp.zeros((n_streams, HIST_ROWS, POOL_WIDTH), f32)
            return _mixer_call(xp, zk, zk, zu, cnt_in, consts_p, stream0=stream0, tile=MIX_TILE,
                               sub=MIX_SUB, n_valid=MIX_TILE, pos0=0, mask_first=True)

        (x1_a, h_a, idx_a, rank_a, gate_a, cnt_a, k_a, v_a, u_a) = prompt_mixer(0, nb1, cnt0)
        xs_pad = jnp.pad(xs, ((0, 0), (0, PAIR - dseq), (0, 0)))
        uh = jnp.pad(state_pool[l], ((0, 0), (HIST_ROWS - POOL_HIST, 0), (0, 0)))
        (xs1, h_s, idx_s, rank_s, gate_s, cnt_s, k_s, v_s, u_s) = _mixer_call(
            xs_pad, cache_k[l].reshape(ndb, WINDOW, KV_WIDTH),
            cache_v[l].reshape(ndb, WINDOW, KV_WIDTH), uh, cnt0,
            _mixer_consts(*wl, vis_sample, PAIR),
            stream0=0, tile=PAIR, sub=PAIR, n_valid=dseq, pos0=PAST_LEN, mask_first=False)
        (x1_b, h_b, idx_b, rank_b, gate_b, cnt_b, k_b, v_b, u_b) = prompt_mixer(nb1, nb2, cnt_s)
        picked_b, w_bf16 = _moe_rows([h_b, h_s], [idx_b, idx_s], [rank_b, rank_s], [seq, dseq], cnt_b,
                                     moe_b, w_f32=(w_gate[l], w_up[l], w_down[l]))
        picked_a, _ = _moe_rows([h_a], [idx_a], [rank_a], [seq], cnt_a, moe_b, w_bf16=w_bf16)

        n_a, n_b = nb1 * seq, nb2 * seq
        no_buf = jnp.zeros((8, 128), f32)
        stride_a, stride_b = _pick_stride(n_a), _pick_stride(n_b + n_s)
        xp_rows = _combine_call(no_buf, x1_b.reshape(n_b, D_MODEL), gate_b, picked_b, stride_b, 0,
                                out_rows=n_p, row0=n_a, tile=COMBINE_TILE)
        xp_rows = _combine_call(xp_rows, x1_a.reshape(n_a, D_MODEL), gate_a, picked_a, stride_a, 0,
                                out_rows=n_p, row0=0, tile=COMBINE_TILE)
        xp = xp_rows.reshape(nb, seq, D_MODEL)
        xs = _combine_call(no_buf, xs1[:, :dseq].reshape(n_s, D_MODEL), gate_s, picked_b, stride_b,
                           n_b, out_rows=n_s, row0=0, tile=n_s).reshape(ndb, dseq, D_MODEL)

        outs[0].append(jnp.concatenate([k_a, k_b]).reshape(nb, WINDOW, N_KV_HEADS, HEAD_DIM))
        outs[1].append(jnp.concatenate([v_a, v_b]).reshape(nb, WINDOW, N_KV_HEADS, HEAD_DIM))
        outs[2].append(jnp.concatenate([u_a, u_b])[:, HIST_ROWS - POOL_HIST:])
        outs[3].append(k_s[:, :dseq].reshape(ndb, dseq, N_KV_HEADS, HEAD_DIM))
        outs[4].append(v_s[:, :dseq].reshape(ndb, dseq, N_KV_HEADS, HEAD_DIM))
        outs[5].append(u_s[:, HIST_ROWS - POOL_HIST:])
    return (xp, xs) + tuple(jnp.stack(o) for o in outs)


def _moe_rows(hs, idxs, ranks, n_reals, cnt, biases, w_f32=None, w_bf16=None):
    n_tok = sum(h.shape[0] for h in hs)
    n_assign = n_tok * TOP_K
    gather_quant = SC_WORKERS * SC_CHUNK
    n_blocks = n_assign // EXPERT_BLOCK + N_EXPERTS
    n_blocks = -(-n_blocks // (gather_quant // EXPERT_BLOCK)) * (gather_quant // EXPERT_BLOCK)
    e_ids = jnp.arange(N_EXPERTS, dtype=jnp.int32)

    def per_token(arrs):
        return jnp.concatenate(
            [jnp.transpose(a[:, :, :n], (1, 0, 2)).reshape(TOP_K, -1) for a, n in zip(arrs, n_reals)],
            axis=1)

    counts = cnt[:, 0].astype(jnp.int32)
    pcounts = (counts + EXPERT_BLOCK - 1) // EXPERT_BLOCK * EXPERT_BLOCK
    if w_bf16 is None:
        pcounts = jnp.maximum(pcounts, EXPERT_BLOCK)
    pend = jnp.cumsum(pcounts)
    pstart = pend - pcounts
    idx_all = per_token(idxs)
    dest = per_token(ranks) + jnp.sum(
        jnp.where(idx_all[None] == e_ids[:, None, None], pstart[:, None, None], 0), axis=0)
    dest = dest.reshape(-1)
    blk0 = jnp.arange(n_blocks, dtype=jnp.int32) * EXPERT_BLOCK
    block_e = jnp.minimum(jnp.sum((pend[None, :] <= blk0[:, None]).astype(jnp.int32), axis=1),
                          N_EXPERTS - 1)
    block_rows = jnp.clip(jnp.sum(jnp.where(block_e[:, None] == e_ids[None, :],
                                             (pstart + counts)[None, :], 0), axis=1) - blk0,
                          0, EXPERT_BLOCK).astype(jnp.int32)

    x_sorted = _scatter_rows(hs, dest, n_blocks * EXPERT_BLOCK)
    if w_bf16 is None:
        block_next = jnp.minimum(block_e + 1, N_EXPERTS - 1)
        block_live = (blk0 < pend[-1]).astype(jnp.int32)
        y_sorted, *w_bf16 = _expert_call_f32(
            block_e, block_rows, block_next, block_live, x_sorted,
            w_f32[0], biases[0], w_f32[1], biases[1], w_f32[2], biases[2])
    else:
        has = counts > 0
        later = has[None, :] & (e_ids[None, :] > e_ids[:, None])
        next_e = jnp.min(jnp.where(later, e_ids[None, :], N_EXPERTS), axis=1)
        next_e = jnp.where(next_e == N_EXPERTS, e_ids, next_e)
        order = jnp.cumsum(has.astype(jnp.int32)) - 1
        table = jnp.stack([next_e, order % 2], axis=0)[:, None, :]
        block_next, block_slot = jnp.sum(
            jnp.where(block_e[None, :, None] == e_ids[None, None, :], table, 0), axis=2).astype(jnp.int32)
        y_sorted = _expert_call_bf16(
            block_e, block_rows, block_next, block_slot, x_sorted,
            w_bf16[0], biases[0], w_bf16[1], biases[1], w_bf16[2], biases[2])
    stride = _pick_stride(n_tok)
    n_pick = -(-(TOP_K * stride) // gather_quant) * gather_quant
    picks = jnp.pad(dest.reshape(TOP_K, n_tok), ((0, 0), (0, stride - n_tok))).reshape(-1)
    return _gather_rows(y_sorted, jnp.pad(picks, (0, n_pick - TOP_K * stride))), w_bf16


def _pick_stride(n_tok):
    return -(-n_tok // COMBINE_TILE) * COMBINE_TILE
```

```python
import functools
import math

import numpy as np
import jax
import jax.numpy as jnp
from jax import lax
from jax.experimental import pallas as pl
from jax.experimental.pallas import tpu as pltpu
from jax.experimental.pallas import tpu_sc as plsc

D_MODEL = 1024
CHUNK = 64
POOL_WIDTH = 512
POOL_WINDOWS = (2, 4, 8, 16)
POOL_GROUP = 128
N_POOL_GROUPS = 4
POOL_HIST = 15
ATTN_WIDTH = 512
HEAD_DIM = 64
N_HEADS = 8
N_KV_HEADS = 2
GROUP = 4
KV_WIDTH = 128
WINDOW = 128
NUM_BUCKETS = 32
MAX_DISTANCE = 128
PAST_LEN = 2048
N_EXPERTS = 32
TOP_K = 4
SWIGLU_LIMIT = 7.0
SWIGLU_ALPHA = 1.702
EPS = 1e-5
NEG_INF = -1e30
ATTN_SCALE = HEAD_DIM ** -0.5

PAIR = 2 * CHUNK
BAND = PAIR + WINDOW
HIST_ROWS = 16
GATE_COLS = 8
MIX_TILE = 1024
MIX_SUB = 1024
EXPERT_BLOCK = 1024
FFN_CHUNK = 256
COMBINE_TILE = 1024
SC_WORKERS = 32
SC_CHUNK = 64
VMEM_LIMIT = 56 * 1024 * 1024


def _pack_bf16_pair(a, b):
    ab = lax.bitcast_convert_type(a.astype(jnp.bfloat16).astype(jnp.float32), jnp.uint32)
    bb = lax.bitcast_convert_type(b.astype(jnp.bfloat16).astype(jnp.float32), jnp.uint32)
    return (ab >> 16) | (bb & jnp.uint32(0xFFFF0000))


def _unpack_bf16_pair(w):
    a = lax.bitcast_convert_type(w << 16, jnp.float32).astype(jnp.bfloat16)
    b = lax.bitcast_convert_type(w & jnp.uint32(0xFFFF0000), jnp.float32).astype(jnp.bfloat16)
    return a, b


def _mixer_kernel(x_ref, kh_ref, vh_ref, uh_ref, cnt_in_ref,
                  g1_ref, wukv_ref, wqvt_ref, qg_ref, kg_ref, bd_ref, bias_ref, sink_ref,
                  wpool_ref, pscale_ref, wout_ref, g2_ref, wr_ref, br_ref, tri_ref,
                  x1_ref, hp_ref, idx_ref, rank_ref, gate_ref, cnt_ref, ko_ref, vo_ref, uo_ref,
                  qt_s, kb_s, vt_s, ub_s, mix_s, cnt_s,
                  *, tile, sub, n_valid, pos0, mask_first):
    b = pl.program_id(0)
    s = pl.program_id(1)
    bf16 = jnp.bfloat16
    f32 = jnp.float32

    @pl.when((b == 0) & (s == 0))
    def _():
        cnt_s[...] = cnt_in_ref[...]

    @pl.when(s == 0)
    def _():
        kb_s[0:WINDOW, :] = kh_ref[...].astype(bf16)
        vt_s[:, 0:WINDOW] = jnp.transpose(vh_ref[...]).astype(bf16)
        ub_s[0:HIST_ROWS, :] = uh_ref[...]

    @pl.when(s > 0)
    def _():
        kb_s[0:WINDOW, :] = kb_s[tile:tile + WINDOW, :]
        vt_s[:, 0:WINDOW] = vt_s[:, tile:tile + WINDOW]
        ub_s[0:HIST_ROWS, :] = ub_s[tile:tile + HIST_ROWS, :]

    for r0 in range(0, tile, sub):
        _mixer_rows(r0, s, x_ref, g1_ref, wukv_ref, wqvt_ref, qg_ref, kg_ref, bd_ref, bias_ref,
                    sink_ref, wpool_ref, pscale_ref, wout_ref, g2_ref, wr_ref, br_ref, tri_ref,
                    x1_ref, hp_ref, idx_ref, rank_ref, gate_ref, cnt_ref, ko_ref, vo_ref, uo_ref,
                    qt_s, kb_s, vt_s, ub_s, mix_s, cnt_s,
                    tile=tile, sub=sub, n_valid=n_valid, pos0=pos0, mask_first=mask_first)


def _mixer_rows(r0, s, x_ref, g1_ref, wukv_ref, wqvt_ref, qg_ref, kg_ref, bd_ref, bias_ref,
                sink_ref, wpool_ref, pscale_ref, wout_ref, g2_ref, wr_ref, br_ref, tri_ref,
                x1_ref, hp_ref, idx_ref, rank_ref, gate_ref, cnt_ref, ko_ref, vo_ref, uo_ref,
                qt_s, kb_s, vt_s, ub_s, mix_s, cnt_s, *, tile, sub, n_valid, pos0, mask_first):
    bf16 = jnp.bfloat16
    f32 = jnp.float32
    rows = slice(r0, r0 + sub)

    x = x_ref[rows, :]
    xn = (x * lax.rsqrt(jnp.mean(x * x, axis=-1, keepdims=True) + EPS) * g1_ref[...]).astype(bf16)
    z = jnp.dot(xn, wukv_ref[...], preferred_element_type=f32)
    zt = lax.dot_general(wqvt_ref[...], xn, (((1,), (1,)), ((), ())),
                         preferred_element_type=f32)
    u = z[:, 0:POOL_WIDTH]
    kz = z[:, POOL_WIDTH:POOL_WIDTH + KV_WIDTH]
    v = z[:, POOL_WIDTH + KV_WIDTH:]

    ksq = kz * kz
    kss = jnp.dot(ksq.astype(bf16), bd_ref[...], preferred_element_type=f32)
    kn = kz * lax.rsqrt(kss * (1.0 / HEAD_DIM) + EPS) * kg_ref[...]
    kb_s[WINDOW + r0:WINDOW + r0 + sub, :] = kn.astype(bf16)
    vt_s[:, WINDOW + r0:WINDOW + r0 + sub] = zt[ATTN_WIDTH:, :].astype(bf16)
    ub_s[HIST_ROWS + r0:HIST_ROWS + r0 + sub, :] = u

    if r0 + sub == tile:
        row0 = max(n_valid, WINDOW) - WINDOW - r0
        ko_ref[...] = kn[row0:row0 + WINDOW, :]
        vo_ref[...] = v[row0:row0 + WINDOW, :]
        uo_ref[...] = u[n_valid - HIST_ROWS - r0:n_valid - r0, :]

    for hd in range(N_HEADS):
        qh = zt[hd * HEAD_DIM:(hd + 1) * HEAD_DIM, :]
        ss = jnp.sum(qh * qh, axis=0, keepdims=True)
        qn = qh * (lax.rsqrt(ss * (1.0 / HEAD_DIM) + EPS) * ATTN_SCALE) * qg_ref[...]
        qt_s[hd * HEAD_DIM:(hd + 1) * HEAD_DIM, rows] = qn.astype(bf16)

    pos = pos0 + s * tile + r0 + lax.broadcasted_iota(jnp.int32, (sub, 1), 0)
    diffs = []
    for g, w in enumerate(POOL_WINDOWS):
        e = ub_s[r0:r0 + HIST_ROWS + sub, g * POOL_GROUP:(g + 1) * POOL_GROUP]
        acc = e
        for lvl in range(g + 1):
            acc = acc + pltpu.roll(acc, 2 ** lvl, axis=0)
        inv_cnt = 1.0 / jnp.minimum(pos + 1, w).astype(f32)
        diffs.append((acc[HIST_ROWS:, :] * inv_cnt - e[HIST_ROWS:, :]).astype(bf16))
    for g2 in range(N_POOL_GROUPS // 2):
        cols = slice(2 * g2 * POOL_GROUP, 2 * (g2 + 1) * POOL_GROUP)
        d = jnp.concatenate(diffs[2 * g2:2 * g2 + 2], axis=1)
        y = jnp.dot(d, wpool_ref[g2], preferred_element_type=f32) * pscale_ref[:, cols]
        mix_s[rows, cols] = y.astype(bf16)

    zeros_q = jnp.zeros((HEAD_DIM, GROUP * PAIR), bf16)
    for p in range(r0 // PAIR, (r0 + sub) // PAIR):
        k_band = kb_s[p * PAIR:p * PAIR + BAND, :]
        o_parts = []
        rhs_heads = []
        for h in range(N_KV_HEADS):
            qcat = jnp.concatenate(
                [qt_s[(h * GROUP + g) * HEAD_DIM:(h * GROUP + g + 1) * HEAD_DIM,
                      p * PAIR:(p + 1) * PAIR] for g in range(GROUP)], axis=1)
            rhs_heads.append(jnp.concatenate([qcat, zeros_q] if h == 0 else [zeros_q, qcat], axis=0))
        st_heads = jnp.dot(k_band, jnp.concatenate(rhs_heads, axis=1),
                           preferred_element_type=f32)
        for h in range(N_KV_HEADS):
            st = st_heads[:, h * GROUP * PAIR:(h + 1) * GROUP * PAIR] + bias_ref[h]
            if mask_first and p == 0:
                krow = lax.broadcasted_iota(jnp.int32, (BAND, 1), 0)
                st = jnp.where((krow >= WINDOW) | (s > 0), st, NEG_INF)
            sink = sink_ref[h:h + 1, :]
            m = jnp.maximum(jnp.max(st, axis=0, keepdims=True), sink)
            ex = jnp.exp(st - m)
            den = jnp.sum(ex, axis=0, keepdims=True) + jnp.exp(sink - m)
            v_band = vt_s[h * HEAD_DIM:(h + 1) * HEAD_DIM, p * PAIR:p * PAIR + BAND]
            ot = jnp.dot(v_band, ex.astype(bf16), preferred_element_type=f32) / den
            for g in range(GROUP):
                o_parts.append(ot[:, g * PAIR:(g + 1) * PAIR])
        o_all = jnp.concatenate(o_parts, axis=0)
        mix_s[p * PAIR:(p + 1) * PAIR, POOL_WIDTH:] = jnp.transpose(o_all).astype(bf16)

    x1 = x + jnp.dot(mix_s[rows, :], wout_ref[...], preferred_element_type=f32)
    x1_ref[rows, :] = x1

    n_real = min(sub, n_valid - r0)
    hn = x1 * lax.rsqrt(jnp.mean(x1 * x1, axis=-1, keepdims=True) + EPS) * g2_ref[...]
    hp_ref[r0:r0 + n_real, :] = _pack_bf16_pair(hn[0:n_real, 0:D_MODEL // 2],
                                                hn[0:n_real, D_MODEL // 2:])
    h_hi = hn.astype(bf16)
    h_lo = (hn - h_hi.astype(f32)).astype(bf16)
    both = jnp.dot(jnp.concatenate([h_hi, h_lo], axis=0), wr_ref[...],
                   preferred_element_type=f32)
    parts = jnp.transpose(both[0:sub, :] + both[sub:, :])
    lt = parts[0:N_EXPERTS, :] + parts[N_EXPERTS:2 * N_EXPERTS, :] + br_ref[...]

    eidx = lax.broadcasted_iota(jnp.int32, (N_EXPERTS, sub), 0).astype(f32)
    vals, hots = [], []
    for j in range(TOP_K):
        m = jnp.max(lt, axis=0, keepdims=True)
        sel = jnp.min(jnp.where(lt == m, eidx, float(N_EXPERTS)), axis=0, keepdims=True)
        hot = eidx == sel
        lt = jnp.where(hot, -jnp.inf, lt)
        idx_ref[j:j + 1, rows] = sel.astype(jnp.int32)
        vals.append(m)
        hots.append(hot)
    exps = [jnp.exp(vv - vals[0]) for vv in vals]
    esum = exps[0] + exps[1] + exps[2] + exps[3]
    grow = lax.broadcasted_iota(jnp.int32, (GATE_COLS, sub), 0)
    gmat = jnp.zeros((GATE_COLS, sub), f32)
    for j in range(TOP_K):
        gmat = jnp.where(grow == j, exps[j] / esum, gmat)
    gmat = jnp.concatenate([gmat, jnp.zeros((128 - GATE_COLS, sub), f32)], axis=0)
    gate_ref[r0:r0 + n_real, :] = jnp.transpose(gmat)[0:n_real, 0:GATE_COLS]

    chosen_f = sum(jnp.where(hot, 1.0, 0.0) for hot in hots)
    if n_real < sub:
        lane = lax.broadcasted_iota(jnp.int32, (N_EXPERTS, sub), 1)
        chosen_f = jnp.where(lane < n_real, chosen_f, 0.0)
    n_lane_blocks = sub // 128
    blocks = [chosen_f[:, k * 128:(k + 1) * 128] for k in range(n_lane_blocks)]
    inside = jnp.dot(jnp.concatenate(blocks, axis=0).astype(bf16), tri_ref[...],
                     preferred_element_type=f32)
    offset = cnt_s[:, 0:1]
    pieces = []
    for k in range(n_lane_blocks):
        pieces.append(inside[k * N_EXPERTS:(k + 1) * N_EXPERTS, :] + offset)
        offset = offset + jnp.sum(blocks[k], axis=1, keepdims=True)
    base = jnp.concatenate(pieces, axis=1)
    for j in range(TOP_K):
        rank_ref[j:j + 1, rows] = jnp.sum(jnp.where(hots[j], base, 0.0), axis=0,
                                          keepdims=True).astype(jnp.int32)
    cnt_new = jnp.broadcast_to(offset, (N_EXPERTS, 128))
    cnt_s[...] = cnt_new
    cnt_ref[...] = cnt_new


def _mixer_call(x, k_hist, v_hist, u_hist, cnt_in, consts, *, stream0, tile, sub, n_valid, pos0,
                mask_first):
    seq = x.shape[1]
    nb = k_hist.shape[0]
    n_tiles = seq // tile
    f32 = jnp.float32
    assert n_valid % 8 == 0 and tile % sub == 0 and sub % PAIR == 0
    assert n_valid == tile or sub == tile

    def full(a):
        nd = a.ndim
        return pl.BlockSpec(a.shape, lambda b, s, _nd=nd: (0,) * _nd)

    in_specs = [
        pl.BlockSpec((None, tile, D_MODEL), lambda b, s: (stream0 + b, s, 0)),
        pl.BlockSpec((None, WINDOW, KV_WIDTH), lambda b, s: (b, 0, 0)),
        pl.BlockSpec((None, WINDOW, KV_WIDTH), lambda b, s: (b, 0, 0)),
        pl.BlockSpec((None, HIST_ROWS, POOL_WIDTH), lambda b, s: (b, 0, 0)),
        full(cnt_in),
    ] + [full(c) for c in consts]
    out_shape = [
        jax.ShapeDtypeStruct((nb, seq, D_MODEL), f32),
        jax.ShapeDtypeStruct((nb * n_tiles * n_valid, D_MODEL // 2), jnp.uint32),
        jax.ShapeDtypeStruct((nb, TOP_K, seq), jnp.int32),
        jax.ShapeDtypeStruct((nb, TOP_K, seq), jnp.int32),
        jax.ShapeDtypeStruct((nb * n_tiles * n_valid, GATE_COLS), f32),
        jax.ShapeDtypeStruct((N_EXPERTS, 128), f32),
        jax.ShapeDtypeStruct((nb, WINDOW, KV_WIDTH), f32),
        jax.ShapeDtypeStruct((nb, WINDOW, KV_WIDTH), f32),
        jax.ShapeDtypeStruct((nb, HIST_ROWS, POOL_WIDTH), f32),
    ]
    out_specs = [
        pl.BlockSpec((None, tile, D_MODEL), lambda b, s: (b, s, 0)),
        pl.BlockSpec((n_valid, D_MODEL // 2), lambda b, s: (b * n_tiles + s, 0)),
        pl.BlockSpec((None, TOP_K, tile), lambda b, s: (b, 0, s)),
        pl.BlockSpec((None, TOP_K, tile), lambda b, s: (b, 0, s)),
        pl.BlockSpec((n_valid, GATE_COLS), lambda b, s: (b * n_tiles + s, 0)),
        pl.BlockSpec((N_EXPERTS, 128), lambda b, s: (0, 0)),
        pl.BlockSpec((None, WINDOW, KV_WIDTH), lambda b, s: (b, 0, 0)),
        pl.BlockSpec((None, WINDOW, KV_WIDTH), lambda b, s: (b, 0, 0)),
        pl.BlockSpec((None, HIST_ROWS, POOL_WIDTH), lambda b, s: (b, 0, 0)),
    ]
    scratch = [
        pltpu.VMEM((ATTN_WIDTH, tile), jnp.bfloat16),
        pltpu.VMEM((WINDOW + tile, KV_WIDTH), jnp.bfloat16),
        pltpu.VMEM((KV_WIDTH, WINDOW + tile), jnp.bfloat16),
        pltpu.VMEM((HIST_ROWS + tile, POOL_WIDTH), f32),
        pltpu.VMEM((tile, D_MODEL), jnp.bfloat16),
        pltpu.VMEM((N_EXPERTS, 128), f32),
    ]
    kern = functools.partial(_mixer_kernel, tile=tile, sub=sub, n_valid=n_valid, pos0=pos0,
                             mask_first=mask_first)
    return pl.pallas_call(
        kern,
        grid=(nb, n_tiles),
        in_specs=in_specs,
        out_specs=out_specs,
        out_shape=out_shape,
        scratch_shapes=scratch,
        compiler_params=pltpu.CompilerParams(
            dimension_semantics=("arbitrary", "arbitrary"),
            vmem_limit_bytes=VMEM_LIMIT),
        name="mixer",
    )(x, k_hist, v_hist, u_hist, cnt_in, *consts)


def _expert_kernel_f32(be_ref, nv_ref, nx_ref, lv_ref, xs_ref, wg_hbm, bg_ref, wu_hbm, bu_ref,
                       wd_hbm, bd_ref, ys_ref, wg16_hbm, wu16_hbm, wd16_hbm,
                       stage_s, wg_s, wu_s, wd_s, in_sems, out_sems, *, n_blocks):
    i = pl.program_id(0)
    expert = be_ref[i]
    bf16 = jnp.bfloat16

    def copies_in(e):
        return [pltpu.make_async_copy(w_hbm.at[e], stage_s.at[k], in_sems.at[k])
                for k, w_hbm in enumerate((wg_hbm, wu_hbm, wd_hbm))]

    def copies_out(e):
        return [pltpu.make_async_copy(w_s, w16_hbm.at[e], out_sems.at[k])
                for k, (w_s, w16_hbm) in enumerate(((wg_s, wg16_hbm), (wu_s, wu16_hbm),
                                                    (wd_s, wd16_hbm)))]

    @pl.when(i == 0)
    def _():
        for copy in copies_in(expert):
            copy.start()

    @pl.when((lv_ref[i] > 0) & ((i == 0) | (expert != be_ref[jnp.maximum(i - 1, 0)])))
    def _():
        for copy in copies_in(expert):
            copy.wait()

        @pl.when(i > 0)
        def _():
            for copy in copies_out(expert):
                copy.wait()

        wg_s[...] = stage_s[0].astype(bf16)
        wu_s[...] = stage_s[1].astype(bf16)
        wd_s[...] = stage_s[2].astype(bf16)
        for copy in copies_out(expert):
            copy.start()

        @pl.when(nx_ref[i] != expert)
        def _():
            for copy in copies_in(nx_ref[i]):
                copy.start()

    _ffn_block(nv_ref[i], expert, xs_ref, ys_ref, wg_s, wu_s, wd_s, bg_ref, bu_ref, bd_ref)

    @pl.when(i == n_blocks - 1)
    def _():
        for copy in copies_out(expert):
            copy.wait()


def _expert_kernel_bf16(be_ref, nv_ref, nx_ref, sl_ref, xs_ref, wg_hbm, bg_ref, wu_hbm, bu_ref,
                        wd_hbm, bd_ref, ys_ref, w_s, sems):
    i = pl.program_id(0)
    n_rows = nv_ref[i]
    expert = be_ref[i]
    slot = sl_ref[i]

    def copies(e, to_slot):
        return [pltpu.make_async_copy(w_hbm.at[e], w_s.at[to_slot, k], sems.at[to_slot, k])
                for k, w_hbm in enumerate((wg_hbm, wu_hbm, wd_hbm))]

    @pl.when((i == 0) & (n_rows > 0))
    def _():
        for copy in copies(expert, slot):
            copy.start()

    @pl.when((n_rows > 0) & ((i == 0) | (expert != be_ref[jnp.maximum(i - 1, 0)])))
    def _():
        for copy in copies(expert, slot):
            copy.wait()

        @pl.when(nx_ref[i] != expert)
        def _():
            for copy in copies(nx_ref[i], 1 - slot):
                copy.start()

    _ffn_block(n_rows, expert, xs_ref, ys_ref, w_s.at[slot, 0], w_s.at[slot, 1], w_s.at[slot, 2],
               bg_ref, bu_ref, bd_ref)


def _ffn_block(n_rows, expert, xs_ref, ys_ref, wg_s, wu_s, wd_s, bg_all_ref, bu_all_ref, bd_all_ref):
    bf16 = jnp.bfloat16

    def ffn(rows):
        f32 = jnp.float32
        half = D_MODEL // 2
        bg_ref, bu_ref, bd_ref = (b.at[pl.ds(expert, 1), :]
                                  for b in (bg_all_ref, bu_all_ref, bd_all_ref))
        for r0 in range(0, rows, FFN_CHUNK):
            row = r0 + lax.broadcasted_iota(jnp.int32, (FFN_CHUNK, 1), 0)
            words = jnp.where(row < n_rows, xs_ref[r0:r0 + FFN_CHUNK, :],
                              jnp.uint32(0))
            xa, xb = _unpack_bf16_pair(words)
            a = (jnp.dot(xa, wg_s[0:half, :], preferred_element_type=f32)
                 + jnp.dot(xb, wg_s[half:, :], preferred_element_type=f32) + bg_ref[...])
            bb = (jnp.dot(xa, wu_s[0:half, :], preferred_element_type=f32)
                  + jnp.dot(xb, wu_s[half:, :], preferred_element_type=f32) + bu_ref[...])
            a = jnp.minimum(a, SWIGLU_LIMIT)
            bb = jnp.clip(bb, -SWIGLU_LIMIT, SWIGLU_LIMIT)
            act = a * (1.0 / (1.0 + jnp.exp(-SWIGLU_ALPHA * a))) * (bb + 1.0)
            y = jnp.dot(act.astype(bf16), wd_s[...], preferred_element_type=f32) + bd_ref[...]
            ys_ref[r0:r0 + FFN_CHUNK, :] = _pack_bf16_pair(y[:, 0:half], y[:, half:])
        if rows < EXPERT_BLOCK:
            ys_ref[rows:, :] = jnp.zeros((EXPERT_BLOCK - rows, half), jnp.uint32)

    quarter = EXPERT_BLOCK // 4
    for q in range(1, 5):
        @pl.when((n_rows > (q - 1) * quarter) & (n_rows <= q * quarter))
        def _(q=q):
            ffn(q * quarter)

    @pl.when(n_rows == 0)
    def _():
        ys_ref[...] = jnp.zeros_like(ys_ref)


def _expert_call_f32(block_e, block_rows, block_next, block_live, xs, wg, bg, wu, bu, wd, bd):
    n_slots = xs.shape[0]
    n_blocks = n_slots // EXPERT_BLOCK
    any_spec = pl.BlockSpec(memory_space=pl.ANY)
    b_spec = pl.BlockSpec((N_EXPERTS, D_MODEL), lambda i, be, nv, nx, lv: (0, 0))
    x_spec = pl.BlockSpec((EXPERT_BLOCK, D_MODEL // 2), lambda i, be, nv, nx, lv: (i, 0))
    grid_spec = pltpu.PrefetchScalarGridSpec(
        num_scalar_prefetch=4,
        grid=(n_blocks,),
        in_specs=[x_spec, any_spec, b_spec, any_spec, b_spec, any_spec, b_spec],
        out_specs=[x_spec, any_spec, any_spec, any_spec],
        scratch_shapes=[pltpu.VMEM((3, D_MODEL, D_MODEL), jnp.float32)]
        + [pltpu.VMEM((D_MODEL, D_MODEL), jnp.bfloat16)] * 3
        + [pltpu.SemaphoreType.DMA((3,)), pltpu.SemaphoreType.DMA((3,))],
    )
    w16 = jax.ShapeDtypeStruct((N_EXPERTS, D_MODEL, D_MODEL), jnp.bfloat16)
    return pl.pallas_call(
        functools.partial(_expert_kernel_f32, n_blocks=n_blocks),
        grid_spec=grid_spec,
        out_shape=[jax.ShapeDtypeStruct((n_slots, D_MODEL // 2), jnp.uint32), w16, w16, w16],
        compiler_params=pltpu.CompilerParams(
            dimension_semantics=("arbitrary",),
            vmem_limit_bytes=VMEM_LIMIT),
        name="experts_f32",
    )(block_e, block_rows, block_next, block_live, xs, wg, bg, wu, bu, wd, bd)


def _expert_call_bf16(block_e, block_rows, block_next, block_slot, xs, wg, bg, wu, bu, wd, bd):
    n_slots = xs.shape[0]
    any_spec = pl.BlockSpec(memory_space=pl.ANY)
    b_spec = pl.BlockSpec((N_EXPERTS, D_MODEL), lambda i, be, nv, nx, sl: (0, 0))
    x_spec = pl.BlockSpec((EXPERT_BLOCK, D_MODEL // 2), lambda i, be, nv, nx, sl: (i, 0))
    grid_spec = pltpu.PrefetchScalarGridSpec(
        num_scalar_prefetch=4,
        grid=(n_slots // EXPERT_BLOCK,),
        in_specs=[x_spec, any_spec, b_spec, any_spec, b_spec, any_spec, b_spec],
        out_specs=x_spec,
        scratch_shapes=[pltpu.VMEM((2, 3, D_MODEL, D_MODEL), jnp.bfloat16),
                        pltpu.SemaphoreType.DMA((2, 3))],
    )
    return pl.pallas_call(
        _expert_kernel_bf16,
        grid_spec=grid_spec,
        out_shape=jax.ShapeDtypeStruct((n_slots, D_MODEL // 2), jnp.uint32),
        compiler_params=pltpu.CompilerParams(
            dimension_semantics=("arbitrary",),
            vmem_limit_bytes=VMEM_LIMIT),
        name="experts_bf16",
    )(block_e, block_rows, block_next, block_slot, xs, wg, bg, wu, bu, wd, bd)


def _combine_kernel(out_buf_ref, x1_ref, g_ref, y0_ref, y1_ref, y2_ref, y3_ref, o_ref):
    del out_buf_ref
    g = g_ref[...]
    half = D_MODEL // 2
    lo, hi = x1_ref[:, 0:half], x1_ref[:, half:]
    for j, y_ref in enumerate((y0_ref, y1_ref, y2_ref, y3_ref)):
        w = y_ref[...]
        gj = g[:, j:j + 1]
        lo = lo + gj * lax.bitcast_convert_type(w << 16, jnp.float32)
        hi = hi + gj * lax.bitcast_convert_type(w & jnp.uint32(0xFFFF0000), jnp.float32)
    o_ref[:, 0:half] = lo
    o_ref[:, half:] = hi


def _combine_call(out_buf, x1, gates, picked, stride, tok0, *, out_rows, row0, tile):
    n = x1.shape[0]
    aliased = out_buf.shape == (out_rows, D_MODEL)
    assert n % tile == 0 and row0 % tile == 0
    y_specs = []
    for j in range(TOP_K):
        assert (j * stride + tok0) % tile == 0
        base = (j * stride + tok0) // tile
        y_specs.append(pl.BlockSpec((tile, D_MODEL // 2), lambda i, _b=base: (_b + i, 0)))
    return pl.pallas_call(
        _combine_kernel,
        grid=(n // tile,),
        in_specs=[pl.BlockSpec(memory_space=pl.ANY),
                  pl.BlockSpec((tile, D_MODEL), lambda i: (i, 0)),
                  pl.BlockSpec((tile, GATE_COLS), lambda i: (i, 0))] + y_specs,
        out_specs=pl.BlockSpec((tile, D_MODEL), lambda i: (row0 // tile + i, 0)),
        out_shape=jax.ShapeDtypeStruct((out_rows, D_MODEL), jnp.float32),
        input_output_aliases={0: 0} if aliased else {},
        compiler_params=pltpu.CompilerParams(dimension_semantics=("arbitrary",),
                                             vmem_limit_bytes=VMEM_LIMIT),
        name="combine",
    )(out_buf, x1, gates, picked, picked, picked, picked)


def _gather_rows(table, idx):
    n = idx.shape[0]
    width = table.shape[1]
    per_worker = n // SC_WORKERS
    n_chunks = per_worker // SC_CHUNK
    mesh = plsc.VectorSubcoreMesh(core_axis_name="c", subcore_axis_name="s")

    @functools.partial(
        pl.kernel, mesh=mesh,
        out_type=jax.ShapeDtypeStruct((n, width), table.dtype),
        scratch_types=[pltpu.VMEM((SC_CHUNK,), jnp.int32),
                       pltpu.VMEM((SC_CHUNK, width), table.dtype),
                       pltpu.SemaphoreType.DMA],
        cost_estimate=pl.CostEstimate(flops=0, transcendentals=0, bytes_accessed=8 * n * width),
    )
    def gather(table_hbm, idx_hbm, out_hbm, idx_v, rows_v, sem):
        wid = lax.axis_index("s") * 2 + lax.axis_index("c")
        base = wid * per_worker

        @pl.loop(0, n_chunks)
        def _(i):
            off = base + i * SC_CHUNK
            pltpu.sync_copy(idx_hbm.at[pl.ds(off, SC_CHUNK)], idx_v)
            pltpu.async_copy(table_hbm.at[idx_v], rows_v, sem).wait()
            pltpu.sync_copy(rows_v, out_hbm.at[pl.ds(off, SC_CHUNK)])

    return gather(table, idx)


def _scatter_rows(srcs, dest, n_out):
    width = srcs[0].shape[1]
    starts = [0]
    for src in srcs:
        assert src.shape[0] % SC_CHUNK == 0
        starts.append(starts[-1] + src.shape[0] // SC_CHUNK)
    n_chunks = starts[-1]
    n = n_chunks * SC_CHUNK
    per_worker = -(-n_chunks // SC_WORKERS)
    mesh = plsc.VectorSubcoreMesh(core_axis_name="c", subcore_axis_name="s")

    @functools.partial(
        pl.kernel, mesh=mesh,
        out_type=jax.ShapeDtypeStruct((n_out, width), srcs[0].dtype),
        scratch_types=[pltpu.VMEM((SC_CHUNK,), jnp.int32)] * TOP_K
        + [pltpu.VMEM((SC_CHUNK, width), srcs[0].dtype)],
        cost_estimate=pl.CostEstimate(flops=0, transcendentals=0,
                                      bytes_accessed=4 * (1 + TOP_K) * n * width),
    )
    def scatter(*refs):
        src_hbms = refs[:len(srcs)]
        dest_hbm, out_hbm = refs[len(srcs)], refs[len(srcs) + 1]
        idx_vs, rows_v = refs[len(srcs) + 2:len(srcs) + 2 + TOP_K], refs[-1]
        wid = lax.axis_index("s") * 2 + lax.axis_index("c")

        @pl.loop(0, per_worker)
        def _(i):
            c = i * SC_WORKERS + wid
            for k, src_hbm in enumerate(src_hbms):

                @pl.when((c >= starts[k]) & (c < starts[k + 1]))
                def _():
                    pltpu.sync_copy(src_hbm.at[pl.ds((c - starts[k]) * SC_CHUNK, SC_CHUNK)], rows_v)
                    for j, idx_v in enumerate(idx_vs):
                        pltpu.sync_copy(dest_hbm.at[pl.ds(j * n + c * SC_CHUNK, SC_CHUNK)], idx_v)
                    for idx_v in idx_vs:
                        pltpu.sync_copy(rows_v, out_hbm.at[idx_v])

    return scatter(*srcs, dest)


def _t5_bucket_np(rel):
    half = NUM_BUCKETS // 2
    max_exact = half // 2
    n = np.abs(rel)
    nf = np.maximum(n, 1).astype(np.float32)
    large = max_exact + (np.log(nf / max_exact) / math.log(MAX_DISTANCE / max_exact)
                         * (half - max_exact)).astype(np.int32)
    large = np.minimum(large, half - 1)
    return np.where(rel > 0, half, 0) + np.where(n < max_exact, n, large)


def _bias_tables(rel_bias, visible):
    kap = np.arange(BAND)[:, None]
    col = np.arange(GROUP * PAIR)[None, :]
    bucket = _t5_bucket_np(kap - WINDOW - col % PAIR)
    onehot = jnp.asarray(bucket[:, :, None] == np.arange(NUM_BUCKETS))
    seen = jnp.asarray(np.tile(visible, (1, GROUP)))
    rb = rel_bias.astype(jnp.float32)
    tabs = []
    for h in range(N_KV_HEADS):
        per_col = jnp.repeat(rb[:, h * GROUP:(h + 1) * GROUP], PAIR, axis=1)
        tab = jnp.sum(jnp.where(onehot, jnp.transpose(per_col)[None], 0.0), axis=-1)
        tabs.append(jnp.where(seen, tab, NEG_INF))
    return jnp.stack(tabs)


def _mixer_consts(l, norm1_g, w_in, q_norm_g, k_norm_g, rel_bias, sinks, w_pool, pool_scale,
                  w_out, norm2_g, w_router, b_router, visible, tile):
    f32, bf16 = jnp.float32, jnp.bfloat16
    q_off, k_off, v_off = POOL_WIDTH, POOL_WIDTH + ATTN_WIDTH, POOL_WIDTH + ATTN_WIDTH + KV_WIDTH
    w = w_in[l]
    w_ukv = jnp.concatenate([w[:, :q_off], w[:, k_off:]], axis=1).astype(bf16)
    w_qvt = jnp.transpose(jnp.concatenate([w[:, q_off:k_off], w[:, v_off:]], axis=1)).astype(bf16)
    lane_head = np.arange(KV_WIDTH) // HEAD_DIM
    blockdiag = jnp.asarray(lane_head[:, None] == lane_head[None, :], bf16)
    sink_rows = jnp.repeat(sinks[l].astype(f32).reshape(N_KV_HEADS, GROUP), PAIR, axis=1)
    wp = w_pool[l].astype(bf16)
    zp = jnp.zeros((POOL_GROUP, POOL_GROUP), bf16)
    w_pool_pairs = jnp.stack([jnp.block([[wp[2 * i], zp], [zp, wp[2 * i + 1]]])
                              for i in range(N_POOL_GROUPS // 2)])
    wr = w_router[l].astype(f32)
    wr_hi = wr.astype(bf16)
    wr_lo = (wr - wr_hi.astype(f32)).astype(bf16)
    wr_parts = jnp.pad(jnp.concatenate([wr_hi, wr_lo], axis=1), ((0, 0), (0, 128 - 2 * N_EXPERTS)))
    tri = jnp.asarray(np.arange(128)[:, None] < np.arange(128)[None, :], bf16)
    return [
        norm1_g[l].reshape(1, D_MODEL).astype(f32), w_ukv, w_qvt,
        q_norm_g[l].reshape(HEAD_DIM, 1).astype(f32),
        jnp.tile(k_norm_g[l].astype(f32), N_KV_HEADS).reshape(1, KV_WIDTH),
        blockdiag, _bias_tables(rel_bias, visible), sink_rows,
        w_pool_pairs, pool_scale[l].reshape(1, POOL_WIDTH).astype(f32),
        w_out[l].astype(bf16), norm2_g[l].reshape(1, D_MODEL).astype(f32),
        wr_parts, b_router[l].reshape(N_EXPERTS, 1).astype(f32), tri,
    ]


def _visibility():
    kap = np.arange(BAND)[:, None]
    rho = np.arange(PAIR)[None, :]
    kc, qc = kap // CHUNK, rho // CHUNK
    prompt = (kc >= qc) & (kc <= qc + WINDOW // CHUNK)
    return prompt


def kernel(x_prompt, x_sample, cache_k, cache_v, state_pool, norm1_g, w_in, q_norm_g, k_norm_g,
           rel_bias, sinks, w_pool, pool_scale, w_out, norm2_g, w_router, b_router,
           w_gate, b_gate, w_up, b_up, w_down, b_down):
    f32, bf16 = jnp.float32, jnp.bfloat16
    depth = w_in.shape[0]
    nb, seq, _ = x_prompt.shape
    ndb, dseq, _ = x_sample.shape
    cache_len = cache_k.shape[2]
    assert seq % MIX_TILE == 0 and cache_len == WINDOW and HIST_ROWS <= dseq <= PAIR
    n_p, n_s = nb * seq, ndb * dseq
    n_tok = n_p + n_s
    assert COMBINE_TILE % n_s == 0 and n_s % SC_CHUNK == 0

    vis_prompt = _visibility()
    vis_sample = np.broadcast_to(np.arange(BAND)[:, None] < WINDOW + dseq, (BAND, PAIR))

    nb1 = nb // 4
    nb2 = nb - nb1
    assert nb1 > 0 and nb2 > 0 and seq % COMBINE_TILE == 0

    xp, xs = x_prompt, x_sample
    outs = [[] for _ in range(6)]
    for l in range(depth):
        wl = (l, norm1_g, w_in, q_norm_g, k_norm_g, rel_bias, sinks, w_pool, pool_scale, w_out,
              norm2_g, w_router, b_router)
        consts_p = _mixer_consts(*wl, vis_prompt, MIX_SUB)
        moe_b = [b.astype(f32) for b in (b_gate[l], b_up[l], b_down[l])]
        cnt0 = jnp.zeros((N_EXPERTS, 128), f32)

        def prompt_mixer(stream0, n_streams, cnt_in):
            zk = jnp.zeros((n_streams, WINDOW, KV_WIDTH), f32)
            zu = jnp.zeros((n_streams, HIST_ROWS, POOL_WIDTH), f32)
            return _mixer_call(xp, zk, zk, zu, cnt_in, consts_p, stream0=stream0, tile=MIX_TILE,
                               sub=MIX_SUB, n_valid=MIX_TILE, pos0=0, mask_first=True)

        (x1_a, h_a, idx_a, rank_a, gate_a, cnt_a, k_a, v_a, u_a) = prompt_mixer(0, nb1, cnt0)
        xs_pad = jnp.pad(xs, ((0, 0), (0, PAIR - dseq), (0, 0)))
        uh = jnp.pad(state_pool[l], ((0, 0), (HIST_ROWS - POOL_HIST, 0), (0, 0)))
        (xs1, h_s, idx_s, rank_s, gate_s, cnt_s, k_s, v_s, u_s) = _mixer_call(
            xs_pad, cache_k[l].reshape(ndb, WINDOW, KV_WIDTH),
            cache_v[l].reshape(ndb, WINDOW, KV_WIDTH), uh, cnt0,
            _mixer_consts(*wl, vis_sample, PAIR),
            stream0=0, tile=PAIR, sub=PAIR, n_valid=dseq, pos0=PAST_LEN, mask_first=False)
        (x1_b, h_b, idx_b, rank_b, gate_b, cnt_b, k_b, v_b, u_b) = prompt_mixer(nb1, nb2, cnt_s)
        picked_b, w_bf16 = _moe_rows([h_b, h_s], [idx_b, idx_s], [rank_b, rank_s], [seq, dseq], cnt_b,
                                     moe_b, w_f32=(w_gate[l], w_up[l], w_down[l]))
        picked_a, _ = _moe_rows([h_a], [idx_a], [rank_a], [seq], cnt_a, moe_b, w_bf16=w_bf16)

        n_a, n_b = nb1 * seq, nb2 * seq
        no_buf = jnp.zeros((8, 128), f32)
        stride_a, stride_b = _pick_stride(n_a), _pick_stride(n_b + n_s)
        xp_rows = _combine_call(no_buf, x1_b.reshape(n_b, D_MODEL), gate_b, picked_b, stride_b, 0,
                                out_rows=n_p, row0=n_a, tile=COMBINE_TILE)
        xp_rows = _combine_call(xp_rows, x1_a.reshape(n_a, D_MODEL), gate_a, picked_a, stride_a, 0,
                                out_rows=n_p, row0=0, tile=COMBINE_TILE)
        xp = xp_rows.reshape(nb, seq, D_MODEL)
        xs = _combine_call(no_buf, xs1[:, :dseq].reshape(n_s, D_MODEL), gate_s, picked_b, stride_b,
                           n_b, out_rows=n_s, row0=0, tile=n_s).reshape(ndb, dseq, D_MODEL)

        outs[0].append(jnp.concatenate([k_a, k_b]).reshape(nb, WINDOW, N_KV_HEADS, HEAD_DIM))
        outs[1].append(jnp.concatenate([v_a, v_b]).reshape(nb, WINDOW, N_KV_HEADS, HEAD_DIM))
        outs[2].append(jnp.concatenate([u_a, u_b])[:, HIST_ROWS - POOL_HIST:])
        outs[3].append(k_s[:, :dseq].reshape(ndb, dseq, N_KV_HEADS, HEAD_DIM))
        outs[4].append(v_s[:, :dseq].reshape(ndb, dseq, N_KV_HEADS, HEAD_DIM))
        outs[5].append(u_s[:, HIST_ROWS - POOL_HIST:])
    return (xp, xs) + tuple(jnp.stack(o) for o in outs)


def _moe_rows(hs, idxs, ranks, n_reals, cnt, biases, w_f32=None, w_bf16=None):
    n_tok = sum(h.shape[0] for h in hs)
    n_assign = n_tok * TOP_K
    gather_quant = SC_WORKERS * SC_CHUNK
    n_blocks = n_assign // EXPERT_BLOCK + N_EXPERTS
    n_blocks = -(-n_blocks // (gather_quant // EXPERT_BLOCK)) * (gather_quant // EXPERT_BLOCK)
    e_ids = jnp.arange(N_EXPERTS, dtype=jnp.int32)

    def per_token(arrs):
        return jnp.concatenate(
            [jnp.transpose(a[:, :, :n], (1, 0, 2)).reshape(TOP_K, -1) for a, n in zip(arrs, n_reals)],
            axis=1)

    counts = cnt[:, 0].astype(jnp.int32)
    pcounts = (counts + EXPERT_BLOCK - 1) // EXPERT_BLOCK * EXPERT_BLOCK
    if w_bf16 is None:
        pcounts = jnp.maximum(pcounts, EXPERT_BLOCK)
    pend = jnp.cumsum(pcounts)
    pstart = pend - pcounts
    idx_all = per_token(idxs)
    dest = per_token(ranks) + jnp.sum(
        jnp.where(idx_all[None] == e_ids[:, None, None], pstart[:, None, None], 0), axis=0)
    dest = dest.reshape(-1)
    blk0 = jnp.arange(n_blocks, dtype=jnp.int32) * EXPERT_BLOCK
    block_e = jnp.minimum(jnp.sum((pend[None, :] <= blk0[:, None]).astype(jnp.int32), axis=1),
                          N_EXPERTS - 1)
    block_rows = jnp.clip(jnp.sum(jnp.where(block_e[:, None] == e_ids[None, :],
                                             (pstart + counts)[None, :], 0), axis=1) - blk0,
                          0, EXPERT_BLOCK).astype(jnp.int32)

    x_sorted = _scatter_rows(hs, dest, n_blocks * EXPERT_BLOCK)
    if w_bf16 is None:
        block_next = jnp.minimum(block_e + 1, N_EXPERTS - 1)
        block_live = (blk0 < pend[-1]).astype(jnp.int32)
        y_sorted, *w_bf16 = _expert_call_f32(
            block_e, block_rows, block_next, block_live, x_sorted,
            w_f32[0], biases[0], w_f32[1], biases[1], w_f32[2], biases[2])
    else:
        has = counts > 0
        later = has[None, :] & (e_ids[None, :] > e_ids[:, None])
        next_e = jnp.min(jnp.where(later, e_ids[None, :], N_EXPERTS), axis=1)
        next_e = jnp.where(next_e == N_EXPERTS, e_ids, next_e)
        order = jnp.cumsum(has.astype(jnp.int32)) - 1
        table = jnp.stack([next_e, order % 2], axis=0)[:, None, :]
        block_next, block_slot = jnp.sum(
            jnp.where(block_e[None, :, None] == e_ids[None, None, :], table, 0), axis=2).astype(jnp.int32)
        y_sorted = _expert_call_bf16(
            block_e, block_rows, block_next, block_slot, x_sorted,
            w_bf16[0], biases[0], w_bf16[1], biases[1], w_bf16[2], biases[2])
    stride = _pick_stride(n_tok)
    n_pick = -(-(TOP_K * stride) // gather_quant) * gather_quant
    picks = jnp.pad(dest.reshape(TOP_K, n_tok), ((0, 0), (0, stride - n_tok))).reshape(-1)
    return _gather_rows(y_sorted, jnp.pad(picks, (0, n_pick - TOP_K * stride))), w_bf16


def _pick_stride(n_tok):
    return -(-n_tok // COMBINE_TILE) * COMBINE_TILE
```

```python
import functools
import math

import numpy as np
import jax
import jax.numpy as jnp
from jax import lax
from jax.experimental import pallas as pl
from jax.experimental.pallas import tpu as pltpu
from jax.experimental.pallas import tpu_sc as plsc

D_MODEL = 1024
CHUNK = 64
POOL_WIDTH = 512
POOL_WINDOWS = (2, 4, 8, 16)
POOL_GROUP = 128
N_POOL_GROUPS = 4
POOL_HIST = 15
ATTN_WIDTH = 512
HEAD_DIM = 64
N_HEADS = 8
N_KV_HEADS = 2
GROUP = 4
KV_WIDTH = 128
WINDOW = 128
NUM_BUCKETS = 32
MAX_DISTANCE = 128
PAST_LEN = 2048
N_EXPERTS = 32
TOP_K = 4
SWIGLU_LIMIT = 7.0
SWIGLU_ALPHA = 1.702
EPS = 1e-5
NEG_INF = -1e30
ATTN_SCALE = HEAD_DIM ** -0.5

PAIR = 2 * CHUNK
BAND = PAIR + WINDOW
HIST_ROWS = 16
GATE_COLS = 8
MIX_TILE = 1024
MIX_SUB = 1024
EXPERT_BLOCK = 1024
FFN_CHUNK = 256
COMBINE_TILE = 1024
SC_WORKERS = 32
SC_CHUNK = 64
SC_GATHER_WAYS = 2
GATHER_QUANT = SC_WORKERS * SC_CHUNK * SC_GATHER_WAYS
VMEM_LIMIT = 56 * 1024 * 1024


def _pack_bf16_pair(a, b):
    ab = lax.bitcast_convert_type(a.astype(jnp.bfloat16).astype(jnp.float32), jnp.uint32)
    bb = lax.bitcast_convert_type(b.astype(jnp.bfloat16).astype(jnp.float32), jnp.uint32)
    return (ab >> 16) | (bb & jnp.uint32(0xFFFF0000))


def _unpack_bf16_pair(w):
    a = lax.bitcast_convert_type(w << 16, jnp.float32).astype(jnp.bfloat16)
    b = lax.bitcast_convert_type(w & jnp.uint32(0xFFFF0000), jnp.float32).astype(jnp.bfloat16)
    return a, b


def _mixer_kernel(x_ref, kh_ref, vh_ref, uh_ref, cnt_in_ref,
                  g1_ref, wukv_ref, wqvt_ref, qg_ref, kg_ref, bd_ref, bias_ref, sink_ref,
                  wpool_ref, pscale_ref, wout_ref, g2_ref, wr_ref, br_ref, tri_ref,
                  x1_ref, hp_ref, idx_ref, rank_ref, gate_ref, cnt_ref, ko_ref, vo_ref, uo_ref,
                  qt_s, kb_s, vt_s, ub_s, mix_s, cnt_s,
                  *, tile, sub, n_valid, pos0, mask_first):
    b = pl.program_id(0)
    s = pl.program_id(1)
    bf16 = jnp.bfloat16
    f32 = jnp.float32

    @pl.when((b == 0) & (s == 0))
    def _():
        cnt_s[...] = cnt_in_ref[...]

    @pl.when(s == 0)
    def _():
        kb_s[0:WINDOW, :] = kh_ref[...].astype(bf16)
        vt_s[:, 0:WINDOW] = jnp.transpose(vh_ref[...]).astype(bf16)
        ub_s[0:HIST_ROWS, :] = uh_ref[...]

    @pl.when(s > 0)
    def _():
        kb_s[0:WINDOW, :] = kb_s[tile:tile + WINDOW, :]
        vt_s[:, 0:WINDOW] = vt_s[:, tile:tile + WINDOW]
        ub_s[0:HIST_ROWS, :] = ub_s[tile:tile + HIST_ROWS, :]

    for r0 in range(0, tile, sub):
        _mixer_rows(r0, s, x_ref, g1_ref, wukv_ref, wqvt_ref, qg_ref, kg_ref, bd_ref, bias_ref,
                    sink_ref, wpool_ref, pscale_ref, wout_ref, g2_ref, wr_ref, br_ref, tri_ref,
                    x1_ref, hp_ref, idx_ref, rank_ref, gate_ref, cnt_ref, ko_ref, vo_ref, uo_ref,
                    qt_s, kb_s, vt_s, ub_s, mix_s, cnt_s,
                    tile=tile, sub=sub, n_valid=n_valid, pos0=pos0, mask_first=mask_first)


def _mixer_rows(r0, s, x_ref, g1_ref, wukv_ref, wqvt_ref, qg_ref, kg_ref, bd_ref, bias_ref,
                sink_ref, wpool_ref, pscale_ref, wout_ref, g2_ref, wr_ref, br_ref, tri_ref,
                x1_ref, hp_ref, idx_ref, rank_ref, gate_ref, cnt_ref, ko_ref, vo_ref, uo_ref,
                qt_s, kb_s, vt_s, ub_s, mix_s, cnt_s, *, tile, sub, n_valid, pos0, mask_first):
    bf16 = jnp.bfloat16
    f32 = jnp.float32
    rows = slice(r0, r0 + sub)

    x = x_ref[rows, :]
    xn = (x * lax.rsqrt(jnp.mean(x * x, axis=-1, keepdims=True) + EPS) * g1_ref[...]).astype(bf16)
    z = jnp.dot(xn, wukv_ref[...], preferred_element_type=f32)
    zt = lax.dot_general(wqvt_ref[...], xn, (((1,), (1,)), ((), ())),
                         preferred_element_type=f32)
    u = z[:, 0:POOL_WIDTH]
    kz = z[:, POOL_WIDTH:POOL_WIDTH + KV_WIDTH]
    v = z[:, POOL_WIDTH + KV_WIDTH:]

    ksq = kz * kz
    kss = jnp.dot(ksq.astype(bf16), bd_ref[...], preferred_element_type=f32)
    kn = kz * lax.rsqrt(kss * (1.0 / HEAD_DIM) + EPS) * kg_ref[...]
    kb_s[WINDOW + r0:WINDOW + r0 + sub, :] = kn.astype(bf16)
    vt_s[:, WINDOW + r0:WINDOW + r0 + sub] = zt[ATTN_WIDTH:, :].astype(bf16)
    ub_s[HIST_ROWS + r0:HIST_ROWS + r0 + sub, :] = u

    if r0 + sub == tile:
        row0 = max(n_valid, WINDOW) - WINDOW - r0
        ko_ref[...] = kn[row0:row0 + WINDOW, :]
        vo_ref[...] = v[row0:row0 + WINDOW, :]
        uo_ref[...] = u[n_valid - HIST_ROWS - r0:n_valid - r0, :]

    for hd in range(N_HEADS):
        qh = zt[hd * HEAD_DIM:(hd + 1) * HEAD_DIM, :]
        ss = jnp.sum(qh * qh, axis=0, keepdims=True)
        qn = qh * (lax.rsqrt(ss * (1.0 / HEAD_DIM) + EPS) * ATTN_SCALE) * qg_ref[...]
        qt_s[hd * HEAD_DIM:(hd + 1) * HEAD_DIM, rows] = qn.astype(bf16)

    pos = pos0 + s * tile + r0 + lax.broadcasted_iota(jnp.int32, (sub, 1), 0)
    diffs = []
    for g, w in enumerate(POOL_WINDOWS):
        e = ub_s[r0:r0 + HIST_ROWS + sub, g * POOL_GROUP:(g + 1) * POOL_GROUP]
        acc = e
        for lvl in range(g + 1):
            acc = acc + pltpu.roll(acc, 2 ** lvl, axis=0)
        inv_cnt = 1.0 / jnp.minimum(pos + 1, w).astype(f32)
        diffs.append((acc[HIST_ROWS:, :] * inv_cnt - e[HIST_ROWS:, :]).astype(bf16))
    for g2 in range(N_POOL_GROUPS // 2):
        cols = slice(2 * g2 * POOL_GROUP, 2 * (g2 + 1) * POOL_GROUP)
        d = jnp.concatenate(diffs[2 * g2:2 * g2 + 2], axis=1)
        y = jnp.dot(d, wpool_ref[g2], preferred_element_type=f32) * pscale_ref[:, cols]
        mix_s[rows, cols] = y.astype(bf16)

    zeros_q = jnp.zeros((HEAD_DIM, GROUP * PAIR), bf16)
    for p in range(r0 // PAIR, (r0 + sub) // PAIR):
        k_band = kb_s[p * PAIR:p * PAIR + BAND, :]
        o_parts = []
        rhs_heads = []
        for h in range(N_KV_HEADS):
            qcat = jnp.concatenate(
                [qt_s[(h * GROUP + g) * HEAD_DIM:(h * GROUP + g + 1) * HEAD_DIM,
                      p * PAIR:(p + 1) * PAIR] for g in range(GROUP)], axis=1)
            rhs_heads.append(jnp.concatenate([qcat, zeros_q] if h == 0 else [zeros_q, qcat], axis=0))
        st_heads = jnp.dot(k_band, jnp.concatenate(rhs_heads, axis=1),
                           preferred_element_type=f32)
        for h in range(N_KV_HEADS):
            st = st_heads[:, h * GROUP * PAIR:(h + 1) * GROUP * PAIR] + bias_ref[h]
            if mask_first and p == 0:
                krow = lax.broadcasted_iota(jnp.int32, (BAND, 1), 0)
                st = jnp.where((krow >= WINDOW) | (s > 0), st, NEG_INF)
            sink = sink_ref[h:h + 1, :]
            m = jnp.maximum(jnp.max(st, axis=0, keepdims=True), sink)
            ex = jnp.exp(st - m)
            den = jnp.sum(ex, axis=0, keepdims=True) + jnp.exp(sink - m)
            v_band = vt_s[h * HEAD_DIM:(h + 1) * HEAD_DIM, p * PAIR:p * PAIR + BAND]
            ot = jnp.dot(v_band, ex.astype(bf16), preferred_element_type=f32) / den
            for g in range(GROUP):
                o_parts.append(ot[:, g * PAIR:(g + 1) * PAIR])
        o_all = jnp.concatenate(o_parts, axis=0)
        mix_s[p * PAIR:(p + 1) * PAIR, POOL_WIDTH:] = jnp.transpose(o_all).astype(bf16)

    x1 = x + jnp.dot(mix_s[rows, :], wout_ref[...], preferred_element_type=f32)
    x1_ref[rows, :] = x1

    n_real = min(sub, n_valid - r0)
    hn = x1 * lax.rsqrt(jnp.mean(x1 * x1, axis=-1, keepdims=True) + EPS) * g2_ref[...]
    hp_ref[r0:r0 + n_real, :] = _pack_bf16_pair(hn[0:n_real, 0:D_MODEL // 2],
                                                hn[0:n_real, D_MODEL // 2:])
    h_hi = hn.astype(bf16)
    h_lo = (hn - h_hi.astype(f32)).astype(bf16)
    both = jnp.dot(jnp.concatenate([h_hi, h_lo], axis=0), wr_ref[...],
                   preferred_element_type=f32)
    parts = jnp.transpose(both[0:sub, :] + both[sub:, :])
    lt = parts[0:N_EXPERTS, :] + parts[N_EXPERTS:2 * N_EXPERTS, :] + br_ref[...]

    eidx = lax.broadcasted_iota(jnp.int32, (N_EXPERTS, sub), 0).astype(f32)
    vals, hots = [], []
    for j in range(TOP_K):
        m = jnp.max(lt, axis=0, keepdims=True)
        sel = jnp.min(jnp.where(lt == m, eidx, float(N_EXPERTS)), axis=0, keepdims=True)
        hot = eidx == sel
        lt = jnp.where(hot, -jnp.inf, lt)
        idx_ref[j:j + 1, rows] = sel.astype(jnp.int32)
        vals.append(m)
        hots.append(hot)
    exps = [jnp.exp(vv - vals[0]) for vv in vals]
    esum = exps[0] + exps[1] + exps[2] + exps[3]
    grow = lax.broadcasted_iota(jnp.int32, (GATE_COLS, sub), 0)
    gmat = jnp.zeros((GATE_COLS, sub), f32)
    for j in range(TOP_K):
        gmat = jnp.where(grow == j, exps[j] / esum, gmat)
    gmat = jnp.concatenate([gmat, jnp.zeros((128 - GATE_COLS, sub), f32)], axis=0)
    gate_ref[r0:r0 + n_real, :] = jnp.transpose(gmat)[0:n_real, 0:GATE_COLS]

    chosen_f = sum(jnp.where(hot, 1.0, 0.0) for hot in hots)
    if n_real < sub:
        lane = lax.broadcasted_iota(jnp.int32, (N_EXPERTS, sub), 1)
        chosen_f = jnp.where(lane < n_real, chosen_f, 0.0)
    n_lane_blocks = sub // 128
    blocks = [chosen_f[:, k * 128:(k + 1) * 128] for k in range(n_lane_blocks)]
    inside = jnp.dot(jnp.concatenate(blocks, axis=0).astype(bf16), tri_ref[...],
                     preferred_element_type=f32)
    offset = cnt_s[:, 0:1]
    pieces = []
    for k in range(n_lane_blocks):
        pieces.append(inside[k * N_EXPERTS:(k + 1) * N_EXPERTS, :] + offset)
        offset = offset + jnp.sum(blocks[k], axis=1, keepdims=True)
    base = jnp.concatenate(pieces, axis=1)
    for j in range(TOP_K):
        rank_ref[j:j + 1, rows] = jnp.sum(jnp.where(hots[j], base, 0.0), axis=0,
                                          keepdims=True).astype(jnp.int32)
    cnt_new = jnp.broadcast_to(offset, (N_EXPERTS, 128))
    cnt_s[...] = cnt_new
    cnt_ref[...] = cnt_new


def _mixer_call(x, k_hist, v_hist, u_hist, cnt_in, consts, *, stream0, tile, sub, n_valid, pos0,
                mask_first):
    seq = x.shape[1]
    nb = k_hist.shape[0]
    n_tiles = seq // tile
    f32 = jnp.float32
    assert n_valid % 8 == 0 and tile % sub == 0 and sub % PAIR == 0
    assert n_valid == tile or sub == tile

    def full(a):
        nd = a.ndim
        return pl.BlockSpec(a.shape, lambda b, s, _nd=nd: (0,) * _nd)

    in_specs = [
        pl.BlockSpec((None, tile, D_MODEL), lambda b, s: (stream0 + b, s, 0)),
        pl.BlockSpec((None, WINDOW, KV_WIDTH), lambda b, s: (b, 0, 0)),
        pl.BlockSpec((None, WINDOW, KV_WIDTH), lambda b, s: (b, 0, 0)),
        pl.BlockSpec((None, HIST_ROWS, POOL_WIDTH), lambda b, s: (b, 0, 0)),
        full(cnt_in),
    ] + [full(c) for c in consts]
    out_shape = [
        jax.ShapeDtypeStruct((nb, seq, D_MODEL), f32),
        jax.ShapeDtypeStruct((nb * n_tiles * n_valid, D_MODEL // 2), jnp.uint32),
        jax.ShapeDtypeStruct((nb, TOP_K, seq), jnp.int32),
        jax.ShapeDtypeStruct((nb, TOP_K, seq), jnp.int32),
        jax.ShapeDtypeStruct((nb * n_tiles * n_valid, GATE_COLS), f32),
        jax.ShapeDtypeStruct((N_EXPERTS, 128), f32),
        jax.ShapeDtypeStruct((nb, WINDOW, KV_WIDTH), f32),
        jax.ShapeDtypeStruct((nb, WINDOW, KV_WIDTH), f32),
        jax.ShapeDtypeStruct((nb, HIST_ROWS, POOL_WIDTH), f32),
    ]
    out_specs = [
        pl.BlockSpec((None, tile, D_MODEL), lambda b, s: (b, s, 0)),
        pl.BlockSpec((n_valid, D_MODEL // 2), lambda b, s: (b * n_tiles + s, 0)),
        pl.BlockSpec((None, TOP_K, tile), lambda b, s: (b, 0, s)),
        pl.BlockSpec((None, TOP_K, tile), lambda b, s: (b, 0, s)),
        pl.BlockSpec((n_valid, GATE_COLS), lambda b, s: (b * n_tiles + s, 0)),
        pl.BlockSpec((N_EXPERTS, 128), lambda b, s: (0, 0)),
        pl.BlockSpec((None, WINDOW, KV_WIDTH), lambda b, s: (b, 0, 0)),
        pl.BlockSpec((None, WINDOW, KV_WIDTH), lambda b, s: (b, 0, 0)),
        pl.BlockSpec((None, HIST_ROWS, POOL_WIDTH), lambda b, s: (b, 0, 0)),
    ]
    scratch = [
        pltpu.VMEM((ATTN_WIDTH, tile), jnp.bfloat16),
        pltpu.VMEM((WINDOW + tile, KV_WIDTH), jnp.bfloat16),
        pltpu.VMEM((KV_WIDTH, WINDOW + tile), jnp.bfloat16),
        pltpu.VMEM((HIST_ROWS + tile, POOL_WIDTH), f32),
        pltpu.VMEM((tile, D_MODEL), jnp.bfloat16),
        pltpu.VMEM((N_EXPERTS, 128), f32),
    ]
    kern = functools.partial(_mixer_kernel, tile=tile, sub=sub, n_valid=n_valid, pos0=pos0,
                             mask_first=mask_first)
    return pl.pallas_call(
        kern,
        grid=(nb, n_tiles),
        in_specs=in_specs,
        out_specs=out_specs,
        out_shape=out_shape,
        scratch_shapes=scratch,
        compiler_params=pltpu.CompilerParams(
            dimension_semantics=("arbitrary", "arbitrary"),
            vmem_limit_bytes=VMEM_LIMIT),
        name="mixer",
    )(x, k_hist, v_hist, u_hist, cnt_in, *consts)


def _expert_kernel_f32(be_ref, nv_ref, nx_ref, lv_ref, xs_ref, wg_hbm, bg_ref, wu_hbm, bu_ref,
                       wd_hbm, bd_ref, ys_ref, wg16_hbm, wu16_hbm, wd16_hbm,
                       stage_s, wg_s, wu_s, wd_s, in_sems, out_sems, *, n_blocks):
    i = pl.program_id(0)
    expert = be_ref[i]
    bf16 = jnp.bfloat16

    def copies_in(e):
        return [pltpu.make_async_copy(w_hbm.at[e], stage_s.at[k], in_sems.at[k])
                for k, w_hbm in enumerate((wg_hbm, wu_hbm, wd_hbm))]

    def copies_out(e):
        return [pltpu.make_async_copy(w_s, w16_hbm.at[e], out_sems.at[k])
                for k, (w_s, w16_hbm) in enumerate(((wg_s, wg16_hbm), (wu_s, wu16_hbm),
                                                    (wd_s, wd16_hbm)))]

    @pl.when(i == 0)
    def _():
        for copy in copies_in(expert):
            copy.start()

    @pl.when((lv_ref[i] > 0) & ((i == 0) | (expert != be_ref[jnp.maximum(i - 1, 0)])))
    def _():
        for copy in copies_in(expert):
            copy.wait()

        @pl.when(i > 0)
        def _():
            for copy in copies_out(expert):
                copy.wait()

        wg_s[...] = stage_s[0].astype(bf16)
        wu_s[...] = stage_s[1].astype(bf16)
        wd_s[...] = stage_s[2].astype(bf16)
        for copy in copies_out(expert):
            copy.start()

        @pl.when(nx_ref[i] != expert)
        def _():
            for copy in copies_in(nx_ref[i]):
                copy.start()

    _ffn_block(nv_ref[i], expert, xs_ref, ys_ref, wg_s, wu_s, wd_s, bg_ref, bu_ref, bd_ref)

    @pl.when(i == n_blocks - 1)
    def _():
        for copy in copies_out(expert):
            copy.wait()


def _expert_kernel_bf16(be_ref, nv_ref, nx_ref, sl_ref, xs_ref, wg_hbm, bg_ref, wu_hbm, bu_ref,
                        wd_hbm, bd_ref, ys_ref, w_s, sems):
    i = pl.program_id(0)
    n_rows = nv_ref[i]
    expert = be_ref[i]
    slot = sl_ref[i]

    def copies(e, to_slot):
        return [pltpu.make_async_copy(w_hbm.at[e], w_s.at[to_slot, k], sems.at[to_slot, k])
                for k, w_hbm in enumerate((wg_hbm, wu_hbm, wd_hbm))]

    @pl.when((i == 0) & (n_rows > 0))
    def _():
        for copy in copies(expert, slot):
            copy.start()

    @pl.when((n_rows > 0) & ((i == 0) | (expert != be_ref[jnp.maximum(i - 1, 0)])))
    def _():
        for copy in copies(expert, slot):
            copy.wait()

        @pl.when(nx_ref[i] != expert)
        def _():
            for copy in copies(nx_ref[i], 1 - slot):
                copy.start()

    _ffn_block(n_rows, expert, xs_ref, ys_ref, w_s.at[slot, 0], w_s.at[slot, 1], w_s.at[slot, 2],
               bg_ref, bu_ref, bd_ref)


def _ffn_block(n_rows, expert, xs_ref, ys_ref, wg_s, wu_s, wd_s, bg_all_ref, bu_all_ref, bd_all_ref):
    bf16 = jnp.bfloat16

    def ffn(rows):
        f32 = jnp.float32
        half = D_MODEL // 2
        bg_ref, bu_ref, bd_ref = (b.at[pl.ds(expert, 1), :]
                                  for b in (bg_all_ref, bu_all_ref, bd_all_ref))
        for r0 in range(0, rows, FFN_CHUNK):
            row = r0 + lax.broadcasted_iota(jnp.int32, (FFN_CHUNK, 1), 0)
            words = jnp.where(row < n_rows, xs_ref[r0:r0 + FFN_CHUNK, :],
                              jnp.uint32(0))
            xa, xb = _unpack_bf16_pair(words)
            a = (jnp.dot(xa, wg_s[0:half, :], preferred_element_type=f32)
                 + jnp.dot(xb, wg_s[half:, :], preferred_element_type=f32) + bg_ref[...])
            bb = (jnp.dot(xa, wu_s[0:half, :], preferred_element_type=f32)
                  + jnp.dot(xb, wu_s[half:, :], preferred_element_type=f32) + bu_ref[...])
            a = jnp.minimum(a, SWIGLU_LIMIT)
            bb = jnp.clip(bb, -SWIGLU_LIMIT, SWIGLU_LIMIT)
            act = a * (1.0 / (1.0 + jnp.exp(-SWIGLU_ALPHA * a))) * (bb + 1.0)
            y = jnp.dot(act.astype(bf16), wd_s[...], preferred_element_type=f32) + bd_ref[...]
            ys_ref[r0:r0 + FFN_CHUNK, :] = _pack_bf16_pair(y[:, 0:half], y[:, half:])
        if rows < EXPERT_BLOCK:
            ys_ref[rows:, :] = jnp.zeros((EXPERT_BLOCK - rows, half), jnp.uint32)

    quarter = EXPERT_BLOCK // 4
    for q in range(1, 5):
        @pl.when((n_rows > (q - 1) * quarter) & (n_rows <= q * quarter))
        def _(q=q):
            ffn(q * quarter)

    @pl.when(n_rows == 0)
    def _():
        ys_ref[...] = jnp.zeros_like(ys_ref)


def _expert_call_f32(block_e, block_rows, block_next, block_live, xs, wg, bg, wu, bu, wd, bd):
    n_slots = xs.shape[0]
    n_blocks = n_slots // EXPERT_BLOCK
    any_spec = pl.BlockSpec(memory_space=pl.ANY)
    b_spec = pl.BlockSpec((N_EXPERTS, D_MODEL), lambda i, be, nv, nx, lv: (0, 0))
    x_spec = pl.BlockSpec((EXPERT_BLOCK, D_MODEL // 2), lambda i, be, nv, nx, lv: (i, 0))
    grid_spec = pltpu.PrefetchScalarGridSpec(
        num_scalar_prefetch=4,
        grid=(n_blocks,),
        in_specs=[x_spec, any_spec, b_spec, any_spec, b_spec, any_spec, b_spec],
        out_specs=[x_spec, any_spec, any_spec, any_spec],
        scratch_shapes=[pltpu.VMEM((3, D_MODEL, D_MODEL), jnp.float32)]
        + [pltpu.VMEM((D_MODEL, D_MODEL), jnp.bfloat16)] * 3
        + [pltpu.SemaphoreType.DMA((3,)), pltpu.SemaphoreType.DMA((3,))],
    )
    w16 = jax.ShapeDtypeStruct((N_EXPERTS, D_MODEL, D_MODEL), jnp.bfloat16)
    return pl.pallas_call(
        functools.partial(_expert_kernel_f32, n_blocks=n_blocks),
        grid_spec=grid_spec,
        out_shape=[jax.ShapeDtypeStruct((n_slots, D_MODEL // 2), jnp.uint32), w16, w16, w16],
        compiler_params=pltpu.CompilerParams(
            dimension_semantics=("arbitrary",),
            vmem_limit_bytes=VMEM_LIMIT),
        name="experts_f32",
    )(block_e, block_rows, block_next, block_live, xs, wg, bg, wu, bu, wd, bd)


def _expert_call_bf16(block_e, block_rows, block_next, block_slot, xs, wg, bg, wu, bu, wd, bd):
    n_slots = xs.shape[0]
    any_spec = pl.BlockSpec(memory_space=pl.ANY)
    b_spec = pl.BlockSpec((N_EXPERTS, D_MODEL), lambda i, be, nv, nx, sl: (0, 0))
    x_spec = pl.BlockSpec((EXPERT_BLOCK, D_MODEL // 2), lambda i, be, nv, nx, sl: (i, 0))
    grid_spec = pltpu.PrefetchScalarGridSpec(
        num_scalar_prefetch=4,
        grid=(n_slots // EXPERT_BLOCK,),
        in_specs=[x_spec, any_spec, b_spec, any_spec, b_spec, any_spec, b_spec],
        out_specs=x_spec,
        scratch_shapes=[pltpu.VMEM((2, 3, D_MODEL, D_MODEL), jnp.bfloat16),
                        pltpu.SemaphoreType.DMA((2, 3))],
    )
    return pl.pallas_call(
        _expert_kernel_bf16,
        grid_spec=grid_spec,
        out_shape=jax.ShapeDtypeStruct((n_slots, D_MODEL // 2), jnp.uint32),
        compiler_params=pltpu.CompilerParams(
            dimension_semantics=("arbitrary",),
            vmem_limit_bytes=VMEM_LIMIT),
        name="experts_bf16",
    )(block_e, block_rows, block_next, block_slot, xs, wg, bg, wu, bu, wd, bd)


def _combine_kernel(out_buf_ref, x1_ref, g_ref, y0_ref, y1_ref, y2_ref, y3_ref, o_ref):
    del out_buf_ref
    g = g_ref[...]
    half = D_MODEL // 2
    lo, hi = x1_ref[:, 0:half], x1_ref[:, half:]
    for j, y_ref in enumerate((y0_ref, y1_ref, y2_ref, y3_ref)):
        w = y_ref[...]
        gj = g[:, j:j + 1]
        lo = lo + gj * lax.bitcast_convert_type(w << 16, jnp.float32)
        hi = hi + gj * lax.bitcast_convert_type(w & jnp.uint32(0xFFFF0000), jnp.float32)
    o_ref[:, 0:half] = lo
    o_ref[:, half:] = hi


def _combine_call(out_buf, x1, gates, picked, stride, tok0, *, out_rows, row0, tile):
    n = x1.shape[0]
    aliased = out_buf.shape == (out_rows, D_MODEL)
    assert n % tile == 0 and row0 % tile == 0
    y_specs = []
    for j in range(TOP_K):
        assert (j * stride + tok0) % tile == 0
        base = (j * stride + tok0) // tile
        y_specs.append(pl.BlockSpec((tile, D_MODEL // 2), lambda i, _b=base: (_b + i, 0)))
    return pl.pallas_call(
        _combine_kernel,
        grid=(n // tile,),
        in_specs=[pl.BlockSpec(memory_space=pl.ANY),
                  pl.BlockSpec((tile, D_MODEL), lambda i: (i, 0)),
                  pl.BlockSpec((tile, GATE_COLS), lambda i: (i, 0))] + y_specs,
        out_specs=pl.BlockSpec((tile, D_MODEL), lambda i: (row0 // tile + i, 0)),
        out_shape=jax.ShapeDtypeStruct((out_rows, D_MODEL), jnp.float32),
        input_output_aliases={0: 0} if aliased else {},
        compiler_params=pltpu.CompilerParams(dimension_semantics=("arbitrary",),
                                             vmem_limit_bytes=VMEM_LIMIT),
        name="combine",
    )(out_buf, x1, gates, picked, picked, picked, picked)


def _gather_rows(table, idx):
    n = idx.shape[0]
    width = table.shape[1]
    assert n % GATHER_QUANT == 0
    per_worker = n // SC_WORKERS
    n_rounds = per_worker // (SC_CHUNK * SC_GATHER_WAYS)
    mesh = plsc.VectorSubcoreMesh(core_axis_name="c", subcore_axis_name="s")

    @functools.partial(
        pl.kernel, mesh=mesh,
        out_type=jax.ShapeDtypeStruct((n, width), table.dtype),
        scratch_types=[pltpu.VMEM((SC_CHUNK,), jnp.int32)] * SC_GATHER_WAYS
        + [pltpu.VMEM((SC_CHUNK, width), table.dtype)] * SC_GATHER_WAYS
        + [pltpu.SemaphoreType.DMA] * 3,
        cost_estimate=pl.CostEstimate(flops=0, transcendentals=0, bytes_accessed=8 * n * width),
    )
    def gather(table_hbm, idx_hbm, out_hbm, *scratch):
        idx_vs = scratch[:SC_GATHER_WAYS]
        rows_vs = scratch[SC_GATHER_WAYS:2 * SC_GATHER_WAYS]
        sem_idx, sem_rows, sem_out = scratch[2 * SC_GATHER_WAYS:]
        wid = lax.axis_index("s") * 2 + lax.axis_index("c")
        base = wid * per_worker

        @pl.loop(0, n_rounds)
        def _(i):
            offs = [base + (i * SC_GATHER_WAYS + u) * SC_CHUNK for u in range(SC_GATHER_WAYS)]
            stage = [pltpu.async_copy(idx_hbm.at[pl.ds(off, SC_CHUNK)], idx_v, sem_idx)
                     for off, idx_v in zip(offs, idx_vs)]
            for copy in stage:
                copy.wait()
            stage = [pltpu.async_copy(table_hbm.at[idx_v], rows_v, sem_rows)
                     for idx_v, rows_v in zip(idx_vs, rows_vs)]
            for copy in stage:
                copy.wait()
            stage = [pltpu.async_copy(rows_v, out_hbm.at[pl.ds(off, SC_CHUNK)], sem_out)
                     for off, rows_v in zip(offs, rows_vs)]
            for copy in stage:
                copy.wait()

    return gather(table, idx)


def _scatter_rows(srcs, dest, n_out):
    width = srcs[0].shape[1]
    starts = [0]
    for src in srcs:
        assert src.shape[0] % SC_CHUNK == 0
        starts.append(starts[-1] + src.shape[0] // SC_CHUNK)
    n_chunks = starts[-1]
    n = n_chunks * SC_CHUNK
    per_worker = -(-n_chunks // SC_WORKERS)
    mesh = plsc.VectorSubcoreMesh(core_axis_name="c", subcore_axis_name="s")

    @functools.partial(
        pl.kernel, mesh=mesh,
        out_type=jax.ShapeDtypeStruct((n_out, width), srcs[0].dtype),
        scratch_types=[pltpu.VMEM((SC_CHUNK,), jnp.int32)] * TOP_K
        + [pltpu.VMEM((SC_CHUNK, width), srcs[0].dtype),
           pltpu.SemaphoreType.DMA, pltpu.SemaphoreType.DMA],
        cost_estimate=pl.CostEstimate(flops=0, transcendentals=0,
                                      bytes_accessed=4 * (1 + TOP_K) * n * width),
    )
    def scatter(*refs):
        src_hbms = refs[:len(srcs)]
        dest_hbm, out_hbm = refs[len(srcs)], refs[len(srcs) + 1]
        idx_vs = refs[len(srcs) + 2:len(srcs) + 2 + TOP_K]
        rows_v, sem_in, sem_out = refs[-3:]
        wid = lax.axis_index("s") * 2 + lax.axis_index("c")

        @pl.loop(0, per_worker)
        def _(i):
            c = i * SC_WORKERS + wid
            for k, src_hbm in enumerate(src_hbms):

                @pl.when((c >= starts[k]) & (c < starts[k + 1]))
                def _():
                    loads = [pltpu.async_copy(
                        src_hbm.at[pl.ds((c - starts[k]) * SC_CHUNK, SC_CHUNK)], rows_v, sem_in)]
                    loads += [pltpu.async_copy(dest_hbm.at[pl.ds(j * n + c * SC_CHUNK, SC_CHUNK)],
                                               idx_v, sem_in) for j, idx_v in enumerate(idx_vs)]
                    for copy in loads:
                        copy.wait()
                    sends = [pltpu.async_copy(rows_v, out_hbm.at[idx_v], sem_out) for idx_v in idx_vs]
                    for copy in sends:
                        copy.wait()

    return scatter(*srcs, dest)


def _t5_bucket_np(rel):
    half = NUM_BUCKETS // 2
    max_exact = half // 2
    n = np.abs(rel)
    nf = np.maximum(n, 1).astype(np.float32)
    large = max_exact + (np.log(nf / max_exact) / math.log(MAX_DISTANCE / max_exact)
                         * (half - max_exact)).astype(np.int32)
    large = np.minimum(large, half - 1)
    return np.where(rel > 0, half, 0) + np.where(n < max_exact, n, large)


def _bias_tables(rel_bias, visible):
    kap = np.arange(BAND)[:, None]
    col = np.arange(GROUP * PAIR)[None, :]
    bucket = _t5_bucket_np(kap - WINDOW - col % PAIR)
    onehot = jnp.asarray(bucket[:, :, None] == np.arange(NUM_BUCKETS))
    seen = jnp.asarray(np.tile(visible, (1, GROUP)))
    rb = rel_bias.astype(jnp.float32)
    tabs = []
    for h in range(N_KV_HEADS):
        per_col = jnp.repeat(rb[:, h * GROUP:(h + 1) * GROUP], PAIR, axis=1)
        tab = jnp.sum(jnp.where(onehot, jnp.transpose(per_col)[None], 0.0), axis=-1)
        tabs.append(jnp.where(seen, tab, NEG_INF))
    return jnp.stack(tabs)


def _mixer_consts(l, norm1_g, w_in, q_norm_g, k_norm_g, rel_bias, sinks, w_pool, pool_scale,
                  w_out, norm2_g, w_router, b_router, visible, tile):
    f32, bf16 = jnp.float32, jnp.bfloat16
    q_off, k_off, v_off = POOL_WIDTH, POOL_WIDTH + ATTN_WIDTH, POOL_WIDTH + ATTN_WIDTH + KV_WIDTH
    w = w_in[l]
    w_ukv = jnp.concatenate([w[:, :q_off], w[:, k_off:]], axis=1).astype(bf16)
    w_qvt = jnp.transpose(jnp.concatenate([w[:, q_off:k_off], w[:, v_off:]], axis=1)).astype(bf16)
    lane_head = np.arange(KV_WIDTH) // HEAD_DIM
    blockdiag = jnp.asarray(lane_head[:, None] == lane_head[None, :], bf16)
    sink_rows = jnp.repeat(sinks[l].astype(f32).reshape(N_KV_HEADS, GROUP), PAIR, axis=1)
    wp = w_pool[l].astype(bf16)
    zp = jnp.zeros((POOL_GROUP, POOL_GROUP), bf16)
    w_pool_pairs = jnp.stack([jnp.block([[wp[2 * i], zp], [zp, wp[2 * i + 1]]])
                              for i in range(N_POOL_GROUPS // 2)])
    wr = w_router[l].astype(f32)
    wr_hi = wr.astype(bf16)
    wr_lo = (wr - wr_hi.astype(f32)).astype(bf16)
    wr_parts = jnp.pad(jnp.concatenate([wr_hi, wr_lo], axis=1), ((0, 0), (0, 128 - 2 * N_EXPERTS)))
    tri = jnp.asarray(np.arange(128)[:, None] < np.arange(128)[None, :], bf16)
    return [
        norm1_g[l].reshape(1, D_MODEL).astype(f32), w_ukv, w_qvt,
        q_norm_g[l].reshape(HEAD_DIM, 1).astype(f32),
        jnp.tile(k_norm_g[l].astype(f32), N_KV_HEADS).reshape(1, KV_WIDTH),
        blockdiag, _bias_tables(rel_bias, visible), sink_rows,
        w_pool_pairs, pool_scale[l].reshape(1, POOL_WIDTH).astype(f32),
        w_out[l].astype(bf16), norm2_g[l].reshape(1, D_MODEL).astype(f32),
        wr_parts, b_router[l].reshape(N_EXPERTS, 1).astype(f32), tri,
    ]


def _visibility():
    kap = np.arange(BAND)[:, None]
    rho = np.arange(PAIR)[None, :]
    kc, qc = kap // CHUNK, rho // CHUNK
    prompt = (kc >= qc) & (kc <= qc + WINDOW // CHUNK)
    return prompt


def kernel(x_prompt, x_sample, cache_k, cache_v, state_pool, norm1_g, w_in, q_norm_g, k_norm_g,
           rel_bias, sinks, w_pool, pool_scale, w_out, norm2_g, w_router, b_router,
           w_gate, b_gate, w_up, b_up, w_down, b_down):
    f32, bf16 = jnp.float32, jnp.bfloat16
    depth = w_in.shape[0]
    nb, seq, _ = x_prompt.shape
    ndb, dseq, _ = x_sample.shape
    cache_len = cache_k.shape[2]
    assert seq % MIX_TILE == 0 and cache_len == WINDOW and HIST_ROWS <= dseq <= PAIR
    n_p, n_s = nb * seq, ndb * dseq
    n_tok = n_p + n_s
    assert COMBINE_TILE % n_s == 0 and n_s % SC_CHUNK == 0

    vis_prompt = _visibility()
    vis_sample = np.broadcast_to(np.arange(BAND)[:, None] < WINDOW + dseq, (BAND, PAIR))

    nb1 = nb // 4
    nb2 = nb - nb1
    assert nb1 > 0 and nb2 > 0 and seq % COMBINE_TILE == 0

    xp, xs = x_prompt, x_sample
    outs = [[] for _ in range(6)]
    for l in range(depth):
        wl = (l, norm1_g, w_in, q_norm_g, k_norm_g, rel_bias, sinks, w_pool, pool_scale, w_out,
              norm2_g, w_router, b_router)
        consts_p = _mixer_consts(*wl, vis_prompt, MIX_SUB)
        moe_b = [b.astype(f32) for b in (b_gate[l], b_up[l], b_down[l])]
        cnt0 = jnp.zeros((N_EXPERTS, 128), f32)

        def prompt_mixer(stream0, n_streams, cnt_in):
            zk = jnp.zeros((n_streams, WINDOW, KV_WIDTH), f32)
            zu = jnp.zeros((n_streams, HIST_ROWS, POOL_WIDTH), f32)
            return _mixer_call(xp, zk, zk, zu, cnt_in, consts_p, stream0=stream0, tile=MIX_TILE,
                               sub=MIX_SUB, n_valid=MIX_TILE, pos0=0, mask_first=True)

        (x1_a, h_a, idx_a, rank_a, gate_a, cnt_a, k_a, v_a, u_a) = prompt_mixer(0, nb1, cnt0)
        xs_pad = jnp.pad(xs, ((0, 0), (0, PAIR - dseq), (0, 0)))
        uh = jnp.pad(state_pool[l], ((0, 0), (HIST_ROWS - POOL_HIST, 0), (0, 0)))
        (xs1, h_s, idx_s, rank_s, gate_s, cnt_s, k_s, v_s, u_s) = _mixer_call(
            xs_pad, cache_k[l].reshape(ndb, WINDOW, KV_WIDTH),
            cache_v[l].reshape(ndb, WINDOW, KV_WIDTH), uh, cnt0,
            _mixer_consts(*wl, vis_sample, PAIR),
            stream0=0, tile=PAIR, sub=PAIR, n_valid=dseq, pos0=PAST_LEN, mask_first=False)
        (x1_b, h_b, idx_b, rank_b, gate_b, cnt_b, k_b, v_b, u_b) = prompt_mixer(nb1, nb2, cnt_s)
        picked_b, w_bf16 = _moe_rows([h_b, h_s], [idx_b, idx_s], [rank_b, rank_s], [seq, dseq], cnt_b,
                                     moe_b, w_f32=(w_gate[l], w_up[l], w_down[l]))
        picked_a, _ = _moe_rows([h_a], [idx_a], [rank_a], [seq], cnt_a, moe_b, w_bf16=w_bf16)

        n_a, n_b = nb1 * seq, nb2 * seq
        no_buf = jnp.zeros((8, 128), f32)
        stride_a, stride_b = _pick_stride(n_a), _pick_stride(n_b + n_s)
        xp_rows = _combine_call(no_buf, x1_b.reshape(n_b, D_MODEL), gate_b, picked_b, stride_b, 0,
                                out_rows=n_p, row0=n_a, tile=COMBINE_TILE)
        xp_rows = _combine_call(xp_rows, x1_a.reshape(n_a, D_MODEL), gate_a, picked_a, stride_a, 0,
                                out_rows=n_p, row0=0, tile=COMBINE_TILE)
        xp = xp_rows.reshape(nb, seq, D_MODEL)
        xs = _combine_call(no_buf, xs1[:, :dseq].reshape(n_s, D_MODEL), gate_s, picked_b, stride_b,
                           n_b, out_rows=n_s, row0=0, tile=n_s).reshape(ndb, dseq, D_MODEL)

        outs[0].append(jnp.concatenate([k_a, k_b]).reshape(nb, WINDOW, N_KV_HEADS, HEAD_DIM))
        outs[1].append(jnp.concatenate([v_a, v_b]).reshape(nb, WINDOW, N_KV_HEADS, HEAD_DIM))
        outs[2].append(jnp.concatenate([u_a, u_b])[:, HIST_ROWS - POOL_HIST:])
        outs[3].append(k_s[:, :dseq].reshape(ndb, dseq, N_KV_HEADS, HEAD_DIM))
        outs[4].append(v_s[:, :dseq].reshape(ndb, dseq, N_KV_HEADS, HEAD_DIM))
        outs[5].append(u_s[:, HIST_ROWS - POOL_HIST:])
    return (xp, xs) + tuple(jnp.stack(o) for o in outs)


def _moe_rows(hs, idxs, ranks, n_reals, cnt, biases, w_f32=None, w_bf16=None):
    n_tok = sum(h.shape[0] for h in hs)
    n_assign = n_tok * TOP_K
    n_blocks = n_assign // EXPERT_BLOCK + N_EXPERTS
    e_ids = jnp.arange(N_EXPERTS, dtype=jnp.int32)

    def per_token(arrs):
        return jnp.concatenate(
            [jnp.transpose(a[:, :, :n], (1, 0, 2)).reshape(TOP_K, -1) for a, n in zip(arrs, n_reals)],
            axis=1)

    counts = cnt[:, 0].astype(jnp.int32)
    pcounts = (counts + EXPERT_BLOCK - 1) // EXPERT_BLOCK * EXPERT_BLOCK
    if w_bf16 is None:
        pcounts = jnp.maximum(pcounts, EXPERT_BLOCK)
    pend = jnp.cumsum(pcounts)
    pstart = pend - pcounts
    idx_all = per_token(idxs)
    dest = per_token(ranks) + jnp.sum(
        jnp.where(idx_all[None] == e_ids[:, None, None], pstart[:, None, None], 0), axis=0)
    dest = dest.reshape(-1)
    blk0 = jnp.arange(n_blocks, dtype=jnp.int32) * EXPERT_BLOCK
    block_e = jnp.minimum(jnp.sum((pend[None, :] <= blk0[:, None]).astype(jnp.int32), axis=1),
                          N_EXPERTS - 1)
    block_rows = jnp.clip(jnp.sum(jnp.where(block_e[:, None] == e_ids[None, :],
                                             (pstart + counts)[None, :], 0), axis=1) - blk0,
                          0, EXPERT_BLOCK).astype(jnp.int32)

    x_sorted = _scatter_rows(hs, dest, n_blocks * EXPERT_BLOCK)
    if w_bf16 is None:
        block_next = jnp.minimum(block_e + 1, N_EXPERTS - 1)
        block_live = (blk0 < pend[-1]).astype(jnp.int32)
        y_sorted, *w_bf16 = _expert_call_f32(
            block_e, block_rows, block_next, block_live, x_sorted,
            w_f32[0], biases[0], w_f32[1], biases[1], w_f32[2], biases[2])
    else:
        has = counts > 0
        later = has[None, :] & (e_ids[None, :] > e_ids[:, None])
        next_e = jnp.min(jnp.where(later, e_ids[None, :], N_EXPERTS), axis=1)
        next_e = jnp.where(next_e == N_EXPERTS, e_ids, next_e)
        order = jnp.cumsum(has.astype(jnp.int32)) - 1
        table = jnp.stack([next_e, order % 2], axis=0)[:, None, :]
        block_next, block_slot = jnp.sum(
            jnp.where(block_e[None, :, None] == e_ids[None, None, :], table, 0), axis=2).astype(jnp.int32)
        y_sorted = _expert_call_bf16(
            block_e, block_rows, block_next, block_slot, x_sorted,
            w_bf16[0], biases[0], w_bf16[1], biases[1], w_bf16[2], biases[2])
    stride = _pick_stride(n_tok)
    n_pick = -(-(TOP_K * stride) // GATHER_QUANT) * GATHER_QUANT
    picks = jnp.pad(dest.reshape(TOP_K, n_tok), ((0, 0), (0, stride - n_tok))).reshape(-1)
    return _gather_rows(y_sorted, jnp.pad(picks, (0, n_pick - TOP_K * stride))), w_bf16


def _pick_stride(n_tok):
    return -(-n_tok // COMBINE_TILE) * COMBINE_TILE
```

```python
import functools
import math

import numpy as np
import jax
import jax.numpy as jnp
from jax import lax
from jax.experimental import pallas as pl
from jax.experimental.pallas import tpu as pltpu
from jax.experimental.pallas import tpu_sc as plsc

D_MODEL = 1024
CHUNK = 64
POOL_WIDTH = 512
POOL_WINDOWS = (2, 4, 8, 16)
POOL_GROUP = 128
N_POOL_GROUPS = 4
POOL_HIST = 15
ATTN_WIDTH = 512
HEAD_DIM = 64
N_HEADS = 8
N_KV_HEADS = 2
GROUP = 4
KV_WIDTH = 128
WINDOW = 128
NUM_BUCKETS = 32
MAX_DISTANCE = 128
PAST_LEN = 2048
N_EXPERTS = 32
TOP_K = 4
SWIGLU_LIMIT = 7.0
SWIGLU_ALPHA = 1.702
EPS = 1e-5
NEG_INF = -1e30
ATTN_SCALE = HEAD_DIM ** -0.5

PAIR = 2 * CHUNK
BAND = PAIR + WINDOW
HIST_ROWS = 16
GATE_COLS = 8
MIX_TILE = 1024
MIX_SUB = 1024
EXPERT_BLOCK = 1024
FFN_CHUNK = 256
COMBINE_TILE = 1024
SC_WORKERS = 32
SC_CHUNK = 64
SC_GATHER_WAYS = 3
GATHER_QUANT = SC_WORKERS * SC_CHUNK * SC_GATHER_WAYS
VMEM_LIMIT = 56 * 1024 * 1024


def _pack_bf16_pair(a, b):
    ab = lax.bitcast_convert_type(a.astype(jnp.bfloat16).astype(jnp.float32), jnp.uint32)
    bb = lax.bitcast_convert_type(b.astype(jnp.bfloat16).astype(jnp.float32), jnp.uint32)
    return (ab >> 16) | (bb & jnp.uint32(0xFFFF0000))


def _unpack_bf16_pair(w):
    a = lax.bitcast_convert_type(w << 16, jnp.float32).astype(jnp.bfloat16)
    b = lax.bitcast_convert_type(w & jnp.uint32(0xFFFF0000), jnp.float32).astype(jnp.bfloat16)
    return a, b


def _mixer_kernel(x_ref, kh_ref, vh_ref, uh_ref, cnt_in_ref,
                  g1_ref, wukv_ref, wqvt_ref, qg_ref, kg_ref, bd_ref, bias_ref, sink_ref,
                  wpool_ref, pscale_ref, wout_ref, g2_ref, wr_ref, br_ref, tri_ref,
                  x1_ref, hp_ref, idx_ref, rank_ref, gate_ref, cnt_ref, ko_ref, vo_ref, uo_ref,
                  qt_s, kb_s, vt_s, ub_s, mix_s, cnt_s,
                  *, tile, sub, n_valid, pos0, mask_first):
    b = pl.program_id(0)
    s = pl.program_id(1)
    bf16 = jnp.bfloat16
    f32 = jnp.float32

    @pl.when((b == 0) & (s == 0))
    def _():
        cnt_s[...] = cnt_in_ref[...]

    @pl.when(s == 0)
    def _():
        kb_s[0:WINDOW, :] = kh_ref[...].astype(bf16)
        vt_s[:, 0:WINDOW] = jnp.transpose(vh_ref[...]).astype(bf16)
        ub_s[0:HIST_ROWS, :] = uh_ref[...]

    @pl.when(s > 0)
    def _():
        kb_s[0:WINDOW, :] = kb_s[tile:tile + WINDOW, :]
        vt_s[:, 0:WINDOW] = vt_s[:, tile:tile + WINDOW]
        ub_s[0:HIST_ROWS, :] = ub_s[tile:tile + HIST_ROWS, :]

    for r0 in range(0, tile, sub):
        _mixer_rows(r0, s, x_ref, g1_ref, wukv_ref, wqvt_ref, qg_ref, kg_ref, bd_ref, bias_ref,
                    sink_ref, wpool_ref, pscale_ref, wout_ref, g2_ref, wr_ref, br_ref, tri_ref,
                    x1_ref, hp_ref, idx_ref, rank_ref, gate_ref, cnt_ref, ko_ref, vo_ref, uo_ref,
                    qt_s, kb_s, vt_s, ub_s, mix_s, cnt_s,
                    tile=tile, sub=sub, n_valid=n_valid, pos0=pos0, mask_first=mask_first)


def _mixer_rows(r0, s, x_ref, g1_ref, wukv_ref, wqvt_ref, qg_ref, kg_ref, bd_ref, bias_ref,
                sink_ref, wpool_ref, pscale_ref, wout_ref, g2_ref, wr_ref, br_ref, tri_ref,
                x1_ref, hp_ref, idx_ref, rank_ref, gate_ref, cnt_ref, ko_ref, vo_ref, uo_ref,
                qt_s, kb_s, vt_s, ub_s, mix_s, cnt_s, *, tile, sub, n_valid, pos0, mask_first):
    bf16 = jnp.bfloat16
    f32 = jnp.float32
    rows = slice(r0, r0 + sub)

    x = x_ref[rows, :]
    xn = (x * lax.rsqrt(jnp.mean(x * x, axis=-1, keepdims=True) + EPS) * g1_ref[...]).astype(bf16)
    z = jnp.dot(xn, wukv_ref[...], preferred_element_type=f32)
    zt = lax.dot_general(wqvt_ref[...], xn, (((1,), (1,)), ((), ())),
                         preferred_element_type=f32)
    u = z[:, 0:POOL_WIDTH]
    kz = z[:, POOL_WIDTH:POOL_WIDTH + KV_WIDTH]
    v = z[:, POOL_WIDTH + KV_WIDTH:]

    ksq = kz * kz
    kss = jnp.dot(ksq.astype(bf16), bd_ref[...], preferred_element_type=f32)
    kn = kz * lax.rsqrt(kss * (1.0 / HEAD_DIM) + EPS) * kg_ref[...]
    kb_s[WINDOW + r0:WINDOW + r0 + sub, :] = kn.astype(bf16)
    vt_s[:, WINDOW + r0:WINDOW + r0 + sub] = zt[ATTN_WIDTH:, :].astype(bf16)
    ub_s[HIST_ROWS + r0:HIST_ROWS + r0 + sub, :] = u

    if r0 + sub == tile:
        row0 = max(n_valid, WINDOW) - WINDOW - r0
        ko_ref[...] = kn[row0:row0 + WINDOW, :]
        vo_ref[...] = v[row0:row0 + WINDOW, :]
        uo_ref[...] = u[n_valid - HIST_ROWS - r0:n_valid - r0, :]

    for hd in range(N_HEADS):
        qh = zt[hd * HEAD_DIM:(hd + 1) * HEAD_DIM, :]
        ss = jnp.sum(qh * qh, axis=0, keepdims=True)
        qn = qh * (lax.rsqrt(ss * (1.0 / HEAD_DIM) + EPS) * ATTN_SCALE) * qg_ref[...]
        qt_s[hd * HEAD_DIM:(hd + 1) * HEAD_DIM, rows] = qn.astype(bf16)

    pos = pos0 + s * tile + r0 + lax.broadcasted_iota(jnp.int32, (sub, 1), 0)
    diffs = []
    for g, w in enumerate(POOL_WINDOWS):
        e = ub_s[r0:r0 + HIST_ROWS + sub, g * POOL_GROUP:(g + 1) * POOL_GROUP]
        acc = e
        for lvl in range(g + 1):
            acc = acc + pltpu.roll(acc, 2 ** lvl, axis=0)
        inv_cnt = 1.0 / jnp.minimum(pos + 1, w).astype(f32)
        diffs.append((acc[HIST_ROWS:, :] * inv_cnt - e[HIST_ROWS:, :]).astype(bf16))
    for g2 in range(N_POOL_GROUPS // 2):
        cols = slice(2 * g2 * POOL_GROUP, 2 * (g2 + 1) * POOL_GROUP)
        d = jnp.concatenate(diffs[2 * g2:2 * g2 + 2], axis=1)
        y = jnp.dot(d, wpool_ref[g2], preferred_element_type=f32) * pscale_ref[:, cols]
        mix_s[rows, cols] = y.astype(bf16)

    zeros_q = jnp.zeros((HEAD_DIM, GROUP * PAIR), bf16)
    for p in range(r0 // PAIR, (r0 + sub) // PAIR):
        k_band = kb_s[p * PAIR:p * PAIR + BAND, :]
        o_parts = []
        rhs_heads = []
        for h in range(N_KV_HEADS):
            qcat = jnp.concatenate(
                [qt_s[(h * GROUP + g) * HEAD_DIM:(h * GROUP + g + 1) * HEAD_DIM,
                      p * PAIR:(p + 1) * PAIR] for g in range(GROUP)], axis=1)
            rhs_heads.append(jnp.concatenate([qcat, zeros_q] if h == 0 else [zeros_q, qcat], axis=0))
        st_heads = jnp.dot(k_band, jnp.concatenate(rhs_heads, axis=1),
                           preferred_element_type=f32)
        for h in range(N_KV_HEADS):
            st = st_heads[:, h * GROUP * PAIR:(h + 1) * GROUP * PAIR] + bias_ref[h]
            if mask_first and p == 0:
                krow = lax.broadcasted_iota(jnp.int32, (BAND, 1), 0)
                st = jnp.where((krow >= WINDOW) | (s > 0), st, NEG_INF)
            sink = sink_ref[h:h + 1, :]
            m = jnp.maximum(jnp.max(st, axis=0, keepdims=True), sink)
            ex = jnp.exp(st - m)
            den = jnp.sum(ex, axis=0, keepdims=True) + jnp.exp(sink - m)
            v_band = vt_s[h * HEAD_DIM:(h + 1) * HEAD_DIM, p * PAIR:p * PAIR + BAND]
            ot = jnp.dot(v_band, ex.astype(bf16), preferred_element_type=f32) / den
            for g in range(GROUP):
                o_parts.append(ot[:, g * PAIR:(g + 1) * PAIR])
        o_all = jnp.concatenate(o_parts, axis=0)
        mix_s[p * PAIR:(p + 1) * PAIR, POOL_WIDTH:] = jnp.transpose(o_all).astype(bf16)

    x1 = x + jnp.dot(mix_s[rows, :], wout_ref[...], preferred_element_type=f32)
    x1_ref[rows, :] = x1

    n_real = min(sub, n_valid - r0)
    hn = x1 * lax.rsqrt(jnp.mean(x1 * x1, axis=-1, keepdims=True) + EPS) * g2_ref[...]
    hp_ref[r0:r0 + n_real, :] = _pack_bf16_pair(hn[0:n_real, 0:D_MODEL // 2],
                                                hn[0:n_real, D_MODEL // 2:])
    h_hi = hn.astype(bf16)
    h_lo = (hn - h_hi.astype(f32)).astype(bf16)
    both = jnp.dot(jnp.concatenate([h_hi, h_lo], axis=0), wr_ref[...],
                   preferred_element_type=f32)
    parts = jnp.transpose(both[0:sub, :] + both[sub:, :])
    lt = parts[0:N_EXPERTS, :] + parts[N_EXPERTS:2 * N_EXPERTS, :] + br_ref[...]

    eidx = lax.broadcasted_iota(jnp.int32, (N_EXPERTS, sub), 0).astype(f32)
    vals, hots = [], []
    for j in range(TOP_K):
        m = jnp.max(lt, axis=0, keepdims=True)
        sel = jnp.min(jnp.where(lt == m, eidx, float(N_EXPERTS)), axis=0, keepdims=True)
        hot = eidx == sel
        lt = jnp.where(hot, -jnp.inf, lt)
        idx_ref[j:j + 1, rows] = sel.astype(jnp.int32)
        vals.append(m)
        hots.append(hot)
    exps = [jnp.exp(vv - vals[0]) for vv in vals]
    esum = exps[0] + exps[1] + exps[2] + exps[3]
    grow = lax.broadcasted_iota(jnp.int32, (GATE_COLS, sub), 0)
    gmat = jnp.zeros((GATE_COLS, sub), f32)
    for j in range(TOP_K):
        gmat = jnp.where(grow == j, exps[j] / esum, gmat)
    gmat = jnp.concatenate([gmat, jnp.zeros((128 - GATE_COLS, sub), f32)], axis=0)
    gate_ref[r0:r0 + n_real, :] = jnp.transpose(gmat)[0:n_real, 0:GATE_COLS]

    chosen_f = sum(jnp.where(hot, 1.0, 0.0) for hot in hots)
    if n_real < sub:
        lane = lax.broadcasted_iota(jnp.int32, (N_EXPERTS, sub), 1)
        chosen_f = jnp.where(lane < n_real, chosen_f, 0.0)
    n_lane_blocks = sub // 128
    blocks = [chosen_f[:, k * 128:(k + 1) * 128] for k in range(n_lane_blocks)]
    inside = jnp.dot(jnp.concatenate(blocks, axis=0).astype(bf16), tri_ref[...],
                     preferred_element_type=f32)
    offset = cnt_s[:, 0:1]
    pieces = []
    for k in range(n_lane_blocks):
        pieces.append(inside[k * N_EXPERTS:(k + 1) * N_EXPERTS, :] + offset)
        offset = offset + jnp.sum(blocks[k], axis=1, keepdims=True)
    base = jnp.concatenate(pieces, axis=1)
    for j in range(TOP_K):
        rank_ref[j:j + 1, rows] = jnp.sum(jnp.where(hots[j], base, 0.0), axis=0,
                                          keepdims=True).astype(jnp.int32)
    cnt_new = jnp.broadcast_to(offset, (N_EXPERTS, 128))
    cnt_s[...] = cnt_new
    cnt_ref[...] = cnt_new


def _mixer_call(x, k_hist, v_hist, u_hist, cnt_in, consts, *, stream0, tile, sub, n_valid, pos0,
                mask_first):
    seq = x.shape[1]
    nb = k_hist.shape[0]
    n_tiles = seq // tile
    f32 = jnp.float32
    assert n_valid % 8 == 0 and tile % sub == 0 and sub % PAIR == 0
    assert n_valid == tile or sub == tile

    def full(a):
        nd = a.ndim
        return pl.BlockSpec(a.shape, lambda b, s, _nd=nd: (0,) * _nd)

    in_specs = [
        pl.BlockSpec((None, tile, D_MODEL), lambda b, s: (stream0 + b, s, 0)),
        pl.BlockSpec((None, WINDOW, KV_WIDTH), lambda b, s: (b, 0, 0)),
        pl.BlockSpec((None, WINDOW, KV_WIDTH), lambda b, s: (b, 0, 0)),
        pl.BlockSpec((None, HIST_ROWS, POOL_WIDTH), lambda b, s: (b, 0, 0)),
        full(cnt_in),
    ] + [full(c) for c in consts]
    out_shape = [
        jax.ShapeDtypeStruct((nb, seq, D_MODEL), f32),
        jax.ShapeDtypeStruct((nb * n_tiles * n_valid, D_MODEL // 2), jnp.uint32),
        jax.ShapeDtypeStruct((nb, TOP_K, seq), jnp.int32),
        jax.ShapeDtypeStruct((nb, TOP_K, seq), jnp.int32),
        jax.ShapeDtypeStruct((nb * n_tiles * n_valid, GATE_COLS), f32),
        jax.ShapeDtypeStruct((N_EXPERTS, 128), f32),
        jax.ShapeDtypeStruct((nb, WINDOW, KV_WIDTH), f32),
        jax.ShapeDtypeStruct((nb, WINDOW, KV_WIDTH), f32),
        jax.ShapeDtypeStruct((nb, HIST_ROWS, POOL_WIDTH), f32),
    ]
    out_specs = [
        pl.BlockSpec((None, tile, D_MODEL), lambda b, s: (b, s, 0)),
        pl.BlockSpec((n_valid, D_MODEL // 2), lambda b, s: (b * n_tiles + s, 0)),
        pl.BlockSpec((None, TOP_K, tile), lambda b, s: (b, 0, s)),
        pl.BlockSpec((None, TOP_K, tile), lambda b, s: (b, 0, s)),
        pl.BlockSpec((n_valid, GATE_COLS), lambda b, s: (b * n_tiles + s, 0)),
        pl.BlockSpec((N_EXPERTS, 128), lambda b, s: (0, 0)),
        pl.BlockSpec((None, WINDOW, KV_WIDTH), lambda b, s: (b, 0, 0)),
        pl.BlockSpec((None, WINDOW, KV_WIDTH), lambda b, s: (b, 0, 0)),
        pl.BlockSpec((None, HIST_ROWS, POOL_WIDTH), lambda b, s: (b, 0, 0)),
    ]
    scratch = [
        pltpu.VMEM((ATTN_WIDTH, tile), jnp.bfloat16),
        pltpu.VMEM((WINDOW + tile, KV_WIDTH), jnp.bfloat16),
        pltpu.VMEM((KV_WIDTH, WINDOW + tile), jnp.bfloat16),
        pltpu.VMEM((HIST_ROWS + tile, POOL_WIDTH), f32),
        pltpu.VMEM((tile, D_MODEL), jnp.bfloat16),
        pltpu.VMEM((N_EXPERTS, 128), f32),
    ]
    kern = functools.partial(_mixer_kernel, tile=tile, sub=sub, n_valid=n_valid, pos0=pos0,
                             mask_first=mask_first)
    return pl.pallas_call(
        kern,
        grid=(nb, n_tiles),
        in_specs=in_specs,
        out_specs=out_specs,
        out_shape=out_shape,
        scratch_shapes=scratch,
        compiler_params=pltpu.CompilerParams(
            dimension_semantics=("arbitrary", "arbitrary"),
            vmem_limit_bytes=VMEM_LIMIT),
        name="mixer",
    )(x, k_hist, v_hist, u_hist, cnt_in, *consts)


def _expert_kernel_f32(be_ref, nv_ref, nx_ref, lv_ref, xs_ref, wg_hbm, bg_ref, wu_hbm, bu_ref,
                       wd_hbm, bd_ref, ys_ref, wg16_hbm, wu16_hbm, wd16_hbm,
                       stage_s, wg_s, wu_s, wd_s, in_sems, out_sems, *, n_blocks):
    i = pl.program_id(0)
    expert = be_ref[i]
    bf16 = jnp.bfloat16

    def copies_in(e):
        return [pltpu.make_async_copy(w_hbm.at[e], stage_s.at[k], in_sems.at[k])
                for k, w_hbm in enumerate((wg_hbm, wu_hbm, wd_hbm))]

    def copies_out(e):
        return [pltpu.make_async_copy(w_s, w16_hbm.at[e], out_sems.at[k])
                for k, (w_s, w16_hbm) in enumerate(((wg_s, wg16_hbm), (wu_s, wu16_hbm),
                                                    (wd_s, wd16_hbm)))]

    @pl.when(i == 0)
    def _():
        for copy in copies_in(expert):
            copy.start()

    @pl.when((lv_ref[i] > 0) & ((i == 0) | (expert != be_ref[jnp.maximum(i - 1, 0)])))
    def _():
        for copy in copies_in(expert):
            copy.wait()

        @pl.when(i > 0)
        def _():
            for copy in copies_out(expert):
                copy.wait()

        wg_s[...] = stage_s[0].astype(bf16)
        wu_s[...] = stage_s[1].astype(bf16)
        wd_s[...] = stage_s[2].astype(bf16)
        for copy in copies_out(expert):
            copy.start()

        @pl.when(nx_ref[i] != expert)
        def _():
            for copy in copies_in(nx_ref[i]):
                copy.start()

    _ffn_block(nv_ref[i], expert, xs_ref, ys_ref, wg_s, wu_s, wd_s, bg_ref, bu_ref, bd_ref)

    @pl.when(i == n_blocks - 1)
    def _():
        for copy in copies_out(expert):
            copy.wait()


def _expert_kernel_bf16(be_ref, nv_ref, nx_ref, sl_ref, xs_ref, wg_hbm, bg_ref, wu_hbm, bu_ref,
                        wd_hbm, bd_ref, ys_ref, w_s, sems):
    i = pl.program_id(0)
    n_rows = nv_ref[i]
    expert = be_ref[i]
    slot = sl_ref[i]

    def copies(e, to_slot):
        return [pltpu.make_async_copy(w_hbm.at[e], w_s.at[to_slot, k], sems.at[to_slot, k])
                for k, w_hbm in enumerate((wg_hbm, wu_hbm, wd_hbm))]

    @pl.when((i == 0) & (n_rows > 0))
    def _():
        for copy in copies(expert, slot):
            copy.start()

    @pl.when((n_rows > 0) & ((i == 0) | (expert != be_ref[jnp.maximum(i - 1, 0)])))
    def _():
        for copy in copies(expert, slot):
            copy.wait()

        @pl.when(nx_ref[i] != expert)
        def _():
            for copy in copies(nx_ref[i], 1 - slot):
                copy.start()

    _ffn_block(n_rows, expert, xs_ref, ys_ref, w_s.at[slot, 0], w_s.at[slot, 1], w_s.at[slot, 2],
               bg_ref, bu_ref, bd_ref)


def _ffn_block(n_rows, expert, xs_ref, ys_ref, wg_s, wu_s, wd_s, bg_all_ref, bu_all_ref, bd_all_ref):
    bf16 = jnp.bfloat16

    def ffn(rows):
        f32 = jnp.float32
        half = D_MODEL // 2
        bg_ref, bu_ref, bd_ref = (b.at[pl.ds(expert, 1), :]
                                  for b in (bg_all_ref, bu_all_ref, bd_all_ref))
        for r0 in range(0, rows, FFN_CHUNK):
            row = r0 + lax.broadcasted_iota(jnp.int32, (FFN_CHUNK, 1), 0)
            words = jnp.where(row < n_rows, xs_ref[r0:r0 + FFN_CHUNK, :],
                              jnp.uint32(0))
            xa, xb = _unpack_bf16_pair(words)
            a = (jnp.dot(xa, wg_s[0:half, :], preferred_element_type=f32)
                 + jnp.dot(xb, wg_s[half:, :], preferred_element_type=f32) + bg_ref[...])
            bb = (jnp.dot(xa, wu_s[0:half, :], preferred_element_type=f32)
                  + jnp.dot(xb, wu_s[half:, :], preferred_element_type=f32) + bu_ref[...])
            a = jnp.minimum(a, SWIGLU_LIMIT)
            bb = jnp.clip(bb, -SWIGLU_LIMIT, SWIGLU_LIMIT)
            act = a * (1.0 / (1.0 + jnp.exp(-SWIGLU_ALPHA * a))) * (bb + 1.0)
            y = jnp.dot(act.astype(bf16), wd_s[...], preferred_element_type=f32) + bd_ref[...]
            ys_ref[r0:r0 + FFN_CHUNK, :] = _pack_bf16_pair(y[:, 0:half], y[:, half:])
        if rows < EXPERT_BLOCK:
            ys_ref[rows:, :] = jnp.zeros((EXPERT_BLOCK - rows, half), jnp.uint32)

    quarter = EXPERT_BLOCK // 4
    for q in range(1, 5):
        @pl.when((n_rows > (q - 1) * quarter) & (n_rows <= q * quarter))
        def _(q=q):
            ffn(q * quarter)

    @pl.when(n_rows == 0)
    def _():
        ys_ref[...] = jnp.zeros_like(ys_ref)


def _expert_call_f32(block_e, block_rows, block_next, block_live, xs, wg, bg, wu, bu, wd, bd):
    n_slots = xs.shape[0]
    n_blocks = n_slots // EXPERT_BLOCK
    any_spec = pl.BlockSpec(memory_space=pl.ANY)
    b_spec = pl.BlockSpec((N_EXPERTS, D_MODEL), lambda i, be, nv, nx, lv: (0, 0))
    x_spec = pl.BlockSpec((EXPERT_BLOCK, D_MODEL // 2), lambda i, be, nv, nx, lv: (i, 0))
    grid_spec = pltpu.PrefetchScalarGridSpec(
        num_scalar_prefetch=4,
        grid=(n_blocks,),
        in_specs=[x_spec, any_spec, b_spec, any_spec, b_spec, any_spec, b_spec],
        out_specs=[x_spec, any_spec, any_spec, any_spec],
        scratch_shapes=[pltpu.VMEM((3, D_MODEL, D_MODEL), jnp.float32)]
        + [pltpu.VMEM((D_MODEL, D_MODEL), jnp.bfloat16)] * 3
        + [pltpu.SemaphoreType.DMA((3,)), pltpu.SemaphoreType.DMA((3,))],
    )
    w16 = jax.ShapeDtypeStruct((N_EXPERTS, D_MODEL, D_MODEL), jnp.bfloat16)
    return pl.pallas_call(
        functools.partial(_expert_kernel_f32, n_blocks=n_blocks),
        grid_spec=grid_spec,
        out_shape=[jax.ShapeDtypeStruct((n_slots, D_MODEL // 2), jnp.uint32), w16, w16, w16],
        compiler_params=pltpu.CompilerParams(
            dimension_semantics=("arbitrary",),
            vmem_limit_bytes=VMEM_LIMIT),
        name="experts_f32",
    )(block_e, block_rows, block_next, block_live, xs, wg, bg, wu, bu, wd, bd)


def _expert_call_bf16(block_e, block_rows, block_next, block_slot, xs, wg, bg, wu, bu, wd, bd):
    n_slots = xs.shape[0]
    any_spec = pl.BlockSpec(memory_space=pl.ANY)
    b_spec = pl.BlockSpec((N_EXPERTS, D_MODEL), lambda i, be, nv, nx, sl: (0, 0))
    x_spec = pl.BlockSpec((EXPERT_BLOCK, D_MODEL // 2), lambda i, be, nv, nx, sl: (i, 0))
    grid_spec = pltpu.PrefetchScalarGridSpec(
        num_scalar_prefetch=4,
        grid=(n_slots // EXPERT_BLOCK,),
        in_specs=[x_spec, any_spec, b_spec, any_spec, b_spec, any_spec, b_spec],
        out_specs=x_spec,
        scratch_shapes=[pltpu.VMEM((2, 3, D_MODEL, D_MODEL), jnp.bfloat16),
                        pltpu.SemaphoreType.DMA((2, 3))],
    )
    return pl.pallas_call(
        _expert_kernel_bf16,
        grid_spec=grid_spec,
        out_shape=jax.ShapeDtypeStruct((n_slots, D_MODEL // 2), jnp.uint32),
        compiler_params=pltpu.CompilerParams(
            dimension_semantics=("arbitrary",),
            vmem_limit_bytes=VMEM_LIMIT),
        name="experts_bf16",
    )(block_e, block_rows, block_next, block_slot, xs, wg, bg, wu, bu, wd, bd)


def _combine_kernel(out_buf_ref, x1_ref, g_ref, y0_ref, y1_ref, y2_ref, y3_ref, o_ref):
    del out_buf_ref
    g = g_ref[...]
    half = D_MODEL // 2
    lo, hi = x1_ref[:, 0:half], x1_ref[:, half:]
    for j, y_ref in enumerate((y0_ref, y1_ref, y2_ref, y3_ref)):
        w = y_ref[...]
        gj = g[:, j:j + 1]
        lo = lo + gj * lax.bitcast_convert_type(w << 16, jnp.float32)
        hi = hi + gj * lax.bitcast_convert_type(w & jnp.uint32(0xFFFF0000), jnp.float32)
    o_ref[:, 0:half] = lo
    o_ref[:, half:] = hi


def _combine_call(out_buf, x1, gates, picked, stride, tok0, *, out_rows, row0, tile):
    n = x1.shape[0]
    aliased = out_buf.shape == (out_rows, D_MODEL)
    assert n % tile == 0 and row0 % tile == 0
    y_specs = []
    for j in range(TOP_K):
        assert (j * stride + tok0) % tile == 0
        base = (j * stride + tok0) // tile
        y_specs.append(pl.BlockSpec((tile, D_MODEL // 2), lambda i, _b=base: (_b + i, 0)))
    return pl.pallas_call(
        _combine_kernel,
        grid=(n // tile,),
        in_specs=[pl.BlockSpec(memory_space=pl.ANY),
                  pl.BlockSpec((tile, D_MODEL), lambda i: (i, 0)),
                  pl.BlockSpec((tile, GATE_COLS), lambda i: (i, 0))] + y_specs,
        out_specs=pl.BlockSpec((tile, D_MODEL), lambda i: (row0 // tile + i, 0)),
        out_shape=jax.ShapeDtypeStruct((out_rows, D_MODEL), jnp.float32),
        input_output_aliases={0: 0} if aliased else {},
        compiler_params=pltpu.CompilerParams(dimension_semantics=("arbitrary",),
                                             vmem_limit_bytes=VMEM_LIMIT),
        name="combine",
    )(out_buf, x1, gates, picked, picked, picked, picked)


def _gather_rows(table, idx):
    n = idx.shape[0]
    width = table.shape[1]
    assert n % GATHER_QUANT == 0
    per_worker = n // SC_WORKERS
    n_rounds = per_worker // (SC_CHUNK * SC_GATHER_WAYS)
    mesh = plsc.VectorSubcoreMesh(core_axis_name="c", subcore_axis_name="s")

    @functools.partial(
        pl.kernel, mesh=mesh,
        out_type=jax.ShapeDtypeStruct((n, width), table.dtype),
        scratch_types=[pltpu.VMEM((SC_CHUNK,), jnp.int32)] * SC_GATHER_WAYS
        + [pltpu.VMEM((SC_CHUNK, width), table.dtype)] * SC_GATHER_WAYS
        + [pltpu.SemaphoreType.DMA] * 3,
        cost_estimate=pl.CostEstimate(flops=0, transcendentals=0, bytes_accessed=8 * n * width),
    )
    def gather(table_hbm, idx_hbm, out_hbm, *scratch):
        idx_vs = scratch[:SC_GATHER_WAYS]
        rows_vs = scratch[SC_GATHER_WAYS:2 * SC_GATHER_WAYS]
        sem_idx, sem_rows, sem_out = scratch[2 * SC_GATHER_WAYS:]
        wid = lax.axis_index("s") * 2 + lax.axis_index("c")
        base = wid * per_worker

        @pl.loop(0, n_rounds)
        def _(i):
            offs = [base + (i * SC_GATHER_WAYS + u) * SC_CHUNK for u in range(SC_GATHER_WAYS)]
            stage = [pltpu.async_copy(idx_hbm.at[pl.ds(off, SC_CHUNK)], idx_v, sem_idx)
                     for off, idx_v in zip(offs, idx_vs)]
            for copy in stage:
                copy.wait()
            stage = [pltpu.async_copy(table_hbm.at[idx_v], rows_v, sem_rows)
                     for idx_v, rows_v in zip(idx_vs, rows_vs)]
            for copy in stage:
                copy.wait()
            stage = [pltpu.async_copy(rows_v, out_hbm.at[pl.ds(off, SC_CHUNK)], sem_out)
                     for off, rows_v in zip(offs, rows_vs)]
            for copy in stage:
                copy.wait()

    return gather(table, idx)


def _scatter_rows(srcs, dest, n_out):
    width = srcs[0].shape[1]
    starts = [0]
    for src in srcs:
        assert src.shape[0] % SC_CHUNK == 0
        starts.append(starts[-1] + src.shape[0] // SC_CHUNK)
    n_chunks = starts[-1]
    n = n_chunks * SC_CHUNK
    per_worker = -(-n_chunks // SC_WORKERS)
    mesh = plsc.VectorSubcoreMesh(core_axis_name="c", subcore_axis_name="s")

    @functools.partial(
        pl.kernel, mesh=mesh,
        out_type=jax.ShapeDtypeStruct((n_out, width), srcs[0].dtype),
        scratch_types=[pltpu.VMEM((SC_CHUNK,), jnp.int32)] * TOP_K
        + [pltpu.VMEM((SC_CHUNK, width), srcs[0].dtype),
           pltpu.SemaphoreType.DMA, pltpu.SemaphoreType.DMA],
        cost_estimate=pl.CostEstimate(flops=0, transcendentals=0,
                                      bytes_accessed=4 * (1 + TOP_K) * n * width),
    )
    def scatter(*refs):
        src_hbms = refs[:len(srcs)]
        dest_hbm, out_hbm = refs[len(srcs)], refs[len(srcs) + 1]
        idx_vs = refs[len(srcs) + 2:len(srcs) + 2 + TOP_K]
        rows_v, sem_in, sem_out = refs[-3:]
        wid = lax.axis_index("s") * 2 + lax.axis_index("c")

        @pl.loop(0, per_worker)
        def _(i):
            c = i * SC_WORKERS + wid
            for k, src_hbm in enumerate(src_hbms):

                @pl.when((c >= starts[k]) & (c < starts[k + 1]))
                def _():
                    loads = [pltpu.async_copy(
                        src_hbm.at[pl.ds((c - starts[k]) * SC_CHUNK, SC_CHUNK)], rows_v, sem_in)]
                    loads += [pltpu.async_copy(dest_hbm.at[pl.ds(j * n + c * SC_CHUNK, SC_CHUNK)],
                                               idx_v, sem_in) for j, idx_v in enumerate(idx_vs)]
                    for copy in loads:
                        copy.wait()
                    sends = [pltpu.async_copy(rows_v, out_hbm.at[idx_v], sem_out) for idx_v in idx_vs]
                    for copy in sends:
                        copy.wait()

    return scatter(*srcs, dest)


def _t5_bucket_np(rel):
    half = NUM_BUCKETS // 2
    max_exact = half // 2
    n = np.abs(rel)
    nf = np.maximum(n, 1).astype(np.float32)
    large = max_exact + (np.log(nf / max_exact) / math.log(MAX_DISTANCE / max_exact)
                         * (half - max_exact)).astype(np.int32)
    large = np.minimum(large, half - 1)
    return np.where(rel > 0, half, 0) + np.where(n < max_exact, n, large)


def _bias_tables(rel_bias, visible):
    kap = np.arange(BAND)[:, None]
    col = np.arange(GROUP * PAIR)[None, :]
    bucket = _t5_bucket_np(kap - WINDOW - col % PAIR)
    onehot = jnp.asarray(bucket[:, :, None] == np.arange(NUM_BUCKETS))
    seen = jnp.asarray(np.tile(visible, (1, GROUP)))
    rb = rel_bias.astype(jnp.float32)
    tabs = []
    for h in range(N_KV_HEADS):
        per_col = jnp.repeat(rb[:, h * GROUP:(h + 1) * GROUP], PAIR, axis=1)
        tab = jnp.sum(jnp.where(onehot, jnp.transpose(per_col)[None], 0.0), axis=-1)
        tabs.append(jnp.where(seen, tab, NEG_INF))
    return jnp.stack(tabs)


def _mixer_consts(l, norm1_g, w_in, q_norm_g, k_norm_g, rel_bias, sinks, w_pool, pool_scale,
                  w_out, norm2_g, w_router, b_router, visible, tile):
    f32, bf16 = jnp.float32, jnp.bfloat16
    q_off, k_off, v_off = POOL_WIDTH, POOL_WIDTH + ATTN_WIDTH, POOL_WIDTH + ATTN_WIDTH + KV_WIDTH
    w = w_in[l]
    w_ukv = jnp.concatenate([w[:, :q_off], w[:, k_off:]], axis=1).astype(bf16)
    w_qvt = jnp.transpose(jnp.concatenate([w[:, q_off:k_off], w[:, v_off:]], axis=1)).astype(bf16)
    lane_head = np.arange(KV_WIDTH) // HEAD_DIM
    blockdiag = jnp.asarray(lane_head[:, None] == lane_head[None, :], bf16)
    sink_rows = jnp.repeat(sinks[l].astype(f32).reshape(N_KV_HEADS, GROUP), PAIR, axis=1)
    wp = w_pool[l].astype(bf16)
    zp = jnp.zeros((POOL_GROUP, POOL_GROUP), bf16)
    w_pool_pairs = jnp.stack([jnp.block([[wp[2 * i], zp], [zp, wp[2 * i + 1]]])
                              for i in range(N_POOL_GROUPS // 2)])
    wr = w_router[l].astype(f32)
    wr_hi = wr.astype(bf16)
    wr_lo = (wr - wr_hi.astype(f32)).astype(bf16)
    wr_parts = jnp.pad(jnp.concatenate([wr_hi, wr_lo], axis=1), ((0, 0), (0, 128 - 2 * N_EXPERTS)))
    tri = jnp.asarray(np.arange(128)[:, None] < np.arange(128)[None, :], bf16)
    return [
        norm1_g[l].reshape(1, D_MODEL).astype(f32), w_ukv, w_qvt,
        q_norm_g[l].reshape(HEAD_DIM, 1).astype(f32),
        jnp.tile(k_norm_g[l].astype(f32), N_KV_HEADS).reshape(1, KV_WIDTH),
        blockdiag, _bias_tables(rel_bias, visible), sink_rows,
        w_pool_pairs, pool_scale[l].reshape(1, POOL_WIDTH).astype(f32),
        w_out[l].astype(bf16), norm2_g[l].reshape(1, D_MODEL).astype(f32),
        wr_parts, b_router[l].reshape(N_EXPERTS, 1).astype(f32), tri,
    ]


def _visibility():
    kap = np.arange(BAND)[:, None]
    rho = np.arange(PAIR)[None, :]
    kc, qc = kap // CHUNK, rho // CHUNK
    prompt = (kc >= qc) & (kc <= qc + WINDOW // CHUNK)
    return prompt


def kernel(x_prompt, x_sample, cache_k, cache_v, state_pool, norm1_g, w_in, q_norm_g, k_norm_g,
           rel_bias, sinks, w_pool, pool_scale, w_out, norm2_g, w_router, b_router,
           w_gate, b_gate, w_up, b_up, w_down, b_down):
    f32, bf16 = jnp.float32, jnp.bfloat16
    depth = w_in.shape[0]
    nb, seq, _ = x_prompt.shape
    ndb, dseq, _ = x_sample.shape
    cache_len = cache_k.shape[2]
    assert seq % MIX_TILE == 0 and cache_len == WINDOW and HIST_ROWS <= dseq <= PAIR
    n_p, n_s = nb * seq, ndb * dseq
    n_tok = n_p + n_s
    assert COMBINE_TILE % n_s == 0 and n_s % SC_CHUNK == 0

    vis_prompt = _visibility()
    vis_sample = np.broadcast_to(np.arange(BAND)[:, None] < WINDOW + dseq, (BAND, PAIR))

    nb1 = max(1, (5 * nb) // 16)
    nb2 = nb - nb1
    assert nb1 > 0 and nb2 > 0 and seq % COMBINE_TILE == 0

    xp, xs = x_prompt, x_sample
    outs = [[] for _ in range(6)]
    for l in range(depth):
        wl = (l, norm1_g, w_in, q_norm_g, k_norm_g, rel_bias, sinks, w_pool, pool_scale, w_out,
              norm2_g, w_router, b_router)
        consts_p = _mixer_consts(*wl, vis_prompt, MIX_SUB)
        moe_b = [b.astype(f32) for b in (b_gate[l], b_up[l], b_down[l])]
        cnt0 = jnp.zeros((N_EXPERTS, 128), f32)

        def prompt_mixer(stream0, n_streams, cnt_in):
            zk = jnp.zeros((n_streams, WINDOW, KV_WIDTH), f32)
            zu = jnp.zeros((n_streams, HIST_ROWS, POOL_WIDTH), f32)
            return _mixer_call(xp, zk, zk, zu, cnt_in, consts_p, stream0=stream0, tile=MIX_TILE,
                               sub=MIX_SUB, n_valid=MIX_TILE, pos0=0, mask_first=True)

        (x1_a, h_a, idx_a, rank_a, gate_a, cnt_a, k_a, v_a, u_a) = prompt_mixer(0, nb1, cnt0)
        xs_pad = jnp.pad(xs, ((0, 0), (0, PAIR - dseq), (0, 0)))
        uh = jnp.pad(state_pool[l], ((0, 0), (HIST_ROWS - POOL_HIST, 0), (0, 0)))
        (xs1, h_s, idx_s, rank_s, gate_s, cnt_s, k_s, v_s, u_s) = _mixer_call(
            xs_pad, cache_k[l].reshape(ndb, WINDOW, KV_WIDTH),
            cache_v[l].reshape(ndb, WINDOW, KV_WIDTH), uh, cnt0,
            _mixer_consts(*wl, vis_sample, PAIR),
            stream0=0, tile=PAIR, sub=PAIR, n_valid=dseq, pos0=PAST_LEN, mask_first=False)
        (x1_b, h_b, idx_b, rank_b, gate_b, cnt_b, k_b, v_b, u_b) = prompt_mixer(nb1, nb2, cnt_s)
        picked_b, w_bf16 = _moe_rows([h_b, h_s], [idx_b, idx_s], [rank_b, rank_s], [seq, dseq], cnt_b,
                                     moe_b, w_f32=(w_gate[l], w_up[l], w_down[l]))
        picked_a, _ = _moe_rows([h_a], [idx_a], [rank_a], [seq], cnt_a, moe_b, w_bf16=w_bf16)

        n_a, n_b = nb1 * seq, nb2 * seq
        no_buf = jnp.zeros((8, 128), f32)
        stride_a, stride_b = _pick_stride(n_a), _pick_stride(n_b + n_s)
        xp_rows = _combine_call(no_buf, x1_b.reshape(n_b, D_MODEL), gate_b, picked_b, stride_b, 0,
                                out_rows=n_p, row0=n_a, tile=COMBINE_TILE)
        xp_rows = _combine_call(xp_rows, x1_a.reshape(n_a, D_MODEL), gate_a, picked_a, stride_a, 0,
                                out_rows=n_p, row0=0, tile=COMBINE_TILE)
        xp = xp_rows.reshape(nb, seq, D_MODEL)
        xs = _combine_call(no_buf, xs1[:, :dseq].reshape(n_s, D_MODEL), gate_s, picked_b, stride_b,
                           n_b, out_rows=n_s, row0=0, tile=n_s).reshape(ndb, dseq, D_MODEL)

        outs[0].append(jnp.concatenate([k_a, k_b]).reshape(nb, WINDOW, N_KV_HEADS, HEAD_DIM))
        outs[1].append(jnp.concatenate([v_a, v_b]).reshape(nb, WINDOW, N_KV_HEADS, HEAD_DIM))
        outs[2].append(jnp.concatenate([u_a, u_b])[:, HIST_ROWS - POOL_HIST:])
        outs[3].append(k_s[:, :dseq].reshape(ndb, dseq, N_KV_HEADS, HEAD_DIM))
        outs[4].append(v_s[:, :dseq].reshape(ndb, dseq, N_KV_HEADS, HEAD_DIM))
        outs[5].append(u_s[:, HIST_ROWS - POOL_HIST:])
    return (xp, xs) + tuple(jnp.stack(o) for o in outs)


def _moe_rows(hs, idxs, ranks, n_reals, cnt, biases, w_f32=None, w_bf16=None):
    n_tok = sum(h.shape[0] for h in hs)
    n_assign = n_tok * TOP_K
    n_blocks = n_assign // EXPERT_BLOCK + N_EXPERTS
    e_ids = jnp.arange(N_EXPERTS, dtype=jnp.int32)

    def per_token(arrs):
        return jnp.concatenate(
            [jnp.transpose(a[:, :, :n], (1, 0, 2)).reshape(TOP_K, -1) for a, n in zip(arrs, n_reals)],
            axis=1)

    counts = cnt[:, 0].astype(jnp.int32)
    pcounts = (counts + EXPERT_BLOCK - 1) // EXPERT_BLOCK * EXPERT_BLOCK
    if w_bf16 is None:
        pcounts = jnp.maximum(pcounts, EXPERT_BLOCK)
    pend = jnp.cumsum(pcounts)
    pstart = pend - pcounts
    idx_all = per_token(idxs)
    dest = per_token(ranks) + jnp.sum(
        jnp.where(idx_all[None] == e_ids[:, None, None], pstart[:, None, None], 0), axis=0)
    dest = dest.reshape(-1)
    blk0 = jnp.arange(n_blocks, dtype=jnp.int32) * EXPERT_BLOCK
    block_e = jnp.minimum(jnp.sum((pend[None, :] <= blk0[:, None]).astype(jnp.int32), axis=1),
                          N_EXPERTS - 1)
    block_rows = jnp.clip(jnp.sum(jnp.where(block_e[:, None] == e_ids[None, :],
                                             (pstart + counts)[None, :], 0), axis=1) - blk0,
                          0, EXPERT_BLOCK).astype(jnp.int32)

    x_sorted = _scatter_rows(hs, dest, n_blocks * EXPERT_BLOCK)
    if w_bf16 is None:
        block_next = jnp.minimum(block_e + 1, N_EXPERTS - 1)
        block_live = (blk0 < pend[-1]).astype(jnp.int32)
        y_sorted, *w_bf16 = _expert_call_f32(
            block_e, block_rows, block_next, block_live, x_sorted,
            w_f32[0], biases[0], w_f32[1], biases[1], w_f32[2], biases[2])
    else:
        has = counts > 0
        later = has[None, :] & (e_ids[None, :] > e_ids[:, None])
        next_e = jnp.min(jnp.where(later, e_ids[None, :], N_EXPERTS), axis=1)
        next_e = jnp.where(next_e == N_EXPERTS, e_ids, next_e)
        order = jnp.cumsum(has.astype(jnp.int32)) - 1
        table = jnp.stack([next_e, order % 2], axis=0)[:, None, :]
        block_next, block_slot = jnp.sum(
            jnp.where(block_e[None, :, None] == e_ids[None, None, :], table, 0), axis=2).astype(jnp.int32)
        y_sorted = _expert_call_bf16(
            block_e, block_rows, block_next, block_slot, x_sorted,
            w_bf16[0], biases[0], w_bf16[1], biases[1], w_bf16[2], biases[2])
    stride = _pick_stride(n_tok)
    n_pick = -(-(TOP_K * stride) // GATHER_QUANT) * GATHER_QUANT
    picks = jnp.pad(dest.reshape(TOP_K, n_tok), ((0, 0), (0, stride - n_tok))).reshape(-1)
    return _gather_rows(y_sorted, jnp.pad(picks, (0, n_pick - TOP_K * stride))), w_bf16


def _pick_stride(n_tok):
    return -(-n_tok // COMBINE_TILE) * COMBINE_TILE
```

```python
import functools
import math

import numpy as np
import jax
import jax.numpy as jnp
from jax import lax
from jax.experimental import pallas as pl
from jax.experimental.pallas import tpu as pltpu
from jax.experimental.pallas import tpu_sc as plsc

D_MODEL = 1024
CHUNK = 64
POOL_WIDTH = 512
POOL_WINDOWS = (2, 4, 8, 16)
POOL_GROUP = 128
N_POOL_GROUPS = 4
POOL_HIST = 15
ATTN_WIDTH = 512
HEAD_DIM = 64
N_HEADS = 8
N_KV_HEADS = 2
GROUP = 4
KV_WIDTH = 128
WINDOW = 128
NUM_BUCKETS = 32
MAX_DISTANCE = 128
PAST_LEN = 2048
N_EXPERTS = 32
TOP_K = 4
SWIGLU_LIMIT = 7.0
SWIGLU_ALPHA = 1.702
EPS = 1e-5
NEG_INF = -1e30
ATTN_SCALE = HEAD_DIM ** -0.5

PAIR = 2 * CHUNK
BAND = PAIR + WINDOW
HIST_ROWS = 16
GATE_COLS = 8
MIX_TILE = 1024
MIX_SUB = 1024
EXPERT_BLOCK = 1024
FFN_CHUNK = 256
COMBINE_TILE = 1024
SC_WORKERS = 32
SC_CHUNK = 64
SC_GATHER_WAYS = 2
GATHER_QUANT = SC_WORKERS * SC_CHUNK * SC_GATHER_WAYS
VMEM_LIMIT = 56 * 1024 * 1024


def _pack_bf16_pair(a, b):
    ab = lax.bitcast_convert_type(a.astype(jnp.bfloat16).astype(jnp.float32), jnp.uint32)
    bb = lax.bitcast_convert_type(b.astype(jnp.bfloat16).astype(jnp.float32), jnp.uint32)
    return (ab >> 16) | (bb & jnp.uint32(0xFFFF0000))


def _unpack_bf16_pair(w):
    a = lax.bitcast_convert_type(w << 16, jnp.float32).astype(jnp.bfloat16)
    b = lax.bitcast_convert_type(w & jnp.uint32(0xFFFF0000), jnp.float32).astype(jnp.bfloat16)
    return a, b


def _mixer_kernel(x_ref, kh_ref, vh_ref, uh_ref, cnt_in_ref,
                  g1_ref, wukv_ref, wqt_ref, qg_ref, kg_ref, bd_ref, bias_ref, sink_ref,
                  wpool_ref, pscale_ref, wout_ref, g2_ref, wr_ref, br_ref, tri_ref,
                  x1_ref, hp_ref, idx_ref, rank_ref, gate_ref, cnt_ref, ko_ref, vo_ref, uo_ref,
                  qt_s, kb_s, vt_s, ub_s, mix_s, cnt_s,
                  *, tile, sub, n_valid, pos0, mask_first):
    b = pl.program_id(0)
    s = pl.program_id(1)
    bf16 = jnp.bfloat16
    f32 = jnp.float32

    @pl.when((b == 0) & (s == 0))
    def _():
        cnt_s[...] = cnt_in_ref[...]

    @pl.when(s == 0)
    def _():
        kb_s[0:WINDOW, :] = kh_ref[...].astype(bf16)
        vt_s[:, 0:WINDOW] = jnp.transpose(vh_ref[...]).astype(bf16)
        ub_s[0:HIST_ROWS, :] = uh_ref[...]

    @pl.when(s > 0)
    def _():
        kb_s[0:WINDOW, :] = kb_s[tile:tile + WINDOW, :]
        vt_s[:, 0:WINDOW] = vt_s[:, tile:tile + WINDOW]
        ub_s[0:HIST_ROWS, :] = ub_s[tile:tile + HIST_ROWS, :]

    for r0 in range(0, tile, sub):
        _mixer_rows(r0, s, x_ref, g1_ref, wukv_ref, wqt_ref, qg_ref, kg_ref, bd_ref, bias_ref,
                    sink_ref, wpool_ref, pscale_ref, wout_ref, g2_ref, wr_ref, br_ref, tri_ref,
                    x1_ref, hp_ref, idx_ref, rank_ref, gate_ref, cnt_ref, ko_ref, vo_ref, uo_ref,
                    qt_s, kb_s, vt_s, ub_s, mix_s, cnt_s,
                    tile=tile, sub=sub, n_valid=n_valid, pos0=pos0, mask_first=mask_first)


def _mixer_rows(r0, s, x_ref, g1_ref, wukv_ref, wqt_ref, qg_ref, kg_ref, bd_ref, bias_ref,
                sink_ref, wpool_ref, pscale_ref, wout_ref, g2_ref, wr_ref, br_ref, tri_ref,
                x1_ref, hp_ref, idx_ref, rank_ref, gate_ref, cnt_ref, ko_ref, vo_ref, uo_ref,
                qt_s, kb_s, vt_s, ub_s, mix_s, cnt_s, *, tile, sub, n_valid, pos0, mask_first):
    bf16 = jnp.bfloat16
    f32 = jnp.float32
    rows = slice(r0, r0 + sub)

    x = x_ref[rows, :]
    xn = (x * lax.rsqrt(jnp.mean(x * x, axis=-1, keepdims=True) + EPS) * g1_ref[...]).astype(bf16)
    z = jnp.dot(xn, wukv_ref[...], preferred_element_type=f32)
    zt = lax.dot_general(wqt_ref[...], xn, (((1,), (1,)), ((), ())),
                         preferred_element_type=f32)
    u = z[:, 0:POOL_WIDTH]
    kz = z[:, POOL_WIDTH:POOL_WIDTH + KV_WIDTH]
    v = z[:, POOL_WIDTH + KV_WIDTH:]

    ksq = kz * kz
    kss = jnp.dot(ksq.astype(bf16), bd_ref[...], preferred_element_type=f32)
    kn = kz * lax.rsqrt(kss * (1.0 / HEAD_DIM) + EPS) * kg_ref[...]
    kb_s[WINDOW + r0:WINDOW + r0 + sub, :] = kn.astype(bf16)
    vt_s[:, WINDOW + r0:WINDOW + r0 + sub] = jnp.transpose(v).astype(bf16)
    ub_s[HIST_ROWS + r0:HIST_ROWS + r0 + sub, :] = u

    if r0 + sub == tile:
        row0 = max(n_valid, WINDOW) - WINDOW - r0
        ko_ref[...] = kn[row0:row0 + WINDOW, :]
        vo_ref[...] = v[row0:row0 + WINDOW, :]
        uo_ref[...] = u[n_valid - HIST_ROWS - r0:n_valid - r0, :]

    for hd in range(N_HEADS):
        qh = zt[hd * HEAD_DIM:(hd + 1) * HEAD_DIM, :]
        ss = jnp.sum(qh * qh, axis=0, keepdims=True)
        qn = qh * (lax.rsqrt(ss * (1.0 / HEAD_DIM) + EPS) * ATTN_SCALE) * qg_ref[...]
        qt_s[hd * HEAD_DIM:(hd + 1) * HEAD_DIM, rows] = qn.astype(bf16)

    pos = pos0 + s * tile + r0 + lax.broadcasted_iota(jnp.int32, (sub, 1), 0)
    diffs = []
    for g, w in enumerate(POOL_WINDOWS):
        e = ub_s[r0:r0 + HIST_ROWS + sub, g * POOL_GROUP:(g + 1) * POOL_GROUP]
        acc = e
        for lvl in range(g + 1):
            acc = acc + pltpu.roll(acc, 2 ** lvl, axis=0)
        inv_cnt = 1.0 / jnp.minimum(pos + 1, w).astype(f32)
        diffs.append((acc[HIST_ROWS:, :] * inv_cnt - e[HIST_ROWS:, :]).astype(bf16))
    for g2 in range(N_POOL_GROUPS // 2):
        cols = slice(2 * g2 * POOL_GROUP, 2 * (g2 + 1) * POOL_GROUP)
        d = jnp.concatenate(diffs[2 * g2:2 * g2 + 2], axis=1)
        y = jnp.dot(d, wpool_ref[g2], preferred_element_type=f32) * pscale_ref[:, cols]
        mix_s[rows, cols] = y.astype(bf16)

    zeros_q = jnp.zeros((HEAD_DIM, GROUP * PAIR), bf16)
    for p in range(r0 // PAIR, (r0 + sub) // PAIR):
        k_band = kb_s[p * PAIR:p * PAIR + BAND, :]
        o_parts = []
        rhs_heads = []
        for h in range(N_KV_HEADS):
            qcat = jnp.concatenate(
                [qt_s[(h * GROUP + g) * HEAD_DIM:(h * GROUP + g + 1) * HEAD_DIM,
                      p * PAIR:(p + 1) * PAIR] for g in range(GROUP)], axis=1)
            rhs_heads.append(jnp.concatenate([qcat, zeros_q] if h == 0 else [zeros_q, qcat], axis=0))
        st_heads = jnp.dot(k_band, jnp.concatenate(rhs_heads, axis=1),
                           preferred_element_type=f32)
        for h in range(N_KV_HEADS):
            st = st_heads[:, h * GROUP * PAIR:(h + 1) * GROUP * PAIR] + bias_ref[h]
            if mask_first and p == 0:
                krow = lax.broadcasted_iota(jnp.int32, (BAND, 1), 0)
                st = jnp.where((krow >= WINDOW) | (s > 0), st, NEG_INF)
            sink = sink_ref[h:h + 1, :]
            m = jnp.maximum(jnp.max(st, axis=0, keepdims=True), sink)
            ex = jnp.exp(st - m)
            den = jnp.sum(ex, axis=0, keepdims=True) + jnp.exp(sink - m)
            v_band = vt_s[h * HEAD_DIM:(h + 1) * HEAD_DIM, p * PAIR:p * PAIR + BAND]
            ot = jnp.dot(v_band, ex.astype(bf16), preferred_element_type=f32) / den
            for g in range(GROUP):
                o_parts.append(ot[:, g * PAIR:(g + 1) * PAIR])
        o_all = jnp.concatenate(o_parts, axis=0)
        mix_s[p * PAIR:(p + 1) * PAIR, POOL_WIDTH:] = jnp.transpose(o_all).astype(bf16)

    x1 = x + jnp.dot(mix_s[rows, :], wout_ref[...], preferred_element_type=f32)
    x1_ref[rows, :] = x1

    n_real = min(sub, n_valid - r0)
    hn = x1 * lax.rsqrt(jnp.mean(x1 * x1, axis=-1, keepdims=True) + EPS) * g2_ref[...]
    hp_ref[r0:r0 + n_real, :] = _pack_bf16_pair(hn[0:n_real, 0:D_MODEL // 2],
                                                hn[0:n_real, D_MODEL // 2:])
    h_hi = hn.astype(bf16)
    h_lo = (hn - h_hi.astype(f32)).astype(bf16)
    both = jnp.dot(jnp.concatenate([h_hi, h_lo], axis=0), wr_ref[...],
                   preferred_element_type=f32)
    parts = jnp.transpose(both[0:sub, :] + both[sub:, :])
    lt = parts[0:N_EXPERTS, :] + parts[N_EXPERTS:2 * N_EXPERTS, :] + br_ref[...]

    eidx = lax.broadcasted_iota(jnp.int32, (N_EXPERTS, sub), 0).astype(f32)
    vals, hots = [], []
    for j in range(TOP_K):
        m = jnp.max(lt, axis=0, keepdims=True)
        sel = jnp.min(jnp.where(lt == m, eidx, float(N_EXPERTS)), axis=0, keepdims=True)
        hot = eidx == sel
        lt = jnp.where(hot, -jnp.inf, lt)
        idx_ref[j:j + 1, rows] = sel.astype(jnp.int32)
        vals.append(m)
        hots.append(hot)
    exps = [jnp.exp(vv - vals[0]) for vv in vals]
    esum = exps[0] + exps[1] + exps[2] + exps[3]
    grow = lax.broadcasted_iota(jnp.int32, (GATE_COLS, sub), 0)
    gmat = jnp.zeros((GATE_COLS, sub), f32)
    for j in range(TOP_K):
        gmat = jnp.where(grow == j, exps[j] / esum, gmat)
    gmat = jnp.concatenate([gmat, jnp.zeros((128 - GATE_COLS, sub), f32)], axis=0)
    gate_ref[r0:r0 + n_real, :] = jnp.transpose(gmat)[0:n_real, 0:GATE_COLS]

    chosen_f = sum(jnp.where(hot, 1.0, 0.0) for hot in hots)
    if n_real < sub:
        lane = lax.broadcasted_iota(jnp.int32, (N_EXPERTS, sub), 1)
        chosen_f = jnp.where(lane < n_real, chosen_f, 0.0)
    n_lane_blocks = sub // 128
    blocks = [chosen_f[:, k * 128:(k + 1) * 128] for k in range(n_lane_blocks)]
    inside = jnp.dot(jnp.concatenate(blocks, axis=0).astype(bf16), tri_ref[...],
                     preferred_element_type=f32)
    offset = cnt_s[:, 0:1]
    pieces = []
    for k in range(n_lane_blocks):
        pieces.append(inside[k * N_EXPERTS:(k + 1) * N_EXPERTS, :] + offset)
        offset = offset + jnp.sum(blocks[k], axis=1, keepdims=True)
    base = jnp.concatenate(pieces, axis=1)
    for j in range(TOP_K):
        rank_ref[j:j + 1, rows] = jnp.sum(jnp.where(hots[j], base, 0.0), axis=0,
                                          keepdims=True).astype(jnp.int32)
    cnt_new = jnp.broadcast_to(offset, (N_EXPERTS, 128))
    cnt_s[...] = cnt_new
    cnt_ref[...] = cnt_new


def _mixer_call(x, k_hist, v_hist, u_hist, cnt_in, consts, *, stream0, tile, sub, n_valid, pos0,
                mask_first):
    seq = x.shape[1]
    nb = k_hist.shape[0]
    n_tiles = seq // tile
    f32 = jnp.float32
    assert n_valid % 8 == 0 and tile % sub == 0 and sub % PAIR == 0
    assert n_valid == tile or sub == tile

    def full(a):
        nd = a.ndim
        return pl.BlockSpec(a.shape, lambda b, s, _nd=nd: (0,) * _nd)

    in_specs = [
        pl.BlockSpec((None, tile, D_MODEL), lambda b, s: (stream0 + b, s, 0)),
        pl.BlockSpec((None, WINDOW, KV_WIDTH), lambda b, s: (b, 0, 0)),
        pl.BlockSpec((None, WINDOW, KV_WIDTH), lambda b, s: (b, 0, 0)),
        pl.BlockSpec((None, HIST_ROWS, POOL_WIDTH), lambda b, s: (b, 0, 0)),
        full(cnt_in),
    ] + [full(c) for c in consts]
    out_shape = [
        jax.ShapeDtypeStruct((nb, seq, D_MODEL), f32),
        jax.ShapeDtypeStruct((nb * n_tiles * n_valid, D_MODEL // 2), jnp.uint32),
        jax.ShapeDtypeStruct((nb, TOP_K, seq), jnp.int32),
        jax.ShapeDtypeStruct((nb, TOP_K, seq), jnp.int32),
        jax.ShapeDtypeStruct((nb * n_tiles * n_valid, GATE_COLS), f32),
        jax.ShapeDtypeStruct((N_EXPERTS, 128), f32),
        jax.ShapeDtypeStruct((nb, WINDOW, KV_WIDTH), f32),
        jax.ShapeDtypeStruct((nb, WINDOW, KV_WIDTH), f32),
        jax.ShapeDtypeStruct((nb, HIST_ROWS, POOL_WIDTH), f32),
    ]
    out_specs = [
        pl.BlockSpec((None, tile, D_MODEL), lambda b, s: (b, s, 0)),
        pl.BlockSpec((n_valid, D_MODEL // 2), lambda b, s: (b * n_tiles + s, 0)),
        pl.BlockSpec((None, TOP_K, tile), lambda b, s: (b, 0, s)),
        pl.BlockSpec((None, TOP_K, tile), lambda b, s: (b, 0, s)),
        pl.BlockSpec((n_valid, GATE_COLS), lambda b, s: (b * n_tiles + s, 0)),
        pl.BlockSpec((N_EXPERTS, 128), lambda b, s: (0, 0)),
        pl.BlockSpec((None, WINDOW, KV_WIDTH), lambda b, s: (b, 0, 0)),
        pl.BlockSpec((None, WINDOW, KV_WIDTH), lambda b, s: (b, 0, 0)),
        pl.BlockSpec((None, HIST_ROWS, POOL_WIDTH), lambda b, s: (b, 0, 0)),
    ]
    scratch = [
        pltpu.VMEM((ATTN_WIDTH, tile), jnp.bfloat16),
        pltpu.VMEM((WINDOW + tile, KV_WIDTH), jnp.bfloat16),
        pltpu.VMEM((KV_WIDTH, WINDOW + tile), jnp.bfloat16),
        pltpu.VMEM((HIST_ROWS + tile, POOL_WIDTH), f32),
        pltpu.VMEM((tile, D_MODEL), jnp.bfloat16),
        pltpu.VMEM((N_EXPERTS, 128), f32),
    ]
    kern = functools.partial(_mixer_kernel, tile=tile, sub=sub, n_valid=n_valid, pos0=pos0,
                             mask_first=mask_first)
    return pl.pallas_call(
        kern,
        grid=(nb, n_tiles),
        in_specs=in_specs,
        out_specs=out_specs,
        out_shape=out_shape,
        scratch_shapes=scratch,
        compiler_params=pltpu.CompilerParams(
            dimension_semantics=("arbitrary", "arbitrary"),
            vmem_limit_bytes=VMEM_LIMIT),
        name="mixer",
    )(x, k_hist, v_hist, u_hist, cnt_in, *consts)


def _expert_kernel_f32(be_ref, nv_ref, nx_ref, lv_ref, xs_ref, wg_hbm, bg_ref, wu_hbm, bu_ref,
                       wd_hbm, bd_ref, ys_ref, wg16_hbm, wu16_hbm, wd16_hbm,
                       stage_s, wg_s, wu_s, wd_s, in_sems, out_sems, *, n_blocks):
    i = pl.program_id(0)
    expert = be_ref[i]
    bf16 = jnp.bfloat16

    def copies_in(e):
        return [pltpu.make_async_copy(w_hbm.at[e], stage_s.at[k], in_sems.at[k])
                for k, w_hbm in enumerate((wg_hbm, wu_hbm, wd_hbm))]

    def copies_out(e):
        return [pltpu.make_async_copy(w_s, w16_hbm.at[e], out_sems.at[k])
                for k, (w_s, w16_hbm) in enumerate(((wg_s, wg16_hbm), (wu_s, wu16_hbm),
                                                    (wd_s, wd16_hbm)))]

    @pl.when(i == 0)
    def _():
        for copy in copies_in(expert):
            copy.start()

    @pl.when((lv_ref[i] > 0) & ((i == 0) | (expert != be_ref[jnp.maximum(i - 1, 0)])))
    def _():
        for copy in copies_in(expert):
            copy.wait()

        @pl.when(i > 0)
        def _():
            for copy in copies_out(expert):
                copy.wait()

        wg_s[...] = stage_s[0].astype(bf16)
        wu_s[...] = stage_s[1].astype(bf16)
        wd_s[...] = stage_s[2].astype(bf16)
        for copy in copies_out(expert):
            copy.start()

        @pl.when(nx_ref[i] != expert)
        def _():
            for copy in copies_in(nx_ref[i]):
                copy.start()

    _ffn_block(nv_ref[i], expert, xs_ref, ys_ref, wg_s, wu_s, wd_s, bg_ref, bu_ref, bd_ref)

    @pl.when(i == n_blocks - 1)
    def _():
        for copy in copies_out(expert):
            copy.wait()


def _expert_kernel_bf16(be_ref, nv_ref, nx_ref, sl_ref, xs_ref, wg_hbm, bg_ref, wu_hbm, bu_ref,
                        wd_hbm, bd_ref, ys_ref, w_s, sems):
    i = pl.program_id(0)
    n_rows = nv_ref[i]
    expert = be_ref[i]
    slot = sl_ref[i]

    def copies(e, to_slot):
        return [pltpu.make_async_copy(w_hbm.at[e], w_s.at[to_slot, k], sems.at[to_slot, k])
                for k, w_hbm in enumerate((wg_hbm, wu_hbm, wd_hbm))]

    @pl.when((i == 0) & (n_rows > 0))
    def _():
        for copy in copies(expert, slot):
            copy.start()

    @pl.when((n_rows > 0) & ((i == 0) | (expert != be_ref[jnp.maximum(i - 1, 0)])))
    def _():
        for copy in copies(expert, slot):
            copy.wait()

        @pl.when(nx_ref[i] != expert)
        def _():
            for copy in copies(nx_ref[i], 1 - slot):
                copy.start()

    _ffn_block(n_rows, expert, xs_ref, ys_ref, w_s.at[slot, 0], w_s.at[slot, 1], w_s.at[slot, 2],
               bg_ref, bu_ref, bd_ref)


def _ffn_block(n_rows, expert, xs_ref, ys_ref, wg_s, wu_s, wd_s, bg_all_ref, bu_all_ref, bd_all_ref):
    bf16 = jnp.bfloat16

    def ffn(rows):
        f32 = jnp.float32
        half = D_MODEL // 2
        bg_ref, bu_ref, bd_ref = (b.at[pl.ds(expert, 1), :]
                                  for b in (bg_all_ref, bu_all_ref, bd_all_ref))
        for r0 in range(0, rows, FFN_CHUNK):
            row = r0 + lax.broadcasted_iota(jnp.int32, (FFN_CHUNK, 1), 0)
            words = jnp.where(row < n_rows, xs_ref[r0:r0 + FFN_CHUNK, :],
                              jnp.uint32(0))
            xa, xb = _unpack_bf16_pair(words)
            a = (jnp.dot(xa, wg_s[0:half, :], preferred_element_type=f32)
                 + jnp.dot(xb, wg_s[half:, :], preferred_element_type=f32) + bg_ref[...])
            bb = (jnp.dot(xa, wu_s[0:half, :], preferred_element_type=f32)
                  + jnp.dot(xb, wu_s[half:, :], preferred_element_type=f32) + bu_ref[...])
            a = jnp.minimum(a, SWIGLU_LIMIT)
            bb = jnp.clip(bb, -SWIGLU_LIMIT, SWIGLU_LIMIT)
            act = a * (1.0 / (1.0 + jnp.exp(-SWIGLU_ALPHA * a))) * (bb + 1.0)
            y = jnp.dot(act.astype(bf16), wd_s[...], preferred_element_type=f32) + bd_ref[...]
            ys_ref[r0:r0 + FFN_CHUNK, :] = _pack_bf16_pair(y[:, 0:half], y[:, half:])
        if rows < EXPERT_BLOCK:
            ys_ref[rows:, :] = jnp.zeros((EXPERT_BLOCK - rows, half), jnp.uint32)

    quarter = EXPERT_BLOCK // 4
    for q in range(1, 5):
        @pl.when((n_rows > (q - 1) * quarter) & (n_rows <= q * quarter))
        def _(q=q):
            ffn(q * quarter)

    @pl.when(n_rows == 0)
    def _():
        ys_ref[...] = jnp.zeros_like(ys_ref)


def _expert_call_f32(block_e, block_rows, block_next, block_live, xs, wg, bg, wu, bu, wd, bd):
    n_slots = xs.shape[0]
    n_blocks = n_slots // EXPERT_BLOCK
    any_spec = pl.BlockSpec(memory_space=pl.ANY)
    b_spec = pl.BlockSpec((N_EXPERTS, D_MODEL), lambda i, be, nv, nx, lv: (0, 0))
    x_spec = pl.BlockSpec((EXPERT_BLOCK, D_MODEL // 2), lambda i, be, nv, nx, lv: (i, 0))
    grid_spec = pltpu.PrefetchScalarGridSpec(
        num_scalar_prefetch=4,
        grid=(n_blocks,),
        in_specs=[x_spec, any_spec, b_spec, any_spec, b_spec, any_spec, b_spec],
        out_specs=[x_spec, any_spec, any_spec, any_spec],
        scratch_shapes=[pltpu.VMEM((3, D_MODEL, D_MODEL), jnp.float32)]
        + [pltpu.VMEM((D_MODEL, D_MODEL), jnp.bfloat16)] * 3
        + [pltpu.SemaphoreType.DMA((3,)), pltpu.SemaphoreType.DMA((3,))],
    )
    w16 = jax.ShapeDtypeStruct((N_EXPERTS, D_MODEL, D_MODEL), jnp.bfloat16)
    return pl.pallas_call(
        functools.partial(_expert_kernel_f32, n_blocks=n_blocks),
        grid_spec=grid_spec,
        out_shape=[jax.ShapeDtypeStruct((n_slots, D_MODEL // 2), jnp.uint32), w16, w16, w16],
        compiler_params=pltpu.CompilerParams(
            dimension_semantics=("arbitrary",),
            vmem_limit_bytes=VMEM_LIMIT),
        name="experts_f32",
    )(block_e, block_rows, block_next, block_live, xs, wg, bg, wu, bu, wd, bd)


def _expert_call_bf16(block_e, block_rows, block_next, block_slot, xs, wg, bg, wu, bu, wd, bd):
    n_slots = xs.shape[0]
    any_spec = pl.BlockSpec(memory_space=pl.ANY)
    b_spec = pl.BlockSpec((N_EXPERTS, D_MODEL), lambda i, be, nv, nx, sl: (0, 0))
    x_spec = pl.BlockSpec((EXPERT_BLOCK, D_MODEL // 2), lambda i, be, nv, nx, sl: (i, 0))
    grid_spec = pltpu.PrefetchScalarGridSpec(
        num_scalar_prefetch=4,
        grid=(n_slots // EXPERT_BLOCK,),
        in_specs=[x_spec, any_spec, b_spec, any_spec, b_spec, any_spec, b_spec],
        out_specs=x_spec,
        scratch_shapes=[pltpu.VMEM((2, 3, D_MODEL, D_MODEL), jnp.bfloat16),
                        pltpu.SemaphoreType.DMA((2, 3))],
    )
    return pl.pallas_call(
        _expert_kernel_bf16,
        grid_spec=grid_spec,
        out_shape=jax.ShapeDtypeStruct((n_slots, D_MODEL // 2), jnp.uint32),
        compiler_params=pltpu.CompilerParams(
            dimension_semantics=("arbitrary",),
            vmem_limit_bytes=VMEM_LIMIT),
        name="experts_bf16",
    )(block_e, block_rows, block_next, block_slot, xs, wg, bg, wu, bu, wd, bd)


def _combine_kernel(out_buf_ref, x1_ref, g_ref, y0_ref, y1_ref, y2_ref, y3_ref, o_ref):
    del out_buf_ref
    g = g_ref[...]
    half = D_MODEL // 2
    lo, hi = x1_ref[:, 0:half], x1_ref[:, half:]
    for j, y_ref in enumerate((y0_ref, y1_ref, y2_ref, y3_ref)):
        w = y_ref[...]
        gj = g[:, j:j + 1]
        lo = lo + gj * lax.bitcast_convert_type(w << 16, jnp.float32)
        hi = hi + gj * lax.bitcast_convert_type(w & jnp.uint32(0xFFFF0000), jnp.float32)
    o_ref[:, 0:half] = lo
    o_ref[:, half:] = hi


def _combine_call(out_buf, x1, gates, picked, stride, tok0, *, out_rows, row0, tile):
    n = x1.shape[0]
    aliased = out_buf.shape == (out_rows, D_MODEL)
    assert n % tile == 0 and row0 % tile == 0
    y_specs = []
    for j in range(TOP_K):
        assert (j * stride + tok0) % tile == 0
        base = (j * stride + tok0) // tile
        y_specs.append(pl.BlockSpec((tile, D_MODEL // 2), lambda i, _b=base: (_b + i, 0)))
    return pl.pallas_call(
        _combine_kernel,
        grid=(n // tile,),
        in_specs=[pl.BlockSpec(memory_space=pl.ANY),
                  pl.BlockSpec((tile, D_MODEL), lambda i: (i, 0)),
                  pl.BlockSpec((tile, GATE_COLS), lambda i: (i, 0))] + y_specs,
        out_specs=pl.BlockSpec((tile, D_MODEL), lambda i: (row0 // tile + i, 0)),
        out_shape=jax.ShapeDtypeStruct((out_rows, D_MODEL), jnp.float32),
        input_output_aliases={0: 0} if aliased else {},
        compiler_params=pltpu.CompilerParams(dimension_semantics=("arbitrary",),
                                             vmem_limit_bytes=VMEM_LIMIT),
        name="combine",
    )(out_buf, x1, gates, picked, picked, picked, picked)


def _gather_rows(table, idx):
    n = idx.shape[0]
    width = table.shape[1]
    assert n % GATHER_QUANT == 0
    per_worker = n // SC_WORKERS
    n_rounds = per_worker // (SC_CHUNK * SC_GATHER_WAYS)
    mesh = plsc.VectorSubcoreMesh(core_axis_name="c", subcore_axis_name="s")

    @functools.partial(
        pl.kernel, mesh=mesh,
        out_type=jax.ShapeDtypeStruct((n, width), table.dtype),
        scratch_types=[pltpu.VMEM((SC_CHUNK,), jnp.int32)] * SC_GATHER_WAYS
        + [pltpu.VMEM((SC_CHUNK, width), table.dtype)] * SC_GATHER_WAYS
        + [pltpu.SemaphoreType.DMA] * 3,
        cost_estimate=pl.CostEstimate(flops=0, transcendentals=0, bytes_accessed=8 * n * width),
    )
    def gather(table_hbm, idx_hbm, out_hbm, *scratch):
        idx_vs = scratch[:SC_GATHER_WAYS]
        rows_vs = scratch[SC_GATHER_WAYS:2 * SC_GATHER_WAYS]
        sem_idx, sem_rows, sem_out = scratch[2 * SC_GATHER_WAYS:]
        wid = lax.axis_index("s") * 2 + lax.axis_index("c")
        base = wid * per_worker

        @pl.loop(0, n_rounds)
        def _(i):
            offs = [base + (i * SC_GATHER_WAYS + u) * SC_CHUNK for u in range(SC_GATHER_WAYS)]
            stage = [pltpu.async_copy(idx_hbm.at[pl.ds(off, SC_CHUNK)], idx_v, sem_idx)
                     for off, idx_v in zip(offs, idx_vs)]
            for copy in stage:
                copy.wait()
            stage = [pltpu.async_copy(table_hbm.at[idx_v], rows_v, sem_rows)
                     for idx_v, rows_v in zip(idx_vs, rows_vs)]
            for copy in stage:
                copy.wait()
            stage = [pltpu.async_copy(rows_v, out_hbm.at[pl.ds(off, SC_CHUNK)], sem_out)
                     for off, rows_v in zip(offs, rows_vs)]
            for copy in stage:
                copy.wait()

    return gather(table, idx)


def _scatter_rows(srcs, dest, n_out):
    width = srcs[0].shape[1]
    starts = [0]
    for src in srcs:
        assert src.shape[0] % SC_CHUNK == 0
        starts.append(starts[-1] + src.shape[0] // SC_CHUNK)
    n_chunks = starts[-1]
    n = n_chunks * SC_CHUNK
    per_worker = -(-n_chunks // SC_WORKERS)
    mesh = plsc.VectorSubcoreMesh(core_axis_name="c", subcore_axis_name="s")

    @functools.partial(
        pl.kernel, mesh=mesh,
        out_type=jax.ShapeDtypeStruct((n_out, width), srcs[0].dtype),
        scratch_types=[pltpu.VMEM((SC_CHUNK,), jnp.int32)] * TOP_K
        + [pltpu.VMEM((SC_CHUNK, width), srcs[0].dtype),
           pltpu.SemaphoreType.DMA, pltpu.SemaphoreType.DMA],
        cost_estimate=pl.CostEstimate(flops=0, transcendentals=0,
                                      bytes_accessed=4 * (1 + TOP_K) * n * width),
    )
    def scatter(*refs):
        src_hbms = refs[:len(srcs)]
        dest_hbm, out_hbm = refs[len(srcs)], refs[len(srcs) + 1]
        idx_vs = refs[len(srcs) + 2:len(srcs) + 2 + TOP_K]
        rows_v, sem_in, sem_out = refs[-3:]
        wid = lax.axis_index("s") * 2 + lax.axis_index("c")

        @pl.loop(0, per_worker)
        def _(i):
            c = i * SC_WORKERS + wid
            for k, src_hbm in enumerate(src_hbms):

                @pl.when((c >= starts[k]) & (c < starts[k + 1]))
                def _():
                    loads = [pltpu.async_copy(
                        src_hbm.at[pl.ds((c - starts[k]) * SC_CHUNK, SC_CHUNK)], rows_v, sem_in)]
                    loads += [pltpu.async_copy(dest_hbm.at[pl.ds(j * n + c * SC_CHUNK, SC_CHUNK)],
                                               idx_v, sem_in) for j, idx_v in enumerate(idx_vs)]
                    for copy in loads:
                        copy.wait()
                    sends = [pltpu.async_copy(rows_v, out_hbm.at[idx_v], sem_out) for idx_v in idx_vs]
                    for copy in sends:
                        copy.wait()

    return scatter(*srcs, dest)


def _t5_bucket_np(rel):
    half = NUM_BUCKETS // 2
    max_exact = half // 2
    n = np.abs(rel)
    nf = np.maximum(n, 1).astype(np.float32)
    large = max_exact + (np.log(nf / max_exact) / math.log(MAX_DISTANCE / max_exact)
                         * (half - max_exact)).astype(np.int32)
    large = np.minimum(large, half - 1)
    return np.where(rel > 0, half, 0) + np.where(n < max_exact, n, large)


def _bias_tables(rel_bias, visible):
    kap = np.arange(BAND)[:, None]
    col = np.arange(GROUP * PAIR)[None, :]
    bucket = _t5_bucket_np(kap - WINDOW - col % PAIR)
    onehot = jnp.asarray(bucket[:, :, None] == np.arange(NUM_BUCKETS))
    seen = jnp.asarray(np.tile(visible, (1, GROUP)))
    rb = rel_bias.astype(jnp.float32)
    tabs = []
    for h in range(N_KV_HEADS):
        per_col = jnp.repeat(rb[:, h * GROUP:(h + 1) * GROUP], PAIR, axis=1)
        tab = jnp.sum(jnp.where(onehot, jnp.transpose(per_col)[None], 0.0), axis=-1)
        tabs.append(jnp.where(seen, tab, NEG_INF))
    return jnp.stack(tabs)


def _mixer_consts(l, norm1_g, w_in, q_norm_g, k_norm_g, rel_bias, sinks, w_pool, pool_scale,
                  w_out, norm2_g, w_router, b_router, visible):
    f32, bf16 = jnp.float32, jnp.bfloat16
    q_off, k_off, v_off = POOL_WIDTH, POOL_WIDTH + ATTN_WIDTH, POOL_WIDTH + ATTN_WIDTH + KV_WIDTH
    w = w_in[l]
    w_ukv = jnp.concatenate([w[:, :q_off], w[:, k_off:]], axis=1).astype(bf16)
    w_qt = jnp.transpose(w[:, q_off:k_off]).astype(bf16)
    lane_head = np.arange(KV_WIDTH) // HEAD_DIM
    blockdiag = jnp.asarray(lane_head[:, None] == lane_head[None, :], bf16)
    sink_rows = jnp.repeat(sinks[l].astype(f32).reshape(N_KV_HEADS, GROUP), PAIR, axis=1)
    wp = w_pool[l].astype(bf16)
    zp = jnp.zeros((POOL_GROUP, POOL_GROUP), bf16)
    w_pool_pairs = jnp.stack([jnp.block([[wp[2 * i], zp], [zp, wp[2 * i + 1]]])
                              for i in range(N_POOL_GROUPS // 2)])
    wr = w_router[l].astype(f32)
    wr_hi = wr.astype(bf16)
    wr_lo = (wr - wr_hi.astype(f32)).astype(bf16)
    wr_parts = jnp.pad(jnp.concatenate([wr_hi, wr_lo], axis=1), ((0, 0), (0, 128 - 2 * N_EXPERTS)))
    tri = jnp.asarray(np.arange(128)[:, None] < np.arange(128)[None, :], bf16)
    return [
        norm1_g[l].reshape(1, D_MODEL).astype(f32), w_ukv, w_qt,
        q_norm_g[l].reshape(HEAD_DIM, 1).astype(f32),
        jnp.tile(k_norm_g[l].astype(f32), N_KV_HEADS).reshape(1, KV_WIDTH),
        blockdiag, _bias_tables(rel_bias, visible), sink_rows,
        w_pool_pairs, pool_scale[l].reshape(1, POOL_WIDTH).astype(f32),
        w_out[l].astype(bf16), norm2_g[l].reshape(1, D_MODEL).astype(f32),
        wr_parts, b_router[l].reshape(N_EXPERTS, 1).astype(f32), tri,
    ]


def _visibility():
    kap = np.arange(BAND)[:, None]
    rho = np.arange(PAIR)[None, :]
    kc, qc = kap // CHUNK, rho // CHUNK
    prompt = (kc >= qc) & (kc <= qc + WINDOW // CHUNK)
    return prompt


def kernel(x_prompt, x_sample, cache_k, cache_v, state_pool, norm1_g, w_in, q_norm_g, k_norm_g,
           rel_bias, sinks, w_pool, pool_scale, w_out, norm2_g, w_router, b_router,
           w_gate, b_gate, w_up, b_up, w_down, b_down):
    f32, bf16 = jnp.float32, jnp.bfloat16
    depth = w_in.shape[0]
    nb, seq, _ = x_prompt.shape
    ndb, dseq, _ = x_sample.shape
    cache_len = cache_k.shape[2]
    assert seq % MIX_TILE == 0 and cache_len == WINDOW and HIST_ROWS <= dseq <= PAIR
    n_p, n_s = nb * seq, ndb * dseq
    n_tok = n_p + n_s
    assert COMBINE_TILE % n_s == 0 and n_s % SC_CHUNK == 0

    vis_prompt = _visibility()
    vis_sample = np.broadcast_to(np.arange(BAND)[:, None] < WINDOW + dseq, (BAND, PAIR))

    nb1 = nb // 4
    nb2 = nb - nb1
    assert nb1 > 0 and nb2 > 0 and seq % COMBINE_TILE == 0

    xp, xs = x_prompt, x_sample
    outs = [[] for _ in range(6)]
    for l in range(depth):
        wl = (l, norm1_g, w_in, q_norm_g, k_norm_g, rel_bias, sinks, w_pool, pool_scale, w_out,
              norm2_g, w_router, b_router)
        consts_p = _mixer_consts(*wl, vis_prompt)
        moe_b = [b.astype(f32) for b in (b_gate[l], b_up[l], b_down[l])]
        cnt0 = jnp.zeros((N_EXPERTS, 128), f32)

        def prompt_mixer(stream0, n_streams, cnt_in):
            zk = jnp.zeros((n_streams, WINDOW, KV_WIDTH), f32)
            zu = jnp.zeros((n_streams, HIST_ROWS, POOL_WIDTH), f32)
            return _mixer_call(xp, zk, zk, zu, cnt_in, consts_p, stream0=stream0, tile=MIX_TILE,
                               sub=MIX_SUB, n_valid=MIX_TILE, pos0=0, mask_first=True)

        (x1_a, h_a, idx_a, rank_a, gate_a, cnt_a, k_a, v_a, u_a) = prompt_mixer(0, nb1, cnt0)
        xs_pad = jnp.pad(xs, ((0, 0), (0, PAIR - dseq), (0, 0)))
        uh = jnp.pad(state_pool[l], ((0, 0), (HIST_ROWS - POOL_HIST, 0), (0, 0)))
        (xs1, h_s, idx_s, rank_s, gate_s, cnt_s, k_s, v_s, u_s) = _mixer_call(
            xs_pad, cache_k[l].reshape(ndb, WINDOW, KV_WIDTH),
            cache_v[l].reshape(ndb, WINDOW, KV_WIDTH), uh, cnt0,
            _mixer_consts(*wl, vis_sample),
            stream0=0, tile=PAIR, sub=PAIR, n_valid=dseq, pos0=PAST_LEN, mask_first=False)
        (x1_b, h_b, idx_b, rank_b, gate_b, cnt_b, k_b, v_b, u_b) = prompt_mixer(nb1, nb2, cnt_s)
        picked_b, w_bf16 = _moe_rows([h_b, h_s], [idx_b, idx_s], [rank_b, rank_s], [seq, dseq], cnt_b,
                                     moe_b, w_f32=(w_gate[l], w_up[l], w_down[l]))
        picked_a, _ = _moe_rows([h_a], [idx_a], [rank_a], [seq], cnt_a, moe_b, w_bf16=w_bf16)

        n_a, n_b = nb1 * seq, nb2 * seq
        no_buf = jnp.zeros((8, 128), f32)
        stride_a, stride_b = _pick_stride(n_a), _pick_stride(n_b + n_s)
        xp_rows = _combine_call(no_buf, x1_b.reshape(n_b, D_MODEL), gate_b, picked_b, stride_b, 0,
                                out_rows=n_p, row0=n_a, tile=COMBINE_TILE)
        xp_rows = _combine_call(xp_rows, x1_a.reshape(n_a, D_MODEL), gate_a, picked_a, stride_a, 0,
                                out_rows=n_p, row0=0, tile=COMBINE_TILE)
        xp = xp_rows.reshape(nb, seq, D_MODEL)
        xs = _combine_call(no_buf, xs1[:, :dseq].reshape(n_s, D_MODEL), gate_s, picked_b, stride_b,
                           n_b, out_rows=n_s, row0=0, tile=n_s).reshape(ndb, dseq, D_MODEL)

        outs[0].append(jnp.concatenate([k_a, k_b]).reshape(nb, WINDOW, N_KV_HEADS, HEAD_DIM))
        outs[1].append(jnp.concatenate([v_a, v_b]).reshape(nb, WINDOW, N_KV_HEADS, HEAD_DIM))
        outs[2].append(jnp.concatenate([u_a, u_b])[:, HIST_ROWS - POOL_HIST:])
        outs[3].append(k_s[:, :dseq].reshape(ndb, dseq, N_KV_HEADS, HEAD_DIM))
        outs[4].append(v_s[:, :dseq].reshape(ndb, dseq, N_KV_HEADS, HEAD_DIM))
        outs[5].append(u_s[:, HIST_ROWS - POOL_HIST:])
    return (xp, xs) + tuple(jnp.stack(o) for o in outs)


def _moe_rows(hs, idxs, ranks, n_reals, cnt, biases, w_f32=None, w_bf16=None):
    n_tok = sum(h.shape[0] for h in hs)
    n_assign = n_tok * TOP_K
    n_blocks = n_assign // EXPERT_BLOCK + N_EXPERTS
    e_ids = jnp.arange(N_EXPERTS, dtype=jnp.int32)

    def per_token(arrs):
        return jnp.concatenate(
            [jnp.transpose(a[:, :, :n], (1, 0, 2)).reshape(TOP_K, -1) for a, n in zip(arrs, n_reals)],
            axis=1)

    counts = cnt[:, 0].astype(jnp.int32)
    pcounts = (counts + EXPERT_BLOCK - 1) // EXPERT_BLOCK * EXPERT_BLOCK
    if w_bf16 is None:
        pcounts = jnp.maximum(pcounts, EXPERT_BLOCK)
    pend = jnp.cumsum(pcounts)
    pstart = pend - pcounts
    idx_all = per_token(idxs)
    dest = per_token(ranks) + jnp.sum(
        jnp.where(idx_all[None] == e_ids[:, None, None], pstart[:, None, None], 0), axis=0)
    dest = dest.reshape(-1)
    blk0 = jnp.arange(n_blocks, dtype=jnp.int32) * EXPERT_BLOCK
    block_e = jnp.minimum(jnp.sum((pend[None, :] <= blk0[:, None]).astype(jnp.int32), axis=1),
                          N_EXPERTS - 1)
    block_rows = jnp.clip(jnp.sum(jnp.where(block_e[:, None] == e_ids[None, :],
                                             (pstart + counts)[None, :], 0), axis=1) - blk0,
                          0, EXPERT_BLOCK).astype(jnp.int32)

    x_sorted = _scatter_rows(hs, dest, n_blocks * EXPERT_BLOCK)
    if w_bf16 is None:
        block_next = jnp.minimum(block_e + 1, N_EXPERTS - 1)
        block_live = (blk0 < pend[-1]).astype(jnp.int32)
        y_sorted, *w_bf16 = _expert_call_f32(
            block_e, block_rows, block_next, block_live, x_sorted,
            w_f32[0], biases[0], w_f32[1], biases[1], w_f32[2], biases[2])
    else:
        has = counts > 0
        later = has[None, :] & (e_ids[None, :] > e_ids[:, None])
        next_e = jnp.min(jnp.where(later, e_ids[None, :], N_EXPERTS), axis=1)
        next_e = jnp.where(next_e == N_EXPERTS, e_ids, next_e)
        order = jnp.cumsum(has.astype(jnp.int32)) - 1
        table = jnp.stack([next_e, order % 2], axis=0)[:, None, :]
        block_next, block_slot = jnp.sum(
            jnp.where(block_e[None, :, None] == e_ids[None, None, :], table, 0), axis=2).astype(jnp.int32)
        y_sorted = _expert_call_bf16(
            block_e, block_rows, block_next, block_slot, x_sorted,
            w_bf16[0], biases[0], w_bf16[1], biases[1], w_bf16[2], biases[2])
    stride = _pick_stride(n_tok)
    n_pick = -(-(TOP_K * stride) // GATHER_QUANT) * GATHER_QUANT
    picks = jnp.pad(dest.reshape(TOP_K, n_tok), ((0, 0), (0, stride - n_tok))).reshape(-1)
    return _gather_rows(y_sorted, jnp.pad(picks, (0, n_pick - TOP_K * stride))), w_bf16


def _pick_stride(n_tok):
    return -(-n_tok // COMBINE_TILE) * COMBINE_TILE
```

```python
import functools
import math

import numpy as np
import jax
import jax.numpy as jnp
from jax import lax
from jax.experimental import pallas as pl
from jax.experimental.pallas import tpu as pltpu
from jax.experimental.pallas import tpu_sc as plsc

D_MODEL = 1024
CHUNK = 64
POOL_WIDTH = 512
POOL_WINDOWS = (2, 4, 8, 16)
POOL_GROUP = 128
N_POOL_GROUPS = 4
POOL_HIST = 15
ATTN_WIDTH = 512
HEAD_DIM = 64
N_HEADS = 8
N_KV_HEADS = 2
GROUP = 4
KV_WIDTH = 128
WINDOW = 128
NUM_BUCKETS = 32
MAX_DISTANCE = 128
PAST_LEN = 2048
N_EXPERTS = 32
TOP_K = 4
SWIGLU_LIMIT = 7.0
SWIGLU_ALPHA = 1.702
EPS = 1e-5
NEG_INF = -1e30
ATTN_SCALE = HEAD_DIM ** -0.5

PAIR = 2 * CHUNK
BAND = PAIR + WINDOW
HIST_ROWS = 16
GATE_COLS = 8
MIX_TILE = 1024
MIX_SUB = 1024
EXPERT_BLOCK = 1024
FFN_CHUNK = 256
COMBINE_TILE = 1024
SC_WORKERS = 32
SC_CHUNK = 64
SC_SCATTER_CHUNK = 128
SC_GATHER_WAYS = 2
GATHER_QUANT = SC_WORKERS * SC_CHUNK * SC_GATHER_WAYS
VMEM_LIMIT = 56 * 1024 * 1024


def _pack_bf16_pair(a, b):
    ab = lax.bitcast_convert_type(a.astype(jnp.bfloat16).astype(jnp.float32), jnp.uint32)
    bb = lax.bitcast_convert_type(b.astype(jnp.bfloat16).astype(jnp.float32), jnp.uint32)
    return (ab >> 16) | (bb & jnp.uint32(0xFFFF0000))


def _unpack_bf16_pair(w):
    a = lax.bitcast_convert_type(w << 16, jnp.float32).astype(jnp.bfloat16)
    b = lax.bitcast_convert_type(w & jnp.uint32(0xFFFF0000), jnp.float32).astype(jnp.bfloat16)
    return a, b


def _mixer_kernel(x_ref, kh_ref, vh_ref, uh_ref, cnt_in_ref,
                  g1_ref, wukv_ref, wqt_ref, qg_ref, kg_ref, bd_ref, bias_ref, sink_ref,
                  wpool_ref, pscale_ref, wout_ref, g2_ref, wr_ref, br_ref, tri_ref,
                  x1_ref, hp_ref, idx_ref, rank_ref, gate_ref, cnt_ref, ko_ref, vo_ref, uo_ref,
                  qt_s, kb_s, vt_s, ub_s, mix_s, cnt_s,
                  *, tile, sub, n_valid, pos0, mask_first):
    b = pl.program_id(0)
    s = pl.program_id(1)
    bf16 = jnp.bfloat16
    f32 = jnp.float32

    @pl.when((b == 0) & (s == 0))
    def _():
        cnt_s[...] = cnt_in_ref[...]

    @pl.when(s == 0)
    def _():
        kb_s[0:WINDOW, :] = kh_ref[...].astype(bf16)
        vt_s[:, 0:WINDOW] = jnp.transpose(vh_ref[...]).astype(bf16)
        ub_s[0:HIST_ROWS, :] = uh_ref[...]

    @pl.when(s > 0)
    def _():
        kb_s[0:WINDOW, :] = kb_s[tile:tile + WINDOW, :]
        vt_s[:, 0:WINDOW] = vt_s[:, tile:tile + WINDOW]
        ub_s[0:HIST_ROWS, :] = ub_s[tile:tile + HIST_ROWS, :]

    for r0 in range(0, tile, sub):
        _mixer_rows(r0, s, x_ref, g1_ref, wukv_ref, wqt_ref, qg_ref, kg_ref, bd_ref, bias_ref,
                    sink_ref, wpool_ref, pscale_ref, wout_ref, g2_ref, wr_ref, br_ref, tri_ref,
                    x1_ref, hp_ref, idx_ref, rank_ref, gate_ref, cnt_ref, ko_ref, vo_ref, uo_ref,
                    qt_s, kb_s, vt_s, ub_s, mix_s, cnt_s,
                    tile=tile, sub=sub, n_valid=n_valid, pos0=pos0, mask_first=mask_first)


def _mixer_rows(r0, s, x_ref, g1_ref, wukv_ref, wqt_ref, qg_ref, kg_ref, bd_ref, bias_ref,
                sink_ref, wpool_ref, pscale_ref, wout_ref, g2_ref, wr_ref, br_ref, tri_ref,
                x1_ref, hp_ref, idx_ref, rank_ref, gate_ref, cnt_ref, ko_ref, vo_ref, uo_ref,
                qt_s, kb_s, vt_s, ub_s, mix_s, cnt_s, *, tile, sub, n_valid, pos0, mask_first):
    bf16 = jnp.bfloat16
    f32 = jnp.float32
    rows = slice(r0, r0 + sub)

    x = x_ref[rows, :]
    xn = (x * lax.rsqrt(jnp.mean(x * x, axis=-1, keepdims=True) + EPS) * g1_ref[...]).astype(bf16)
    z = jnp.dot(xn, wukv_ref[...], preferred_element_type=f32)
    zt = lax.dot_general(wqt_ref[...], xn, (((1,), (1,)), ((), ())),
                         preferred_element_type=f32)
    u = z[:, 0:POOL_WIDTH]
    kz = z[:, POOL_WIDTH:POOL_WIDTH + KV_WIDTH]
    v = z[:, POOL_WIDTH + KV_WIDTH:]

    ksq = kz * kz
    kss = jnp.dot(ksq.astype(bf16), bd_ref[...], preferred_element_type=f32)
    kn = kz * lax.rsqrt(kss * (1.0 / HEAD_DIM) + EPS) * kg_ref[...]
    kb_s[WINDOW + r0:WINDOW + r0 + sub, :] = kn.astype(bf16)
    vt_s[:, WINDOW + r0:WINDOW + r0 + sub] = jnp.transpose(v).astype(bf16)
    ub_s[HIST_ROWS + r0:HIST_ROWS + r0 + sub, :] = u

    if r0 + sub == tile:
        row0 = max(n_valid, WINDOW) - WINDOW - r0
        ko_ref[...] = kn[row0:row0 + WINDOW, :]
        vo_ref[...] = v[row0:row0 + WINDOW, :]
        uo_ref[...] = u[n_valid - HIST_ROWS - r0:n_valid - r0, :]

    for hd in range(N_HEADS):
        qh = zt[hd * HEAD_DIM:(hd + 1) * HEAD_DIM, :]
        ss = jnp.sum(qh * qh, axis=0, keepdims=True)
        qn = qh * (lax.rsqrt(ss * (1.0 / HEAD_DIM) + EPS) * ATTN_SCALE) * qg_ref[...]
        qt_s[hd * HEAD_DIM:(hd + 1) * HEAD_DIM, rows] = qn.astype(bf16)

    pos = pos0 + s * tile + r0 + lax.broadcasted_iota(jnp.int32, (sub, 1), 0)
    diffs = []
    for g, w in enumerate(POOL_WINDOWS):
        e = ub_s[r0:r0 + HIST_ROWS + sub, g * POOL_GROUP:(g + 1) * POOL_GROUP]
        acc = e
        for lvl in range(g + 1):
            acc = acc + pltpu.roll(acc, 2 ** lvl, axis=0)
        inv_cnt = 1.0 / jnp.minimum(pos + 1, w).astype(f32)
        diffs.append((acc[HIST_ROWS:, :] * inv_cnt - e[HIST_ROWS:, :]).astype(bf16))
    for g2 in range(N_POOL_GROUPS // 2):
        cols = slice(2 * g2 * POOL_GROUP, 2 * (g2 + 1) * POOL_GROUP)
        d = jnp.concatenate(diffs[2 * g2:2 * g2 + 2], axis=1)
        y = jnp.dot(d, wpool_ref[g2], preferred_element_type=f32) * pscale_ref[:, cols]
        mix_s[rows, cols] = y.astype(bf16)

    zeros_q = jnp.zeros((HEAD_DIM, GROUP * PAIR), bf16)
    for p in range(r0 // PAIR, (r0 + sub) // PAIR):
        k_band = kb_s[p * PAIR:p * PAIR + BAND, :]
        o_parts = []
        rhs_heads = []
        for h in range(N_KV_HEADS):
            qcat = jnp.concatenate(
                [qt_s[(h * GROUP + g) * HEAD_DIM:(h * GROUP + g + 1) * HEAD_DIM,
                      p * PAIR:(p + 1) * PAIR] for g in range(GROUP)], axis=1)
            rhs_heads.append(jnp.concatenate([qcat, zeros_q] if h == 0 else [zeros_q, qcat], axis=0))
        st_heads = jnp.dot(k_band, jnp.concatenate(rhs_heads, axis=1),
                           preferred_element_type=f32)
        for h in range(N_KV_HEADS):
            st = st_heads[:, h * GROUP * PAIR:(h + 1) * GROUP * PAIR] + bias_ref[h]
            if mask_first and p == 0:
                krow = lax.broadcasted_iota(jnp.int32, (BAND, 1), 0)
                st = jnp.where((krow >= WINDOW) | (s > 0), st, NEG_INF)
            sink = sink_ref[h:h + 1, :]
            m = jnp.maximum(jnp.max(st, axis=0, keepdims=True), sink)
            ex = jnp.exp(st - m)
            den = jnp.sum(ex, axis=0, keepdims=True) + jnp.exp(sink - m)
            v_band = vt_s[h * HEAD_DIM:(h + 1) * HEAD_DIM, p * PAIR:p * PAIR + BAND]
            ot = jnp.dot(v_band, ex.astype(bf16), preferred_element_type=f32) / den
            for g in range(GROUP):
                o_parts.append(ot[:, g * PAIR:(g + 1) * PAIR])
        o_all = jnp.concatenate(o_parts, axis=0)
        mix_s[p * PAIR:(p + 1) * PAIR, POOL_WIDTH:] = jnp.transpose(o_all).astype(bf16)

    x1 = x + jnp.dot(mix_s[rows, :], wout_ref[...], preferred_element_type=f32)
    x1_ref[rows, :] = x1

    n_real = min(sub, n_valid - r0)
    hn = x1 * lax.rsqrt(jnp.mean(x1 * x1, axis=-1, keepdims=True) + EPS) * g2_ref[...]
    hp_ref[r0:r0 + n_real, :] = _pack_bf16_pair(hn[0:n_real, 0:D_MODEL // 2],
                                                hn[0:n_real, D_MODEL // 2:])
    h_hi = hn.astype(bf16)
    h_lo = (hn - h_hi.astype(f32)).astype(bf16)
    both = jnp.dot(jnp.concatenate([h_hi, h_lo], axis=0), wr_ref[...],
                   preferred_element_type=f32)
    parts = jnp.transpose(both[0:sub, :] + both[sub:, :])
    lt = parts[0:N_EXPERTS, :] + parts[N_EXPERTS:2 * N_EXPERTS, :] + br_ref[...]

    eidx = lax.broadcasted_iota(jnp.int32, (N_EXPERTS, sub), 0).astype(f32)
    vals, hots = [], []
    for j in range(TOP_K):
        m = jnp.max(lt, axis=0, keepdims=True)
        sel = jnp.min(jnp.where(lt == m, eidx, float(N_EXPERTS)), axis=0, keepdims=True)
        hot = eidx == sel
        lt = jnp.where(hot, -jnp.inf, lt)
        idx_ref[j:j + 1, rows] = sel.astype(jnp.int32)
        vals.append(m)
        hots.append(hot)
    exps = [jnp.exp(vv - vals[0]) for vv in vals]
    esum = exps[0] + exps[1] + exps[2] + exps[3]
    grow = lax.broadcasted_iota(jnp.int32, (GATE_COLS, sub), 0)
    gmat = jnp.zeros((GATE_COLS, sub), f32)
    for j in range(TOP_K):
        gmat = jnp.where(grow == j, exps[j] / esum, gmat)
    gmat = jnp.concatenate([gmat, jnp.zeros((128 - GATE_COLS, sub), f32)], axis=0)
    gate_ref[r0:r0 + n_real, :] = jnp.transpose(gmat)[0:n_real, 0:GATE_COLS]

    chosen_f = sum(jnp.where(hot, 1.0, 0.0) for hot in hots)
    if n_real < sub:
        lane = lax.broadcasted_iota(jnp.int32, (N_EXPERTS, sub), 1)
        chosen_f = jnp.where(lane < n_real, chosen_f, 0.0)
    n_lane_blocks = sub // 128
    blocks = [chosen_f[:, k * 128:(k + 1) * 128] for k in range(n_lane_blocks)]
    inside = jnp.dot(jnp.concatenate(blocks, axis=0).astype(bf16), tri_ref[...],
                     preferred_element_type=f32)
    offset = cnt_s[:, 0:1]
    pieces = []
    for k in range(n_lane_blocks):
        pieces.append(inside[k * N_EXPERTS:(k + 1) * N_EXPERTS, :] + offset)
        offset = offset + jnp.sum(blocks[k], axis=1, keepdims=True)
    base = jnp.concatenate(pieces, axis=1)
    for j in range(TOP_K):
        rank_ref[j:j + 1, rows] = jnp.sum(jnp.where(hots[j], base, 0.0), axis=0,
                                          keepdims=True).astype(jnp.int32)
    cnt_new = jnp.broadcast_to(offset, (N_EXPERTS, 128))
    cnt_s[...] = cnt_new
    cnt_ref[...] = cnt_new


def _mixer_call(x, k_hist, v_hist, u_hist, cnt_in, consts, *, stream0, tile, sub, n_valid, pos0,
                mask_first):
    seq = x.shape[1]
    nb = k_hist.shape[0]
    n_tiles = seq // tile
    f32 = jnp.float32
    assert n_valid % 8 == 0 and tile % sub == 0 and sub % PAIR == 0
    assert n_valid == tile or sub == tile

    def full(a):
        nd = a.ndim
        return pl.BlockSpec(a.shape, lambda b, s, _nd=nd: (0,) * _nd)

    in_specs = [
        pl.BlockSpec((None, tile, D_MODEL), lambda b, s: (stream0 + b, s, 0)),
        pl.BlockSpec((None, WINDOW, KV_WIDTH), lambda b, s: (b, 0, 0)),
        pl.BlockSpec((None, WINDOW, KV_WIDTH), lambda b, s: (b, 0, 0)),
        pl.BlockSpec((None, HIST_ROWS, POOL_WIDTH), lambda b, s: (b, 0, 0)),
        full(cnt_in),
    ] + [full(c) for c in consts]
    out_shape = [
        jax.ShapeDtypeStruct((nb, seq, D_MODEL), f32),
        jax.ShapeDtypeStruct((nb * n_tiles * n_valid, D_MODEL // 2), jnp.uint32),
        jax.ShapeDtypeStruct((nb, TOP_K, seq), jnp.int32),
        jax.ShapeDtypeStruct((nb, TOP_K, seq), jnp.int32),
        jax.ShapeDtypeStruct((nb * n_tiles * n_valid, GATE_COLS), f32),
        jax.ShapeDtypeStruct((N_EXPERTS, 128), f32),
        jax.ShapeDtypeStruct((nb, WINDOW, KV_WIDTH), f32),
        jax.ShapeDtypeStruct((nb, WINDOW, KV_WIDTH), f32),
        jax.ShapeDtypeStruct((nb, HIST_ROWS, POOL_WIDTH), f32),
    ]
    out_specs = [
        pl.BlockSpec((None, tile, D_MODEL), lambda b, s: (b, s, 0)),
        pl.BlockSpec((n_valid, D_MODEL // 2), lambda b, s: (b * n_tiles + s, 0)),
        pl.BlockSpec((None, TOP_K, tile), lambda b, s: (b, 0, s)),
        pl.BlockSpec((None, TOP_K, tile), lambda b, s: (b, 0, s)),
        pl.BlockSpec((n_valid, GATE_COLS), lambda b, s: (b * n_tiles + s, 0)),
        pl.BlockSpec((N_EXPERTS, 128), lambda b, s: (0, 0)),
        pl.BlockSpec((None, WINDOW, KV_WIDTH), lambda b, s: (b, 0, 0)),
        pl.BlockSpec((None, WINDOW, KV_WIDTH), lambda b, s: (b, 0, 0)),
        pl.BlockSpec((None, HIST_ROWS, POOL_WIDTH), lambda b, s: (b, 0, 0)),
    ]
    scratch = [
        pltpu.VMEM((ATTN_WIDTH, tile), jnp.bfloat16),
        pltpu.VMEM((WINDOW + tile, KV_WIDTH), jnp.bfloat16),
        pltpu.VMEM((KV_WIDTH, WINDOW + tile), jnp.bfloat16),
        pltpu.VMEM((HIST_ROWS + tile, POOL_WIDTH), f32),
        pltpu.VMEM((tile, D_MODEL), jnp.bfloat16),
        pltpu.VMEM((N_EXPERTS, 128), f32),
    ]
    kern = functools.partial(_mixer_kernel, tile=tile, sub=sub, n_valid=n_valid, pos0=pos0,
                             mask_first=mask_first)
    return pl.pallas_call(
        kern,
        grid=(nb, n_tiles),
        in_specs=in_specs,
        out_specs=out_specs,
        out_shape=out_shape,
        scratch_shapes=scratch,
        compiler_params=pltpu.CompilerParams(
            dimension_semantics=("arbitrary", "arbitrary"),
            vmem_limit_bytes=VMEM_LIMIT),
        name="mixer",
    )(x, k_hist, v_hist, u_hist, cnt_in, *consts)


def _expert_kernel_f32(be_ref, nv_ref, nx_ref, lv_ref, xs_ref, wg_hbm, bg_ref, wu_hbm, bu_ref,
                       wd_hbm, bd_ref, ys_ref, wg16_hbm, wu16_hbm, wd16_hbm,
                       stage_s, wg_s, wu_s, wd_s, in_sems, out_sems, *, n_blocks):
    i = pl.program_id(0)
    expert = be_ref[i]
    bf16 = jnp.bfloat16

    def copies_in(e):
        return [pltpu.make_async_copy(w_hbm.at[e], stage_s.at[k], in_sems.at[k])
                for k, w_hbm in enumerate((wg_hbm, wu_hbm, wd_hbm))]

    def copies_out(e):
        return [pltpu.make_async_copy(w_s, w16_hbm.at[e], out_sems.at[k])
                for k, (w_s, w16_hbm) in enumerate(((wg_s, wg16_hbm), (wu_s, wu16_hbm),
                                                    (wd_s, wd16_hbm)))]

    @pl.when(i == 0)
    def _():
        for copy in copies_in(expert):
            copy.start()

    @pl.when((lv_ref[i] > 0) & ((i == 0) | (expert != be_ref[jnp.maximum(i - 1, 0)])))
    def _():
        for copy in copies_in(expert):
            copy.wait()

        @pl.when(i > 0)
        def _():
            for copy in copies_out(expert):
                copy.wait()

        wg_s[...] = stage_s[0].astype(bf16)
        wu_s[...] = stage_s[1].astype(bf16)
        wd_s[...] = stage_s[2].astype(bf16)
        for copy in copies_out(expert):
            copy.start()

        @pl.when(nx_ref[i] != expert)
        def _():
            for copy in copies_in(nx_ref[i]):
                copy.start()

    _ffn_block(nv_ref[i], expert, xs_ref, ys_ref, wg_s, wu_s, wd_s, bg_ref, bu_ref, bd_ref)

    @pl.when(i == n_blocks - 1)
    def _():
        for copy in copies_out(expert):
            copy.wait()


def _expert_kernel_bf16(be_ref, nv_ref, nx_ref, sl_ref, xs_ref, wg_hbm, bg_ref, wu_hbm, bu_ref,
                        wd_hbm, bd_ref, ys_ref, w_s, sems):
    i = pl.program_id(0)
    n_rows = nv_ref[i]
    expert = be_ref[i]
    slot = sl_ref[i]

    def copies(e, to_slot):
        return [pltpu.make_async_copy(w_hbm.at[e], w_s.at[to_slot, k], sems.at[to_slot, k])
                for k, w_hbm in enumerate((wg_hbm, wu_hbm, wd_hbm))]

    @pl.when((i == 0) & (n_rows > 0))
    def _():
        for copy in copies(expert, slot):
            copy.start()

    @pl.when((n_rows > 0) & ((i == 0) | (expert != be_ref[jnp.maximum(i - 1, 0)])))
    def _():
        for copy in copies(expert, slot):
            copy.wait()

        @pl.when(nx_ref[i] != expert)
        def _():
            for copy in copies(nx_ref[i], 1 - slot):
                copy.start()

    _ffn_block(n_rows, expert, xs_ref, ys_ref, w_s.at[slot, 0], w_s.at[slot, 1], w_s.at[slot, 2],
               bg_ref, bu_ref, bd_ref)


def _ffn_block(n_rows, expert, xs_ref, ys_ref, wg_s, wu_s, wd_s, bg_all_ref, bu_all_ref, bd_all_ref):
    bf16 = jnp.bfloat16

    def ffn(rows):
        f32 = jnp.float32
        half = D_MODEL // 2
        bg_ref, bu_ref, bd_ref = (b.at[pl.ds(expert, 1), :]
                                  for b in (bg_all_ref, bu_all_ref, bd_all_ref))
        for r0 in range(0, rows, FFN_CHUNK):
            row = r0 + lax.broadcasted_iota(jnp.int32, (FFN_CHUNK, 1), 0)
            words = jnp.where(row < n_rows, xs_ref[r0:r0 + FFN_CHUNK, :],
                              jnp.uint32(0))
            xa, xb = _unpack_bf16_pair(words)
            a = (jnp.dot(xa, wg_s[0:half, :], preferred_element_type=f32)
                 + jnp.dot(xb, wg_s[half:, :], preferred_element_type=f32) + bg_ref[...])
            bb = (jnp.dot(xa, wu_s[0:half, :], preferred_element_type=f32)
                  + jnp.dot(xb, wu_s[half:, :], preferred_element_type=f32) + bu_ref[...])
            a = jnp.minimum(a, SWIGLU_LIMIT)
            bb = jnp.clip(bb, -SWIGLU_LIMIT, SWIGLU_LIMIT)
            act = a * (1.0 / (1.0 + jnp.exp(-SWIGLU_ALPHA * a))) * (bb + 1.0)
            y = jnp.dot(act.astype(bf16), wd_s[...], preferred_element_type=f32) + bd_ref[...]
            ys_ref[r0:r0 + FFN_CHUNK, :] = _pack_bf16_pair(y[:, 0:half], y[:, half:])
        if rows < EXPERT_BLOCK:
            ys_ref[rows:, :] = jnp.zeros((EXPERT_BLOCK - rows, half), jnp.uint32)

    quarter = EXPERT_BLOCK // 4
    for q in range(1, 5):
        @pl.when((n_rows > (q - 1) * quarter) & (n_rows <= q * quarter))
        def _(q=q):
            ffn(q * quarter)

    @pl.when(n_rows == 0)
    def _():
        ys_ref[...] = jnp.zeros_like(ys_ref)


def _expert_call_f32(block_e, block_rows, block_next, block_live, xs, wg, bg, wu, bu, wd, bd):
    n_slots = xs.shape[0]
    n_blocks = n_slots // EXPERT_BLOCK
    any_spec = pl.BlockSpec(memory_space=pl.ANY)
    b_spec = pl.BlockSpec((N_EXPERTS, D_MODEL), lambda i, be, nv, nx, lv: (0, 0))
    x_spec = pl.BlockSpec((EXPERT_BLOCK, D_MODEL // 2), lambda i, be, nv, nx, lv: (i, 0))
    grid_spec = pltpu.PrefetchScalarGridSpec(
        num_scalar_prefetch=4,
        grid=(n_blocks,),
        in_specs=[x_spec, any_spec, b_spec, any_spec, b_spec, any_spec, b_spec],
        out_specs=[x_spec, any_spec, any_spec, any_spec],
        scratch_shapes=[pltpu.VMEM((3, D_MODEL, D_MODEL), jnp.float32)]
        + [pltpu.VMEM((D_MODEL, D_MODEL), jnp.bfloat16)] * 3
        + [pltpu.SemaphoreType.DMA((3,)), pltpu.SemaphoreType.DMA((3,))],
    )
    w16 = jax.ShapeDtypeStruct((N_EXPERTS, D_MODEL, D_MODEL), jnp.bfloat16)
    return pl.pallas_call(
        functools.partial(_expert_kernel_f32, n_blocks=n_blocks),
        grid_spec=grid_spec,
        out_shape=[jax.ShapeDtypeStruct((n_slots, D_MODEL // 2), jnp.uint32), w16, w16, w16],
        compiler_params=pltpu.CompilerParams(
            dimension_semantics=("arbitrary",),
            vmem_limit_bytes=VMEM_LIMIT),
        name="experts_f32",
    )(block_e, block_rows, block_next, block_live, xs, wg, bg, wu, bu, wd, bd)


def _expert_call_bf16(block_e, block_rows, block_next, block_slot, xs, wg, bg, wu, bu, wd, bd):
    n_slots = xs.shape[0]
    any_spec = pl.BlockSpec(memory_space=pl.ANY)
    b_spec = pl.BlockSpec((N_EXPERTS, D_MODEL), lambda i, be, nv, nx, sl: (0, 0))
    x_spec = pl.BlockSpec((EXPERT_BLOCK, D_MODEL // 2), lambda i, be, nv, nx, sl: (i, 0))
    grid_spec = pltpu.PrefetchScalarGridSpec(
        num_scalar_prefetch=4,
        grid=(n_slots // EXPERT_BLOCK,),
        in_specs=[x_spec, any_spec, b_spec, any_spec, b_spec, any_spec, b_spec],
        out_specs=x_spec,
        scratch_shapes=[pltpu.VMEM((2, 3, D_MODEL, D_MODEL), jnp.bfloat16),
                        pltpu.SemaphoreType.DMA((2, 3))],
    )
    return pl.pallas_call(
        _expert_kernel_bf16,
        grid_spec=grid_spec,
        out_shape=jax.ShapeDtypeStruct((n_slots, D_MODEL // 2), jnp.uint32),
        compiler_params=pltpu.CompilerParams(
            dimension_semantics=("arbitrary",),
            vmem_limit_bytes=VMEM_LIMIT),
        name="experts_bf16",
    )(block_e, block_rows, block_next, block_slot, xs, wg, bg, wu, bu, wd, bd)


def _combine_kernel(out_buf_ref, x1_ref, g_ref, y0_ref, y1_ref, y2_ref, y3_ref, o_ref):
    del out_buf_ref
    g = g_ref[...]
    half = D_MODEL // 2
    lo, hi = x1_ref[:, 0:half], x1_ref[:, half:]
    for j, y_ref in enumerate((y0_ref, y1_ref, y2_ref, y3_ref)):
        w = y_ref[...]
        gj = g[:, j:j + 1]
        lo = lo + gj * lax.bitcast_convert_type(w << 16, jnp.float32)
        hi = hi + gj * lax.bitcast_convert_type(w & jnp.uint32(0xFFFF0000), jnp.float32)
    o_ref[:, 0:half] = lo
    o_ref[:, half:] = hi


def _combine_call(out_buf, x1, gates, picked, stride, tok0, *, out_rows, row0, tile):
    n = x1.shape[0]
    aliased = out_buf.shape == (out_rows, D_MODEL)
    assert n % tile == 0 and row0 % tile == 0
    y_specs = []
    for j in range(TOP_K):
        assert (j * stride + tok0) % tile == 0
        base = (j * stride + tok0) // tile
        y_specs.append(pl.BlockSpec((tile, D_MODEL // 2), lambda i, _b=base: (_b + i, 0)))
    return pl.pallas_call(
        _combine_kernel,
        grid=(n // tile,),
        in_specs=[pl.BlockSpec(memory_space=pl.ANY),
                  pl.BlockSpec((tile, D_MODEL), lambda i: (i, 0)),
                  pl.BlockSpec((tile, GATE_COLS), lambda i: (i, 0))] + y_specs,
        out_specs=pl.BlockSpec((tile, D_MODEL), lambda i: (row0 // tile + i, 0)),
        out_shape=jax.ShapeDtypeStruct((out_rows, D_MODEL), jnp.float32),
        input_output_aliases={0: 0} if aliased else {},
        compiler_params=pltpu.CompilerParams(dimension_semantics=("arbitrary",),
                                             vmem_limit_bytes=VMEM_LIMIT),
        name="combine",
    )(out_buf, x1, gates, picked, picked, picked, picked)


def _gather_rows(table, idx):
    n = idx.shape[0]
    width = table.shape[1]
    assert n % GATHER_QUANT == 0
    per_worker = n // SC_WORKERS
    n_rounds = per_worker // (SC_CHUNK * SC_GATHER_WAYS)
    mesh = plsc.VectorSubcoreMesh(core_axis_name="c", subcore_axis_name="s")

    @functools.partial(
        pl.kernel, mesh=mesh,
        out_type=jax.ShapeDtypeStruct((n, width), table.dtype),
        scratch_types=[pltpu.VMEM((SC_CHUNK,), jnp.int32)] * SC_GATHER_WAYS
        + [pltpu.VMEM((SC_CHUNK, width), table.dtype)] * SC_GATHER_WAYS
        + [pltpu.SemaphoreType.DMA] * 3,
        cost_estimate=pl.CostEstimate(flops=0, transcendentals=0, bytes_accessed=8 * n * width),
    )
    def gather(table_hbm, idx_hbm, out_hbm, *scratch):
        idx_vs = scratch[:SC_GATHER_WAYS]
        rows_vs = scratch[SC_GATHER_WAYS:2 * SC_GATHER_WAYS]
        sem_idx, sem_rows, sem_out = scratch[2 * SC_GATHER_WAYS:]
        wid = lax.axis_index("s") * 2 + lax.axis_index("c")
        base = wid * per_worker

        @pl.loop(0, n_rounds)
        def _(i):
            offs = [base + (i * SC_GATHER_WAYS + u) * SC_CHUNK for u in range(SC_GATHER_WAYS)]
            stage = [pltpu.async_copy(idx_hbm.at[pl.ds(off, SC_CHUNK)], idx_v, sem_idx)
                     for off, idx_v in zip(offs, idx_vs)]
            for copy in stage:
                copy.wait()
            stage = [pltpu.async_copy(table_hbm.at[idx_v], rows_v, sem_rows)
                     for idx_v, rows_v in zip(idx_vs, rows_vs)]
            for copy in stage:
                copy.wait()
            stage = [pltpu.async_copy(rows_v, out_hbm.at[pl.ds(off, SC_CHUNK)], sem_out)
                     for off, rows_v in zip(offs, rows_vs)]
            for copy in stage:
                copy.wait()

    return gather(table, idx)


def _scatter_rows(srcs, dest, n_out):
    width = srcs[0].shape[1]
    chunk = SC_SCATTER_CHUNK
    starts = [0]
    for src in srcs:
        assert src.shape[0] % chunk == 0
        starts.append(starts[-1] + src.shape[0] // chunk)
    n_chunks = starts[-1]
    n = n_chunks * chunk
    per_worker = -(-n_chunks // SC_WORKERS)
    mesh = plsc.VectorSubcoreMesh(core_axis_name="c", subcore_axis_name="s")

    @functools.partial(
        pl.kernel, mesh=mesh,
        out_type=jax.ShapeDtypeStruct((n_out, width), srcs[0].dtype),
        scratch_types=[pltpu.VMEM((chunk,), jnp.int32)] * TOP_K
        + [pltpu.VMEM((chunk, width), srcs[0].dtype),
           pltpu.SemaphoreType.DMA, pltpu.SemaphoreType.DMA],
        cost_estimate=pl.CostEstimate(flops=0, transcendentals=0,
                                      bytes_accessed=4 * (1 + TOP_K) * n * width),
    )
    def scatter(*refs):
        src_hbms = refs[:len(srcs)]
        dest_hbm, out_hbm = refs[len(srcs)], refs[len(srcs) + 1]
        idx_vs = refs[len(srcs) + 2:len(srcs) + 2 + TOP_K]
        rows_v, sem_in, sem_out = refs[-3:]
        wid = lax.axis_index("s") * 2 + lax.axis_index("c")

        @pl.loop(0, per_worker)
        def _(i):
            c = i * SC_WORKERS + wid
            for k, src_hbm in enumerate(src_hbms):

                @pl.when((c >= starts[k]) & (c < starts[k + 1]))
                def _():
                    loads = [pltpu.async_copy(
                        src_hbm.at[pl.ds((c - starts[k]) * chunk, chunk)], rows_v, sem_in)]
                    loads += [pltpu.async_copy(dest_hbm.at[pl.ds(j * n + c * chunk, chunk)],
                                               idx_v, sem_in) for j, idx_v in enumerate(idx_vs)]
                    for copy in loads:
                        copy.wait()
                    sends = [pltpu.async_copy(rows_v, out_hbm.at[idx_v], sem_out) for idx_v in idx_vs]
                    for copy in sends:
                        copy.wait()

    return scatter(*srcs, dest)


def _t5_bucket_np(rel):
    half = NUM_BUCKETS // 2
    max_exact = half // 2
    n = np.abs(rel)
    nf = np.maximum(n, 1).astype(np.float32)
    large = max_exact + (np.log(nf / max_exact) / math.log(MAX_DISTANCE / max_exact)
                         * (half - max_exact)).astype(np.int32)
    large = np.minimum(large, half - 1)
    return np.where(rel > 0, half, 0) + np.where(n < max_exact, n, large)


def _bias_tables(rel_bias, visible):
    kap = np.arange(BAND)[:, None]
    col = np.arange(GROUP * PAIR)[None, :]
    bucket = _t5_bucket_np(kap - WINDOW - col % PAIR)
    onehot = jnp.asarray(bucket[:, :, None] == np.arange(NUM_BUCKETS))
    seen = jnp.asarray(np.tile(visible, (1, GROUP)))
    rb = rel_bias.astype(jnp.float32)
    tabs = []
    for h in range(N_KV_HEADS):
        per_col = jnp.repeat(rb[:, h * GROUP:(h + 1) * GROUP], PAIR, axis=1)
        tab = jnp.sum(jnp.where(onehot, jnp.transpose(per_col)[None], 0.0), axis=-1)
        tabs.append(jnp.where(seen, tab, NEG_INF))
    return jnp.stack(tabs)


def _mixer_consts(l, norm1_g, w_in, q_norm_g, k_norm_g, rel_bias, sinks, w_pool, pool_scale,
                  w_out, norm2_g, w_router, b_router, visible):
    f32, bf16 = jnp.float32, jnp.bfloat16
    q_off, k_off, v_off = POOL_WIDTH, POOL_WIDTH + ATTN_WIDTH, POOL_WIDTH + ATTN_WIDTH + KV_WIDTH
    w = w_in[l]
    w_ukv = jnp.concatenate([w[:, :q_off], w[:, k_off:]], axis=1).astype(bf16)
    w_qt = jnp.transpose(w[:, q_off:k_off]).astype(bf16)
    lane_head = np.arange(KV_WIDTH) // HEAD_DIM
    blockdiag = jnp.asarray(lane_head[:, None] == lane_head[None, :], bf16)
    sink_rows = jnp.repeat(sinks[l].astype(f32).reshape(N_KV_HEADS, GROUP), PAIR, axis=1)
    wp = w_pool[l].astype(bf16)
    zp = jnp.zeros((POOL_GROUP, POOL_GROUP), bf16)
    w_pool_pairs = jnp.stack([jnp.block([[wp[2 * i], zp], [zp, wp[2 * i + 1]]])
                              for i in range(N_POOL_GROUPS // 2)])
    wr = w_router[l].astype(f32)
    wr_hi = wr.astype(bf16)
    wr_lo = (wr - wr_hi.astype(f32)).astype(bf16)
    wr_parts = jnp.pad(jnp.concatenate([wr_hi, wr_lo], axis=1), ((0, 0), (0, 128 - 2 * N_EXPERTS)))
    tri = jnp.asarray(np.arange(128)[:, None] < np.arange(128)[None, :], bf16)
    return [
        norm1_g[l].reshape(1, D_MODEL).astype(f32), w_ukv, w_qt,
        q_norm_g[l].reshape(HEAD_DIM, 1).astype(f32),
        jnp.tile(k_norm_g[l].astype(f32), N_KV_HEADS).reshape(1, KV_WIDTH),
        blockdiag, _bias_tables(rel_bias, visible), sink_rows,
        w_pool_pairs, pool_scale[l].reshape(1, POOL_WIDTH).astype(f32),
        w_out[l].astype(bf16), norm2_g[l].reshape(1, D_MODEL).astype(f32),
        wr_parts, b_router[l].reshape(N_EXPERTS, 1).astype(f32), tri,
    ]


def _visibility():
    kap = np.arange(BAND)[:, None]
    rho = np.arange(PAIR)[None, :]
    kc, qc = kap // CHUNK, rho // CHUNK
    prompt = (kc >= qc) & (kc <= qc + WINDOW // CHUNK)
    return prompt


def kernel(x_prompt, x_sample, cache_k, cache_v, state_pool, norm1_g, w_in, q_norm_g, k_norm_g,
           rel_bias, sinks, w_pool, pool_scale, w_out, norm2_g, w_router, b_router,
           w_gate, b_gate, w_up, b_up, w_down, b_down):
    f32, bf16 = jnp.float32, jnp.bfloat16
    depth = w_in.shape[0]
    nb, seq, _ = x_prompt.shape
    ndb, dseq, _ = x_sample.shape
    cache_len = cache_k.shape[2]
    assert seq % MIX_TILE == 0 and cache_len == WINDOW and HIST_ROWS <= dseq <= PAIR
    n_p, n_s = nb * seq, ndb * dseq
    n_tok = n_p + n_s
    assert COMBINE_TILE % n_s == 0 and n_s % SC_SCATTER_CHUNK == 0

    vis_prompt = _visibility()
    vis_sample = np.broadcast_to(np.arange(BAND)[:, None] < WINDOW + dseq, (BAND, PAIR))

    nb1 = nb // 4
    nb2 = nb - nb1
    assert nb1 > 0 and nb2 > 0 and seq % COMBINE_TILE == 0

    xp, xs = x_prompt, x_sample
    outs = [[] for _ in range(6)]
    for l in range(depth):
        wl = (l, norm1_g, w_in, q_norm_g, k_norm_g, rel_bias, sinks, w_pool, pool_scale, w_out,
              norm2_g, w_router, b_router)
        consts_p = _mixer_consts(*wl, vis_prompt)
        moe_b = [b.astype(f32) for b in (b_gate[l], b_up[l], b_down[l])]
        cnt0 = jnp.zeros((N_EXPERTS, 128), f32)

        def prompt_mixer(stream0, n_streams, cnt_in):
            zk = jnp.zeros((n_streams, WINDOW, KV_WIDTH), f32)
            zu = jnp.zeros((n_streams, HIST_ROWS, POOL_WIDTH), f32)
            return _mixer_call(xp, zk, zk, zu, cnt_in, consts_p, stream0=stream0, tile=MIX_TILE,
                               sub=MIX_SUB, n_valid=MIX_TILE, pos0=0, mask_first=True)

        (x1_a, h_a, idx_a, rank_a, gate_a, cnt_a, k_a, v_a, u_a) = prompt_mixer(0, nb1, cnt0)
        xs_pad = jnp.pad(xs, ((0, 0), (0, PAIR - dseq), (0, 0)))
        uh = jnp.pad(state_pool[l], ((0, 0), (HIST_ROWS - POOL_HIST, 0), (0, 0)))
        (xs1, h_s, idx_s, rank_s, gate_s, cnt_s, k_s, v_s, u_s) = _mixer_call(
            xs_pad, cache_k[l].reshape(ndb, WINDOW, KV_WIDTH),
            cache_v[l].reshape(ndb, WINDOW, KV_WIDTH), uh, cnt0,
            _mixer_consts(*wl, vis_sample),
            stream0=0, tile=PAIR, sub=PAIR, n_valid=dseq, pos0=PAST_LEN, mask_first=False)
        (x1_b, h_b, idx_b, rank_b, gate_b, cnt_b, k_b, v_b, u_b) = prompt_mixer(nb1, nb2, cnt_s)
        picked_b, w_bf16 = _moe_rows([h_b, h_s], [idx_b, idx_s], [rank_b, rank_s], [seq, dseq], cnt_b,
                                     moe_b, w_f32=(w_gate[l], w_up[l], w_down[l]))
        picked_a, _ = _moe_rows([h_a], [idx_a], [rank_a], [seq], cnt_a, moe_b, w_bf16=w_bf16)

        n_a, n_b = nb1 * seq, nb2 * seq
        no_buf = jnp.zeros((8, 128), f32)
        stride_a, stride_b = _pick_stride(n_a), _pick_stride(n_b + n_s)
        xp_rows = _combine_call(no_buf, x1_b.reshape(n_b, D_MODEL), gate_b, picked_b, stride_b, 0,
                                out_rows=n_p, row0=n_a, tile=COMBINE_TILE)
        xp_rows = _combine_call(xp_rows, x1_a.reshape(n_a, D_MODEL), gate_a, picked_a, stride_a, 0,
                                out_rows=n_p, row0=0, tile=COMBINE_TILE)
        xp = xp_rows.reshape(nb, seq, D_MODEL)
        xs = _combine_call(no_buf, xs1[:, :dseq].reshape(n_s, D_MODEL), gate_s, picked_b, stride_b,
                           n_b, out_rows=n_s, row0=0, tile=n_s).reshape(ndb, dseq, D_MODEL)

        outs[0].append(jnp.concatenate([k_a, k_b]).reshape(nb, WINDOW, N_KV_HEADS, HEAD_DIM))
        outs[1].append(jnp.concatenate([v_a, v_b]).reshape(nb, WINDOW, N_KV_HEADS, HEAD_DIM))
        outs[2].append(jnp.concatenate([u_a, u_b])[:, HIST_ROWS - POOL_HIST:])
        outs[3].append(k_s[:, :dseq].reshape(ndb, dseq, N_KV_HEADS, HEAD_DIM))
        outs[4].append(v_s[:, :dseq].reshape(ndb, dseq, N_KV_HEADS, HEAD_DIM))
        outs[5].append(u_s[:, HIST_ROWS - POOL_HIST:])
    return (xp, xs) + tuple(jnp.stack(o) for o in outs)


def _moe_rows(hs, idxs, ranks, n_reals, cnt, biases, w_f32=None, w_bf16=None):
    n_tok = sum(h.shape[0] for h in hs)
    n_assign = n_tok * TOP_K
    n_blocks = n_assign // EXPERT_BLOCK + N_EXPERTS
    e_ids = jnp.arange(N_EXPERTS, dtype=jnp.int32)

    def per_token(arrs):
        return jnp.concatenate(
            [jnp.transpose(a[:, :, :n], (1, 0, 2)).reshape(TOP_K, -1) for a, n in zip(arrs, n_reals)],
            axis=1)

    counts = cnt[:, 0].astype(jnp.int32)
    pcounts = (counts + EXPERT_BLOCK - 1) // EXPERT_BLOCK * EXPERT_BLOCK
    if w_bf16 is None:
        pcounts = jnp.maximum(pcounts, EXPERT_BLOCK)
    pend = jnp.cumsum(pcounts)
    pstart = pend - pcounts
    idx_all = per_token(idxs)
    dest = per_token(ranks) + jnp.sum(
        jnp.where(idx_all[None] == e_ids[:, None, None], pstart[:, None, None], 0), axis=0)
    dest = dest.reshape(-1)
    blk0 = jnp.arange(n_blocks, dtype=jnp.int32) * EXPERT_BLOCK
    block_e = jnp.minimum(jnp.sum((pend[None, :] <= blk0[:, None]).astype(jnp.int32), axis=1),
                          N_EXPERTS - 1)
    block_rows = jnp.clip(jnp.sum(jnp.where(block_e[:, None] == e_ids[None, :],
                                             (pstart + counts)[None, :], 0), axis=1) - blk0,
                          0, EXPERT_BLOCK).astype(jnp.int32)

    x_sorted = _scatter_rows(hs, dest, n_blocks * EXPERT_BLOCK)
    if w_bf16 is None:
        block_next = jnp.minimum(block_e + 1, N_EXPERTS - 1)
        block_live = (blk0 < pend[-1]).astype(jnp.int32)
        y_sorted, *w_bf16 = _expert_call_f32(
            block_e, block_rows, block_next, block_live, x_sorted,
            w_f32[0], biases[0], w_f32[1], biases[1], w_f32[2], biases[2])
    else:
        has = counts > 0
        later = has[None, :] & (e_ids[None, :] > e_ids[:, None])
        next_e = jnp.min(jnp.where(later, e_ids[None, :], N_EXPERTS), axis=1)
        next_e = jnp.where(next_e == N_EXPERTS, e_ids, next_e)
        order = jnp.cumsum(has.astype(jnp.int32)) - 1
        table = jnp.stack([next_e, order % 2], axis=0)[:, None, :]
        block_next, block_slot = jnp.sum(
            jnp.where(block_e[None, :, None] == e_ids[None, None, :], table, 0), axis=2).astype(jnp.int32)
        y_sorted = _expert_call_bf16(
            block_e, block_rows, block_next, block_slot, x_sorted,
            w_bf16[0], biases[0], w_bf16[1], biases[1], w_bf16[2], biases[2])
    stride = _pick_stride(n_tok)
    n_pick = -(-(TOP_K * stride) // GATHER_QUANT) * GATHER_QUANT
    picks = jnp.pad(dest.reshape(TOP_K, n_tok), ((0, 0), (0, stride - n_tok))).reshape(-1)
    return _gather_rows(y_sorted, jnp.pad(picks, (0, n_pick - TOP_K * stride))), w_bf16


def _pick_stride(n_tok):
    return -(-n_tok // COMBINE_TILE) * COMBINE_TILE
```

```python
import functools
import math

import numpy as np
import jax
import jax.numpy as jnp
from jax import lax
from jax.experimental import pallas as pl
from jax.experimental.pallas import tpu as pltpu
from jax.experimental.pallas import tpu_sc as plsc

D_MODEL = 1024
CHUNK = 64
POOL_WIDTH = 512
POOL_WINDOWS = (2, 4, 8, 16)
POOL_GROUP = 128
N_POOL_GROUPS = 4
POOL_HIST = 15
ATTN_WIDTH = 512
HEAD_DIM = 64
N_HEADS = 8
N_KV_HEADS = 2
GROUP = 4
KV_WIDTH = 128
WINDOW = 128
NUM_BUCKETS = 32
MAX_DISTANCE = 128
PAST_LEN = 2048
N_EXPERTS = 32
TOP_K = 4
SWIGLU_LIMIT = 7.0
SWIGLU_ALPHA = 1.702
EPS = 1e-5
NEG_INF = -1e30
ATTN_SCALE = HEAD_DIM ** -0.5

PAIR = 2 * CHUNK
BAND = PAIR + WINDOW
HIST_ROWS = 16
GATE_COLS = 8
MIX_TILE = 1024
MIX_SUB = 1024
EXPERT_BLOCK = 1024
FFN_CHUNK = 256
COMBINE_TILE = 1024
SC_WORKERS = 32
SC_CHUNK = 64
SC_SCATTER_CHUNK = 128
SC_GATHER_WAYS = 2
GATHER_QUANT = SC_WORKERS * SC_CHUNK * SC_GATHER_WAYS
VMEM_LIMIT = 56 * 1024 * 1024


def _pack_bf16_pair(a, b):
    ab = lax.bitcast_convert_type(a.astype(jnp.bfloat16).astype(jnp.float32), jnp.uint32)
    bb = lax.bitcast_convert_type(b.astype(jnp.bfloat16).astype(jnp.float32), jnp.uint32)
    return (ab >> 16) | (bb & jnp.uint32(0xFFFF0000))


def _unpack_bf16_pair(w):
    a = lax.bitcast_convert_type(w << 16, jnp.float32).astype(jnp.bfloat16)
    b = lax.bitcast_convert_type(w & jnp.uint32(0xFFFF0000), jnp.float32).astype(jnp.bfloat16)
    return a, b


def _mixer_kernel(x_ref, kh_ref, vh_ref, uh_ref, cnt_in_ref,
                  g1_ref, wukv_ref, wqt_ref, qg_ref, kg_ref, bd_ref, bias_ref, sink_ref,
                  wpool_ref, pscale_ref, wout_ref, g2_ref, wr_ref, br_ref, tri_ref,
                  x1_ref, hp_ref, idx_ref, rank_ref, gate_ref, cnt_ref, ko_ref, vo_ref, uo_ref,
                  qt_s, kb_s, vt_s, ub_s, mix_s, cnt_s,
                  *, tile, sub, n_valid, pos0, mask_first):
    b = pl.program_id(0)
    s = pl.program_id(1)
    bf16 = jnp.bfloat16
    f32 = jnp.float32

    @pl.when((b == 0) & (s == 0))
    def _():
        cnt_s[...] = cnt_in_ref[...]

    @pl.when(s == 0)
    def _():
        kb_s[0:WINDOW, :] = kh_ref[...].astype(bf16)
        vt_s[:, 0:WINDOW] = jnp.transpose(vh_ref[...]).astype(bf16)
        ub_s[0:HIST_ROWS, :] = uh_ref[...]

    @pl.when(s > 0)
    def _():
        kb_s[0:WINDOW, :] = kb_s[tile:tile + WINDOW, :]
        vt_s[:, 0:WINDOW] = vt_s[:, tile:tile + WINDOW]
        ub_s[0:HIST_ROWS, :] = ub_s[tile:tile + HIST_ROWS, :]

    for r0 in range(0, tile, sub):
        _mixer_rows(r0, s, x_ref, g1_ref, wukv_ref, wqt_ref, qg_ref, kg_ref, bd_ref, bias_ref,
                    sink_ref, wpool_ref, pscale_ref, wout_ref, g2_ref, wr_ref, br_ref, tri_ref,
                    x1_ref, hp_ref, idx_ref, rank_ref, gate_ref, cnt_ref, ko_ref, vo_ref, uo_ref,
                    qt_s, kb_s, vt_s, ub_s, mix_s, cnt_s,
                    tile=tile, sub=sub, n_valid=n_valid, pos0=pos0, mask_first=mask_first)


def _mixer_rows(r0, s, x_ref, g1_ref, wukv_ref, wqt_ref, qg_ref, kg_ref, bd_ref, bias_ref,
                sink_ref, wpool_ref, pscale_ref, wout_ref, g2_ref, wr_ref, br_ref, tri_ref,
                x1_ref, hp_ref, idx_ref, rank_ref, gate_ref, cnt_ref, ko_ref, vo_ref, uo_ref,
                qt_s, kb_s, vt_s, ub_s, mix_s, cnt_s, *, tile, sub, n_valid, pos0, mask_first):
    bf16 = jnp.bfloat16
    f32 = jnp.float32
    rows = slice(r0, r0 + sub)

    x = x_ref[rows, :]
    xn = (x * lax.rsqrt(jnp.mean(x * x, axis=-1, keepdims=True) + EPS) * g1_ref[...]).astype(bf16)
    z = jnp.dot(xn, wukv_ref[...], preferred_element_type=f32)
    zt = jnp.transpose(lax.dot_general(xn, wqt_ref[...], (((1,), (1,)), ((), ())),
                                       preferred_element_type=f32))
    u = z[:, 0:POOL_WIDTH]
    kz = z[:, POOL_WIDTH:POOL_WIDTH + KV_WIDTH]
    v = z[:, POOL_WIDTH + KV_WIDTH:]

    ksq = kz * kz
    kss = jnp.dot(ksq.astype(bf16), bd_ref[...], preferred_element_type=f32)
    kn = kz * lax.rsqrt(kss * (1.0 / HEAD_DIM) + EPS) * kg_ref[...]
    kb_s[WINDOW + r0:WINDOW + r0 + sub, :] = kn.astype(bf16)
    vt_s[:, WINDOW + r0:WINDOW + r0 + sub] = jnp.transpose(v).astype(bf16)
    ub_s[HIST_ROWS + r0:HIST_ROWS + r0 + sub, :] = u

    if r0 + sub == tile:
        row0 = max(n_valid, WINDOW) - WINDOW - r0
        ko_ref[...] = kn[row0:row0 + WINDOW, :]
        vo_ref[...] = v[row0:row0 + WINDOW, :]
        uo_ref[...] = u[n_valid - HIST_ROWS - r0:n_valid - r0, :]

    for hd in range(N_HEADS):
        qh = zt[hd * HEAD_DIM:(hd + 1) * HEAD_DIM, :]
        ss = jnp.sum(qh * qh, axis=0, keepdims=True)
        qn = qh * (lax.rsqrt(ss * (1.0 / HEAD_DIM) + EPS) * ATTN_SCALE) * qg_ref[...]
        qt_s[hd * HEAD_DIM:(hd + 1) * HEAD_DIM, rows] = qn.astype(bf16)

    pos = pos0 + s * tile + r0 + lax.broadcasted_iota(jnp.int32, (sub, 1), 0)
    diffs = []
    for g, w in enumerate(POOL_WINDOWS):
        e = ub_s[r0:r0 + HIST_ROWS + sub, g * POOL_GROUP:(g + 1) * POOL_GROUP]
        acc = e
        for lvl in range(g + 1):
            acc = acc + pltpu.roll(acc, 2 ** lvl, axis=0)
        inv_cnt = 1.0 / jnp.minimum(pos + 1, w).astype(f32)
        diffs.append((acc[HIST_ROWS:, :] * inv_cnt - e[HIST_ROWS:, :]).astype(bf16))
    for g2 in range(N_POOL_GROUPS // 2):
        cols = slice(2 * g2 * POOL_GROUP, 2 * (g2 + 1) * POOL_GROUP)
        d = jnp.concatenate(diffs[2 * g2:2 * g2 + 2], axis=1)
        y = jnp.dot(d, wpool_ref[g2], preferred_element_type=f32) * pscale_ref[:, cols]
        mix_s[rows, cols] = y.astype(bf16)

    zeros_q = jnp.zeros((HEAD_DIM, GROUP * PAIR), bf16)
    for p in range(r0 // PAIR, (r0 + sub) // PAIR):
        k_band = kb_s[p * PAIR:p * PAIR + BAND, :]
        o_parts = []
        rhs_heads = []
        for h in range(N_KV_HEADS):
            qcat = jnp.concatenate(
                [qt_s[(h * GROUP + g) * HEAD_DIM:(h * GROUP + g + 1) * HEAD_DIM,
                      p * PAIR:(p + 1) * PAIR] for g in range(GROUP)], axis=1)
            rhs_heads.append(jnp.concatenate([qcat, zeros_q] if h == 0 else [zeros_q, qcat], axis=0))
        st_heads = jnp.dot(k_band, jnp.concatenate(rhs_heads, axis=1),
                           preferred_element_type=f32)
        for h in range(N_KV_HEADS):
            st = st_heads[:, h * GROUP * PAIR:(h + 1) * GROUP * PAIR] + bias_ref[h]
            if mask_first and p == 0:
                krow = lax.broadcasted_iota(jnp.int32, (BAND, 1), 0)
                st = jnp.where((krow >= WINDOW) | (s > 0), st, NEG_INF)
            sink = sink_ref[h:h + 1, :]
            m = jnp.maximum(jnp.max(st, axis=0, keepdims=True), sink)
            ex = jnp.exp(st - m)
            den = jnp.sum(ex, axis=0, keepdims=True) + jnp.exp(sink - m)
            v_band = vt_s[h * HEAD_DIM:(h + 1) * HEAD_DIM, p * PAIR:p * PAIR + BAND]
            ot = jnp.dot(v_band, ex.astype(bf16), preferred_element_type=f32) / den
            for g in range(GROUP):
                o_parts.append(ot[:, g * PAIR:(g + 1) * PAIR])
        o_all = jnp.concatenate(o_parts, axis=0)
        mix_s[p * PAIR:(p + 1) * PAIR, POOL_WIDTH:] = jnp.transpose(o_all).astype(bf16)

    x1 = x + jnp.dot(mix_s[rows, :], wout_ref[...], preferred_element_type=f32)
    x1_ref[rows, :] = x1

    n_real = min(sub, n_valid - r0)
    hn = x1 * lax.rsqrt(jnp.mean(x1 * x1, axis=-1, keepdims=True) + EPS) * g2_ref[...]
    hp_ref[r0:r0 + n_real, :] = _pack_bf16_pair(hn[0:n_real, 0:D_MODEL // 2],
                                                hn[0:n_real, D_MODEL // 2:])
    h_hi = hn.astype(bf16)
    h_lo = (hn - h_hi.astype(f32)).astype(bf16)
    both = jnp.dot(jnp.concatenate([h_hi, h_lo], axis=0), wr_ref[...],
                   preferred_element_type=f32)
    parts = jnp.transpose(both[0:sub, :] + both[sub:, :])
    lt = parts[0:N_EXPERTS, :] + parts[N_EXPERTS:2 * N_EXPERTS, :] + br_ref[...]

    eidx = lax.broadcasted_iota(jnp.int32, (N_EXPERTS, sub), 0).astype(f32)
    vals, hots = [], []
    for j in range(TOP_K):
        m = jnp.max(lt, axis=0, keepdims=True)
        sel = jnp.min(jnp.where(lt == m, eidx, float(N_EXPERTS)), axis=0, keepdims=True)
        hot = eidx == sel
        lt = jnp.where(hot, -jnp.inf, lt)
        idx_ref[j:j + 1, rows] = sel.astype(jnp.int32)
        vals.append(m)
        hots.append(hot)
    exps = [jnp.exp(vv - vals[0]) for vv in vals]
    esum = exps[0] + exps[1] + exps[2] + exps[3]
    grow = lax.broadcasted_iota(jnp.int32, (GATE_COLS, sub), 0)
    gmat = jnp.zeros((GATE_COLS, sub), f32)
    for j in range(TOP_K):
        gmat = jnp.where(grow == j, exps[j] / esum, gmat)
    gmat = jnp.concatenate([gmat, jnp.zeros((128 - GATE_COLS, sub), f32)], axis=0)
    gate_ref[r0:r0 + n_real, :] = jnp.transpose(gmat)[0:n_real, 0:GATE_COLS]

    chosen_f = sum(jnp.where(hot, 1.0, 0.0) for hot in hots)
    if n_real < sub:
        lane = lax.broadcasted_iota(jnp.int32, (N_EXPERTS, sub), 1)
        chosen_f = jnp.where(lane < n_real, chosen_f, 0.0)
    n_lane_blocks = sub // 128
    blocks = [chosen_f[:, k * 128:(k + 1) * 128] for k in range(n_lane_blocks)]
    inside = jnp.dot(jnp.concatenate(blocks, axis=0).astype(bf16), tri_ref[...],
                     preferred_element_type=f32)
    offset = cnt_s[:, 0:1]
    pieces = []
    for k in range(n_lane_blocks):
        pieces.append(inside[k * N_EXPERTS:(k + 1) * N_EXPERTS, :] + offset)
        offset = offset + jnp.sum(blocks[k], axis=1, keepdims=True)
    base = jnp.concatenate(pieces, axis=1)
    for j in range(TOP_K):
        rank_ref[j:j + 1, rows] = jnp.sum(jnp.where(hots[j], base, 0.0), axis=0,
                                          keepdims=True).astype(jnp.int32)
    cnt_new = jnp.broadcast_to(offset, (N_EXPERTS, 128))
    cnt_s[...] = cnt_new
    cnt_ref[...] = cnt_new


def _mixer_call(x, k_hist, v_hist, u_hist, cnt_in, consts, *, stream0, tile, sub, n_valid, pos0,
                mask_first):
    seq = x.shape[1]
    nb = k_hist.shape[0]
    n_tiles = seq // tile
    f32 = jnp.float32
    assert n_valid % 8 == 0 and tile % sub == 0 and sub % PAIR == 0
    assert n_valid == tile or sub == tile

    def full(a):
        nd = a.ndim
        return pl.BlockSpec(a.shape, lambda b, s, _nd=nd: (0,) * _nd)

    in_specs = [
        pl.BlockSpec((None, tile, D_MODEL), lambda b, s: (stream0 + b, s, 0)),
        pl.BlockSpec((None, WINDOW, KV_WIDTH), lambda b, s: (b, 0, 0)),
        pl.BlockSpec((None, WINDOW, KV_WIDTH), lambda b, s: (b, 0, 0)),
        pl.BlockSpec((None, HIST_ROWS, POOL_WIDTH), lambda b, s: (b, 0, 0)),
        full(cnt_in),
    ] + [full(c) for c in consts]
    out_shape = [
        jax.ShapeDtypeStruct((nb, seq, D_MODEL), f32),
        jax.ShapeDtypeStruct((nb * n_tiles * n_valid, D_MODEL // 2), jnp.uint32),
        jax.ShapeDtypeStruct((nb, TOP_K, seq), jnp.int32),
        jax.ShapeDtypeStruct((nb, TOP_K, seq), jnp.int32),
        jax.ShapeDtypeStruct((nb * n_tiles * n_valid, GATE_COLS), f32),
        jax.ShapeDtypeStruct((N_EXPERTS, 128), f32),
        jax.ShapeDtypeStruct((nb, WINDOW, KV_WIDTH), f32),
        jax.ShapeDtypeStruct((nb, WINDOW, KV_WIDTH), f32),
        jax.ShapeDtypeStruct((nb, HIST_ROWS, POOL_WIDTH), f32),
    ]
    out_specs = [
        pl.BlockSpec((None, tile, D_MODEL), lambda b, s: (b, s, 0)),
        pl.BlockSpec((n_valid, D_MODEL // 2), lambda b, s: (b * n_tiles + s, 0)),
        pl.BlockSpec((None, TOP_K, tile), lambda b, s: (b, 0, s)),
        pl.BlockSpec((None, TOP_K, tile), lambda b, s: (b, 0, s)),
        pl.BlockSpec((n_valid, GATE_COLS), lambda b, s: (b * n_tiles + s, 0)),
        pl.BlockSpec((N_EXPERTS, 128), lambda b, s: (0, 0)),
        pl.BlockSpec((None, WINDOW, KV_WIDTH), lambda b, s: (b, 0, 0)),
        pl.BlockSpec((None, WINDOW, KV_WIDTH), lambda b, s: (b, 0, 0)),
        pl.BlockSpec((None, HIST_ROWS, POOL_WIDTH), lambda b, s: (b, 0, 0)),
    ]
    scratch = [
        pltpu.VMEM((ATTN_WIDTH, tile), jnp.bfloat16),
        pltpu.VMEM((WINDOW + tile, KV_WIDTH), jnp.bfloat16),
        pltpu.VMEM((KV_WIDTH, WINDOW + tile), jnp.bfloat16),
        pltpu.VMEM((HIST_ROWS + tile, POOL_WIDTH), f32),
        pltpu.VMEM((tile, D_MODEL), jnp.bfloat16),
        pltpu.VMEM((N_EXPERTS, 128), f32),
    ]
    kern = functools.partial(_mixer_kernel, tile=tile, sub=sub, n_valid=n_valid, pos0=pos0,
                             mask_first=mask_first)
    return pl.pallas_call(
        kern,
        grid=(nb, n_tiles),
        in_specs=in_specs,
        out_specs=out_specs,
        out_shape=out_shape,
        scratch_shapes=scratch,
        compiler_params=pltpu.CompilerParams(
            dimension_semantics=("arbitrary", "arbitrary"),
            vmem_limit_bytes=VMEM_LIMIT),
        name="mixer",
    )(x, k_hist, v_hist, u_hist, cnt_in, *consts)


def _expert_kernel_f32(be_ref, nv_ref, nx_ref, lv_ref, xs_ref, wg_hbm, bg_ref, wu_hbm, bu_ref,
                       wd_hbm, bd_ref, ys_ref, wg16_hbm, wu16_hbm, wd16_hbm,
                       stage_s, wg_s, wu_s, wd_s, in_sems, out_sems, *, n_blocks):
    i = pl.program_id(0)
    expert = be_ref[i]
    bf16 = jnp.bfloat16

    def copies_in(e):
        return [pltpu.make_async_copy(w_hbm.at[e], stage_s.at[k], in_sems.at[k])
                for k, w_hbm in enumerate((wg_hbm, wu_hbm, wd_hbm))]

    def copies_out(e):
        return [pltpu.make_async_copy(w_s, w16_hbm.at[e], out_sems.at[k])
                for k, (w_s, w16_hbm) in enumerate(((wg_s, wg16_hbm), (wu_s, wu16_hbm),
                                                    (wd_s, wd16_hbm)))]

    @pl.when(i == 0)
    def _():
        for copy in copies_in(expert):
            copy.start()

    @pl.when((lv_ref[i] > 0) & ((i == 0) | (expert != be_ref[jnp.maximum(i - 1, 0)])))
    def _():
        for copy in copies_in(expert):
            copy.wait()

        @pl.when(i > 0)
        def _():
            for copy in copies_out(expert):
                copy.wait()

        wg_s[...] = stage_s[0].astype(bf16)
        wu_s[...] = stage_s[1].astype(bf16)
        wd_s[...] = stage_s[2].astype(bf16)
        for copy in copies_out(expert):
            copy.start()

        @pl.when(nx_ref[i] != expert)
        def _():
            for copy in copies_in(nx_ref[i]):
                copy.start()

    _ffn_block(nv_ref[i], expert, xs_ref, ys_ref, wg_s, wu_s, wd_s, bg_ref, bu_ref, bd_ref)

    @pl.when(i == n_blocks - 1)
    def _():
        for copy in copies_out(expert):
            copy.wait()


def _expert_kernel_bf16(be_ref, nv_ref, nx_ref, sl_ref, xs_ref, wg_hbm, bg_ref, wu_hbm, bu_ref,
                        wd_hbm, bd_ref, ys_ref, w_s, sems):
    i = pl.program_id(0)
    n_rows = nv_ref[i]
    expert = be_ref[i]
    slot = sl_ref[i]

    def copies(e, to_slot):
        return [pltpu.make_async_copy(w_hbm.at[e], w_s.at[to_slot, k], sems.at[to_slot, k])
                for k, w_hbm in enumerate((wg_hbm, wu_hbm, wd_hbm))]

    @pl.when((i == 0) & (n_rows > 0))
    def _():
        for copy in copies(expert, slot):
            copy.start()

    @pl.when((n_rows > 0) & ((i == 0) | (expert != be_ref[jnp.maximum(i - 1, 0)])))
    def _():
        for copy in copies(expert, slot):
            copy.wait()

        @pl.when(nx_ref[i] != expert)
        def _():
            for copy in copies(nx_ref[i], 1 - slot):
                copy.start()

    _ffn_block(n_rows, expert, xs_ref, ys_ref, w_s.at[slot, 0], w_s.at[slot, 1], w_s.at[slot, 2],
               bg_ref, bu_ref, bd_ref)


def _ffn_block(n_rows, expert, xs_ref, ys_ref, wg_s, wu_s, wd_s, bg_all_ref, bu_all_ref, bd_all_ref):
    bf16 = jnp.bfloat16

    def ffn(rows):
        f32 = jnp.float32
        half = D_MODEL // 2
        bg_ref, bu_ref, bd_ref = (b.at[pl.ds(expert, 1), :]
                                  for b in (bg_all_ref, bu_all_ref, bd_all_ref))
        for r0 in range(0, rows, FFN_CHUNK):
            row = r0 + lax.broadcasted_iota(jnp.int32, (FFN_CHUNK, 1), 0)
            words = jnp.where(row < n_rows, xs_ref[r0:r0 + FFN_CHUNK, :],
                              jnp.uint32(0))
            xa, xb = _unpack_bf16_pair(words)
            a = (jnp.dot(xa, wg_s[0:half, :], preferred_element_type=f32)
                 + jnp.dot(xb, wg_s[half:, :], preferred_element_type=f32) + bg_ref[...])
            bb = (jnp.dot(xa, wu_s[0:half, :], preferred_element_type=f32)
                  + jnp.dot(xb, wu_s[half:, :], preferred_element_type=f32) + bu_ref[...])
            a = jnp.minimum(a, SWIGLU_LIMIT)
            bb = jnp.clip(bb, -SWIGLU_LIMIT, SWIGLU_LIMIT)
            act = a * (1.0 / (1.0 + jnp.exp(-SWIGLU_ALPHA * a))) * (bb + 1.0)
            y = jnp.dot(act.astype(bf16), wd_s[...], preferred_element_type=f32) + bd_ref[...]
            ys_ref[r0:r0 + FFN_CHUNK, :] = _pack_bf16_pair(y[:, 0:half], y[:, half:])
        if rows < EXPERT_BLOCK:
            ys_ref[rows:, :] = jnp.zeros((EXPERT_BLOCK - rows, half), jnp.uint32)

    quarter = EXPERT_BLOCK // 4
    for q in range(1, 5):
        @pl.when((n_rows > (q - 1) * quarter) & (n_rows <= q * quarter))
        def _(q=q):
            ffn(q * quarter)

    @pl.when(n_rows == 0)
    def _():
        ys_ref[...] = jnp.zeros_like(ys_ref)


def _expert_call_f32(block_e, block_rows, block_next, block_live, xs, wg, bg, wu, bu, wd, bd):
    n_slots = xs.shape[0]
    n_blocks = n_slots // EXPERT_BLOCK
    any_spec = pl.BlockSpec(memory_space=pl.ANY)
    b_spec = pl.BlockSpec((N_EXPERTS, D_MODEL), lambda i, be, nv, nx, lv: (0, 0))
    x_spec = pl.BlockSpec((EXPERT_BLOCK, D_MODEL // 2), lambda i, be, nv, nx, lv: (i, 0))
    grid_spec = pltpu.PrefetchScalarGridSpec(
        num_scalar_prefetch=4,
        grid=(n_blocks,),
        in_specs=[x_spec, any_spec, b_spec, any_spec, b_spec, any_spec, b_spec],
        out_specs=[x_spec, any_spec, any_spec, any_spec],
        scratch_shapes=[pltpu.VMEM((3, D_MODEL, D_MODEL), jnp.float32)]
        + [pltpu.VMEM((D_MODEL, D_MODEL), jnp.bfloat16)] * 3
        + [pltpu.SemaphoreType.DMA((3,)), pltpu.SemaphoreType.DMA((3,))],
    )
    w16 = jax.ShapeDtypeStruct((N_EXPERTS, D_MODEL, D_MODEL), jnp.bfloat16)
    return pl.pallas_call(
        functools.partial(_expert_kernel_f32, n_blocks=n_blocks),
        grid_spec=grid_spec,
        out_shape=[jax.ShapeDtypeStruct((n_slots, D_MODEL // 2), jnp.uint32), w16, w16, w16],
        compiler_params=pltpu.CompilerParams(
            dimension_semantics=("arbitrary",),
            vmem_limit_bytes=VMEM_LIMIT),
        name="experts_f32",
    )(block_e, block_rows, block_next, block_live, xs, wg, bg, wu, bu, wd, bd)


def _expert_call_bf16(block_e, block_rows, block_next, block_slot, xs, wg, bg, wu, bu, wd, bd):
    n_slots = xs.shape[0]
    any_spec = pl.BlockSpec(memory_space=pl.ANY)
    b_spec = pl.BlockSpec((N_EXPERTS, D_MODEL), lambda i, be, nv, nx, sl: (0, 0))
    x_spec = pl.BlockSpec((EXPERT_BLOCK, D_MODEL // 2), lambda i, be, nv, nx, sl: (i, 0))
    grid_spec = pltpu.PrefetchScalarGridSpec(
        num_scalar_prefetch=4,
        grid=(n_slots // EXPERT_BLOCK,),
        in_specs=[x_spec, any_spec, b_spec, any_spec, b_spec, any_spec, b_spec],
        out_specs=x_spec,
        scratch_shapes=[pltpu.VMEM((2, 3, D_MODEL, D_MODEL), jnp.bfloat16),
                        pltpu.SemaphoreType.DMA((2, 3))],
    )
    return pl.pallas_call(
        _expert_kernel_bf16,
        grid_spec=grid_spec,
        out_shape=jax.ShapeDtypeStruct((n_slots, D_MODEL // 2), jnp.uint32),
        compiler_params=pltpu.CompilerParams(
            dimension_semantics=("arbitrary",),
            vmem_limit_bytes=VMEM_LIMIT),
        name="experts_bf16",
    )(block_e, block_rows, block_next, block_slot, xs, wg, bg, wu, bu, wd, bd)


def _combine_kernel(out_buf_ref, x1_ref, g_ref, y0_ref, y1_ref, y2_ref, y3_ref, o_ref):
    del out_buf_ref
    g = g_ref[...]
    half = D_MODEL // 2
    lo, hi = x1_ref[:, 0:half], x1_ref[:, half:]
    for j, y_ref in enumerate((y0_ref, y1_ref, y2_ref, y3_ref)):
        w = y_ref[...]
        gj = g[:, j:j + 1]
        lo = lo + gj * lax.bitcast_convert_type(w << 16, jnp.float32)
        hi = hi + gj * lax.bitcast_convert_type(w & jnp.uint32(0xFFFF0000), jnp.float32)
    o_ref[:, 0:half] = lo
    o_ref[:, half:] = hi


def _combine_call(out_buf, x1, gates, picked, stride, tok0, *, out_rows, row0, tile):
    n = x1.shape[0]
    aliased = out_buf.shape == (out_rows, D_MODEL)
    assert n % tile == 0 and row0 % tile == 0
    y_specs = []
    for j in range(TOP_K):
        assert (j * stride + tok0) % tile == 0
        base = (j * stride + tok0) // tile
        y_specs.append(pl.BlockSpec((tile, D_MODEL // 2), lambda i, _b=base: (_b + i, 0)))
    return pl.pallas_call(
        _combine_kernel,
        grid=(n // tile,),
        in_specs=[pl.BlockSpec(memory_space=pl.ANY),
                  pl.BlockSpec((tile, D_MODEL), lambda i: (i, 0)),
                  pl.BlockSpec((tile, GATE_COLS), lambda i: (i, 0))] + y_specs,
        out_specs=pl.BlockSpec((tile, D_MODEL), lambda i: (row0 // tile + i, 0)),
        out_shape=jax.ShapeDtypeStruct((out_rows, D_MODEL), jnp.float32),
        input_output_aliases={0: 0} if aliased else {},
        compiler_params=pltpu.CompilerParams(dimension_semantics=("arbitrary",),
                                             vmem_limit_bytes=VMEM_LIMIT),
        name="combine",
    )(out_buf, x1, gates, picked, picked, picked, picked)


def _gather_rows(table, idx):
    n = idx.shape[0]
    width = table.shape[1]
    assert n % GATHER_QUANT == 0
    per_worker = n // SC_WORKERS
    n_rounds = per_worker // (SC_CHUNK * SC_GATHER_WAYS)
    mesh = plsc.VectorSubcoreMesh(core_axis_name="c", subcore_axis_name="s")

    @functools.partial(
        pl.kernel, mesh=mesh,
        out_type=jax.ShapeDtypeStruct((n, width), table.dtype),
        scratch_types=[pltpu.VMEM((SC_CHUNK,), jnp.int32)] * SC_GATHER_WAYS
        + [pltpu.VMEM((SC_CHUNK, width), table.dtype)] * SC_GATHER_WAYS
        + [pltpu.SemaphoreType.DMA] * 3,
        cost_estimate=pl.CostEstimate(flops=0, transcendentals=0, bytes_accessed=8 * n * width),
    )
    def gather(table_hbm, idx_hbm, out_hbm, *scratch):
        idx_vs = scratch[:SC_GATHER_WAYS]
        rows_vs = scratch[SC_GATHER_WAYS:2 * SC_GATHER_WAYS]
        sem_idx, sem_rows, sem_out = scratch[2 * SC_GATHER_WAYS:]
        wid = lax.axis_index("s") * 2 + lax.axis_index("c")
        base = wid * per_worker

        @pl.loop(0, n_rounds)
        def _(i):
            offs = [base + (i * SC_GATHER_WAYS + u) * SC_CHUNK for u in range(SC_GATHER_WAYS)]
            stage = [pltpu.async_copy(idx_hbm.at[pl.ds(off, SC_CHUNK)], idx_v, sem_idx)
                     for off, idx_v in zip(offs, idx_vs)]
            for copy in stage:
                copy.wait()
            stage = [pltpu.async_copy(table_hbm.at[idx_v], rows_v, sem_rows)
                     for idx_v, rows_v in zip(idx_vs, rows_vs)]
            for copy in stage:
                copy.wait()
            stage = [pltpu.async_copy(rows_v, out_hbm.at[pl.ds(off, SC_CHUNK)], sem_out)
                     for off, rows_v in zip(offs, rows_vs)]
            for copy in stage:
                copy.wait()

    return gather(table, idx)


def _scatter_rows(srcs, dest, n_out):
    width = srcs[0].shape[1]
    chunk = SC_SCATTER_CHUNK
    starts = [0]
    for src in srcs:
        assert src.shape[0] % chunk == 0
        starts.append(starts[-1] + src.shape[0] // chunk)
    n_chunks = starts[-1]
    n = n_chunks * chunk
    per_worker = -(-n_chunks // SC_WORKERS)
    mesh = plsc.VectorSubcoreMesh(core_axis_name="c", subcore_axis_name="s")

    @functools.partial(
        pl.kernel, mesh=mesh,
        out_type=jax.ShapeDtypeStruct((n_out, width), srcs[0].dtype),
        scratch_types=[pltpu.VMEM((chunk,), jnp.int32)] * TOP_K
        + [pltpu.VMEM((chunk, width), srcs[0].dtype),
           pltpu.SemaphoreType.DMA, pltpu.SemaphoreType.DMA],
        cost_estimate=pl.CostEstimate(flops=0, transcendentals=0,
                                      bytes_accessed=4 * (1 + TOP_K) * n * width),
    )
    def scatter(*refs):
        src_hbms = refs[:len(srcs)]
        dest_hbm, out_hbm = refs[len(srcs)], refs[len(srcs) + 1]
        idx_vs = refs[len(srcs) + 2:len(srcs) + 2 + TOP_K]
        rows_v, sem_in, sem_out = refs[-3:]
        wid = lax.axis_index("s") * 2 + lax.axis_index("c")

        @pl.loop(0, per_worker)
        def _(i):
            c = i * SC_WORKERS + wid
            for k, src_hbm in enumerate(src_hbms):

                @pl.when((c >= starts[k]) & (c < starts[k + 1]))
                def _():
                    loads = [pltpu.async_copy(
                        src_hbm.at[pl.ds((c - starts[k]) * chunk, chunk)], rows_v, sem_in)]
                    loads += [pltpu.async_copy(dest_hbm.at[pl.ds(j * n + c * chunk, chunk)],
                                               idx_v, sem_in) for j, idx_v in enumerate(idx_vs)]
                    for copy in loads:
                        copy.wait()
                    sends = [pltpu.async_copy(rows_v, out_hbm.at[idx_v], sem_out) for idx_v in idx_vs]
                    for copy in sends:
                        copy.wait()

    return scatter(*srcs, dest)


def _t5_bucket_np(rel):
    half = NUM_BUCKETS // 2
    max_exact = half // 2
    n = np.abs(rel)
    nf = np.maximum(n, 1).astype(np.float32)
    large = max_exact + (np.log(nf / max_exact) / math.log(MAX_DISTANCE / max_exact)
                         * (half - max_exact)).astype(np.int32)
    large = np.minimum(large, half - 1)
    return np.where(rel > 0, half, 0) + np.where(n < max_exact, n, large)


def _bias_tables(rel_bias, visible):
    kap = np.arange(BAND)[:, None]
    col = np.arange(GROUP * PAIR)[None, :]
    bucket = _t5_bucket_np(kap - WINDOW - col % PAIR)
    onehot = jnp.asarray(bucket[:, :, None] == np.arange(NUM_BUCKETS))
    seen = jnp.asarray(np.tile(visible, (1, GROUP)))
    rb = rel_bias.astype(jnp.float32)
    tabs = []
    for h in range(N_KV_HEADS):
        per_col = jnp.repeat(rb[:, h * GROUP:(h + 1) * GROUP], PAIR, axis=1)
        tab = jnp.sum(jnp.where(onehot, jnp.transpose(per_col)[None], 0.0), axis=-1)
        tabs.append(jnp.where(seen, tab, NEG_INF))
    return jnp.stack(tabs)


def _mixer_consts(l, norm1_g, w_in, q_norm_g, k_norm_g, rel_bias, sinks, w_pool, pool_scale,
                  w_out, norm2_g, w_router, b_router, visible):
    f32, bf16 = jnp.float32, jnp.bfloat16
    q_off, k_off = POOL_WIDTH, POOL_WIDTH + ATTN_WIDTH
    w = w_in[l]
    w_ukv = jnp.concatenate([w[:, :q_off], w[:, k_off:]], axis=1).astype(bf16)
    w_qt = jnp.transpose(w[:, q_off:k_off]).astype(bf16)
    lane_head = np.arange(KV_WIDTH) // HEAD_DIM
    blockdiag = jnp.asarray(lane_head[:, None] == lane_head[None, :], bf16)
    sink_rows = jnp.repeat(sinks[l].astype(f32).reshape(N_KV_HEADS, GROUP), PAIR, axis=1)
    wp = w_pool[l].astype(bf16)
    zp = jnp.zeros((POOL_GROUP, POOL_GROUP), bf16)
    w_pool_pairs = jnp.stack([jnp.block([[wp[2 * i], zp], [zp, wp[2 * i + 1]]])
                              for i in range(N_POOL_GROUPS // 2)])
    wr = w_router[l].astype(f32)
    wr_hi = wr.astype(bf16)
    wr_lo = (wr - wr_hi.astype(f32)).astype(bf16)
    wr_parts = jnp.pad(jnp.concatenate([wr_hi, wr_lo], axis=1), ((0, 0), (0, 128 - 2 * N_EXPERTS)))
    tri = jnp.asarray(np.arange(128)[:, None] < np.arange(128)[None, :], bf16)
    return [
        norm1_g[l].reshape(1, D_MODEL).astype(f32), w_ukv, w_qt,
        q_norm_g[l].reshape(HEAD_DIM, 1).astype(f32),
        jnp.tile(k_norm_g[l].astype(f32), N_KV_HEADS).reshape(1, KV_WIDTH),
        blockdiag, _bias_tables(rel_bias, visible), sink_rows,
        w_pool_pairs, pool_scale[l].reshape(1, POOL_WIDTH).astype(f32),
        w_out[l].astype(bf16), norm2_g[l].reshape(1, D_MODEL).astype(f32),
        wr_parts, b_router[l].reshape(N_EXPERTS, 1).astype(f32), tri,
    ]


def _visibility():
    kap = np.arange(BAND)[:, None]
    rho = np.arange(PAIR)[None, :]
    kc, qc = kap // CHUNK, rho // CHUNK
    prompt = (kc >= qc) & (kc <= qc + WINDOW // CHUNK)
    return prompt


def kernel(x_prompt, x_sample, cache_k, cache_v, state_pool, norm1_g, w_in, q_norm_g, k_norm_g,
           rel_bias, sinks, w_pool, pool_scale, w_out, norm2_g, w_router, b_router,
           w_gate, b_gate, w_up, b_up, w_down, b_down):
    f32, bf16 = jnp.float32, jnp.bfloat16
    depth = w_in.shape[0]
    nb, seq, _ = x_prompt.shape
    ndb, dseq, _ = x_sample.shape
    cache_len = cache_k.shape[2]
    assert seq % MIX_TILE == 0 and cache_len == WINDOW and HIST_ROWS <= dseq <= PAIR
    n_p, n_s = nb * seq, ndb * dseq
    n_tok = n_p + n_s
    assert COMBINE_TILE % n_s == 0 and n_s % SC_SCATTER_CHUNK == 0

    vis_prompt = _visibility()
    vis_sample = np.broadcast_to(np.arange(BAND)[:, None] < WINDOW + dseq, (BAND, PAIR))

    nb1 = nb // 4
    nb2 = nb - nb1
    assert nb1 > 0 and nb2 > 0 and seq % COMBINE_TILE == 0

    xp, xs = x_prompt, x_sample
    outs = [[] for _ in range(6)]
    for l in range(depth):
        wl = (l, norm1_g, w_in, q_norm_g, k_norm_g, rel_bias, sinks, w_pool, pool_scale, w_out,
              norm2_g, w_router, b_router)
        consts_p = _mixer_consts(*wl, vis_prompt)
        moe_b = [b.astype(f32) for b in (b_gate[l], b_up[l], b_down[l])]
        cnt0 = jnp.zeros((N_EXPERTS, 128), f32)

        def prompt_mixer(stream0, n_streams, cnt_in):
            zk = jnp.zeros((n_streams, WINDOW, KV_WIDTH), f32)
            zu = jnp.zeros((n_streams, HIST_ROWS, POOL_WIDTH), f32)
            return _mixer_call(xp, zk, zk, zu, cnt_in, consts_p, stream0=stream0, tile=MIX_TILE,
                               sub=MIX_SUB, n_valid=MIX_TILE, pos0=0, mask_first=True)

        (x1_a, h_a, idx_a, rank_a, gate_a, cnt_a, k_a, v_a, u_a) = prompt_mixer(0, nb1, cnt0)
        xs_pad = jnp.pad(xs, ((0, 0), (0, PAIR - dseq), (0, 0)))
        uh = jnp.pad(state_pool[l], ((0, 0), (HIST_ROWS - POOL_HIST, 0), (0, 0)))
        (xs1, h_s, idx_s, rank_s, gate_s, cnt_s, k_s, v_s, u_s) = _mixer_call(
            xs_pad, cache_k[l].reshape(ndb, WINDOW, KV_WIDTH),
            cache_v[l].reshape(ndb, WINDOW, KV_WIDTH), uh, cnt0,
            _mixer_consts(*wl, vis_sample),
            stream0=0, tile=PAIR, sub=PAIR, n_valid=dseq, pos0=PAST_LEN, mask_first=False)
        (x1_b, h_b, idx_b, rank_b, gate_b, cnt_b, k_b, v_b, u_b) = prompt_mixer(nb1, nb2, cnt_s)
        picked_b, w_bf16 = _moe_rows([h_b, h_s], [idx_b, idx_s], [rank_b, rank_s], [seq, dseq], cnt_b,
                                     moe_b, w_f32=(w_gate[l], w_up[l], w_down[l]))
        picked_a, _ = _moe_rows([h_a], [idx_a], [rank_a], [seq], cnt_a, moe_b, w_bf16=w_bf16)

        n_a, n_b = nb1 * seq, nb2 * seq
        no_buf = jnp.zeros((8, 128), f32)
        stride_a, stride_b = _pick_stride(n_a), _pick_stride(n_b + n_s)
        xp_rows = _combine_call(no_buf, x1_b.reshape(n_b, D_MODEL), gate_b, picked_b, stride_b, 0,
                                out_rows=n_p, row0=n_a, tile=COMBINE_TILE)
        xp_rows = _combine_call(xp_rows, x1_a.reshape(n_a, D_MODEL), gate_a, picked_a, stride_a, 0,
                                out_rows=n_p, row0=0, tile=COMBINE_TILE)
        xp = xp_rows.reshape(nb, seq, D_MODEL)
        xs = _combine_call(no_buf, xs1[:, :dseq].reshape(n_s, D_MODEL), gate_s, picked_b, stride_b,
                           n_b, out_rows=n_s, row0=0, tile=n_s).reshape(ndb, dseq, D_MODEL)

        outs[0].append(jnp.concatenate([k_a, k_b]).reshape(nb, WINDOW, N_KV_HEADS, HEAD_DIM))
        outs[1].append(jnp.concatenate([v_a, v_b]).reshape(nb, WINDOW, N_KV_HEADS, HEAD_DIM))
        outs[2].append(jnp.concatenate([u_a, u_b])[:, HIST_ROWS - POOL_HIST:])
        outs[3].append(k_s[:, :dseq].reshape(ndb, dseq, N_KV_HEADS, HEAD_DIM))
        outs[4].append(v_s[:, :dseq].reshape(ndb, dseq, N_KV_HEADS, HEAD_DIM))
        outs[5].append(u_s[:, HIST_ROWS - POOL_HIST:])
    return (xp, xs) + tuple(jnp.stack(o) for o in outs)


def _moe_rows(hs, idxs, ranks, n_reals, cnt, biases, w_f32=None, w_bf16=None):
    n_tok = sum(h.shape[0] for h in hs)
    n_assign = n_tok * TOP_K
    n_blocks = n_assign // EXPERT_BLOCK + N_EXPERTS
    e_ids = jnp.arange(N_EXPERTS, dtype=jnp.int32)

    def per_token(arrs):
        return jnp.concatenate(
            [jnp.transpose(a[:, :, :n], (1, 0, 2)).reshape(TOP_K, -1) for a, n in zip(arrs, n_reals)],
            axis=1)

    counts = cnt[:, 0].astype(jnp.int32)
    pcounts = (counts + EXPERT_BLOCK - 1) // EXPERT_BLOCK * EXPERT_BLOCK
    if w_bf16 is None:
        pcounts = jnp.maximum(pcounts, EXPERT_BLOCK)
    pend = jnp.cumsum(pcounts)
    pstart = pend - pcounts
    idx_all = per_token(idxs)
    dest = per_token(ranks) + jnp.sum(
        jnp.where(idx_all[None] == e_ids[:, None, None], pstart[:, None, None], 0), axis=0)
    dest = dest.reshape(-1)
    blk0 = jnp.arange(n_blocks, dtype=jnp.int32) * EXPERT_BLOCK
    block_e = jnp.minimum(jnp.sum((pend[None, :] <= blk0[:, None]).astype(jnp.int32), axis=1),
                          N_EXPERTS - 1)
    block_rows = jnp.clip(jnp.sum(jnp.where(block_e[:, None] == e_ids[None, :],
                                             (pstart + counts)[None, :], 0), axis=1) - blk0,
                          0, EXPERT_BLOCK).astype(jnp.int32)

    x_sorted = _scatter_rows(hs, dest, n_blocks * EXPERT_BLOCK)
    if w_bf16 is None:
        block_next = jnp.minimum(block_e + 1, N_EXPERTS - 1)
        block_live = (blk0 < pend[-1]).astype(jnp.int32)
        y_sorted, *w_bf16 = _expert_call_f32(
            block_e, block_rows, block_next, block_live, x_sorted,
            w_f32[0], biases[0], w_f32[1], biases[1], w_f32[2], biases[2])
    else:
        has = counts > 0
        later = has[None, :] & (e_ids[None, :] > e_ids[:, None])
        next_e = jnp.min(jnp.where(later, e_ids[None, :], N_EXPERTS), axis=1)
        next_e = jnp.where(next_e == N_EXPERTS, e_ids, next_e)
        order = jnp.cumsum(has.astype(jnp.int32)) - 1
        table = jnp.stack([next_e, order % 2], axis=0)[:, None, :]
        block_next, block_slot = jnp.sum(
            jnp.where(block_e[None, :, None] == e_ids[None, None, :], table, 0), axis=2).astype(jnp.int32)
        y_sorted = _expert_call_bf16(
            block_e, block_rows, block_next, block_slot, x_sorted,
            w_bf16[0], biases[0], w_bf16[1], biases[1], w_bf16[2], biases[2])
    stride = _pick_stride(n_tok)
    n_pick = -(-(TOP_K * stride) // GATHER_QUANT) * GATHER_QUANT
    picks = jnp.pad(dest.reshape(TOP_K, n_tok), ((0, 0), (0, stride - n_tok))).reshape(-1)
    return _gather_rows(y_sorted, jnp.pad(picks, (0, n_pick - TOP_K * stride))), w_bf16


def _pick_stride(n_tok):
    return -(-n_tok // COMBINE_TILE) * COMBINE_TILE
```
